```python
import math
import jax, jax.numpy as jnp
from jax import lax
import numpy as np

D_MODEL = 1024
BATCH = 2
SEQ = 8192
DEPTH = 4

MEM_LEN = 256
N_HEADS = 8
HEAD_DIM = 64
KV_LATENT = 128
IDX_HEADS = 8
IDX_DIM = 32
INDEX_TOPK = 256
Q_BLOCK = 128
CONV_WIDTH = 512
CONV_K = 3
SSM_WIDTH = 512
SSM_GROUP = 16
SSM_GROUPS = SSM_WIDTH // SSM_GROUP
SSM_STATE = 64
MEM_HEADS = 4
MEM_HEAD_DIM = 128
N_BRANCH = 4
BRANCH_WIDTH = 512
N_EXPERTS = 32
TOP_K = 4
D_EXPERT = 1024
SWIGLU_LIMIT = 7.0
SWIGLU_ALPHA = 1.702
EXPERT_BLOCK = 256
LN_EPS = 1e-5
DEEPNORM_ALPHA = (2 * DEPTH) ** 0.25
DEEPNORM_BETA = (8 * DEPTH) ** -0.25

SPLITS = (N_HEADS * HEAD_DIM, KV_LATENT, IDX_HEADS * IDX_DIM, IDX_DIM, IDX_HEADS,
          CONV_WIDTH, CONV_WIDTH, CONV_WIDTH, SSM_WIDTH, MEM_HEADS * MEM_HEAD_DIM, N_BRANCH * D_MODEL)
D_IN = sum(SPLITS)

kernel_name = 'hybrid_dsa_conv_s5_mem_moe_deepnorm'


def layer_norm(x, g, b):
    xf = x.astype(jnp.float32)
    mu = jnp.mean(xf, axis=-1, keepdims=True)
    var = jnp.mean(jnp.square(xf - mu), axis=-1, keepdims=True)
    return ((xf - mu) * lax.rsqrt(var + LN_EPS) * g.astype(jnp.float32) + b.astype(jnp.float32)).astype(x.dtype)


def rms_norm(x, g):
    xf = x.astype(jnp.float32)
    return (xf * lax.rsqrt(jnp.mean(jnp.square(xf), axis=-1, keepdims=True) + LN_EPS) * g.astype(jnp.float32)).astype(x.dtype)


def dsa_attention(q, ckv, q_idx, k_idx, w_idx, w_uk, w_uv):
    bsz, seq = ckv.shape[:2]
    topk = min(INDEX_TOPK, seq // 4)
    n_blk = seq // Q_BLOCK
    q_lat = jnp.einsum('bthd,chd->bthc', q, w_uk) * (HEAD_DIM ** -0.5)

    def to_blocks(a):
        return jnp.moveaxis(a.reshape((bsz, n_blk, Q_BLOCK) + a.shape[2:]), 1, 0)

    k_pos = jnp.arange(seq)
    gather = jax.vmap(lambda table, idx: table[idx])

    def block(args):
        i, qb, qib, wb = args
        q_pos = i * Q_BLOCK + jnp.arange(Q_BLOCK)
        rel = jax.nn.relu(jnp.einsum('bqhd,bsd->bqhs', qib, k_idx))
        score = jnp.einsum('bqhs,bqh->bqs', rel, wb).astype(jnp.float32)
        causal = k_pos[None, :] <= q_pos[:, None]
        score = jnp.where(causal[None], score, -jnp.inf)
        _, sel = lax.top_k(score, topk)
        valid = sel <= q_pos[None, :, None]
        kv = gather(ckv, sel)
        logits = jnp.einsum('bqhc,bqkc->bqhk', qb, kv).astype(jnp.float32)
        logits = jnp.where(valid[:, :, None, :], logits, -jnp.inf)
        p = jax.nn.softmax(logits, axis=-1).astype(kv.dtype)
        return jnp.einsum('bqhk,bqkc->bqhc', p, kv)

    o_lat = lax.map(block, (jnp.arange(n_blk), to_blocks(q_lat), to_blocks(q_idx), to_blocks(w_idx)))
    o_lat = jnp.moveaxis(o_lat, 0, 1).reshape(bsz, seq, N_HEADS, KV_LATENT)
    return jnp.einsum('bthc,chd->bthd', o_lat, w_uv).reshape(bsz, seq, N_HEADS * HEAD_DIM)


def short_conv(u, gate_b, gate_c, conv_w, conv_b):
    v = gate_c * u
    y = lax.conv_general_dilated(v, conv_w[:, None, :], window_strides=(1,), padding=[(CONV_K - 1, 0)],
                                 dimension_numbers=('NWC', 'WIO', 'NWC'), feature_group_count=CONV_WIDTH)
    return gate_b * (y + conv_b)


def s5_mixer(u, lam_re, lam_im, b_re, b_im, c_re, c_im, d_skip, log_dt, w_glu, b_glu):
    f32 = jnp.float32
    bsz, seq, _ = u.shape
    uf = u.astype(f32).reshape(bsz, seq, SSM_GROUPS, SSM_GROUP)
    lam = lax.complex(lam_re.astype(f32), lam_im.astype(f32))
    dt = jnp.exp(log_dt.astype(f32))[:, None]
    a_bar = jnp.exp(lam * dt)
    b_bar = ((a_bar - 1.0) / lam)[:, :, None] * lax.complex(b_re.astype(f32), b_im.astype(f32))
    bu = lax.complex(jnp.einsum('gpn,btgn->btgp', jnp.real(b_bar), uf),
                     jnp.einsum('gpn,btgn->btgp', jnp.imag(b_bar), uf))
    a = jnp.broadcast_to(a_bar, bu.shape)

    def combine(left, right):
        a_l, b_l = left
        a_r, b_r = right
        return a_r * a_l, a_r * b_l + b_r

    _, h = lax.associative_scan(combine, (a, bu), axis=1)
    y = (jnp.einsum('gnp,btgp->btgn', c_re.astype(f32), jnp.real(h))
         - jnp.einsum('gnp,btgp->btgn', c_im.astype(f32), jnp.imag(h))
         + d_skip.astype(f32) * uf)
    y = y.reshape(bsz, seq, SSM_WIDTH).astype(u.dtype)
    z = jax.nn.gelu(y)
    return z * jax.nn.sigmoid(z @ w_glu + b_glu)


def memory_attention(q, mem, w_mem_kv):
    bsz, seq = q.shape[:2]
    mlen = mem.shape[1]
    kv = (mem @ w_mem_kv).reshape(bsz, mlen, 2, MEM_HEADS, MEM_HEAD_DIM)
    k, v = kv[:, :, 0], kv[:, :, 1]
    logits = jnp.einsum('bthd,bmhd->bhtm', q, k).astype(jnp.float32) * (MEM_HEAD_DIM ** -0.5)
    p = jax.nn.softmax(logits, axis=-1).astype(v.dtype)
    return jnp.einsum('bhtm,bmhd->bthd', p, v).reshape(bsz, seq, MEM_HEADS * MEM_HEAD_DIM)


def moe(x, w_router, b_router, w_up, b_up, w_down, b_down):
    bsz, seq, d = x.shape
    n_tok = bsz * seq
    xf = x.reshape(n_tok, d)
    logits = (xf @ w_router + b_router).astype(jnp.float32)
    top_v, top_e = lax.top_k(logits, TOP_K)
    gates = jax.nn.softmax(top_v, axis=-1)
    n_asg = n_tok * TOP_K
    e_flat = top_e.reshape(n_asg).astype(jnp.int32)
    tok_flat = jnp.arange(n_asg, dtype=jnp.int32) // TOP_K
    g_flat = gates.reshape(n_asg)
    order = jnp.argsort(e_flat)
    e_sorted = e_flat[order]
    counts = jnp.bincount(e_flat, length=N_EXPERTS)
    starts = jnp.cumsum(counts) - counts
    nblk_per = (counts + EXPERT_BLOCK - 1) // EXPERT_BLOCK
    blk_end = jnp.cumsum(nblk_per)
    pstarts = (blk_end - nblk_per) * EXPERT_BLOCK
    dest = pstarts[e_sorted] + (jnp.arange(n_asg) - starts[e_sorted])
    n_blocks = -(-n_asg // EXPERT_BLOCK) + N_EXPERTS
    n_rows = n_blocks * EXPERT_BLOCK
    row_tok = jnp.full((n_rows,), n_tok, jnp.int32).at[dest].set(tok_flat[order])
    row_gate = jnp.zeros((n_rows,), jnp.float32).at[dest].set(g_flat[order])
    blk_expert = jnp.minimum(jnp.searchsorted(blk_end, jnp.arange(n_blocks), side='right'), N_EXPERTS - 1)
    x_pad = jnp.concatenate([xf, jnp.zeros((1, d), xf.dtype)], axis=0)
    xs = x_pad[row_tok].reshape(n_blocks, EXPERT_BLOCK, d)

    def expert_rows(args):
        xb, e = args
        hdn = xb @ w_up[e] + b_up[e]
        h_glu = jnp.minimum(hdn[..., ::2], SWIGLU_LIMIT)
        h_lin = jnp.clip(hdn[..., 1::2], -SWIGLU_LIMIT, SWIGLU_LIMIT)
        act = h_glu * jax.nn.sigmoid(SWIGLU_ALPHA * h_glu) * (h_lin + 1.0)
        return act @ w_down[e] + b_down[e]

    ys = lax.map(expert_rows, (xs, blk_expert)).reshape(n_rows, d)
    y = jax.ops.segment_sum(ys * row_gate[:, None].astype(ys.dtype), row_tok, num_segments=n_tok + 1)[:n_tok]
    return y.reshape(bsz, seq, d)


def setup_inputs(seed: int = 0) -> dict:
    key = jax.random.key(seed)
    ks = jax.random.split(key, 40)
    f32 = jnp.float32
    L, G, P, N = DEPTH, SSM_GROUPS, SSM_STATE, SSM_GROUP
    beta = DEEPNORM_BETA

    def nrm(i, shape, scale):
        return scale * jax.random.normal(ks[i], shape, f32)

    n_idx = jnp.arange(P, dtype=f32)
    return {
        'x': nrm(0, (BATCH, SEQ, D_MODEL), 1.0),
        'mem': nrm(1, (BATCH, MEM_LEN, D_MODEL), 1.0),
        'ln_in_g': 1.0 + nrm(2, (D_MODEL,), 0.02),
        'ln_in_b': nrm(3, (D_MODEL,), 0.02),
        'w_in': nrm(4, (L, D_MODEL, D_IN), D_MODEL ** -0.5),
        'kv_norm_g': 1.0 + nrm(5, (L, KV_LATENT), 0.02),
        'w_uk': nrm(6, (L, KV_LATENT, N_HEADS, HEAD_DIM), KV_LATENT ** -0.5),
        'w_uv': nrm(7, (L, KV_LATENT, N_HEADS, HEAD_DIM), KV_LATENT ** -0.5 * beta),
        'conv_w': nrm(8, (L, CONV_K, CONV_WIDTH), CONV_K ** -0.5),
        'conv_b': nrm(9, (L, CONV_WIDTH), 0.01),
        'lam_re': -0.5 + nrm(10, (L, G, P), 0.01),
        'lam_im': math.pi * n_idx + nrm(11, (L, G, P), 0.01),
        'b_re': nrm(12, (L, G, P, N), (2 * N) ** -0.5),
        'b_im': nrm(13, (L, G, P, N), (2 * N) ** -0.5),
        'c_re': nrm(14, (L, G, N, P), (2 * P) ** -0.5),
        'c_im': nrm(15, (L, G, N, P), (2 * P) ** -0.5),
        'd_skip': nrm(16, (L, G, N), 1.0),
        'log_dt': jax.random.uniform(ks[17], (L, G), f32, math.log(1e-3), math.log(1e-1)),
        'w_glu': nrm(18, (L, SSM_WIDTH, SSM_WIDTH), SSM_WIDTH ** -0.5),
        'b_glu': nrm(19, (L, SSM_WIDTH), 0.01),
        'w_mem_kv': nrm(20, (L, D_MODEL, 2 * MEM_HEADS * MEM_HEAD_DIM), D_MODEL ** -0.5),
        'w_branch': nrm(21, (L, N_BRANCH, BRANCH_WIDTH, D_MODEL), BRANCH_WIDTH ** -0.5),
        'w_o': nrm(22, (L, D_MODEL, D_MODEL), D_MODEL ** -0.5 * beta),
        'ln1_g': 1.0 + nrm(23, (L, D_MODEL), 0.02),
        'ln1_b': nrm(24, (L, D_MODEL), 0.02),
        'w_router': nrm(25, (L, D_MODEL, N_EXPERTS), D_MODEL ** -0.5),
        'b_router': nrm(26, (L, N_EXPERTS), 0.01),
        'w_up': nrm(27, (L, N_EXPERTS, D_MODEL, 2 * D_EXPERT), D_MODEL ** -0.5),
        'b_up': nrm(28, (L, N_EXPERTS, 2 * D_EXPERT), 0.01),
        'w_down': nrm(29, (L, N_EXPERTS, D_EXPERT, D_MODEL), D_EXPERT ** -0.5 * beta),
        'b_down': nrm(30, (L, N_EXPERTS, D_MODEL), 0.01),
        'ln2_g': 1.0 + nrm(31, (L, D_MODEL), 0.02),
        'ln2_b': nrm(32, (L, D_MODEL), 0.02),
    }


def reference(x, mem, ln_in_g, ln_in_b, w_in, kv_norm_g, w_uk, w_uv, conv_w, conv_b,
              lam_re, lam_im, b_re, b_im, c_re, c_im, d_skip, log_dt, w_glu, b_glu,
              w_mem_kv, w_branch, w_o, ln1_g, ln1_b, w_router, b_router, w_up, b_up,
              w_down, b_down, ln2_g, ln2_b):
    bsz, seq, _ = x.shape
    split_points = np.cumsum(SPLITS)[:-1].tolist()
    h = layer_norm(x, ln_in_g, ln_in_b)
    for l in range(DEPTH):
        proj = h @ w_in[l]
        (q, ckv, q_idx, k_idx, w_idx, conv_u, conv_gb, conv_gc,
         ssm_u, mem_q, gate_logits) = jnp.split(proj, split_points, axis=-1)
        att = dsa_attention(q.reshape(bsz, seq, N_HEADS, HEAD_DIM), rms_norm(ckv, kv_norm_g[l]),
                            q_idx.reshape(bsz, seq, IDX_HEADS, IDX_DIM), k_idx, w_idx, w_uk[l], w_uv[l])
        cnv = short_conv(conv_u, conv_gb, conv_gc, conv_w[l], conv_b[l])
        ssm = s5_mixer(ssm_u, lam_re[l], lam_im[l], b_re[l], b_im[l], c_re[l], c_im[l],
                       d_skip[l], log_dt[l], w_glu[l], b_glu[l])
        mem_o = memory_attention(mem_q.reshape(bsz, seq, MEM_HEADS, MEM_HEAD_DIM), mem, w_mem_kv[l])
        branches = jnp.stack([att, cnv, ssm, mem_o], axis=2)
        gates = jax.nn.sigmoid(gate_logits.reshape(bsz, seq, N_BRANCH, D_MODEL))
        merged = jnp.sum(jnp.einsum('btrc,rcd->btrd', branches, w_branch[l]) * gates, axis=2)
        h = layer_norm(DEEPNORM_ALPHA * h + merged @ w_o[l], ln1_g[l], ln1_b[l])
        ffn = moe(h, w_router[l], b_router[l], w_up[l], b_up[l], w_down[l], b_down[l])
        h = layer_norm(DEEPNORM_ALPHA * h + ffn, ln2_g[l], ln2_b[l])
    return h
```

```python
import functools
import math

import numpy as np
import jax
import jax.numpy as jnp
from jax import lax
from jax.experimental import pallas as pl
from jax.experimental.pallas import tpu as pltpu

F32 = jnp.float32
BF16 = jnp.bfloat16
I32 = jnp.int32

N_HEADS = 8
HEAD_DIM = 64
KV_LATENT = 128
IDX_HEADS = 8
IDX_DIM = 32
INDEX_TOPK = 256
Q_BLOCK = 128
CONV_WIDTH = 512
SSM_WIDTH = 512
SSM_GROUP = 16
SSM_GROUPS = SSM_WIDTH // SSM_GROUP
SSM_STATE = 64
MEM_HEADS = 4
MEM_HEAD_DIM = 128
N_BRANCH = 4
BRANCH_WIDTH = 512
N_EXPERTS = 32
TOP_K = 4
D_EXPERT = 1024
SWIGLU_LIMIT = 7.0
SWIGLU_ALPHA = 1.702
EXPERT_BLOCK = 256
LN_EPS = 1e-5

LANES = 128
SUBLANES = 8
VMEM_LIMIT_BYTES = 56 * 1024 * 1024

INT_MIN = -(2 ** 31)
NEG_BIG = -1e30

COL_GATES = 0
COL_Q = 4096
COL_CONV_U = 4608
COL_CONV_GB = 5120
COL_CONV_GC = 5632
COL_SSM = 6144
COL_MEMQ = 6656
COL_QIDX = 7168
COL_CKV = 7424
COL_TAIL = 7552
D_PROJ = 7680
TAIL_W_OFF = IDX_DIM

SSM_SEQS = 8


def _cparams(sem):
    return pltpu.CompilerParams(dimension_semantics=sem, vmem_limit_bytes=VMEM_LIMIT_BYTES)


def _layer_norm(x, g, b):
    mu = jnp.mean(x, axis=-1, keepdims=True)
    xc = x - mu
    var = jnp.mean(xc * xc, axis=-1, keepdims=True)
    return xc * lax.rsqrt(var + LN_EPS) * g + b


def _const_spec(shape):
    nd = len(shape)
    return pl.BlockSpec(shape, lambda *_: (0,) * nd)


def _ln_kernel(x_ref, g_ref, b_ref, o_ref):
    o_ref[...] = _layer_norm(x_ref[...], g_ref[...], b_ref[...])


def _ln_in(x2, g, b, tm=512):
    n, d = x2.shape
    return pl.pallas_call(
        _ln_kernel,
        out_shape=jax.ShapeDtypeStruct((n, d), F32),
        grid=(n // tm,),
        in_specs=[pl.BlockSpec((tm, d), lambda i: (i, 0)), _const_spec((1, d)), _const_spec((1, d))],
        out_specs=pl.BlockSpec((tm, d), lambda i: (i, 0)),
        compiler_params=_cparams(("parallel",)),
        name="ln_in",
    )(x2, g.reshape(1, d), b.reshape(1, d))


def _matmul_kernel(x_ref, w_ref, o_ref):
    o_ref[...] = jnp.dot(x_ref[...].astype(BF16), w_ref[...], preferred_element_type=F32)


def _matmul(x, w_bf16, tm, name):
    n, k = x.shape
    m = w_bf16.shape[1]
    return pl.pallas_call(
        _matmul_kernel,
        out_shape=jax.ShapeDtypeStruct((n, m), F32),
        grid=(n // tm,),
        in_specs=[pl.BlockSpec((tm, k), lambda i: (i, 0)),
                  pl.BlockSpec((k, m), lambda i: (0, 0), pipeline_mode=pl.Buffered(1))],
        out_specs=pl.BlockSpec((tm, m), lambda i: (i, 0)),
        compiler_params=_cparams(("parallel",)),
        name=name,
    )(x, w_bf16)


def _split_hi_lo(x):
    hi = x.astype(BF16)
    lo = (x - hi.astype(F32)).astype(BF16)
    return hi, lo


def _prep_kernel(qidx_ref, ckv_ref, tail_ref, g_ref, sq_ref, sk_ref, qcat_ref, kcat_ref, ckvn_ref):
    q_hi, q_lo = _split_hi_lo(qidx_ref[...])
    qcat = jnp.dot(jnp.concatenate([q_hi, q_lo], axis=1), sq_ref[...], preferred_element_type=F32)
    qcat_ref[...] = qcat.astype(BF16)
    k_hi, k_lo = _split_hi_lo(tail_ref[...])
    kcat = jnp.dot(jnp.concatenate([k_hi, k_lo], axis=1), sk_ref[...], preferred_element_type=F32)
    kcat_ref[...] = kcat.astype(BF16)
    c = ckv_ref[...]
    ms = jnp.mean(c * c, axis=-1, keepdims=True)
    ckvn_ref[...] = (c * lax.rsqrt(ms + LN_EPS) * g_ref[...]).astype(BF16)


def _selection_matrices():
    sq = np.zeros((2 * IDX_HEADS * IDX_DIM, IDX_HEADS * LANES), np.float32)
    for h in range(IDX_HEADS):
        for d in range(IDX_DIM):
            hi_in = h * IDX_DIM + d
            lo_in = IDX_HEADS * IDX_DIM + hi_in
            sq[hi_in, h * LANES + d] = 1.0
            sq[hi_in, h * LANES + IDX_DIM + d] = 1.0
            sq[lo_in, h * LANES + 2 * IDX_DIM + d] = 1.0
    sk = np.zeros((2 * LANES, LANES), np.float32)
    for d in range(IDX_DIM):
        sk[d, d] = 1.0
        sk[LANES + d, IDX_DIM + d] = 1.0
        sk[d, 2 * IDX_DIM + d] = 1.0
    return jnp.asarray(sq, BF16), jnp.asarray(sk, BF16)


def _prep(proj, kv_norm_g, tm=512):
    n = proj.shape[0]
    sq, sk = _selection_matrices()
    return pl.pallas_call(
        _prep_kernel,
        out_shape=(jax.ShapeDtypeStruct((n, IDX_HEADS * LANES), BF16),
                   jax.ShapeDtypeStruct((n, LANES), BF16),
                   jax.ShapeDtypeStruct((n, KV_LATENT), BF16)),
        grid=(n // tm,),
        in_specs=[pl.BlockSpec((tm, 256), lambda i: (i, COL_QIDX // 256)),
                  pl.BlockSpec((tm, 128), lambda i: (i, COL_CKV // 128)),
                  pl.BlockSpec((tm, 128), lambda i: (i, COL_TAIL // 128)),
                  _const_spec((1, KV_LATENT)), _const_spec(sq.shape), _const_spec(sk.shape)],
        out_specs=(pl.BlockSpec((tm, IDX_HEADS * LANES), lambda i: (i, 0)),
                   pl.BlockSpec((tm, LANES), lambda i: (i, 0)),
                   pl.BlockSpec((tm, KV_LATENT), lambda i: (i, 0))),
        compiler_params=_cparams(("parallel",)),
        name="dsa_prep",
    )(proj, proj, proj, kv_norm_g.reshape(1, KV_LATENT), sq, sk)


def _dsa_kernel(q_ref, tail_ref, qcat_ref, kcat_ref, ckv_ref, wuk_ref, wuv_ref, o_ref,
                key_ref, qlat_ref, m_ref, l_ref, acc_ref, *, seq, topk, ck):
    i = pl.program_id(1)
    q0 = i * Q_BLOCK
    nch = (q0 + Q_BLOCK + ck - 1) // ck
    n_sub = ck // LANES
    q_pos = q0 + lax.broadcasted_iota(I32, (Q_BLOCK, 1), 0)
    lane_pos = lax.broadcasted_iota(I32, (1, ck), 1)
    tail = tail_ref[...]

    def score_chunk(c, carry):
        k0 = pl.multiple_of(c * ck, ck)
        kc = kcat_ref[pl.ds(k0, ck), :]
        s = jnp.zeros((Q_BLOCK, ck), F32)
        for h in range(IDX_HEADS):
            d = lax.dot_general(qcat_ref[:, h * LANES:(h + 1) * LANES], kc,
                                (((1,), (1,)), ((), ())), preferred_element_type=F32)
            s = s + jnp.maximum(d, 0.0) * tail[:, TAIL_W_OFF + h:TAIL_W_OFF + h + 1]
        s = s + 0.0
        bits = pltpu.bitcast(s, I32)
        key = bits ^ ((bits >> 31) & jnp.int32(0x7FFFFFFF))
        key = jnp.where(k0 + lane_pos <= q_pos, key, jnp.int32(INT_MIN))
        key_ref[:, pl.ds(k0, ck)] = key
        return carry

    lax.fori_loop(0, nch, score_chunk, 0)

    def count_rows(pred_fn):
        def body(c, acc):
            k0 = pl.multiple_of(c * ck, ck)
            hit = jnp.where(pred_fn(key_ref[:, pl.ds(k0, ck)], k0), 1.0, 0.0)
            for j in range(n_sub):
                acc = acc + hit[:, j * LANES:(j + 1) * LANES]
            return acc
        acc = lax.fori_loop(0, nch, body, jnp.zeros((Q_BLOCK, LANES), F32))
        return jnp.sum(acc, axis=1, keepdims=True)

    kf = float(topk)

    def bit_pass(b, v):
        cand = v + jnp.left_shift(jnp.int32(1), 31 - b)
        cnt = count_rows(lambda key, k0: key >= cand)
        return jnp.where(cnt >= kf, cand, v)

    v = lax.fori_loop(0, 32, bit_pass, jnp.full((Q_BLOCK, 1), INT_MIN, I32))
    thr = jnp.maximum(v, jnp.int32(INT_MIN + 1))
    c_ge = count_rows(lambda key, k0: key >= thr)
    c_gt = count_rows(lambda key, k0: key > thr)
    need = kf - c_gt

    jlim_ref = m_ref.at[0]
    jlim_ref[...] = jnp.full((Q_BLOCK, 1), float(seq), F32)
    has_tie = jnp.max(jnp.where(c_ge > kf, 1.0, 0.0)) > 0.0

    @pl.when(has_tie)
    def _():
        n_bits = max(1, int(math.ceil(math.log2(seq))))

        def pos_pass(b, j):
            cand = j + jnp.left_shift(jnp.int32(1), n_bits - 1 - b)
            cnt = count_rows(lambda key, k0: (key == thr) & (k0 + lane_pos < cand))
            return jnp.where(cnt < need, cand, j)

        j = lax.fori_loop(0, n_bits, pos_pass, jnp.zeros((Q_BLOCK, 1), I32))
        jlim_ref[...] = jnp.where(c_ge > kf, j.astype(F32), float(seq))

    jlim = jlim_ref[...].astype(I32)

    qb = q_ref[...].astype(BF16)
    for h in range(N_HEADS):
        ql = jnp.dot(qb, wuk_ref[h], preferred_element_type=F32) * (HEAD_DIM ** -0.5)
        qlat_ref[h * Q_BLOCK:(h + 1) * Q_BLOCK, :] = ql.astype(BF16)

    m_ref[...] = jnp.full(m_ref.shape, NEG_BIG, F32)
    l_ref[...] = jnp.zeros(l_ref.shape, F32)
    acc_ref[...] = jnp.zeros(acc_ref.shape, F32)

    def att_chunk(c, carry):
        k0 = pl.multiple_of(c * ck, ck)
        kv = ckv_ref[pl.ds(k0, ck), :]
        key = key_ref[:, pl.ds(k0, ck)]
        sel = (key > thr) | ((key == thr) & (k0 + lane_pos <= jlim))
        for h in range(N_HEADS):
            lg = lax.dot_general(qlat_ref[h * Q_BLOCK:(h + 1) * Q_BLOCK, :], kv,
                                 (((1,), (1,)), ((), ())), preferred_element_type=F32)
            m_prev = m_ref[h]
            m_new = jnp.maximum(m_prev, jnp.max(jnp.where(sel, lg, NEG_BIG), axis=1, keepdims=True))
            p = jnp.where(sel, jnp.exp(lg - m_new), 0.0)
            alpha = jnp.exp(m_prev - m_new)
            l_ref[h] = alpha * l_ref[h] + jnp.sum(p, axis=1, keepdims=True)
            acc_ref[h] = alpha * acc_ref[h] + jnp.dot(p.astype(BF16), kv, preferred_element_type=F32)
            m_ref[h] = m_new
        return carry

    lax.fori_loop(0, nch, att_chunk, 0)

    out = jnp.zeros((Q_BLOCK, N_HEADS * HEAD_DIM), F32)
    for h in range(N_HEADS):
        o_lat = acc_ref[h] / l_ref[h]
        out = out + jnp.dot(o_lat.astype(BF16), wuv_ref[h], preferred_element_type=F32)
    o_ref[...] = out


def _dsa_attention(proj, qcat, kcat, ckvn, w_uk, w_uv, bsz, seq):
    n = proj.shape[0]
    n_blk = seq // Q_BLOCK
    topk = min(INDEX_TOPK, seq // 4)
    ck = min(512, seq)
    eye = jnp.eye(N_HEADS, dtype=F32)
    wuk = jnp.einsum('chd,hg->hgdc', w_uk, eye).reshape(N_HEADS, N_HEADS * HEAD_DIM, KV_LATENT).astype(BF16)
    wuv = jnp.einsum('chd,hg->hcgd', w_uv, eye).reshape(N_HEADS, KV_LATENT, N_HEADS * HEAD_DIM).astype(BF16)
    kernel = functools.partial(_dsa_kernel, seq=seq, topk=topk, ck=ck)
    return pl.pallas_call(
        kernel,
        out_shape=jax.ShapeDtypeStruct((n, N_HEADS * HEAD_DIM), F32),
        grid=(bsz, n_blk),
        in_specs=[pl.BlockSpec((Q_BLOCK, 512), lambda b, i: (b * n_blk + i, COL_Q // 512)),
                  pl.BlockSpec((Q_BLOCK, 128), lambda b, i: (b * n_blk + i, COL_TAIL // 128)),
                  pl.BlockSpec((Q_BLOCK, IDX_HEADS * LANES), lambda b, i: (b * n_blk + i, 0)),
                  pl.BlockSpec((seq, LANES), lambda b, i: (b, 0)),
                  pl.BlockSpec((seq, KV_LATENT), lambda b, i: (b, 0)),
                  _const_spec(wuk.shape), _const_spec(wuv.shape)],
        out_specs=pl.BlockSpec((Q_BLOCK, N_HEADS * HEAD_DIM), lambda b, i: (b * n_blk + i, 0)),
        scratch_shapes=[pltpu.VMEM((Q_BLOCK, seq), I32),
                        pltpu.VMEM((N_HEADS * Q_BLOCK, KV_LATENT), BF16),
                        pltpu.VMEM((N_HEADS, Q_BLOCK, 1), F32),
                        pltpu.VMEM((N_HEADS, Q_BLOCK, 1), F32),
                        pltpu.VMEM((N_HEADS, Q_BLOCK, KV_LATENT), F32)],
        compiler_params=_cparams(("parallel", "parallel")),
        name="dsa_attention",
    )(proj, proj, qcat, kcat, ckvn, wuk, wuv)


def _conv_kernel(u_ref, gb_ref, gc_ref, pu_ref, pgc_ref, w_ref, b_ref, o_ref):
    i = pl.program_id(1)
    tb = u_ref.shape[0]
    v = gc_ref[...] * u_ref[...]
    halo = jnp.where(i > 0, pgc_ref[...] * pu_ref[...], 0.0)
    vfull = jnp.concatenate([halo, v], axis=0)
    v1 = pltpu.roll(vfull, 1, 0)[SUBLANES:SUBLANES + tb]
    v2 = pltpu.roll(vfull, 2, 0)[SUBLANES:SUBLANES + tb]
    w = w_ref[...]
    y = w[0:1] * v2 + w[1:2] * v1 + w[2:3] * v
    o_ref[...] = gb_ref[...] * (y + b_ref[...])


def _short_conv(proj, conv_w, conv_b, bsz, seq, tb=512):
    n = proj.shape[0]
    tb = min(tb, seq)
    nb = seq // tb
    cw = CONV_WIDTH
    blk = lambda col: pl.BlockSpec((tb, cw), lambda b, i: (b * nb + i, col // cw))
    prev = lambda col: pl.BlockSpec(
        (SUBLANES, cw), lambda b, i: (jnp.maximum((b * nb + i) * (tb // SUBLANES) - 1, 0), col // cw))
    return pl.pallas_call(
        _conv_kernel,
        out_shape=jax.ShapeDtypeStruct((n, cw), F32),
        grid=(bsz, nb),
        in_specs=[blk(COL_CONV_U), blk(COL_CONV_GB), blk(COL_CONV_GC), prev(COL_CONV_U), prev(COL_CONV_GC),
                  _const_spec((SUBLANES, cw)), _const_spec((1, cw))],
        out_specs=pl.BlockSpec((tb, cw), lambda b, i: (b * nb + i, 0)),
        compiler_params=_cparams(("parallel", "parallel")),
        name="short_conv",
    )(proj, proj, proj, proj, proj,
      jnp.pad(conv_w, ((0, SUBLANES - conv_w.shape[0]), (0, 0))), conv_b.reshape(1, cw))


def _s5_scan_chunk(bu_ref, a_ref, st_ref, n_steps, store):
    s_tot = a_ref.shape[1] // 2
    cb = 512
    for blk in range(s_tot // cb):
        re_sl = pl.ds(blk * cb, cb)
        im_sl = pl.ds(s_tot + blk * cb, cb)
        ar = jnp.broadcast_to(a_ref[0:1, blk * cb:(blk + 1) * cb], (SSM_SEQS, cb))
        ai = jnp.broadcast_to(a_ref[0:1, s_tot + blk * cb:s_tot + (blk + 1) * cb], (SSM_SEQS, cb))

        def step(t, carry):
            re, im = carry
            r0 = pl.multiple_of(t * SSM_SEQS, SSM_SEQS)
            br = bu_ref[pl.ds(r0, SSM_SEQS), re_sl]
            bi = bu_ref[pl.ds(r0, SSM_SEQS), im_sl]
            nre = ar * re - ai * im + br
            nim = ar * im + ai * re + bi
            if store:
                bu_ref[pl.ds(r0, SSM_SEQS), re_sl] = nre
                bu_ref[pl.ds(r0, SSM_SEQS), im_sl] = nim
            return nre, nim

        re, im = lax.fori_loop(0, n_steps, step, (st_ref[:, re_sl], st_ref[:, im_sl]), unroll=8)
        st_ref[:, re_sl] = re
        st_ref[:, im_sl] = im


def _s5_ends_kernel(u_ref, wb_ref, a_ref, e_ref, bu_ref, st_ref, *, n_steps):
    c = pl.program_id(0)

    @pl.when(c == 0)
    def _():
        st_ref[...] = jnp.zeros(st_ref.shape, F32)

    bu_ref[...] = jnp.dot(u_ref[...].astype(BF16), wb_ref[...], preferred_element_type=F32)
    _s5_scan_chunk(bu_ref, a_ref, st_ref, n_steps, store=False)

    @pl.when(c == pl.num_programs(0) - 1)
    def _():
        e_ref[...] = st_ref[...]


def _gelu_tanh(x):
    return 0.5 * x * (1.0 + jnp.tanh(math.sqrt(2.0 / math.pi) * (x + 0.044715 * (x * x * x))))


def _s5_main_kernel(u_ref, wb_ref, a_ref, ends_ref, apow_ref, wc_ref, d_ref, wg_ref, bg_ref, o_ref,
                    bu_ref, st_ref, *, n_steps, segs):
    c = pl.program_id(0)
    s_tot = a_ref.shape[1] // 2

    @pl.when(c == 0)
    def _():
        pr = apow_ref[0:1, :s_tot]
        pi = apow_ref[0:1, s_tot:]
        n_batch = SSM_SEQS // segs
        for b in range(n_batch):
            hre = jnp.zeros((1, s_tot), F32)
            him = jnp.zeros((1, s_tot), F32)
            for k in range(segs):
                row = b * segs + k
                st_ref[row:row + 1, :s_tot] = hre
                st_ref[row:row + 1, s_tot:] = him
                ere = ends_ref[row:row + 1, :s_tot]
                eim = ends_ref[row:row + 1, s_tot:]
                hre, him = pr * hre - pi * him + ere, pr * him + pi * hre + eim

    u = u_ref[...]
    bu_ref[...] = jnp.dot(u.astype(BF16), wb_ref[...], preferred_element_type=F32)
    _s5_scan_chunk(bu_ref, a_ref, st_ref, n_steps, store=True)
    y = jnp.dot(bu_ref[...].astype(BF16), wc_ref[...], preferred_element_type=F32) + d_ref[...] * u
    z = _gelu_tanh(y)
    gate = jnp.dot(z.astype(BF16), wg_ref[...], preferred_element_type=F32) + bg_ref[...]
    o_ref[...] = z * jax.nn.sigmoid(gate)


def _s5_mixer(proj, lam_re, lam_im, b_re, b_im, c_re, c_im, d_skip, log_dt, w_glu, b_glu, bsz, seq):
    n = proj.shape[0]
    g, p, nn = SSM_GROUPS, SSM_STATE, SSM_GROUP
    s_tot = g * p
    segs = SSM_SEQS // bsz
    tseg = seq // segs
    ch = min(64, tseg)
    rows = ch * SSM_SEQS
    n_chunks = tseg // ch

    dt = jnp.exp(log_dt.astype(F32))[:, None]
    lam = lax.complex(lam_re.astype(F32), lam_im.astype(F32))
    a_bar = jnp.exp(lam * dt)
    b_bar = ((a_bar - 1.0) / lam)[:, :, None] * lax.complex(b_re.astype(F32), b_im.astype(F32))
    a_pow = jnp.exp(lam * dt * tseg)
    a_vec = jnp.concatenate([jnp.real(a_bar).reshape(1, s_tot), jnp.imag(a_bar).reshape(1, s_tot)], axis=1)
    apow_vec = jnp.concatenate([jnp.real(a_pow).reshape(1, s_tot), jnp.imag(a_pow).reshape(1, s_tot)], axis=1)
    eye = jnp.eye(g, dtype=F32)
    wb = jnp.concatenate(
        [jnp.einsum('gpn,gh->gnhp', jnp.real(b_bar), eye).reshape(g * nn, s_tot),
         jnp.einsum('gpn,gh->gnhp', jnp.imag(b_bar), eye).reshape(g * nn, s_tot)], axis=1).astype(BF16)
    wc = jnp.concatenate(
        [jnp.einsum('gnp,gh->gphn', c_re.astype(F32), eye).reshape(s_tot, g * nn),
         -jnp.einsum('gnp,gh->gphn', c_im.astype(F32), eye).reshape(s_tot, g * nn)], axis=0).astype(BF16)

    u = proj[:, COL_SSM:COL_SSM + SSM_WIDTH].reshape(bsz, segs, tseg, SSM_WIDTH)
    u_perm = jnp.transpose(u, (2, 0, 1, 3)).reshape(n, SSM_WIDTH)

    ends = pl.pallas_call(
        functools.partial(_s5_ends_kernel, n_steps=ch),
        out_shape=jax.ShapeDtypeStruct((SSM_SEQS, 2 * s_tot), F32),
        grid=(n_chunks,),
        in_specs=[pl.BlockSpec((rows, SSM_WIDTH), lambda c: (c, 0)),
                  _const_spec(wb.shape), _const_spec(a_vec.shape)],
        out_specs=_const_spec((SSM_SEQS, 2 * s_tot)),
        scratch_shapes=[pltpu.VMEM((rows, 2 * s_tot), F32), pltpu.VMEM((SSM_SEQS, 2 * s_tot), F32)],
        compiler_params=_cparams(("arbitrary",)),
        name="s5_segment_ends",
    )(u_perm, wb, a_vec)

    out_perm = pl.pallas_call(
        functools.partial(_s5_main_kernel, n_steps=ch, segs=segs),
        out_shape=jax.ShapeDtypeStruct((n, SSM_WIDTH), F32),
        grid=(n_chunks,),
        in_specs=[pl.BlockSpec((rows, SSM_WIDTH), lambda c: (c, 0)),
                  _const_spec(wb.shape), _const_spec(a_vec.shape),
                  _const_spec((SSM_SEQS, 2 * s_tot)), _const_spec(apow_vec.shape),
                  _const_spec(wc.shape), _const_spec((1, SSM_WIDTH)),
                  _const_spec((SSM_WIDTH, SSM_WIDTH)), _const_spec((1, SSM_WIDTH))],
        out_specs=pl.BlockSpec((rows, SSM_WIDTH), lambda c: (c, 0)),
        scratch_shapes=[pltpu.VMEM((rows, 2 * s_tot), F32), pltpu.VMEM((SSM_SEQS, 2 * s_tot), F32)],
        compiler_params=_cparams(("arbitrary",)),
        name="s5_scan_glu",
    )(u_perm, wb, a_vec, ends, apow_vec, wc, d_skip.reshape(1, SSM_WIDTH).astype(F32),
      w_glu.astype(BF16), b_glu.reshape(1, SSM_WIDTH))

    out = out_perm.reshape(tseg, bsz, segs, SSM_WIDTH)
    return jnp.transpose(out, (1, 2, 0, 3)).reshape(n, SSM_WIDTH)


def _mem_attn_kernel(q_ref, kv_ref, o_ref):
    hw = MEM_HEADS * MEM_HEAD_DIM
    q = q_ref[...].astype(BF16)
    kv = kv_ref[...].astype(BF16)
    for h in range(MEM_HEADS):
        sl = slice(h * MEM_HEAD_DIM, (h + 1) * MEM_HEAD_DIM)
        k = kv[:, sl]
        v = kv[:, hw + h * MEM_HEAD_DIM:hw + (h + 1) * MEM_HEAD_DIM]
        lg = lax.dot_general(q[:, sl], k, (((1,), (1,)), ((), ())),
                             preferred_element_type=F32) * (MEM_HEAD_DIM ** -0.5)
        m = jnp.max(lg, axis=1, keepdims=True)
        p = jnp.exp(lg - m)
        p = p / jnp.sum(p, axis=1, keepdims=True)
        o_ref[:, sl] = jnp.dot(p.astype(BF16), v, preferred_element_type=F32)


def _memory_attention(proj, kvm, bsz, seq, mlen, tm=512):
    n = proj.shape[0]
    tm = min(tm, seq)
    nb = seq // tm
    hw = MEM_HEADS * MEM_HEAD_DIM
    return pl.pallas_call(
        _mem_attn_kernel,
        out_shape=jax.ShapeDtypeStruct((n, hw), F32),
        grid=(bsz, nb),
        in_specs=[pl.BlockSpec((tm, hw), lambda b, i: (b * nb + i, COL_MEMQ // hw)),
                  pl.BlockSpec((mlen, 2 * hw), lambda b, i: (b, 0))],
        out_specs=pl.BlockSpec((tm, hw), lambda b, i: (b * nb + i, 0)),
        compiler_params=_cparams(("parallel", "parallel")),
        name="memory_attention",
    )(proj, kvm)


def _merge_kernel(att_ref, cnv_ref, ssm_ref, mem_ref, gl_ref, h_ref, wbr_ref, wo_ref, g_ref, b_ref,
                  wr_ref, br_ref, h1_ref, te_ref, tg_ref, *, alpha):
    d = h_ref.shape[1]
    merged = jnp.zeros(h_ref.shape, F32)
    for r, br in enumerate((att_ref, cnv_ref, ssm_ref, mem_ref)):
        y = jnp.dot(br[...].astype(BF16), wbr_ref[r], preferred_element_type=F32)
        merged = merged + y * jax.nn.sigmoid(gl_ref[:, r * d:(r + 1) * d])
    y = alpha * h_ref[...] + jnp.dot(merged.astype(BF16), wo_ref[...], preferred_element_type=F32)
    h1 = _layer_norm(y, g_ref[...], b_ref[...])
    h1_ref[...] = h1

    logits = jnp.dot(h1, wr_ref[...], preferred_element_type=F32, precision=lax.Precision.HIGHEST) + br_ref[...]
    tm = logits.shape[0]
    lane = lax.broadcasted_iota(I32, (tm, N_EXPERTS), 1)
    out_lane = lax.broadcasted_iota(I32, (tm, LANES), 1)
    work = logits
    top_e = jnp.zeros((tm, LANES), I32)
    top_v = jnp.zeros((tm, LANES), F32)
    vals = []
    for k in range(TOP_K):
        mx = jnp.max(work, axis=1, keepdims=True)
        idx = jnp.min(jnp.where(work == mx, lane, N_EXPERTS), axis=1, keepdims=True)
        work = jnp.where(lane == idx, -jnp.inf, work)
        top_e = jnp.where(out_lane == k, idx, top_e)
        vals.append(mx)
    den = sum(jnp.exp(vk - vals[0]) for vk in vals)
    for k in range(TOP_K):
        top_v = jnp.where(out_lane == k, jnp.exp(vals[k] - vals[0]) / den, top_v)
    te_ref[...] = top_e
    tg_ref[...] = top_v


def _merge_router(att, cnv, ssm, mem_o, proj, h, w_branch, w_o, ln_g, ln_b, w_router, b_router, alpha, tm=256):
    n, d = h.shape
    bw = BRANCH_WIDTH
    row = lambda w: pl.BlockSpec((tm, w), lambda i: (i, 0))
    return pl.pallas_call(
        functools.partial(_merge_kernel, alpha=alpha),
        out_shape=(jax.ShapeDtypeStruct((n, d), F32),
                   jax.ShapeDtypeStruct((n, LANES), I32),
                   jax.ShapeDtypeStruct((n, LANES), F32)),
        grid=(n // tm,),
        in_specs=[row(bw), row(bw), row(bw), row(bw),
                  pl.BlockSpec((tm, N_BRANCH * d), lambda i: (i, COL_GATES // (N_BRANCH * d))),
                  row(d),
                  _const_spec((N_BRANCH, bw, d)), _const_spec((d, d)), _const_spec((1, d)), _const_spec((1, d)),
                  _const_spec((d, N_EXPERTS)), _const_spec((1, N_EXPERTS))],
        out_specs=(row(d), row(LANES), row(LANES)),
        compiler_params=_cparams(("parallel",)),
        name="merge_router",
    )(att, cnv, ssm, mem_o, proj, h, w_branch.astype(BF16), w_o.astype(BF16),
      ln_g.reshape(1, d), ln_b.reshape(1, d), w_router, b_router.reshape(1, N_EXPERTS))


def _expert_kernel(meta_ref, cur_ref, nxt_ref, x_hbm, wg_ref, wl_ref, bg_ref, bl_ref, wd_ref, bd_ref, o_ref,
                   xbuf, sem):
    j = pl.program_id(0)
    nblk = pl.num_programs(0)
    n_used = meta_ref[0]
    slot = j % 2

    def gather(idx_ref, s):
        def body(r, carry):
            pltpu.make_async_copy(x_hbm.at[pl.ds(idx_ref[0, r], 1), :],
                                  xbuf.at[s, pl.ds(r, 1), :], sem.at[s]).start()
            return carry
        lax.fori_loop(0, EXPERT_BLOCK, body, 0)

    @pl.when((j == 0) & (n_used > 0))
    def _():
        gather(cur_ref, 0)

    @pl.when(j + 1 < jnp.minimum(n_used, nblk))
    def _():
        gather(nxt_ref, 1 - slot)

    @pl.when(j < n_used)
    def _():
        pltpu.make_async_copy(x_hbm.at[pl.ds(0, EXPERT_BLOCK), :], xbuf.at[slot], sem.at[slot]).wait()
        xb = xbuf[slot].astype(BF16)
        h_glu = jnp.dot(xb, wg_ref[0], preferred_element_type=F32) + bg_ref[0]
        h_lin = jnp.dot(xb, wl_ref[0], preferred_element_type=F32) + bl_ref[0]
        h_glu = jnp.minimum(h_glu, SWIGLU_LIMIT)
        h_lin = jnp.clip(h_lin, -SWIGLU_LIMIT, SWIGLU_LIMIT)
        act = h_glu * jax.nn.sigmoid(SWIGLU_ALPHA * h_glu) * (h_lin + 1.0)
        o_ref[...] = jnp.dot(act.astype(BF16), wd_ref[0], preferred_element_type=F32) + bd_ref[0]

    @pl.when(j >= n_used)
    def _():
        o_ref[...] = jnp.zeros(o_ref.shape, F32)


def _expert_ffn(h1, row_src, blk_expert, n_used, w_up_glu, w_up_lin, b_glu, b_lin, w_down, b_down):
    n, d = h1.shape
    n_blocks = row_src.shape[0]
    f = D_EXPERT
    meta = jnp.concatenate([n_used.reshape(1).astype(I32), blk_expert.astype(I32)])
    idx3 = row_src.reshape(n_blocks, 1, EXPERT_BLOCK)
    e_of = lambda j, m: m[1 + j]
    grid_spec = pltpu.PrefetchScalarGridSpec(
        num_scalar_prefetch=1,
        grid=(n_blocks,),
        in_specs=[
            pl.BlockSpec((None, 1, EXPERT_BLOCK), lambda j, m: (j, 0, 0), memory_space=pltpu.SMEM),
            pl.BlockSpec((None, 1, EXPERT_BLOCK), lambda j, m: (jnp.minimum(j + 1, n_blocks - 1), 0, 0),
                         memory_space=pltpu.SMEM),
            pl.BlockSpec(memory_space=pl.ANY),
            pl.BlockSpec((1, d, f), lambda j, m: (e_of(j, m), 0, 0)),
            pl.BlockSpec((1, d, f), lambda j, m: (e_of(j, m), 0, 0)),
            pl.BlockSpec((1, 1, f), lambda j, m: (e_of(j, m), 0, 0)),
            pl.BlockSpec((1, 1, f), lambda j, m: (e_of(j, m), 0, 0)),
            pl.BlockSpec((1, f, d), lambda j, m: (e_of(j, m), 0, 0)),
            pl.BlockSpec((1, 1, d), lambda j, m: (e_of(j, m), 0, 0)),
        ],
        out_specs=pl.BlockSpec((EXPERT_BLOCK, d), lambda j, m: (j, 0)),
        scratch_shapes=[pltpu.VMEM((2, EXPERT_BLOCK, d), F32), pltpu.SemaphoreType.DMA((2,))],
    )
    return pl.pallas_call(
        _expert_kernel,
        out_shape=jax.ShapeDtypeStruct((n_blocks * EXPERT_BLOCK, d), F32),
        grid_spec=grid_spec,
        compiler_params=_cparams(("arbitrary",)),
        name="moe_experts",
    )(meta, idx3, idx3, h1, w_up_glu, w_up_lin, b_glu, b_lin, w_down, b_down)


def _combine_kernel(cur_ref, nxt_ref, ys_hbm, g4_ref, h_ref, lg_ref, lb_ref, o_ref, ybuf, sem, *, alpha, tm):
    j = pl.program_id(0)
    nblk = pl.num_programs(0)
    slot = j % 2
    n_rows = tm * TOP_K

    def gather(idx_ref, s):
        def body(r, carry):
            pltpu.make_async_copy(ys_hbm.at[pl.ds(idx_ref[0, r], 1), :],
                                  ybuf.at[s, pl.ds(r, 1), :], sem.at[s]).start()
            return carry
        lax.fori_loop(0, n_rows, body, 0)

    @pl.when(j == 0)
    def _():
        gather(cur_ref, 0)

    @pl.when(j + 1 < nblk)
    def _():
        gather(nxt_ref, 1 - slot)

    pltpu.make_async_copy(ys_hbm.at[pl.ds(0, n_rows), :], ybuf.at[slot], sem.at[slot]).wait()
    g4 = g4_ref[...]
    ffn = jnp.zeros((tm, h_ref.shape[1]), F32)
    for k in range(TOP_K):
        ffn = ffn + ybuf[slot, k * tm:(k + 1) * tm, :] * g4[:, k:k + 1]
    o_ref[...] = _layer_norm(alpha * h_ref[...] + ffn, lg_ref[...], lb_ref[...])


def _combine(ys, pos_km, gates_pad, h1, ln_g, ln_b, alpha, tm=128):
    n, d = h1.shape
    nb = n // tm
    idx3 = pos_km.reshape(nb, 1, tm * TOP_K)
    return pl.pallas_call(
        functools.partial(_combine_kernel, alpha=alpha, tm=tm),
        out_shape=jax.ShapeDtypeStruct((n, d), F32),
        grid=(nb,),
        in_specs=[
            pl.BlockSpec((None, 1, tm * TOP_K), lambda j: (j, 0, 0), memory_space=pltpu.SMEM),
            pl.BlockSpec((None, 1, tm * TOP_K), lambda j: (jnp.minimum(j + 1, nb - 1), 0, 0),
                         memory_space=pltpu.SMEM),
            pl.BlockSpec(memory_space=pl.ANY),
            pl.BlockSpec((tm, LANES), lambda j: (j, 0)),
            pl.BlockSpec((tm, d), lambda j: (j, 0)),
            _const_spec((1, d)), _const_spec((1, d)),
        ],
        out_specs=pl.BlockSpec((tm, d), lambda j: (j, 0)),
        scratch_shapes=[pltpu.VMEM((2, tm * TOP_K, d), F32), pltpu.SemaphoreType.DMA((2,))],
        compiler_params=_cparams(("arbitrary",)),
        name="moe_combine",
    )(idx3, idx3, ys, gates_pad, h1, ln_g.reshape(1, d), ln_b.reshape(1, d))


def _moe_routing(top_e, n_tok, tm_combine):
    n_asg = n_tok * TOP_K
    e_flat = top_e.reshape(n_asg)
    onehot = (e_flat[:, None] == jnp.arange(N_EXPERTS, dtype=I32)[None, :]).astype(I32)
    csum = jnp.cumsum(onehot, axis=0)
    counts = csum[-1]
    rank = jnp.take_along_axis(csum, e_flat[:, None], axis=1)[:, 0] - 1
    nblk_per = (counts + EXPERT_BLOCK - 1) // EXPERT_BLOCK
    blk_end = jnp.cumsum(nblk_per)
    pstarts = (blk_end - nblk_per) * EXPERT_BLOCK
    dest = pstarts[e_flat] + rank
    n_blocks = -(-n_asg // EXPERT_BLOCK) + N_EXPERTS
    n_rows = n_blocks * EXPERT_BLOCK
    tok_flat = jnp.arange(n_asg, dtype=I32) // TOP_K
    row_src = jnp.zeros((n_rows,), I32).at[dest].set(tok_flat)
    blk_expert = jnp.minimum(jnp.searchsorted(blk_end, jnp.arange(n_blocks), side='right'), N_EXPERTS - 1)
    n_used = blk_end[-1]
    nb = n_tok // tm_combine
    pos_km = jnp.transpose(dest.reshape(nb, tm_combine, TOP_K), (0, 2, 1)).reshape(nb, TOP_K * tm_combine)
    return row_src.reshape(n_blocks, EXPERT_BLOCK), blk_expert.astype(I32), n_used.astype(I32), pos_km.astype(I32)


def _permute_w_in(w):
    sizes = (N_HEADS * HEAD_DIM, KV_LATENT, IDX_HEADS * IDX_DIM, IDX_DIM, IDX_HEADS,
             CONV_WIDTH, CONV_WIDTH, CONV_WIDTH, SSM_WIDTH, MEM_HEADS * MEM_HEAD_DIM)
    offs = np.cumsum((0,) + sizes)
    q, ckv, qidx, kidx, widx, cu, cgb, cgc, ssm, memq = [w[:, offs[k]:offs[k + 1]] for k in range(len(sizes))]
    gates = w[:, offs[-1]:]
    pad = jnp.zeros((w.shape[0], LANES - IDX_DIM - IDX_HEADS), w.dtype)
    return jnp.concatenate([gates, q, cu, cgb, cgc, ssm, memq, qidx, ckv, kidx, widx, pad], axis=1).astype(BF16)


def kernel(x, mem, ln_in_g, ln_in_b, w_in, kv_norm_g, w_uk, w_uv, conv_w, conv_b, lam_re, lam_im, b_re, b_im, c_re, c_im, d_skip, log_dt, w_glu, b_glu, w_mem_kv, w_branch, w_o, ln1_g, ln1_b, w_router, b_router, w_up, b_up, w_down, b_down, ln2_g, ln2_b):
    bsz, seq, d = x.shape
    depth = w_in.shape[0]
    mlen = mem.shape[1]
    n = bsz * seq
    alpha = float((2 * depth) ** 0.25)
    tm_combine = 128

    h = _ln_in(x.reshape(n, d), ln_in_g, ln_in_b)
    mem2 = mem.reshape(bsz * mlen, d)
    for l in range(depth):
        proj = _matmul(h, _permute_w_in(w_in[l]), 256, "in_proj")
        qcat, kcat, ckvn = _prep(proj, kv_norm_g[l])
        att = _dsa_attention(proj, qcat, kcat, ckvn, w_uk[l], w_uv[l], bsz, seq)
        cnv = _short_conv(proj, conv_w[l], conv_b[l], bsz, seq)
        ssm = _s5_mixer(proj, lam_re[l], lam_im[l], b_re[l], b_im[l], c_re[l], c_im[l], d_skip[l], log_dt[l],
                        w_glu[l], b_glu[l], bsz, seq)
        kvm = _matmul(mem2, w_mem_kv[l].astype(BF16), min(256, bsz * mlen), "mem_kv")
        mem_o = _memory_attention(proj, kvm, bsz, seq, mlen)
        h1, te_pad, tg_pad = _merge_router(att, cnv, ssm, mem_o, proj, h, w_branch[l], w_o[l], ln1_g[l], ln1_b[l],
                                           w_router[l], b_router[l], alpha)
        top_e = te_pad[:, :TOP_K]
        row_src, blk_expert, n_used, pos_km = _moe_routing(top_e, n, tm_combine)
        wu = w_up[l].astype(BF16)
        ys = _expert_ffn(h1, row_src, blk_expert, n_used, wu[:, :, 0::2], wu[:, :, 1::2],
                         b_up[l][:, None, 0::2], b_up[l][:, None, 1::2],
                         w_down[l].astype(BF16), b_down[l][:, None, :])
        h = _combine(ys, pos_km, tg_pad, h1, ln2_g[l], ln2_b[l], alpha, tm_combine)
    return h.reshape(bsz, seq, d)
```

```python
import functools
import math

import numpy as np
import jax
import jax.numpy as jnp
from jax import lax
from jax.experimental import pallas as pl
from jax.experimental.pallas import tpu as pltpu

F32 = jnp.float32
BF16 = jnp.bfloat16
I32 = jnp.int32

N_HEADS = 8
HEAD_DIM = 64
KV_LATENT = 128
IDX_HEADS = 8
IDX_DIM = 32
INDEX_TOPK = 256
DSA_QUERY_BLOCK = 256
CONV_WIDTH = 512
SSM_WIDTH = 512
SSM_GROUP = 16
SSM_GROUPS = SSM_WIDTH // SSM_GROUP
SSM_STATE = 64
MEM_HEADS = 4
MEM_HEAD_DIM = 128
N_BRANCH = 4
BRANCH_WIDTH = 512
N_EXPERTS = 32
TOP_K = 4
D_EXPERT = 1024
SWIGLU_LIMIT = 7.0
SWIGLU_ALPHA = 1.702
EXPERT_BLOCK = 256
LN_EPS = 1e-5

LANES = 128
SUBLANES = 8
VMEM_LIMIT_BYTES = 56 * 1024 * 1024

INT_MIN = -(2 ** 31)
NEG_BIG = -1e30

COL_GATES = 0
COL_Q = 4096
COL_CONV_U = 4608
COL_CONV_GB = 5120
COL_CONV_GC = 5632
COL_SSM = 6144
COL_MEMQ = 6656
COL_QIDX = 7168
COL_CKV = 7424
COL_TAIL = 7552
D_PROJ = 7680
TAIL_W_OFF = IDX_DIM

SSM_SEQS = 8


def _cparams(sem):
    return pltpu.CompilerParams(dimension_semantics=sem, vmem_limit_bytes=VMEM_LIMIT_BYTES)


def _layer_norm(x, g, b):
    mu = jnp.mean(x, axis=-1, keepdims=True)
    xc = x - mu
    var = jnp.mean(xc * xc, axis=-1, keepdims=True)
    return xc * lax.rsqrt(var + LN_EPS) * g + b


def _const_spec(shape):
    nd = len(shape)
    return pl.BlockSpec(shape, lambda *_: (0,) * nd)


def _ln_kernel(x_ref, g_ref, b_ref, o_ref):
    o_ref[...] = _layer_norm(x_ref[...], g_ref[...], b_ref[...])


def _ln_in(x2, g, b, tm=512):
    n, d = x2.shape
    return pl.pallas_call(
        _ln_kernel,
        out_shape=jax.ShapeDtypeStruct((n, d), F32),
        grid=(n // tm,),
        in_specs=[pl.BlockSpec((tm, d), lambda i: (i, 0)), _const_spec((1, d)), _const_spec((1, d))],
        out_specs=pl.BlockSpec((tm, d), lambda i: (i, 0)),
        compiler_params=_cparams(("parallel",)),
        name="ln_in",
    )(x2, g.reshape(1, d), b.reshape(1, d))


def _matmul_kernel(x_ref, w_ref, o_ref):
    o_ref[...] = jnp.dot(x_ref[...].astype(BF16), w_ref[...], preferred_element_type=F32)


def _matmul(x, w_bf16, tm, name):
    n, k = x.shape
    m = w_bf16.shape[1]
    return pl.pallas_call(
        _matmul_kernel,
        out_shape=jax.ShapeDtypeStruct((n, m), F32),
        grid=(n // tm,),
        in_specs=[pl.BlockSpec((tm, k), lambda i: (i, 0)),
                  pl.BlockSpec((k, m), lambda i: (0, 0), pipeline_mode=pl.Buffered(1))],
        out_specs=pl.BlockSpec((tm, m), lambda i: (i, 0)),
        compiler_params=_cparams(("parallel",)),
        name=name,
    )(x, w_bf16)


def _split_hi_lo(x):
    hi = x.astype(BF16)
    lo = (x - hi.astype(F32)).astype(BF16)
    return hi, lo


def _prep_kernel(qidx_ref, ckv_ref, tail_ref, g_ref, sq_ref, sk_ref, qcat_ref, kcat_ref, ckvn_ref, ckvt_ref):
    q_hi, q_lo = _split_hi_lo(qidx_ref[...])
    qcat = jnp.dot(jnp.concatenate([q_hi, q_lo], axis=1), sq_ref[...], preferred_element_type=F32)
    qcat_ref[...] = qcat.astype(BF16)
    k_hi, k_lo = _split_hi_lo(tail_ref[...])
    kcat = jnp.dot(jnp.concatenate([k_hi, k_lo], axis=1), sk_ref[...], preferred_element_type=F32)
    kcat_ref[...] = kcat.astype(BF16)
    c = ckv_ref[...]
    ms = jnp.mean(c * c, axis=-1, keepdims=True)
    cn = c * lax.rsqrt(ms + LN_EPS) * g_ref[...]
    ckvn_ref[...] = cn.astype(BF16)
    ckvt_ref[...] = cn.T.astype(BF16)


def _selection_matrices():
    sq = np.zeros((2 * IDX_HEADS * IDX_DIM, IDX_HEADS * LANES), np.float32)
    for h in range(IDX_HEADS):
        for d in range(IDX_DIM):
            hi_in = h * IDX_DIM + d
            lo_in = IDX_HEADS * IDX_DIM + hi_in
            sq[hi_in, h * LANES + d] = 1.0
            sq[hi_in, h * LANES + IDX_DIM + d] = 1.0
            sq[lo_in, h * LANES + 2 * IDX_DIM + d] = 1.0
    sk = np.zeros((2 * LANES, LANES), np.float32)
    for d in range(IDX_DIM):
        sk[d, d] = 1.0
        sk[LANES + d, IDX_DIM + d] = 1.0
        sk[d, 2 * IDX_DIM + d] = 1.0
    return jnp.asarray(sq, BF16), jnp.asarray(sk, BF16)


def _prep(proj, kv_norm_g, bsz, seq, tm=512):
    n = proj.shape[0]
    tm = min(tm, seq)
    nbt = seq // tm
    sq, sk = _selection_matrices()
    return pl.pallas_call(
        _prep_kernel,
        out_shape=(jax.ShapeDtypeStruct((n, IDX_HEADS * LANES), BF16),
                   jax.ShapeDtypeStruct((n, LANES), BF16),
                   jax.ShapeDtypeStruct((n, KV_LATENT), BF16),
                   jax.ShapeDtypeStruct((bsz, KV_LATENT, seq), BF16)),
        grid=(n // tm,),
        in_specs=[pl.BlockSpec((tm, 256), lambda i: (i, COL_QIDX // 256)),
                  pl.BlockSpec((tm, 128), lambda i: (i, COL_CKV // 128)),
                  pl.BlockSpec((tm, 128), lambda i: (i, COL_TAIL // 128)),
                  _const_spec((1, KV_LATENT)), _const_spec(sq.shape), _const_spec(sk.shape)],
        out_specs=(pl.BlockSpec((tm, IDX_HEADS * LANES), lambda i: (i, 0)),
                   pl.BlockSpec((tm, LANES), lambda i: (i, 0)),
                   pl.BlockSpec((tm, KV_LATENT), lambda i: (i, 0)),
                   pl.BlockSpec((None, KV_LATENT, tm), lambda i: (i // nbt, 0, i % nbt))),
        compiler_params=_cparams(("parallel",)),
        name="dsa_prep",
    )(proj, proj, proj, kv_norm_g.reshape(1, KV_LATENT), sq, sk)


def _dsa_kernel(q_ref, tail_ref, qcat_ref, kcat_ref, ckv_ref, ckvt_ref, wuk_ref, wuv_ref, o_ref,
                key_ref, qlat_ref, m_ref, l_ref, acc_ref, *, seq, topk, ck, cka, qb):
    i = pl.program_id(1)
    q0 = i * qb
    nch = (q0 + qb + ck - 1) // ck
    q_pos = q0 + lax.broadcasted_iota(I32, (1, qb), 1)
    row_pos = lax.broadcasted_iota(I32, (ck, 1), 0)
    w_t = tail_ref[...].T
    nt_dims = (((1,), (1,)), ((), ()))
    cnt_rows = min(64, ck)

    def score_chunk(c, carry):
        k0 = pl.multiple_of(c * ck, ck)
        kc = kcat_ref[pl.ds(k0, ck), :]
        s = jnp.zeros((ck, qb), F32)
        for h in range(IDX_HEADS):
            d = lax.dot_general(kc, qcat_ref[:, h * LANES:(h + 1) * LANES], nt_dims, preferred_element_type=F32)
            s = s + jnp.maximum(d, 0.0) * w_t[TAIL_W_OFF + h:TAIL_W_OFF + h + 1, :]
        s = s + 0.0
        bits = pltpu.bitcast(s, I32)
        key = bits ^ ((bits >> 31) & jnp.int32(0x7FFFFFFF))
        key = jnp.where(k0 + row_pos <= q_pos, key, jnp.int32(INT_MIN))
        key_ref[pl.ds(k0, ck), :] = key
        return carry

    lax.fori_loop(0, nch, score_chunk, 0)

    def count_keys(pred_fn):
        def body(c, acc):
            k0 = pl.multiple_of(c * ck, ck)
            hit = jnp.where(pred_fn(key_ref[pl.ds(k0, ck), :], k0), 1.0, 0.0)
            return acc + jnp.sum(hit.reshape(ck // cnt_rows, cnt_rows, qb), axis=0)
        acc = lax.fori_loop(0, nch, body, jnp.zeros((cnt_rows, qb), F32))
        return jnp.sum(acc, axis=0, keepdims=True)

    kf = float(topk)

    def bit_pass(b, v):
        cand = v + jnp.left_shift(jnp.int32(1), 31 - b)
        cnt = count_keys(lambda key, k0: key >= cand)
        return jnp.where(cnt >= kf, cand, v)

    v = lax.fori_loop(0, 32, bit_pass, jnp.full((1, qb), INT_MIN, I32))
    thr = jnp.maximum(v, jnp.int32(INT_MIN + 1))
    c_ge = count_keys(lambda key, k0: key >= thr)
    c_gt = count_keys(lambda key, k0: key > thr)
    need = kf - c_gt

    jlim_ref = m_ref.at[0:1]
    jlim_ref[...] = jnp.full((1, qb), float(seq), F32)
    has_tie = jnp.max(jnp.where(c_ge > kf, 1.0, 0.0)) > 0.0

    @pl.when(has_tie)
    def _():
        n_bits = max(1, int(math.ceil(math.log2(seq))))

        def pos_pass(b, j):
            cand = j + jnp.left_shift(jnp.int32(1), n_bits - 1 - b)
            cnt = count_keys(lambda key, k0: (key == thr) & (k0 + row_pos < cand))
            return jnp.where(cnt < need, cand, j)

        j = lax.fori_loop(0, n_bits, pos_pass, jnp.zeros((1, qb), I32))
        jlim_ref[...] = jnp.where(c_ge > kf, j.astype(F32), float(seq))

    jlim = jlim_ref[...].astype(I32)

    q_bf = q_ref[...].astype(BF16)
    for h in range(N_HEADS):
        ql = jnp.dot(q_bf, wuk_ref[h], preferred_element_type=F32) * (HEAD_DIM ** -0.5 * math.log2(math.e))
        qlat_ref[h] = ql.T.astype(BF16)

    m_ref[...] = jnp.full(m_ref.shape, NEG_BIG, F32)
    l_ref[...] = jnp.zeros(l_ref.shape, F32)
    acc_ref[...] = jnp.zeros(acc_ref.shape, F32)

    row_pos_a = lax.broadcasted_iota(I32, (cka, 1), 0)

    def att_chunk(c, carry):
        k0 = pl.multiple_of(c * cka, cka)
        kv = ckv_ref[pl.ds(k0, cka), :]
        kvt = ckvt_ref[:, pl.ds(k0, cka)]
        key = key_ref[pl.ds(k0, cka), :]
        sel = (key > thr) | ((key == thr) & (k0 + row_pos_a <= jlim))
        bias = jnp.where(sel, 0.0, NEG_BIG)
        for h in range(N_HEADS):
            lg = jnp.dot(kv, qlat_ref[h], preferred_element_type=F32) + bias
            m_prev = m_ref[h:h + 1, :]
            m_new = jnp.maximum(m_prev, jnp.max(lg, axis=0, keepdims=True))
            p = jnp.exp2(lg - m_new)
            alpha = jnp.exp2(m_prev - m_new)
            l_ref[h:h + 1, :] = alpha * l_ref[h:h + 1, :] + jnp.sum(p, axis=0, keepdims=True)
            acc_ref[h] = alpha * acc_ref[h] + jnp.dot(kvt, p.astype(BF16), preferred_element_type=F32)
            m_ref[h:h + 1, :] = m_new
        return carry

    lax.fori_loop(0, (q0 + qb + cka - 1) // cka, att_chunk, 0)

    out = jnp.zeros((qb, N_HEADS * HEAD_DIM), F32)
    for h in range(N_HEADS):
        o_lat = (acc_ref[h] / l_ref[h:h + 1, :]).T
        out = out + jnp.dot(o_lat.astype(BF16), wuv_ref[h], preferred_element_type=F32)
    o_ref[...] = out


def _dsa_attention(proj, qcat, kcat, ckvn, ckvt, w_uk, w_uv, bsz, seq):
    n = proj.shape[0]
    qb = min(DSA_QUERY_BLOCK, seq)
    n_blk = seq // qb
    topk = min(INDEX_TOPK, seq // 4)
    ck = min(256, seq)
    cka = min(128, seq)
    eye = jnp.eye(N_HEADS, dtype=F32)
    wuk = jnp.einsum('chd,hg->hgdc', w_uk, eye).reshape(N_HEADS, N_HEADS * HEAD_DIM, KV_LATENT).astype(BF16)
    wuv = jnp.einsum('chd,hg->hcgd', w_uv, eye).reshape(N_HEADS, KV_LATENT, N_HEADS * HEAD_DIM).astype(BF16)
    kernel = functools.partial(_dsa_kernel, seq=seq, topk=topk, ck=ck, cka=cka, qb=qb)
    return pl.pallas_call(
        kernel,
        out_shape=jax.ShapeDtypeStruct((n, N_HEADS * HEAD_DIM), F32),
        grid=(bsz, n_blk),
        in_specs=[pl.BlockSpec((qb, 512), lambda b, i: (b * n_blk + i, COL_Q // 512)),
                  pl.BlockSpec((qb, 128), lambda b, i: (b * n_blk + i, COL_TAIL // 128)),
                  pl.BlockSpec((qb, IDX_HEADS * LANES), lambda b, i: (b * n_blk + i, 0)),
                  pl.BlockSpec((seq, LANES), lambda b, i: (b, 0)),
                  pl.BlockSpec((seq, KV_LATENT), lambda b, i: (b, 0)),
                  pl.BlockSpec((None, KV_LATENT, seq), lambda b, i: (b, 0, 0)),
                  _const_spec(wuk.shape), _const_spec(wuv.shape)],
        out_specs=pl.BlockSpec((qb, N_HEADS * HEAD_DIM), lambda b, i: (b * n_blk + i, 0)),
        scratch_shapes=[pltpu.VMEM((seq, qb), I32),
                        pltpu.VMEM((N_HEADS, KV_LATENT, qb), BF16),
                        pltpu.VMEM((N_HEADS, qb), F32),
                        pltpu.VMEM((N_HEADS, qb), F32),
                        pltpu.VMEM((N_HEADS, KV_LATENT, qb), F32)],
        compiler_params=_cparams(("parallel", "parallel")),
        name="dsa_attention",
    )(proj, proj, qcat, kcat, ckvn, ckvt, wuk, wuv)


def _conv_kernel(u_ref, gb_ref, gc_ref, pu_ref, pgc_ref, w_ref, b_ref, o_ref):
    i = pl.program_id(1)
    tb = u_ref.shape[0]
    v = gc_ref[...] * u_ref[...]
    halo = jnp.where(i > 0, pgc_ref[...] * pu_ref[...], 0.0)
    vfull = jnp.concatenate([halo, v], axis=0)
    v1 = pltpu.roll(vfull, 1, 0)[SUBLANES:SUBLANES + tb]
    v2 = pltpu.roll(vfull, 2, 0)[SUBLANES:SUBLANES + tb]
    w = w_ref[...]
    y = w[0:1] * v2 + w[1:2] * v1 + w[2:3] * v
    o_ref[...] = gb_ref[...] * (y + b_ref[...])


def _short_conv(proj, conv_w, conv_b, bsz, seq, tb=512):
    n = proj.shape[0]
    tb = min(tb, seq)
    nb = seq // tb
    cw = CONV_WIDTH
    blk = lambda col: pl.BlockSpec((tb, cw), lambda b, i: (b * nb + i, col // cw))
    prev = lambda col: pl.BlockSpec(
        (SUBLANES, cw), lambda b, i: (jnp.maximum((b * nb + i) * (tb // SUBLANES) - 1, 0), col // cw))
    return pl.pallas_call(
        _conv_kernel,
        out_shape=jax.ShapeDtypeStruct((n, cw), F32),
        grid=(bsz, nb),
        in_specs=[blk(COL_CONV_U), blk(COL_CONV_GB), blk(COL_CONV_GC), prev(COL_CONV_U), prev(COL_CONV_GC),
                  _const_spec((SUBLANES, cw)), _const_spec((1, cw))],
        out_specs=pl.BlockSpec((tb, cw), lambda b, i: (b * nb + i, 0)),
        compiler_params=_cparams(("parallel", "parallel")),
        name="short_conv",
    )(proj, proj, proj, proj, proj,
      jnp.pad(conv_w, ((0, SUBLANES - conv_w.shape[0]), (0, 0))), conv_b.reshape(1, cw))


def _s5_scan_chunk(bu_ref, a_ref, st_ref, n_steps, store):
    s_tot = a_ref.shape[1] // 2
    cb = 512
    for blk in range(s_tot // cb):
        re_sl = pl.ds(blk * cb, cb)
        im_sl = pl.ds(s_tot + blk * cb, cb)
        ar = jnp.broadcast_to(a_ref[0:1, blk * cb:(blk + 1) * cb], (SSM_SEQS, cb))
        ai = jnp.broadcast_to(a_ref[0:1, s_tot + blk * cb:s_tot + (blk + 1) * cb], (SSM_SEQS, cb))

        def step(t, carry):
            re, im = carry
            r0 = pl.multiple_of(t * SSM_SEQS, SSM_SEQS)
            br = bu_ref[pl.ds(r0, SSM_SEQS), re_sl]
            bi = bu_ref[pl.ds(r0, SSM_SEQS), im_sl]
            nre = ar * re - ai * im + br
            nim = ar * im + ai * re + bi
            if store:
                bu_ref[pl.ds(r0, SSM_SEQS), re_sl] = nre
                bu_ref[pl.ds(r0, SSM_SEQS), im_sl] = nim
            return nre, nim

        re, im = lax.fori_loop(0, n_steps, step, (st_ref[:, re_sl], st_ref[:, im_sl]), unroll=8)
        st_ref[:, re_sl] = re
        st_ref[:, im_sl] = im


def _s5_ends_kernel(u_ref, wb_ref, a_ref, e_ref, bu_ref, st_ref, *, n_steps):
    c = pl.program_id(0)

    @pl.when(c == 0)
    def _():
        st_ref[...] = jnp.zeros(st_ref.shape, F32)

    bu_ref[...] = jnp.dot(u_ref[...].astype(BF16), wb_ref[...], preferred_element_type=F32)
    _s5_scan_chunk(bu_ref, a_ref, st_ref, n_steps, store=False)

    @pl.when(c == pl.num_programs(0) - 1)
    def _():
        e_ref[...] = st_ref[...]


def _gelu_tanh(x):
    return 0.5 * x * (1.0 + jnp.tanh(math.sqrt(2.0 / math.pi) * (x + 0.044715 * (x * x * x))))


def _s5_main_kernel(u_ref, wb_ref, a_ref, ends_ref, apow_ref, wc_ref, d_ref, wg_ref, bg_ref, o_ref,
                    bu_ref, st_ref, *, n_steps, segs):
    c = pl.program_id(0)
    s_tot = a_ref.shape[1] // 2

    @pl.when(c == 0)
    def _():
        pr = apow_ref[0:1, :s_tot]
        pi = apow_ref[0:1, s_tot:]
        n_batch = SSM_SEQS // segs
        for b in range(n_batch):
            hre = jnp.zeros((1, s_tot), F32)
            him = jnp.zeros((1, s_tot), F32)
            for k in range(segs):
                row = b * segs + k
                st_ref[row:row + 1, :s_tot] = hre
                st_ref[row:row + 1, s_tot:] = him
                ere = ends_ref[row:row + 1, :s_tot]
                eim = ends_ref[row:row + 1, s_tot:]
                hre, him = pr * hre - pi * him + ere, pr * him + pi * hre + eim

    u = u_ref[...]
    bu_ref[...] = jnp.dot(u.astype(BF16), wb_ref[...], preferred_element_type=F32)
    _s5_scan_chunk(bu_ref, a_ref, st_ref, n_steps, store=True)
    y = jnp.dot(bu_ref[...].astype(BF16), wc_ref[...], preferred_element_type=F32) + d_ref[...] * u
    z = _gelu_tanh(y)
    gate = jnp.dot(z.astype(BF16), wg_ref[...], preferred_element_type=F32) + bg_ref[...]
    o_ref[...] = z * jax.nn.sigmoid(gate)


def _s5_mixer(proj, lam_re, lam_im, b_re, b_im, c_re, c_im, d_skip, log_dt, w_glu, b_glu, bsz, seq):
    n = proj.shape[0]
    g, p, nn = SSM_GROUPS, SSM_STATE, SSM_GROUP
    s_tot = g * p
    segs = SSM_SEQS // bsz
    tseg = seq // segs
    ch = min(64, tseg)
    rows = ch * SSM_SEQS
    n_chunks = tseg // ch

    dt = jnp.exp(log_dt.astype(F32))[:, None]
    lam = lax.complex(lam_re.astype(F32), lam_im.astype(F32))
    a_bar = jnp.exp(lam * dt)
    b_bar = ((a_bar - 1.0) / lam)[:, :, None] * lax.complex(b_re.astype(F32), b_im.astype(F32))
    a_pow = jnp.exp(lam * dt * tseg)
    a_vec = jnp.concatenate([jnp.real(a_bar).reshape(1, s_tot), jnp.imag(a_bar).reshape(1, s_tot)], axis=1)
    apow_vec = jnp.concatenate([jnp.real(a_pow).reshape(1, s_tot), jnp.imag(a_pow).reshape(1, s_tot)], axis=1)
    eye = jnp.eye(g, dtype=F32)
    wb = jnp.concatenate(
        [jnp.einsum('gpn,gh->gnhp', jnp.real(b_bar), eye).reshape(g * nn, s_tot),
         jnp.einsum('gpn,gh->gnhp', jnp.imag(b_bar), eye).reshape(g * nn, s_tot)], axis=1).astype(BF16)
    wc = jnp.concatenate(
        [jnp.einsum('gnp,gh->gphn', c_re.astype(F32), eye).reshape(s_tot, g * nn),
         -jnp.einsum('gnp,gh->gphn', c_im.astype(F32), eye).reshape(s_tot, g * nn)], axis=0).astype(BF16)

    u = proj[:, COL_SSM:COL_SSM + SSM_WIDTH].reshape(bsz, segs, tseg, SSM_WIDTH)
    u_perm = jnp.transpose(u, (2, 0, 1, 3)).reshape(n, SSM_WIDTH)

    ends = pl.pallas_call(
        functools.partial(_s5_ends_kernel, n_steps=ch),
        out_shape=jax.ShapeDtypeStruct((SSM_SEQS, 2 * s_tot), F32),
        grid=(n_chunks,),
        in_specs=[pl.BlockSpec((rows, SSM_WIDTH), lambda c: (c, 0)),
                  _const_spec(wb.shape), _const_spec(a_vec.shape)],
        out_specs=_const_spec((SSM_SEQS, 2 * s_tot)),
        scratch_shapes=[pltpu.VMEM((rows, 2 * s_tot), F32), pltpu.VMEM((SSM_SEQS, 2 * s_tot), F32)],
        compiler_params=_cparams(("arbitrary",)),
        name="s5_segment_ends",
    )(u_perm, wb, a_vec)

    out_perm = pl.pallas_call(
        functools.partial(_s5_main_kernel, n_steps=ch, segs=segs),
        out_shape=jax.ShapeDtypeStruct((n, SSM_WIDTH), F32),
        grid=(n_chunks,),
        in_specs=[pl.BlockSpec((rows, SSM_WIDTH), lambda c: (c, 0)),
                  _const_spec(wb.shape), _const_spec(a_vec.shape),
                  _const_spec((SSM_SEQS, 2 * s_tot)), _const_spec(apow_vec.shape),
                  _const_spec(wc.shape), _const_spec((1, SSM_WIDTH)),
                  _const_spec((SSM_WIDTH, SSM_WIDTH)), _const_spec((1, SSM_WIDTH))],
        out_specs=pl.BlockSpec((rows, SSM_WIDTH), lambda c: (c, 0)),
        scratch_shapes=[pltpu.VMEM((rows, 2 * s_tot), F32), pltpu.VMEM((SSM_SEQS, 2 * s_tot), F32)],
        compiler_params=_cparams(("arbitrary",)),
        name="s5_scan_glu",
    )(u_perm, wb, a_vec, ends, apow_vec, wc, d_skip.reshape(1, SSM_WIDTH).astype(F32),
      w_glu.astype(BF16), b_glu.reshape(1, SSM_WIDTH))

    out = out_perm.reshape(tseg, bsz, segs, SSM_WIDTH)
    return jnp.transpose(out, (1, 2, 0, 3)).reshape(n, SSM_WIDTH)


def _mem_attn_kernel(q_ref, kv_ref, o_ref):
    hw = MEM_HEADS * MEM_HEAD_DIM
    q = q_ref[...].astype(BF16)
    kv = kv_ref[...].astype(BF16)
    for h in range(MEM_HEADS):
        sl = slice(h * MEM_HEAD_DIM, (h + 1) * MEM_HEAD_DIM)
        k = kv[:, sl]
        v = kv[:, hw + h * MEM_HEAD_DIM:hw + (h + 1) * MEM_HEAD_DIM]
        lg = lax.dot_general(q[:, sl], k, (((1,), (1,)), ((), ())),
                             preferred_element_type=F32) * (MEM_HEAD_DIM ** -0.5)
        m = jnp.max(lg, axis=1, keepdims=True)
        p = jnp.exp(lg - m)
        p = p / jnp.sum(p, axis=1, keepdims=True)
        o_ref[:, sl] = jnp.dot(p.astype(BF16), v, preferred_element_type=F32)


def _memory_attention(proj, kvm, bsz, seq, mlen, tm=512):
    n = proj.shape[0]
    tm = min(tm, seq)
    nb = seq // tm
    hw = MEM_HEADS * MEM_HEAD_DIM
    return pl.pallas_call(
        _mem_attn_kernel,
        out_shape=jax.ShapeDtypeStruct((n, hw), F32),
        grid=(bsz, nb),
        in_specs=[pl.BlockSpec((tm, hw), lambda b, i: (b * nb + i, COL_MEMQ // hw)),
                  pl.BlockSpec((mlen, 2 * hw), lambda b, i: (b, 0))],
        out_specs=pl.BlockSpec((tm, hw), lambda b, i: (b * nb + i, 0)),
        compiler_params=_cparams(("parallel", "parallel")),
        name="memory_attention",
    )(proj, kvm)


def _merge_kernel(att_ref, cnv_ref, ssm_ref, mem_ref, gl_ref, h_ref, wbr_ref, wo_ref, g_ref, b_ref,
                  wr_ref, br_ref, h1_ref, te_ref, tg_ref, *, alpha):
    d = h_ref.shape[1]
    merged = jnp.zeros(h_ref.shape, F32)
    for r, br in enumerate((att_ref, cnv_ref, ssm_ref, mem_ref)):
        y = jnp.dot(br[...].astype(BF16), wbr_ref[r], preferred_element_type=F32)
        merged = merged + y * jax.nn.sigmoid(gl_ref[:, r * d:(r + 1) * d])
    y = alpha * h_ref[...] + jnp.dot(merged.astype(BF16), wo_ref[...], preferred_element_type=F32)
    h1 = _layer_norm(y, g_ref[...], b_ref[...])
    h1_ref[...] = h1

    logits = jnp.dot(h1, wr_ref[...], preferred_element_type=F32, precision=lax.Precision.HIGHEST) + br_ref[...]
    tm = logits.shape[0]
    lane = lax.broadcasted_iota(I32, (tm, N_EXPERTS), 1)
    out_lane = lax.broadcasted_iota(I32, (tm, LANES), 1)
    work = logits
    top_e = jnp.zeros((tm, LANES), I32)
    top_v = jnp.zeros((tm, LANES), F32)
    vals = []
    for k in range(TOP_K):
        mx = jnp.max(work, axis=1, keepdims=True)
        idx = jnp.min(jnp.where(work == mx, lane, N_EXPERTS), axis=1, keepdims=True)
        work = jnp.where(lane == idx, -jnp.inf, work)
        top_e = jnp.where(out_lane == k, idx, top_e)
        vals.append(mx)
    den = sum(jnp.exp(vk - vals[0]) for vk in vals)
    for k in range(TOP_K):
        top_v = jnp.where(out_lane == k, jnp.exp(vals[k] - vals[0]) / den, top_v)
    te_ref[...] = top_e
    tg_ref[...] = top_v


def _merge_router(att, cnv, ssm, mem_o, proj, h, w_branch, w_o, ln_g, ln_b, w_router, b_router, alpha, tm=256):
    n, d = h.shape
    bw = BRANCH_WIDTH
    row = lambda w: pl.BlockSpec((tm, w), lambda i: (i, 0))
    return pl.pallas_call(
        functools.partial(_merge_kernel, alpha=alpha),
        out_shape=(jax.ShapeDtypeStruct((n, d), F32),
                   jax.ShapeDtypeStruct((n, LANES), I32),
                   jax.ShapeDtypeStruct((n, LANES), F32)),
        grid=(n // tm,),
        in_specs=[row(bw), row(bw), row(bw), row(bw),
                  pl.BlockSpec((tm, N_BRANCH * d), lambda i: (i, COL_GATES // (N_BRANCH * d))),
                  row(d),
                  _const_spec((N_BRANCH, bw, d)), _const_spec((d, d)), _const_spec((1, d)), _const_spec((1, d)),
                  _const_spec((d, N_EXPERTS)), _const_spec((1, N_EXPERTS))],
        out_specs=(row(d), row(LANES), row(LANES)),
        compiler_params=_cparams(("parallel",)),
        name="merge_router",
    )(att, cnv, ssm, mem_o, proj, h, w_branch.astype(BF16), w_o.astype(BF16),
      ln_g.reshape(1, d), ln_b.reshape(1, d), w_router, b_router.reshape(1, N_EXPERTS))


def _expert_kernel(meta_ref, cur_ref, nxt_ref, x_hbm, wu_ref, bu_ref, wd_ref, bd_ref, sel_ref, o_ref,
                   xbuf, wu_bf, wd_bf, sem):
    j = pl.program_id(0)
    nblk = pl.num_programs(0)
    n_used = meta_ref[0]
    slot = j % 2
    f2 = wu_ref.shape[2]

    def gather(idx_ref, s):
        def body(r, carry):
            pltpu.make_async_copy(x_hbm.at[pl.ds(idx_ref[0, r], 1), :],
                                  xbuf.at[s, pl.ds(r, 1), :], sem.at[s]).start()
            return carry
        lax.fori_loop(0, EXPERT_BLOCK, body, 0)

    @pl.when((j == 0) & (n_used > 0))
    def _():
        gather(cur_ref, 0)

    @pl.when(j + 1 < jnp.minimum(n_used, nblk))
    def _():
        gather(nxt_ref, 1 - slot)

    @pl.when((j < n_used) & ((j == 0) | (meta_ref[1 + j] != meta_ref[jnp.maximum(j, 1)])))
    def _():
        wu_bf[...] = wu_ref[0].astype(BF16)
        wd_bf[...] = wd_ref[0].astype(BF16)

    @pl.when(j < n_used)
    def _():
        pltpu.make_async_copy(x_hbm.at[pl.ds(0, EXPERT_BLOCK), :], xbuf.at[slot], sem.at[slot]).wait()
        xb = xbuf[slot].astype(BF16)
        hdn = jnp.dot(xb, wu_bf[...], preferred_element_type=F32) + bu_ref[0]
        nxt = pltpu.roll(hdn, f2 - 1, 1)
        h_glu = jnp.minimum(hdn, SWIGLU_LIMIT)
        h_lin = jnp.clip(nxt, -SWIGLU_LIMIT, SWIGLU_LIMIT)
        act = (h_glu * jax.nn.sigmoid(SWIGLU_ALPHA * h_glu) * (h_lin + 1.0)).astype(BF16)
        sel = sel_ref[...]
        cw = sel.shape[0]
        parts = [jnp.dot(act[:, c * cw:(c + 1) * cw], sel, preferred_element_type=F32).astype(BF16)
                 for c in range(f2 // cw)]
        act_c = jnp.concatenate(parts, axis=1)
        o_ref[...] = jnp.dot(act_c, wd_bf[...], preferred_element_type=F32) + bd_ref[0]

    @pl.when(j >= n_used)
    def _():
        o_ref[...] = jnp.zeros(o_ref.shape, F32)


def _expert_ffn(h1, row_src, blk_expert, n_used, w_up, b_up, w_down, b_down):
    n, d = h1.shape
    n_blocks = row_src.shape[0]
    f = D_EXPERT
    meta = jnp.concatenate([n_used.reshape(1).astype(I32), blk_expert.astype(I32)])
    idx3 = row_src.reshape(n_blocks, 1, EXPERT_BLOCK)
    sel_np = np.zeros((2 * LANES, LANES), np.float32)
    sel_np[2 * np.arange(LANES), np.arange(LANES)] = 1.0
    sel = jnp.asarray(sel_np, BF16)
    e_of = lambda j, m: m[1 + j]
    grid_spec = pltpu.PrefetchScalarGridSpec(
        num_scalar_prefetch=1,
        grid=(n_blocks,),
        in_specs=[
            pl.BlockSpec((None, 1, EXPERT_BLOCK), lambda j, m: (j, 0, 0), memory_space=pltpu.SMEM),
            pl.BlockSpec((None, 1, EXPERT_BLOCK), lambda j, m: (jnp.minimum(j + 1, n_blocks - 1), 0, 0),
                         memory_space=pltpu.SMEM),
            pl.BlockSpec(memory_space=pl.ANY),
            pl.BlockSpec((1, d, 2 * f), lambda j, m: (e_of(j, m), 0, 0)),
            pl.BlockSpec((1, 1, 2 * f), lambda j, m: (e_of(j, m), 0, 0)),
            pl.BlockSpec((1, f, d), lambda j, m: (e_of(j, m), 0, 0)),
            pl.BlockSpec((1, 1, d), lambda j, m: (e_of(j, m), 0, 0)),
            pl.BlockSpec(sel.shape, lambda j, m: (0, 0)),
        ],
        out_specs=pl.BlockSpec((EXPERT_BLOCK, d), lambda j, m: (j, 0)),
        scratch_shapes=[pltpu.VMEM((2, EXPERT_BLOCK, d), F32),
                        pltpu.VMEM((d, 2 * f), BF16), pltpu.VMEM((f, d), BF16),
                        pltpu.SemaphoreType.DMA((2,))],
    )
    return pl.pallas_call(
        _expert_kernel,
        out_shape=jax.ShapeDtypeStruct((n_blocks * EXPERT_BLOCK, d), F32),
        grid_spec=grid_spec,
        compiler_params=_cparams(("arbitrary",)),
        name="moe_experts",
    )(meta, idx3, idx3, h1, w_up, b_up[:, None, :], w_down, b_down[:, None, :], sel)


def _combine_kernel(cur_ref, nxt_ref, ys_hbm, g4_ref, h_ref, lg_ref, lb_ref, o_ref, ybuf, sem, *, alpha, tm):
    j = pl.program_id(0)
    nblk = pl.num_programs(0)
    slot = j % 2
    n_rows = tm * TOP_K

    def gather(idx_ref, s):
        def body(r, carry):
            pltpu.make_async_copy(ys_hbm.at[pl.ds(idx_ref[0, r], 1), :],
                                  ybuf.at[s, pl.ds(r, 1), :], sem.at[s]).start()
            return carry
        lax.fori_loop(0, n_rows, body, 0)

    @pl.when(j == 0)
    def _():
        gather(cur_ref, 0)

    @pl.when(j + 1 < nblk)
    def _():
        gather(nxt_ref, 1 - slot)

    pltpu.make_async_copy(ys_hbm.at[pl.ds(0, n_rows), :], ybuf.at[slot], sem.at[slot]).wait()
    g4 = g4_ref[...]
    ffn = jnp.zeros((tm, h_ref.shape[1]), F32)
    for k in range(TOP_K):
        ffn = ffn + ybuf[slot, k * tm:(k + 1) * tm, :] * g4[:, k:k + 1]
    o_ref[...] = _layer_norm(alpha * h_ref[...] + ffn, lg_ref[...], lb_ref[...])


def _combine(ys, pos_km, gates_pad, h1, ln_g, ln_b, alpha, tm=128):
    n, d = h1.shape
    nb = n // tm
    idx3 = pos_km.reshape(nb, 1, tm * TOP_K)
    return pl.pallas_call(
        functools.partial(_combine_kernel, alpha=alpha, tm=tm),
        out_shape=jax.ShapeDtypeStruct((n, d), F32),
        grid=(nb,),
        in_specs=[
            pl.BlockSpec((None, 1, tm * TOP_K), lambda j: (j, 0, 0), memory_space=pltpu.SMEM),
            pl.BlockSpec((None, 1, tm * TOP_K), lambda j: (jnp.minimum(j + 1, nb - 1), 0, 0),
                         memory_space=pltpu.SMEM),
            pl.BlockSpec(memory_space=pl.ANY),
            pl.BlockSpec((tm, LANES), lambda j: (j, 0)),
            pl.BlockSpec((tm, d), lambda j: (j, 0)),
            _const_spec((1, d)), _const_spec((1, d)),
        ],
        out_specs=pl.BlockSpec((tm, d), lambda j: (j, 0)),
        scratch_shapes=[pltpu.VMEM((2, tm * TOP_K, d), F32), pltpu.SemaphoreType.DMA((2,))],
        compiler_params=_cparams(("arbitrary",)),
        name="moe_combine",
    )(idx3, idx3, ys, gates_pad, h1, ln_g.reshape(1, d), ln_b.reshape(1, d))


def _moe_routing(top_e, n_tok, tm_combine):
    n_asg = n_tok * TOP_K
    e_flat = top_e.reshape(n_asg)
    onehot = (e_flat[:, None] == jnp.arange(N_EXPERTS, dtype=I32)[None, :]).astype(I32)
    csum = jnp.cumsum(onehot, axis=0)
    counts = csum[-1]
    rank = jnp.take_along_axis(csum, e_flat[:, None], axis=1)[:, 0] - 1
    nblk_per = (counts + EXPERT_BLOCK - 1) // EXPERT_BLOCK
    blk_end = jnp.cumsum(nblk_per)
    pstarts = (blk_end - nblk_per) * EXPERT_BLOCK
    dest = pstarts[e_flat] + rank
    n_blocks = -(-n_asg // EXPERT_BLOCK) + N_EXPERTS
    n_rows = n_blocks * EXPERT_BLOCK
    tok_flat = jnp.arange(n_asg, dtype=I32) // TOP_K
    row_src = jnp.zeros((n_rows,), I32).at[dest].set(tok_flat)
    blk_expert = jnp.minimum(jnp.searchsorted(blk_end, jnp.arange(n_blocks), side='right'), N_EXPERTS - 1)
    n_used = blk_end[-1]
    nb = n_tok // tm_combine
    pos_km = jnp.transpose(dest.reshape(nb, tm_combine, TOP_K), (0, 2, 1)).reshape(nb, TOP_K * tm_combine)
    return row_src.reshape(n_blocks, EXPERT_BLOCK), blk_expert.astype(I32), n_used.astype(I32), pos_km.astype(I32)


def _permute_w_in(w):
    sizes = (N_HEADS * HEAD_DIM, KV_LATENT, IDX_HEADS * IDX_DIM, IDX_DIM, IDX_HEADS,
             CONV_WIDTH, CONV_WIDTH, CONV_WIDTH, SSM_WIDTH, MEM_HEADS * MEM_HEAD_DIM)
    offs = np.cumsum((0,) + sizes)
    q, ckv, qidx, kidx, widx, cu, cgb, cgc, ssm, memq = [w[:, offs[k]:offs[k + 1]] for k in range(len(sizes))]
    gates = w[:, offs[-1]:]
    pad = jnp.zeros((w.shape[0], LANES - IDX_DIM - IDX_HEADS), w.dtype)
    return jnp.concatenate([gates, q, cu, cgb, cgc, ssm, memq, qidx, ckv, kidx, widx, pad], axis=1).astype(BF16)


def kernel(x, mem, ln_in_g, ln_in_b, w_in, kv_norm_g, w_uk, w_uv, conv_w, conv_b, lam_re, lam_im, b_re, b_im, c_re, c_im, d_skip, log_dt, w_glu, b_glu, w_mem_kv, w_branch, w_o, ln1_g, ln1_b, w_router, b_router, w_up, b_up, w_down, b_down, ln2_g, ln2_b):
    bsz, seq, d = x.shape
    depth = w_in.shape[0]
    mlen = mem.shape[1]
    n = bsz * seq
    alpha = float((2 * depth) ** 0.25)
    tm_combine = 128

    h = _ln_in(x.reshape(n, d), ln_in_g, ln_in_b)
    mem2 = mem.reshape(bsz * mlen, d)
    for l in range(depth):
        proj = _matmul(h, _permute_w_in(w_in[l]), 256, "in_proj")
        qcat, kcat, ckvn, ckvt = _prep(proj, kv_norm_g[l], bsz, seq)
        att = _dsa_attention(proj, qcat, kcat, ckvn, ckvt, w_uk[l], w_uv[l], bsz, seq)
        cnv = _short_conv(proj, conv_w[l], conv_b[l], bsz, seq)
        ssm = _s5_mixer(proj, lam_re[l], lam_im[l], b_re[l], b_im[l], c_re[l], c_im[l], d_skip[l], log_dt[l],
                        w_glu[l], b_glu[l], bsz, seq)
        kvm = _matmul(mem2, w_mem_kv[l].astype(BF16), min(256, bsz * mlen), "mem_kv")
        mem_o = _memory_attention(proj, kvm, bsz, seq, mlen)
        h1, te_pad, tg_pad = _merge_router(att, cnv, ssm, mem_o, proj, h, w_branch[l], w_o[l], ln1_g[l], ln1_b[l],
                                           w_router[l], b_router[l], alpha)
        top_e = te_pad[:, :TOP_K]
        row_src, blk_expert, n_used, pos_km = _moe_routing(top_e, n, tm_combine)
        ys = _expert_ffn(h1, row_src, blk_expert, n_used, w_up[l], b_up[l], w_down[l], b_down[l])
        h = _combine(ys, pos_km, tg_pad, h1, ln2_g[l], ln2_b[l], alpha, tm_combine)
    return h.reshape(bsz, seq, d)
```

```python
import functools
import math

import numpy as np
import jax
import jax.numpy as jnp
from jax import lax
from jax.experimental import pallas as pl
from jax.experimental.pallas import tpu as pltpu

F32 = jnp.float32
BF16 = jnp.bfloat16
I32 = jnp.int32

N_HEADS = 8
HEAD_DIM = 64
KV_LATENT = 128
IDX_HEADS = 8
IDX_DIM = 32
INDEX_TOPK = 256
DSA_QUERY_BLOCK = 256
CONV_WIDTH = 512
SSM_WIDTH = 512
SSM_GROUP = 16
SSM_GROUPS = SSM_WIDTH // SSM_GROUP
SSM_STATE = 64
MEM_HEADS = 4
MEM_HEAD_DIM = 128
N_BRANCH = 4
BRANCH_WIDTH = 512
N_EXPERTS = 32
TOP_K = 4
D_EXPERT = 1024
SWIGLU_LIMIT = 7.0
SWIGLU_ALPHA = 1.702
EXPERT_BLOCK = 256
LN_EPS = 1e-5

LANES = 128
SUBLANES = 8
VMEM_LIMIT_BYTES = 56 * 1024 * 1024

INT_MIN = -(2 ** 31)
NEG_BIG = -1e30

COL_GATES = 0
COL_Q = 4096
COL_CONV_U = 4608
COL_CONV_GB = 5120
COL_CONV_GC = 5632
COL_SSM = 6144
COL_MEMQ = 6656
COL_QIDX = 7168
COL_CKV = 7424
COL_TAIL = 7552
D_PROJ = 7680
TAIL_W_OFF = IDX_DIM

SSM_SEQS = 8


def _cparams(sem):
    return pltpu.CompilerParams(dimension_semantics=sem, vmem_limit_bytes=VMEM_LIMIT_BYTES)


def _layer_norm(x, g, b):
    mu = jnp.mean(x, axis=-1, keepdims=True)
    xc = x - mu
    var = jnp.mean(xc * xc, axis=-1, keepdims=True)
    return xc * lax.rsqrt(var + LN_EPS) * g + b


def _const_spec(shape):
    nd = len(shape)
    return pl.BlockSpec(shape, lambda *_: (0,) * nd)


def _ln_kernel(x_ref, g_ref, b_ref, o_ref):
    o_ref[...] = _layer_norm(x_ref[...], g_ref[...], b_ref[...])


def _ln_in(x2, g, b, tm=512):
    n, d = x2.shape
    return pl.pallas_call(
        _ln_kernel,
        out_shape=jax.ShapeDtypeStruct((n, d), F32),
        grid=(n // tm,),
        in_specs=[pl.BlockSpec((tm, d), lambda i: (i, 0)), _const_spec((1, d)), _const_spec((1, d))],
        out_specs=pl.BlockSpec((tm, d), lambda i: (i, 0)),
        compiler_params=_cparams(("parallel",)),
        name="ln_in",
    )(x2, g.reshape(1, d), b.reshape(1, d))


def _matmul_kernel(x_ref, w_ref, o_ref):
    o_ref[...] = jnp.dot(x_ref[...].astype(BF16), w_ref[...], preferred_element_type=F32)


def _matmul(x, w_bf16, tm, name):
    n, k = x.shape
    m = w_bf16.shape[1]
    return pl.pallas_call(
        _matmul_kernel,
        out_shape=jax.ShapeDtypeStruct((n, m), F32),
        grid=(n // tm,),
        in_specs=[pl.BlockSpec((tm, k), lambda i: (i, 0)),
                  pl.BlockSpec((k, m), lambda i: (0, 0), pipeline_mode=pl.Buffered(1))],
        out_specs=pl.BlockSpec((tm, m), lambda i: (i, 0)),
        compiler_params=_cparams(("parallel",)),
        name=name,
    )(x, w_bf16)


def _split_hi_lo(x):
    hi = x.astype(BF16)
    lo = (x - hi.astype(F32)).astype(BF16)
    return hi, lo


def _prep_kernel(qidx_ref, ckv_ref, tail_ref, g_ref, sq_ref, sk_ref, qcat_ref, kcat_ref, ckvn_ref, ckvt_ref):
    q_hi, q_lo = _split_hi_lo(qidx_ref[...])
    qcat = jnp.dot(jnp.concatenate([q_hi, q_lo], axis=1), sq_ref[...], preferred_element_type=F32)
    qcat_ref[...] = qcat.astype(BF16)
    k_hi, k_lo = _split_hi_lo(tail_ref[...])
    kcat = jnp.dot(jnp.concatenate([k_hi, k_lo], axis=1), sk_ref[...], preferred_element_type=F32)
    kcat_ref[...] = kcat.astype(BF16)
    c = ckv_ref[...]
    ms = jnp.mean(c * c, axis=-1, keepdims=True)
    cn = c * lax.rsqrt(ms + LN_EPS) * g_ref[...]
    ckvn_ref[...] = cn.astype(BF16)
    ckvt_ref[...] = cn.T.astype(BF16)


def _selection_matrices():
    sq = np.zeros((2 * IDX_HEADS * IDX_DIM, IDX_HEADS * LANES), np.float32)
    for h in range(IDX_HEADS):
        for d in range(IDX_DIM):
            hi_in = h * IDX_DIM + d
            lo_in = IDX_HEADS * IDX_DIM + hi_in
            sq[hi_in, h * LANES + d] = 1.0
            sq[hi_in, h * LANES + IDX_DIM + d] = 1.0
            sq[lo_in, h * LANES + 2 * IDX_DIM + d] = 1.0
    sk = np.zeros((2 * LANES, LANES), np.float32)
    for d in range(IDX_DIM):
        sk[d, d] = 1.0
        sk[LANES + d, IDX_DIM + d] = 1.0
        sk[d, 2 * IDX_DIM + d] = 1.0
    return jnp.asarray(sq, BF16), jnp.asarray(sk, BF16)


def _prep(proj, kv_norm_g, bsz, seq, tm=512):
    n = proj.shape[0]
    tm = min(tm, seq)
    nbt = seq // tm
    sq, sk = _selection_matrices()
    return pl.pallas_call(
        _prep_kernel,
        out_shape=(jax.ShapeDtypeStruct((n, IDX_HEADS * LANES), BF16),
                   jax.ShapeDtypeStruct((n, LANES), BF16),
                   jax.ShapeDtypeStruct((n, KV_LATENT), BF16),
                   jax.ShapeDtypeStruct((bsz, KV_LATENT, seq), BF16)),
        grid=(n // tm,),
        in_specs=[pl.BlockSpec((tm, 256), lambda i: (i, COL_QIDX // 256)),
                  pl.BlockSpec((tm, 128), lambda i: (i, COL_CKV // 128)),
                  pl.BlockSpec((tm, 128), lambda i: (i, COL_TAIL // 128)),
                  _const_spec((1, KV_LATENT)), _const_spec(sq.shape), _const_spec(sk.shape)],
        out_specs=(pl.BlockSpec((tm, IDX_HEADS * LANES), lambda i: (i, 0)),
                   pl.BlockSpec((tm, LANES), lambda i: (i, 0)),
                   pl.BlockSpec((tm, KV_LATENT), lambda i: (i, 0)),
                   pl.BlockSpec((None, KV_LATENT, tm), lambda i: (i // nbt, 0, i % nbt))),
        compiler_params=_cparams(("parallel",)),
        name="dsa_prep",
    )(proj, proj, proj, kv_norm_g.reshape(1, KV_LATENT), sq, sk)


def _dsa_kernel(q_ref, tail_ref, qcat_ref, kcat_ref, ckv_ref, ckvt_ref, wuk_ref, wuv_ref, o_ref,
                key_ref, qlat_ref, m_ref, l_ref, acc_ref, *, seq, topk, ck, cka, qb):
    i = pl.program_id(1)
    q0 = i * qb
    nch = (q0 + qb + ck - 1) // ck
    q_pos = q0 + lax.broadcasted_iota(I32, (1, qb), 1)
    row_pos = lax.broadcasted_iota(I32, (ck, 1), 0)
    w_t = tail_ref[...].T
    nt_dims = (((1,), (1,)), ((), ()))
    cnt_rows = min(64, ck)

    def score_chunk(c, carry):
        k0 = pl.multiple_of(c * ck, ck)
        kc = kcat_ref[pl.ds(k0, ck), :]
        s = jnp.zeros((ck, qb), F32)
        for h in range(IDX_HEADS):
            d = lax.dot_general(kc, qcat_ref[:, h * LANES:(h + 1) * LANES], nt_dims, preferred_element_type=F32)
            s = s + jnp.maximum(d, 0.0) * w_t[TAIL_W_OFF + h:TAIL_W_OFF + h + 1, :]
        s = s + 0.0
        bits = pltpu.bitcast(s, I32)
        key = bits ^ ((bits >> 31) & jnp.int32(0x7FFFFFFF))
        key = jnp.where(k0 + row_pos <= q_pos, key, jnp.int32(INT_MIN))
        key_ref[pl.ds(k0, ck), :] = key
        return carry

    lax.fori_loop(0, nch, score_chunk, 0)

    def count_keys(pred_fn):
        def body(c, acc):
            k0 = pl.multiple_of(c * ck, ck)
            hit = jnp.where(pred_fn(key_ref[pl.ds(k0, ck), :], k0), 1.0, 0.0)
            return acc + jnp.sum(hit.reshape(ck // cnt_rows, cnt_rows, qb), axis=0)
        acc = lax.fori_loop(0, nch, body, jnp.zeros((cnt_rows, qb), F32))
        return jnp.sum(acc, axis=0, keepdims=True)

    kf = float(topk)

    def bit_pass(b, v):
        cand = v + jnp.left_shift(jnp.int32(1), 31 - b)
        cnt = count_keys(lambda key, k0: key >= cand)
        return jnp.where(cnt >= kf, cand, v)

    v = lax.fori_loop(0, 32, bit_pass, jnp.full((1, qb), INT_MIN, I32))
    thr = jnp.maximum(v, jnp.int32(INT_MIN + 1))
    c_ge = count_keys(lambda key, k0: key >= thr)
    c_gt = count_keys(lambda key, k0: key > thr)
    need = kf - c_gt

    jlim_ref = m_ref.at[0:1]
    jlim_ref[...] = jnp.full((1, qb), float(seq), F32)
    has_tie = jnp.max(jnp.where(c_ge > kf, 1.0, 0.0)) > 0.0

    @pl.when(has_tie)
    def _():
        n_bits = max(1, int(math.ceil(math.log2(seq))))

        def pos_pass(b, j):
            cand = j + jnp.left_shift(jnp.int32(1), n_bits - 1 - b)
            cnt = count_keys(lambda key, k0: (key == thr) & (k0 + row_pos < cand))
            return jnp.where(cnt < need, cand, j)

        j = lax.fori_loop(0, n_bits, pos_pass, jnp.zeros((1, qb), I32))
        jlim_ref[...] = jnp.where(c_ge > kf, j.astype(F32), float(seq))

    jlim = jlim_ref[...].astype(I32)

    q_bf = q_ref[...].astype(BF16)
    for h in range(N_HEADS):
        ql = jnp.dot(q_bf, wuk_ref[h], preferred_element_type=F32) * (HEAD_DIM ** -0.5 * math.log2(math.e))
        qlat_ref[h] = ql.T.astype(BF16)

    m_ref[...] = jnp.full(m_ref.shape, NEG_BIG, F32)
    l_ref[...] = jnp.zeros(l_ref.shape, F32)
    acc_ref[...] = jnp.zeros(acc_ref.shape, F32)

    row_pos_a = lax.broadcasted_iota(I32, (cka, 1), 0)

    def att_chunk(c, carry):
        k0 = pl.multiple_of(c * cka, cka)
        kv = ckv_ref[pl.ds(k0, cka), :]
        kvt = ckvt_ref[:, pl.ds(k0, cka)]
        key = key_ref[pl.ds(k0, cka), :]
        sel = (key > thr) | ((key == thr) & (k0 + row_pos_a <= jlim))
        bias = jnp.where(sel, 0.0, NEG_BIG)
        for h in range(N_HEADS):
            lg = jnp.dot(kv, qlat_ref[h], preferred_element_type=F32) + bias
            m_prev = m_ref[h:h + 1, :]
            m_new = jnp.maximum(m_prev, jnp.max(lg, axis=0, keepdims=True))
            p = jnp.exp2(lg - m_new)
            alpha = jnp.exp2(m_prev - m_new)
            l_ref[h:h + 1, :] = alpha * l_ref[h:h + 1, :] + jnp.sum(p, axis=0, keepdims=True)
            acc_ref[h] = alpha * acc_ref[h] + jnp.dot(kvt, p.astype(BF16), preferred_element_type=F32)
            m_ref[h:h + 1, :] = m_new
        return carry

    lax.fori_loop(0, (q0 + qb + cka - 1) // cka, att_chunk, 0)

    out = jnp.zeros((qb, N_HEADS * HEAD_DIM), F32)
    for h in range(N_HEADS):
        o_lat = (acc_ref[h] / l_ref[h:h + 1, :]).T
        out = out + jnp.dot(o_lat.astype(BF16), wuv_ref[h], preferred_element_type=F32)
    o_ref[...] = out


def _dsa_attention(proj, qcat, kcat, ckvn, ckvt, w_uk, w_uv, bsz, seq):
    n = proj.shape[0]
    qb = min(DSA_QUERY_BLOCK, seq)
    n_blk = seq // qb
    topk = min(INDEX_TOPK, seq // 4)
    ck = min(256, seq)
    cka = min(128, seq)
    eye = jnp.eye(N_HEADS, dtype=F32)
    wuk = jnp.einsum('chd,hg->hgdc', w_uk, eye).reshape(N_HEADS, N_HEADS * HEAD_DIM, KV_LATENT).astype(BF16)
    wuv = jnp.einsum('chd,hg->hcgd', w_uv, eye).reshape(N_HEADS, KV_LATENT, N_HEADS * HEAD_DIM).astype(BF16)
    kernel = functools.partial(_dsa_kernel, seq=seq, topk=topk, ck=ck, cka=cka, qb=qb)
    return pl.pallas_call(
        kernel,
        out_shape=jax.ShapeDtypeStruct((n, N_HEADS * HEAD_DIM), F32),
        grid=(bsz, n_blk),
        in_specs=[pl.BlockSpec((qb, 512), lambda b, i: (b * n_blk + i, COL_Q // 512)),
                  pl.BlockSpec((qb, 128), lambda b, i: (b * n_blk + i, COL_TAIL // 128)),
                  pl.BlockSpec((qb, IDX_HEADS * LANES), lambda b, i: (b * n_blk + i, 0)),
                  pl.BlockSpec((seq, LANES), lambda b, i: (b, 0)),
                  pl.BlockSpec((seq, KV_LATENT), lambda b, i: (b, 0)),
                  pl.BlockSpec((None, KV_LATENT, seq), lambda b, i: (b, 0, 0)),
                  _const_spec(wuk.shape), _const_spec(wuv.shape)],
        out_specs=pl.BlockSpec((qb, N_HEADS * HEAD_DIM), lambda b, i: (b * n_blk + i, 0)),
        scratch_shapes=[pltpu.VMEM((seq, qb), I32),
                        pltpu.VMEM((N_HEADS, KV_LATENT, qb), BF16),
                        pltpu.VMEM((N_HEADS, qb), F32),
                        pltpu.VMEM((N_HEADS, qb), F32),
                        pltpu.VMEM((N_HEADS, KV_LATENT, qb), F32)],
        compiler_params=_cparams(("parallel", "parallel")),
        name="dsa_attention",
    )(proj, proj, qcat, kcat, ckvn, ckvt, wuk, wuv)


def _conv_kernel(u_ref, gb_ref, gc_ref, pu_ref, pgc_ref, w_ref, b_ref, o_ref):
    i = pl.program_id(1)
    tb = u_ref.shape[0]
    v = gc_ref[...] * u_ref[...]
    halo = jnp.where(i > 0, pgc_ref[...] * pu_ref[...], 0.0)
    vfull = jnp.concatenate([halo, v], axis=0)
    v1 = pltpu.roll(vfull, 1, 0)[SUBLANES:SUBLANES + tb]
    v2 = pltpu.roll(vfull, 2, 0)[SUBLANES:SUBLANES + tb]
    w = w_ref[...]
    y = w[0:1] * v2 + w[1:2] * v1 + w[2:3] * v
    o_ref[...] = gb_ref[...] * (y + b_ref[...])


def _short_conv(proj, conv_w, conv_b, bsz, seq, tb=512):
    n = proj.shape[0]
    tb = min(tb, seq)
    nb = seq // tb
    cw = CONV_WIDTH
    blk = lambda col: pl.BlockSpec((tb, cw), lambda b, i: (b * nb + i, col // cw))
    prev = lambda col: pl.BlockSpec(
        (SUBLANES, cw), lambda b, i: (jnp.maximum((b * nb + i) * (tb // SUBLANES) - 1, 0), col // cw))
    return pl.pallas_call(
        _conv_kernel,
        out_shape=jax.ShapeDtypeStruct((n, cw), F32),
        grid=(bsz, nb),
        in_specs=[blk(COL_CONV_U), blk(COL_CONV_GB), blk(COL_CONV_GC), prev(COL_CONV_U), prev(COL_CONV_GC),
                  _const_spec((SUBLANES, cw)), _const_spec((1, cw))],
        out_specs=pl.BlockSpec((tb, cw), lambda b, i: (b * nb + i, 0)),
        compiler_params=_cparams(("parallel", "parallel")),
        name="short_conv",
    )(proj, proj, proj, proj, proj,
      jnp.pad(conv_w, ((0, SUBLANES - conv_w.shape[0]), (0, 0))), conv_b.reshape(1, cw))


def _s5_scan_chunk(bu_ref, a_ref, st_ref, n_steps, store):
    s_tot = a_ref.shape[1] // 2
    cb = 512
    for blk in range(s_tot // cb):
        re_sl = pl.ds(blk * cb, cb)
        im_sl = pl.ds(s_tot + blk * cb, cb)
        ar = jnp.broadcast_to(a_ref[0:1, blk * cb:(blk + 1) * cb], (SSM_SEQS, cb))
        ai = jnp.broadcast_to(a_ref[0:1, s_tot + blk * cb:s_tot + (blk + 1) * cb], (SSM_SEQS, cb))

        def step(t, carry):
            re, im = carry
            r0 = pl.multiple_of(t * SSM_SEQS, SSM_SEQS)
            br = bu_ref[pl.ds(r0, SSM_SEQS), re_sl]
            bi = bu_ref[pl.ds(r0, SSM_SEQS), im_sl]
            nre = ar * re - ai * im + br
            nim = ar * im + ai * re + bi
            if store:
                bu_ref[pl.ds(r0, SSM_SEQS), re_sl] = nre
                bu_ref[pl.ds(r0, SSM_SEQS), im_sl] = nim
            return nre, nim

        re, im = lax.fori_loop(0, n_steps, step, (st_ref[:, re_sl], st_ref[:, im_sl]), unroll=8)
        st_ref[:, re_sl] = re
        st_ref[:, im_sl] = im


def _s5_ends_kernel(u_ref, wb_ref, a_ref, e_ref, bu_ref, st_ref, *, n_steps):
    c = pl.program_id(0)

    @pl.when(c == 0)
    def _():
        st_ref[...] = jnp.zeros(st_ref.shape, F32)

    bu_ref[...] = jnp.dot(u_ref[...].astype(BF16), wb_ref[...], preferred_element_type=F32)
    _s5_scan_chunk(bu_ref, a_ref, st_ref, n_steps, store=False)

    @pl.when(c == pl.num_programs(0) - 1)
    def _():
        e_ref[...] = st_ref[...]


def _gelu_tanh(x):
    return 0.5 * x * (1.0 + jnp.tanh(math.sqrt(2.0 / math.pi) * (x + 0.044715 * (x * x * x))))


def _s5_main_kernel(u_ref, wb_ref, a_ref, ends_ref, apow_ref, wc_ref, d_ref, wg_ref, bg_ref, o_ref,
                    bu_ref, st_ref, *, n_steps, segs):
    c = pl.program_id(0)
    s_tot = a_ref.shape[1] // 2

    @pl.when(c == 0)
    def _():
        pr = apow_ref[0:1, :s_tot]
        pi = apow_ref[0:1, s_tot:]
        n_batch = SSM_SEQS // segs
        for b in range(n_batch):
            hre = jnp.zeros((1, s_tot), F32)
            him = jnp.zeros((1, s_tot), F32)
            for k in range(segs):
                row = b * segs + k
                st_ref[row:row + 1, :s_tot] = hre
                st_ref[row:row + 1, s_tot:] = him
                ere = ends_ref[row:row + 1, :s_tot]
                eim = ends_ref[row:row + 1, s_tot:]
                hre, him = pr * hre - pi * him + ere, pr * him + pi * hre + eim

    u = u_ref[...]
    bu_ref[...] = jnp.dot(u.astype(BF16), wb_ref[...], preferred_element_type=F32)
    _s5_scan_chunk(bu_ref, a_ref, st_ref, n_steps, store=True)
    y = jnp.dot(bu_ref[...].astype(BF16), wc_ref[...], preferred_element_type=F32) + d_ref[...] * u
    z = _gelu_tanh(y)
    gate = jnp.dot(z.astype(BF16), wg_ref[...], preferred_element_type=F32) + bg_ref[...]
    o_ref[...] = z * jax.nn.sigmoid(gate)


def _s5_mixer(proj, lam_re, lam_im, b_re, b_im, c_re, c_im, d_skip, log_dt, w_glu, b_glu, bsz, seq):
    n = proj.shape[0]
    g, p, nn = SSM_GROUPS, SSM_STATE, SSM_GROUP
    s_tot = g * p
    segs = SSM_SEQS // bsz
    tseg = seq // segs
    ch = min(64, tseg)
    rows = ch * SSM_SEQS
    n_chunks = tseg // ch

    dt = jnp.exp(log_dt.astype(F32))[:, None]
    lam = lax.complex(lam_re.astype(F32), lam_im.astype(F32))
    a_bar = jnp.exp(lam * dt)
    b_bar = ((a_bar - 1.0) / lam)[:, :, None] * lax.complex(b_re.astype(F32), b_im.astype(F32))
    a_pow = jnp.exp(lam * dt * tseg)
    a_vec = jnp.concatenate([jnp.real(a_bar).reshape(1, s_tot), jnp.imag(a_bar).reshape(1, s_tot)], axis=1)
    apow_vec = jnp.concatenate([jnp.real(a_pow).reshape(1, s_tot), jnp.imag(a_pow).reshape(1, s_tot)], axis=1)
    eye = jnp.eye(g, dtype=F32)
    wb = jnp.concatenate(
        [jnp.einsum('gpn,gh->gnhp', jnp.real(b_bar), eye).reshape(g * nn, s_tot),
         jnp.einsum('gpn,gh->gnhp', jnp.imag(b_bar), eye).reshape(g * nn, s_tot)], axis=1).astype(BF16)
    wc = jnp.concatenate(
        [jnp.einsum('gnp,gh->gphn', c_re.astype(F32), eye).reshape(s_tot, g * nn),
         -jnp.einsum('gnp,gh->gphn', c_im.astype(F32), eye).reshape(s_tot, g * nn)], axis=0).astype(BF16)

    u = proj[:, COL_SSM:COL_SSM + SSM_WIDTH].reshape(bsz, segs, tseg, SSM_WIDTH)
    u_perm = jnp.transpose(u, (2, 0, 1, 3)).reshape(n, SSM_WIDTH)

    ends = pl.pallas_call(
        functools.partial(_s5_ends_kernel, n_steps=ch),
        out_shape=jax.ShapeDtypeStruct((SSM_SEQS, 2 * s_tot), F32),
        grid=(n_chunks,),
        in_specs=[pl.BlockSpec((rows, SSM_WIDTH), lambda c: (c, 0)),
                  _const_spec(wb.shape), _const_spec(a_vec.shape)],
        out_specs=_const_spec((SSM_SEQS, 2 * s_tot)),
        scratch_shapes=[pltpu.VMEM((rows, 2 * s_tot), F32), pltpu.VMEM((SSM_SEQS, 2 * s_tot), F32)],
        compiler_params=_cparams(("arbitrary",)),
        name="s5_segment_ends",
    )(u_perm, wb, a_vec)

    out_perm = pl.pallas_call(
        functools.partial(_s5_main_kernel, n_steps=ch, segs=segs),
        out_shape=jax.ShapeDtypeStruct((n, SSM_WIDTH), F32),
        grid=(n_chunks,),
        in_specs=[pl.BlockSpec((rows, SSM_WIDTH), lambda c: (c, 0)),
                  _const_spec(wb.shape), _const_spec(a_vec.shape),
                  _const_spec((SSM_SEQS, 2 * s_tot)), _const_spec(apow_vec.shape),
                  _const_spec(wc.shape), _const_spec((1, SSM_WIDTH)),
                  _const_spec((SSM_WIDTH, SSM_WIDTH)), _const_spec((1, SSM_WIDTH))],
        out_specs=pl.BlockSpec((rows, SSM_WIDTH), lambda c: (c, 0)),
        scratch_shapes=[pltpu.VMEM((rows, 2 * s_tot), F32), pltpu.VMEM((SSM_SEQS, 2 * s_tot), F32)],
        compiler_params=_cparams(("arbitrary",)),
        name="s5_scan_glu",
    )(u_perm, wb, a_vec, ends, apow_vec, wc, d_skip.reshape(1, SSM_WIDTH).astype(F32),
      w_glu.astype(BF16), b_glu.reshape(1, SSM_WIDTH))

    out = out_perm.reshape(tseg, bsz, segs, SSM_WIDTH)
    return jnp.transpose(out, (1, 2, 0, 3)).reshape(n, SSM_WIDTH)


def _mem_attn_kernel(q_ref, kv_ref, o_ref):
    hw = MEM_HEADS * MEM_HEAD_DIM
    q = q_ref[...].astype(BF16)
    kv = kv_ref[...].astype(BF16)
    for h in range(MEM_HEADS):
        sl = slice(h * MEM_HEAD_DIM, (h + 1) * MEM_HEAD_DIM)
        k = kv[:, sl]
        v = kv[:, hw + h * MEM_HEAD_DIM:hw + (h + 1) * MEM_HEAD_DIM]
        lg = lax.dot_general(q[:, sl], k, (((1,), (1,)), ((), ())),
                             preferred_element_type=F32) * (MEM_HEAD_DIM ** -0.5)
        m = jnp.max(lg, axis=1, keepdims=True)
        p = jnp.exp(lg - m)
        p = p / jnp.sum(p, axis=1, keepdims=True)
        o_ref[:, sl] = jnp.dot(p.astype(BF16), v, preferred_element_type=F32)


def _memory_attention(proj, kvm, bsz, seq, mlen, tm=512):
    n = proj.shape[0]
    tm = min(tm, seq)
    nb = seq // tm
    hw = MEM_HEADS * MEM_HEAD_DIM
    return pl.pallas_call(
        _mem_attn_kernel,
        out_shape=jax.ShapeDtypeStruct((n, hw), F32),
        grid=(bsz, nb),
        in_specs=[pl.BlockSpec((tm, hw), lambda b, i: (b * nb + i, COL_MEMQ // hw)),
                  pl.BlockSpec((mlen, 2 * hw), lambda b, i: (b, 0))],
        out_specs=pl.BlockSpec((tm, hw), lambda b, i: (b * nb + i, 0)),
        compiler_params=_cparams(("parallel", "parallel")),
        name="memory_attention",
    )(proj, kvm)


def _merge_kernel(att_ref, cnv_ref, ssm_ref, mem_ref, gl_ref, h_ref, wbr_ref, wo_ref, g_ref, b_ref,
                  wr_ref, br_ref, ltri_ref, h1_ref, te_ref, tg_ref, cnt_ref, base_ref, *, alpha):
    d = h_ref.shape[1]

    @pl.when(pl.program_id(0) == 0)
    def _():
        base_ref[...] = jnp.zeros(base_ref.shape, F32)

    merged = jnp.zeros(h_ref.shape, F32)
    for r, br in enumerate((att_ref, cnv_ref, ssm_ref, mem_ref)):
        y = jnp.dot(br[...].astype(BF16), wbr_ref[r], preferred_element_type=F32)
        merged = merged + y * jax.nn.sigmoid(gl_ref[:, r * d:(r + 1) * d])
    y = alpha * h_ref[...] + jnp.dot(merged.astype(BF16), wo_ref[...], preferred_element_type=F32)
    h1 = _layer_norm(y, g_ref[...], b_ref[...])
    h1_ref[...] = h1

    logits = jnp.dot(h1, wr_ref[...], preferred_element_type=F32, precision=lax.Precision.HIGHEST) + br_ref[...]
    tm = logits.shape[0]
    lane = lax.broadcasted_iota(I32, (tm, N_EXPERTS), 1)
    out_lane = lax.broadcasted_iota(I32, (tm, LANES), 1)
    work = logits
    top_e = jnp.zeros((tm, LANES), I32)
    top_v = jnp.zeros((tm, LANES), F32)
    vals = []
    onehots = []
    for k in range(TOP_K):
        mx = jnp.max(work, axis=1, keepdims=True)
        idx = jnp.min(jnp.where(work == mx, lane, N_EXPERTS), axis=1, keepdims=True)
        hit = lane == idx
        work = jnp.where(hit, -jnp.inf, work)
        top_e = jnp.where(out_lane == k, idx, top_e)
        vals.append(mx)
        onehots.append(jnp.where(hit, 1.0, 0.0))
    den = sum(jnp.exp(vk - vals[0]) for vk in vals)
    for k in range(TOP_K):
        top_v = jnp.where(out_lane == k, jnp.exp(vals[k] - vals[0]) / den, top_v)

    per_tok = onehots[0] + onehots[1] + onehots[2] + onehots[3]
    before = jnp.dot(ltri_ref[...], per_tok.astype(BF16), preferred_element_type=F32) + base_ref[...]
    for k in range(TOP_K):
        rank = jnp.sum(onehots[k] * before, axis=1, keepdims=True)
        top_e = jnp.where(out_lane == TOP_K + k, rank.astype(I32), top_e)
    base_ref[...] = base_ref[...] + jnp.sum(per_tok, axis=0, keepdims=True)
    cnt_ref[...] = base_ref[...]
    te_ref[...] = top_e
    tg_ref[...] = top_v


def _merge_router(att, cnv, ssm, mem_o, proj, h, w_branch, w_o, ln_g, ln_b, w_router, b_router, alpha, tm=256):
    n, d = h.shape
    bw = BRANCH_WIDTH
    row = lambda w: pl.BlockSpec((tm, w), lambda i: (i, 0))
    ltri = jnp.asarray(np.tril(np.ones((tm, tm), np.float32), -1), BF16)
    return pl.pallas_call(
        functools.partial(_merge_kernel, alpha=alpha),
        out_shape=(jax.ShapeDtypeStruct((n, d), F32),
                   jax.ShapeDtypeStruct((n, LANES), I32),
                   jax.ShapeDtypeStruct((n, LANES), F32),
                   jax.ShapeDtypeStruct((1, N_EXPERTS), F32)),
        grid=(n // tm,),
        in_specs=[row(bw), row(bw), row(bw), row(bw),
                  pl.BlockSpec((tm, N_BRANCH * d), lambda i: (i, COL_GATES // (N_BRANCH * d))),
                  row(d),
                  _const_spec((N_BRANCH, bw, d)), _const_spec((d, d)), _const_spec((1, d)), _const_spec((1, d)),
                  _const_spec((d, N_EXPERTS)), _const_spec((1, N_EXPERTS)), _const_spec((tm, tm))],
        out_specs=(row(d), row(LANES), row(LANES), _const_spec((1, N_EXPERTS))),
        scratch_shapes=[pltpu.VMEM((1, N_EXPERTS), F32)],
        compiler_params=_cparams(("arbitrary",)),
        name="merge_router",
    )(att, cnv, ssm, mem_o, proj, h, w_branch.astype(BF16), w_o.astype(BF16),
      ln_g.reshape(1, d), ln_b.reshape(1, d), w_router, b_router.reshape(1, N_EXPERTS), ltri)


DISPATCH_UNROLL = 4


def _dispatch_kernel(dest_ref, h_ref, xs_in_hbm, xs_hbm, sem, *, tm):
    del xs_in_hbm

    def body(r, carry):
        for k in range(TOP_K):
            pltpu.make_async_copy(h_ref.at[pl.ds(r, 1), :],
                                  xs_hbm.at[pl.ds(dest_ref[0, r * TOP_K + k], 1), :], sem).start()
        return carry

    lax.fori_loop(0, tm, body, 0, unroll=DISPATCH_UNROLL)
    pltpu.make_async_copy(xs_hbm.at[pl.ds(0, tm * TOP_K), :], xs_hbm.at[pl.ds(0, tm * TOP_K), :], sem).wait()


def _dispatch(h1, dest, xs_prev, tm=256):
    n, d = h1.shape
    nb = n // tm
    return pl.pallas_call(
        functools.partial(_dispatch_kernel, tm=tm),
        out_shape=jax.ShapeDtypeStruct(xs_prev.shape, F32),
        grid=(nb,),
        in_specs=[pl.BlockSpec((None, 1, tm * TOP_K), lambda i: (i, 0, 0), memory_space=pltpu.SMEM),
                  pl.BlockSpec((tm, d), lambda i: (i, 0)),
                  pl.BlockSpec(memory_space=pl.ANY)],
        out_specs=pl.BlockSpec(memory_space=pl.ANY),
        scratch_shapes=[pltpu.SemaphoreType.DMA(())],
        input_output_aliases={2: 0},
        compiler_params=_cparams(("arbitrary",)),
        name="moe_dispatch",
    )(dest.reshape(nb, 1, tm * TOP_K), h1, xs_prev)


def _expert_kernel(meta_ref, x_ref, wu_ref, bu_ref, wd_ref, bd_ref, sel_ref, o_ref, wu_bf, wd_bf):
    j = pl.program_id(0)
    n_used = meta_ref[0]
    f2 = wu_ref.shape[1]

    @pl.when((j < n_used) & ((j == 0) | (meta_ref[1 + j] != meta_ref[jnp.maximum(j, 1)])))
    def _():
        wu_bf[...] = wu_ref[...].astype(BF16)
        wd_bf[...] = wd_ref[...].astype(BF16)

    @pl.when(j < n_used)
    def _():
        xb = x_ref[...].astype(BF16)
        hdn = jnp.dot(xb, wu_bf[...], preferred_element_type=F32) + bu_ref[...]
        nxt = pltpu.roll(hdn, f2 - 1, 1)
        h_glu = jnp.minimum(hdn, SWIGLU_LIMIT)
        h_lin = jnp.clip(nxt, -SWIGLU_LIMIT, SWIGLU_LIMIT)
        act = (h_glu * jax.nn.sigmoid(SWIGLU_ALPHA * h_glu) * (h_lin + 1.0)).astype(BF16)
        sel = sel_ref[...]
        cw = sel.shape[0]
        parts = [jnp.dot(act[:, c * cw:(c + 1) * cw], sel, preferred_element_type=F32).astype(BF16)
                 for c in range(f2 // cw)]
        act_c = jnp.concatenate(parts, axis=1)
        o_ref[...] = jnp.dot(act_c, wd_bf[...], preferred_element_type=F32) + bd_ref[...]

    @pl.when(j >= n_used)
    def _():
        o_ref[...] = jnp.zeros(o_ref.shape, F32)


def _expert_ffn(xs, blk_expert, n_used, layer, w_up, b_up, w_down, b_down):
    n_rows, d = xs.shape
    n_blocks = n_rows // EXPERT_BLOCK
    f = D_EXPERT
    meta = jnp.concatenate([n_used.reshape(1).astype(I32), blk_expert.astype(I32)])
    sel_np = np.zeros((2 * LANES, LANES), np.float32)
    sel_np[2 * np.arange(LANES), np.arange(LANES)] = 1.0
    sel = jnp.asarray(sel_np, BF16)
    e_of = lambda j, m: m[1 + j]
    grid_spec = pltpu.PrefetchScalarGridSpec(
        num_scalar_prefetch=1,
        grid=(n_blocks,),
        in_specs=[
            pl.BlockSpec((EXPERT_BLOCK, d), lambda j, m: (j, 0)),
            pl.BlockSpec((None, None, d, 2 * f), lambda j, m: (layer, e_of(j, m), 0, 0)),
            pl.BlockSpec((None, None, 1, 2 * f), lambda j, m: (layer, e_of(j, m), 0, 0)),
            pl.BlockSpec((None, None, f, d), lambda j, m: (layer, e_of(j, m), 0, 0)),
            pl.BlockSpec((None, None, 1, d), lambda j, m: (layer, e_of(j, m), 0, 0)),
            pl.BlockSpec(sel.shape, lambda j, m: (0, 0)),
        ],
        out_specs=pl.BlockSpec((EXPERT_BLOCK, d), lambda j, m: (j, 0)),
        scratch_shapes=[pltpu.VMEM((d, 2 * f), BF16), pltpu.VMEM((f, d), BF16)],
    )
    return pl.pallas_call(
        _expert_kernel,
        out_shape=jax.ShapeDtypeStruct((n_rows, d), F32),
        grid_spec=grid_spec,
        compiler_params=_cparams(("arbitrary",)),
        name="moe_experts",
    )(meta, xs, w_up, b_up[:, :, None, :], w_down, b_down[:, :, None, :], sel)


def _combine_kernel(cur_ref, nxt_ref, ys_hbm, g4_ref, h_ref, lg_ref, lb_ref, o_ref, ybuf, sem, *, alpha, tm):
    j = pl.program_id(0)
    nblk = pl.num_programs(0)
    slot = j % 2
    n_rows = tm * TOP_K

    def gather(idx_ref, s):
        def body(r, carry):
            pltpu.make_async_copy(ys_hbm.at[pl.ds(idx_ref[0, r], 1), :],
                                  ybuf.at[s, pl.ds(r, 1), :], sem.at[s]).start()
            return carry
        lax.fori_loop(0, n_rows, body, 0, unroll=8)

    @pl.when(j == 0)
    def _():
        gather(cur_ref, 0)

    @pl.when(j + 1 < nblk)
    def _():
        gather(nxt_ref, 1 - slot)

    pltpu.make_async_copy(ys_hbm.at[pl.ds(0, n_rows), :], ybuf.at[slot], sem.at[slot]).wait()
    g4 = g4_ref[...]
    ffn = jnp.zeros((tm, h_ref.shape[1]), F32)
    for k in range(TOP_K):
        ffn = ffn + ybuf[slot, k * tm:(k + 1) * tm, :] * g4[:, k:k + 1]
    o_ref[...] = _layer_norm(alpha * h_ref[...] + ffn, lg_ref[...], lb_ref[...])


def _combine(ys, pos_km, gates_pad, h1, ln_g, ln_b, alpha, tm=128):
    n, d = h1.shape
    nb = n // tm
    idx3 = pos_km.reshape(nb, 1, tm * TOP_K)
    return pl.pallas_call(
        functools.partial(_combine_kernel, alpha=alpha, tm=tm),
        out_shape=jax.ShapeDtypeStruct((n, d), F32),
        grid=(nb,),
        in_specs=[
            pl.BlockSpec((None, 1, tm * TOP_K), lambda j: (j, 0, 0), memory_space=pltpu.SMEM),
            pl.BlockSpec((None, 1, tm * TOP_K), lambda j: (jnp.minimum(j + 1, nb - 1), 0, 0),
                         memory_space=pltpu.SMEM),
            pl.BlockSpec(memory_space=pl.ANY),
            pl.BlockSpec((tm, LANES), lambda j: (j, 0)),
            pl.BlockSpec((tm, d), lambda j: (j, 0)),
            _const_spec((1, d)), _const_spec((1, d)),
        ],
        out_specs=pl.BlockSpec((tm, d), lambda j: (j, 0)),
        scratch_shapes=[pltpu.VMEM((2, tm * TOP_K, d), F32), pltpu.SemaphoreType.DMA((2,))],
        compiler_params=_cparams(("arbitrary",)),
        name="moe_combine",
    )(idx3, idx3, ys, gates_pad, h1, ln_g.reshape(1, d), ln_b.reshape(1, d))


def _moe_routing(top_e, rank, counts, n_tok, tm_combine):
    n_asg = n_tok * TOP_K
    counts = counts.reshape(N_EXPERTS).astype(I32)
    nblk_per = (counts + EXPERT_BLOCK - 1) // EXPERT_BLOCK
    blk_end = jnp.cumsum(nblk_per)
    pstarts = (blk_end - nblk_per) * EXPERT_BLOCK
    dest = jnp.take(pstarts, top_e, axis=0) + rank
    n_blocks = -(-n_asg // EXPERT_BLOCK) + N_EXPERTS
    blk_expert = jnp.minimum(jnp.searchsorted(blk_end, jnp.arange(n_blocks), side='right'), N_EXPERTS - 1)
    n_used = blk_end[-1]
    nb = n_tok // tm_combine
    pos_km = jnp.transpose(dest.reshape(nb, tm_combine, TOP_K), (0, 2, 1)).reshape(nb, TOP_K * tm_combine)
    return dest.astype(I32), blk_expert.astype(I32), n_used.astype(I32), pos_km.astype(I32)


def _permute_w_in(w):
    sizes = (N_HEADS * HEAD_DIM, KV_LATENT, IDX_HEADS * IDX_DIM, IDX_DIM, IDX_HEADS,
             CONV_WIDTH, CONV_WIDTH, CONV_WIDTH, SSM_WIDTH, MEM_HEADS * MEM_HEAD_DIM)
    offs = np.cumsum((0,) + sizes)
    q, ckv, qidx, kidx, widx, cu, cgb, cgc, ssm, memq = [w[:, offs[k]:offs[k + 1]] for k in range(len(sizes))]
    gates = w[:, offs[-1]:]
    pad = jnp.zeros((w.shape[0], LANES - IDX_DIM - IDX_HEADS), w.dtype)
    return jnp.concatenate([gates, q, cu, cgb, cgc, ssm, memq, qidx, ckv, kidx, widx, pad], axis=1).astype(BF16)


def kernel(x, mem, ln_in_g, ln_in_b, w_in, kv_norm_g, w_uk, w_uv, conv_w, conv_b, lam_re, lam_im, b_re, b_im, c_re, c_im, d_skip, log_dt, w_glu, b_glu, w_mem_kv, w_branch, w_o, ln1_g, ln1_b, w_router, b_router, w_up, b_up, w_down, b_down, ln2_g, ln2_b):
    bsz, seq, d = x.shape
    depth = w_in.shape[0]
    mlen = mem.shape[1]
    n = bsz * seq
    alpha = float((2 * depth) ** 0.25)
    tm_combine = 128

    h = _ln_in(x.reshape(n, d), ln_in_g, ln_in_b)
    n_row_blocks = -(-n * TOP_K // EXPERT_BLOCK) + N_EXPERTS
    xs = jnp.zeros((n_row_blocks * EXPERT_BLOCK, d), F32)
    mem2 = mem.reshape(bsz * mlen, d)
    for l in range(depth):
        proj = _matmul(h, _permute_w_in(w_in[l]), 256, "in_proj")
        qcat, kcat, ckvn, ckvt = _prep(proj, kv_norm_g[l], bsz, seq)
        att = _dsa_attention(proj, qcat, kcat, ckvn, ckvt, w_uk[l], w_uv[l], bsz, seq)
        cnv = _short_conv(proj, conv_w[l], conv_b[l], bsz, seq)
        ssm = _s5_mixer(proj, lam_re[l], lam_im[l], b_re[l], b_im[l], c_re[l], c_im[l], d_skip[l], log_dt[l],
                        w_glu[l], b_glu[l], bsz, seq)
        kvm = _matmul(mem2, w_mem_kv[l].astype(BF16), min(256, bsz * mlen), "mem_kv")
        mem_o = _memory_attention(proj, kvm, bsz, seq, mlen)
        h1, te_pad, tg_pad, counts = _merge_router(att, cnv, ssm, mem_o, proj, h, w_branch[l], w_o[l],
                                                   ln1_g[l], ln1_b[l], w_router[l], b_router[l], alpha)
        dest, blk_expert, n_used, pos_km = _moe_routing(te_pad[:, :TOP_K], te_pad[:, TOP_K:2 * TOP_K], counts,
                                                        n, tm_combine)
        xs = _dispatch(h1, dest, xs)
        ys = _expert_ffn(xs, blk_expert, n_used, l, w_up, b_up, w_down, b_down)
        h = _combine(ys, pos_km, tg_pad, h1, ln2_g[l], ln2_b[l], alpha, tm_combine)
    return h.reshape(bsz, seq, d)
```

```python
import functools
import math

import numpy as np
import jax
import jax.numpy as jnp
from jax import lax
from jax.experimental import pallas as pl
from jax.experimental.pallas import tpu as pltpu

F32 = jnp.float32
BF16 = jnp.bfloat16
I32 = jnp.int32

N_HEADS = 8
HEAD_DIM = 64
KV_LATENT = 128
IDX_HEADS = 8
IDX_DIM = 32
INDEX_TOPK = 256
DSA_QUERY_BLOCK = 256
CONV_WIDTH = 512
SSM_WIDTH = 512
SSM_GROUP = 16
SSM_GROUPS = SSM_WIDTH // SSM_GROUP
SSM_STATE = 64
MEM_HEADS = 4
MEM_HEAD_DIM = 128
N_BRANCH = 4
BRANCH_WIDTH = 512
N_EXPERTS = 32
TOP_K = 4
D_EXPERT = 1024
SWIGLU_LIMIT = 7.0
SWIGLU_ALPHA = 1.702
EXPERT_BLOCK = 256
LN_EPS = 1e-5

LANES = 128
SUBLANES = 8
VMEM_LIMIT_BYTES = 56 * 1024 * 1024

INT_MIN = -(2 ** 31)
NEG_BIG = -1e30
TINY_NORMAL = 1.1754943508222875e-38

COL_GATES = 0
COL_Q = 4096
COL_CONV_U = 4608
COL_CONV_GB = 5120
COL_CONV_GC = 5632
COL_SSM = 6144
COL_MEMQ = 6656
COL_QIDX = 7168
COL_CKV = 7424
COL_TAIL = 7552
D_PROJ = 7680
TAIL_W_OFF = IDX_DIM

SSM_SEQS = 8


def _cparams(sem):
    return pltpu.CompilerParams(dimension_semantics=sem, vmem_limit_bytes=VMEM_LIMIT_BYTES)


def _layer_norm(x, g, b):
    mu = jnp.mean(x, axis=-1, keepdims=True)
    xc = x - mu
    var = jnp.mean(xc * xc, axis=-1, keepdims=True)
    return xc * lax.rsqrt(var + LN_EPS) * g + b


def _const_spec(shape):
    nd = len(shape)
    return pl.BlockSpec(shape, lambda *_: (0,) * nd)


def _ln_kernel(x_ref, g_ref, b_ref, o_ref):
    o_ref[...] = _layer_norm(x_ref[...], g_ref[...], b_ref[...])


def _ln_in(x2, g, b, tm=512):
    n, d = x2.shape
    return pl.pallas_call(
        _ln_kernel,
        out_shape=jax.ShapeDtypeStruct((n, d), F32),
        grid=(n // tm,),
        in_specs=[pl.BlockSpec((tm, d), lambda i: (i, 0)), _const_spec((1, d)), _const_spec((1, d))],
        out_specs=pl.BlockSpec((tm, d), lambda i: (i, 0)),
        compiler_params=_cparams(("parallel",)),
        name="ln_in",
    )(x2, g.reshape(1, d), b.reshape(1, d))


def _matmul_kernel(x_ref, w_ref, o_ref):
    o_ref[...] = jnp.dot(x_ref[...].astype(BF16), w_ref[...], preferred_element_type=F32)


def _matmul(x, w_bf16, tm, name):
    n, k = x.shape
    m = w_bf16.shape[1]
    return pl.pallas_call(
        _matmul_kernel,
        out_shape=jax.ShapeDtypeStruct((n, m), F32),
        grid=(n // tm,),
        in_specs=[pl.BlockSpec((tm, k), lambda i: (i, 0)),
                  pl.BlockSpec((k, m), lambda i: (0, 0), pipeline_mode=pl.Buffered(1))],
        out_specs=pl.BlockSpec((tm, m), lambda i: (i, 0)),
        compiler_params=_cparams(("parallel",)),
        name=name,
    )(x, w_bf16)


def _split_hi_lo(x):
    hi = x.astype(BF16)
    lo = (x - hi.astype(F32)).astype(BF16)
    return hi, lo


def _prep_kernel(qidx_ref, ckv_ref, tail_ref, g_ref, sq_ref, sk_ref, qcat_ref, kcat_ref, ckvn_ref, ckvt_ref):
    q_hi, q_lo = _split_hi_lo(qidx_ref[...])
    qcat = jnp.dot(jnp.concatenate([q_hi, q_lo], axis=1), sq_ref[...], preferred_element_type=F32)
    qcat_ref[...] = qcat.astype(BF16)
    k_hi, k_lo = _split_hi_lo(tail_ref[...])
    kcat = jnp.dot(jnp.concatenate([k_hi, k_lo], axis=1), sk_ref[...], preferred_element_type=F32)
    kcat_ref[...] = kcat.astype(BF16)
    c = ckv_ref[...]
    ms = jnp.mean(c * c, axis=-1, keepdims=True)
    cn = c * lax.rsqrt(ms + LN_EPS) * g_ref[...]
    ckvn_ref[...] = cn.astype(BF16)
    ckvt_ref[...] = cn.T.astype(BF16)


def _selection_matrices():
    sq = np.zeros((2 * IDX_HEADS * IDX_DIM, IDX_HEADS * LANES), np.float32)
    for h in range(IDX_HEADS):
        for d in range(IDX_DIM):
            hi_in = h * IDX_DIM + d
            lo_in = IDX_HEADS * IDX_DIM + hi_in
            sq[hi_in, h * LANES + d] = 1.0
            sq[hi_in, h * LANES + IDX_DIM + d] = 1.0
            sq[lo_in, h * LANES + 2 * IDX_DIM + d] = 1.0
    sk = np.zeros((2 * LANES, LANES), np.float32)
    for d in range(IDX_DIM):
        sk[d, d] = 1.0
        sk[LANES + d, IDX_DIM + d] = 1.0
        sk[d, 2 * IDX_DIM + d] = 1.0
    return jnp.asarray(sq, BF16), jnp.asarray(sk, BF16)


def _prep(proj, kv_norm_g, bsz, seq, tm=512):
    n = proj.shape[0]
    tm = min(tm, seq)
    nbt = seq // tm
    sq, sk = _selection_matrices()
    return pl.pallas_call(
        _prep_kernel,
        out_shape=(jax.ShapeDtypeStruct((n, IDX_HEADS * LANES), BF16),
                   jax.ShapeDtypeStruct((n, LANES), BF16),
                   jax.ShapeDtypeStruct((n, KV_LATENT), BF16),
                   jax.ShapeDtypeStruct((bsz, KV_LATENT, seq), BF16)),
        grid=(n // tm,),
        in_specs=[pl.BlockSpec((tm, 256), lambda i: (i, COL_QIDX // 256)),
                  pl.BlockSpec((tm, 128), lambda i: (i, COL_CKV // 128)),
                  pl.BlockSpec((tm, 128), lambda i: (i, COL_TAIL // 128)),
                  _const_spec((1, KV_LATENT)), _const_spec(sq.shape), _const_spec(sk.shape)],
        out_specs=(pl.BlockSpec((tm, IDX_HEADS * LANES), lambda i: (i, 0)),
                   pl.BlockSpec((tm, LANES), lambda i: (i, 0)),
                   pl.BlockSpec((tm, KV_LATENT), lambda i: (i, 0)),
                   pl.BlockSpec((None, KV_LATENT, tm), lambda i: (i // nbt, 0, i % nbt))),
        compiler_params=_cparams(("parallel",)),
        name="dsa_prep",
    )(proj, proj, proj, kv_norm_g.reshape(1, KV_LATENT), sq, sk)


def _dsa_kernel(q_ref, tail_ref, qcat_ref, kcat_ref, ckv_ref, ckvt_ref, wuk_ref, wuv_ref, o_ref,
                key_ref, khi_ref, qlat_ref, m_ref, l_ref, acc_ref, *, seq, topk, cks, ck, cka, qb):
    i = pl.program_id(1)
    q0 = i * qb
    nchs = (q0 + qb + cks - 1) // cks
    nch = nchs * (cks // ck)
    q_pos = q0 + lax.broadcasted_iota(I32, (1, qb), 1)
    row_pos_s = lax.broadcasted_iota(I32, (cks, 1), 0)
    row_pos = lax.broadcasted_iota(I32, (ck, 1), 0)
    w_t = tail_ref[...].T
    nt_dims = (((1,), (1,)), ((), ()))
    cnt_rows = min(64, ck)

    def score_chunk(c, carry):
        k0 = pl.multiple_of(c * cks, cks)
        kc = kcat_ref[pl.ds(k0, cks), :]
        s = jnp.zeros((cks, qb), F32)
        for h in range(IDX_HEADS):
            d = lax.dot_general(kc, qcat_ref[:, h * LANES:(h + 1) * LANES], nt_dims, preferred_element_type=F32)
            s = s + jnp.maximum(d, 0.0) * w_t[TAIL_W_OFF + h:TAIL_W_OFF + h + 1, :]
        s = jnp.where(jnp.abs(s) < TINY_NORMAL, 0.0, s)
        bits = pltpu.bitcast(s, I32)
        key = bits ^ ((bits >> 31) & jnp.int32(0x7FFFFFFF))
        causal = k0 + row_pos_s <= q_pos
        key_ref[pl.ds(k0, cks), :] = jnp.where(causal, key, jnp.int32(INT_MIN))
        hi = pltpu.bitcast(bits & jnp.int32(-65536), F32)
        khi_ref[pl.ds(k0, cks), :] = jnp.where(causal, hi, -jnp.inf).astype(BF16)
        return carry

    lax.fori_loop(0, nchs, score_chunk, 0)

    def count_keys(pred_fn):
        def body(c, acc):
            k0 = pl.multiple_of(c * ck, ck)
            hit = jnp.where(pred_fn(key_ref[pl.ds(k0, ck), :], k0), 1.0, 0.0)
            return acc + jnp.sum(hit.reshape(ck // cnt_rows, cnt_rows, qb), axis=0)
        acc = lax.fori_loop(0, nch, body, jnp.zeros((cnt_rows, qb), F32))
        return jnp.sum(acc, axis=0, keepdims=True)

    kf = float(topk)

    def bit_pass(b, v):
        cand = v + jnp.left_shift(jnp.int32(1), 31 - b)
        cnt = count_keys(lambda key, k0: key >= cand)
        return jnp.where(cnt >= kf, cand, v)

    pack = 16
    n_part = ck // (2 * pack)

    def count_hi(cand16):
        cand16 = jnp.where((cand16 >= 1) & (cand16 < 128), 128, jnp.where((cand16 >= -128) & (cand16 <= 0), 0, cand16))
        pat = cand16 ^ ((cand16 >> 31) & jnp.int32(0x7FFF))
        cand = jnp.broadcast_to(pltpu.bitcast(pat << 16, F32), (pack, qb)).astype(BF16)
        one = jnp.ones((), BF16)
        zero = jnp.zeros((), BF16)

        def body(c, acc):
            k0 = pl.multiple_of(c * ck, ck)
            hit = jnp.where(khi_ref[pl.ds(k0, ck), :].reshape(ck // pack, pack, qb) >= cand[None], one, zero)
            part = hit[0:2]
            for j in range(1, n_part):
                part = part + hit[2 * j:2 * j + 2]
            return acc + part.astype(F32)
        acc = lax.fori_loop(0, nch, body, jnp.zeros((2, pack, qb), F32))
        return jnp.sum(jnp.sum(acc, axis=0), axis=0, keepdims=True)

    def hi_pass(b, v16):
        cand16 = v16 + jnp.left_shift(jnp.int32(1), 15 - b)
        return jnp.where(count_hi(cand16) >= kf, cand16, v16)

    v16 = lax.fori_loop(0, 16, hi_pass, jnp.full((1, qb), -32768, I32))
    v = lax.fori_loop(16, 32, bit_pass, v16 << 16)
    thr = jnp.maximum(v, jnp.int32(INT_MIN + 1))
    c_ge = count_keys(lambda key, k0: key >= thr)
    c_gt = count_keys(lambda key, k0: key > thr)
    need = kf - c_gt

    jlim_ref = m_ref.at[0:1]
    jlim_ref[...] = jnp.full((1, qb), float(seq), F32)
    has_tie = jnp.max(jnp.where(c_ge > kf, 1.0, 0.0)) > 0.0

    @pl.when(has_tie)
    def _():
        n_bits = max(1, int(math.ceil(math.log2(seq))))

        def pos_pass(b, j):
            cand = j + jnp.left_shift(jnp.int32(1), n_bits - 1 - b)
            cnt = count_keys(lambda key, k0: (key == thr) & (k0 + row_pos < cand))
            return jnp.where(cnt < need, cand, j)

        j = lax.fori_loop(0, n_bits, pos_pass, jnp.zeros((1, qb), I32))
        jlim_ref[...] = jnp.where(c_ge > kf, j.astype(F32), float(seq))

    jlim = jlim_ref[...].astype(I32)

    q_bf = q_ref[...].astype(BF16)
    for h in range(N_HEADS):
        ql = jnp.dot(q_bf, wuk_ref[h], preferred_element_type=F32) * (HEAD_DIM ** -0.5 * math.log2(math.e))
        qlat_ref[h] = ql.T.astype(BF16)

    m_ref[...] = jnp.full(m_ref.shape, NEG_BIG, F32)
    l_ref[...] = jnp.zeros(l_ref.shape, F32)
    acc_ref[...] = jnp.zeros(acc_ref.shape, F32)

    row_pos_a = lax.broadcasted_iota(I32, (cka, 1), 0)

    def att_chunk(c, carry):
        for sub in range(cks // cka):
            k0 = pl.multiple_of(c * cks + sub * cka, cka)
            kv = ckv_ref[pl.ds(k0, cka), :]
            kvt = ckvt_ref[:, pl.ds(k0, cka)]
            key = key_ref[pl.ds(k0, cka), :]
            sel = (key > thr) | ((key == thr) & (k0 + row_pos_a <= jlim))
            bias = jnp.where(sel, 0.0, NEG_BIG)
            for h in range(N_HEADS):
                lg = jnp.dot(kv, qlat_ref[h], preferred_element_type=F32) + bias
                m_prev = m_ref[h:h + 1, :]
                m_new = jnp.maximum(m_prev, jnp.max(lg, axis=0, keepdims=True))
                p = jnp.exp2(lg - m_new)
                alpha = jnp.exp2(m_prev - m_new)
                l_ref[h:h + 1, :] = alpha * l_ref[h:h + 1, :] + jnp.sum(p, axis=0, keepdims=True)
                acc_ref[h] = alpha * acc_ref[h] + jnp.dot(kvt, p.astype(BF16), preferred_element_type=F32)
                m_ref[h:h + 1, :] = m_new
        return carry

    lax.fori_loop(0, nchs, att_chunk, 0)

    out = jnp.zeros((qb, N_HEADS * HEAD_DIM), F32)
    for h in range(N_HEADS):
        o_lat = (acc_ref[h] / l_ref[h:h + 1, :]).T
        out = out + jnp.dot(o_lat.astype(BF16), wuv_ref[h], preferred_element_type=F32)
    o_ref[...] = out


def _dsa_attention(proj, qcat, kcat, ckvn, ckvt, w_uk, w_uv, bsz, seq):
    n = proj.shape[0]
    qb = min(DSA_QUERY_BLOCK, seq)
    n_blk = seq // qb
    topk = min(INDEX_TOPK, seq // 4)
    cks = min(512, seq)
    ck = min(256, seq)
    cka = min(128, seq)
    eye = jnp.eye(N_HEADS, dtype=F32)
    wuk = jnp.einsum('chd,hg->hgdc', w_uk, eye).reshape(N_HEADS, N_HEADS * HEAD_DIM, KV_LATENT).astype(BF16)
    wuv = jnp.einsum('chd,hg->hcgd', w_uv, eye).reshape(N_HEADS, KV_LATENT, N_HEADS * HEAD_DIM).astype(BF16)
    kernel = functools.partial(_dsa_kernel, seq=seq, topk=topk, cks=cks, ck=ck, cka=cka, qb=qb)
    return pl.pallas_call(
        kernel,
        out_shape=jax.ShapeDtypeStruct((n, N_HEADS * HEAD_DIM), F32),
        grid=(bsz, n_blk),
        in_specs=[pl.BlockSpec((qb, 512), lambda b, i: (b * n_blk + i, COL_Q // 512)),
                  pl.BlockSpec((qb, 128), lambda b, i: (b * n_blk + i, COL_TAIL // 128)),
                  pl.BlockSpec((qb, IDX_HEADS * LANES), lambda b, i: (b * n_blk + i, 0)),
                  pl.BlockSpec((seq, LANES), lambda b, i: (b, 0)),
                  pl.BlockSpec((seq, KV_LATENT), lambda b, i: (b, 0)),
                  pl.BlockSpec((None, KV_LATENT, seq), lambda b, i: (b, 0, 0)),
                  _const_spec(wuk.shape), _const_spec(wuv.shape)],
        out_specs=pl.BlockSpec((qb, N_HEADS * HEAD_DIM), lambda b, i: (b * n_blk + i, 0)),
        scratch_shapes=[pltpu.VMEM((seq, qb), I32),
                        pltpu.VMEM((seq, qb), BF16),
                        pltpu.VMEM((N_HEADS, KV_LATENT, qb), BF16),
                        pltpu.VMEM((N_HEADS, qb), F32),
                        pltpu.VMEM((N_HEADS, qb), F32),
                        pltpu.VMEM((N_HEADS, KV_LATENT, qb), F32)],
        compiler_params=_cparams(("parallel", "parallel")),
        name="dsa_attention",
    )(proj, proj, qcat, kcat, ckvn, ckvt, wuk, wuv)


def _conv_kernel(u_ref, gb_ref, gc_ref, pu_ref, pgc_ref, w_ref, b_ref, o_ref):
    i = pl.program_id(1)
    tb = u_ref.shape[0]
    v = gc_ref[...] * u_ref[...]
    halo = jnp.where(i > 0, pgc_ref[...] * pu_ref[...], 0.0)
    vfull = jnp.concatenate([halo, v], axis=0)
    v1 = pltpu.roll(vfull, 1, 0)[SUBLANES:SUBLANES + tb]
    v2 = pltpu.roll(vfull, 2, 0)[SUBLANES:SUBLANES + tb]
    w = w_ref[...]
    y = w[0:1] * v2 + w[1:2] * v1 + w[2:3] * v
    o_ref[...] = gb_ref[...] * (y + b_ref[...])


def _short_conv(proj, conv_w, conv_b, bsz, seq, tb=512):
    n = proj.shape[0]
    tb = min(tb, seq)
    nb = seq // tb
    cw = CONV_WIDTH
    blk = lambda col: pl.BlockSpec((tb, cw), lambda b, i: (b * nb + i, col // cw))
    prev = lambda col: pl.BlockSpec(
        (SUBLANES, cw), lambda b, i: (jnp.maximum((b * nb + i) * (tb // SUBLANES) - 1, 0), col // cw))
    return pl.pallas_call(
        _conv_kernel,
        out_shape=jax.ShapeDtypeStruct((n, cw), F32),
        grid=(bsz, nb),
        in_specs=[blk(COL_CONV_U), blk(COL_CONV_GB), blk(COL_CONV_GC), prev(COL_CONV_U), prev(COL_CONV_GC),
                  _const_spec((SUBLANES, cw)), _const_spec((1, cw))],
        out_specs=pl.BlockSpec((tb, cw), lambda b, i: (b * nb + i, 0)),
        compiler_params=_cparams(("parallel", "parallel")),
        name="short_conv",
    )(proj, proj, proj, proj, proj,
      jnp.pad(conv_w, ((0, SUBLANES - conv_w.shape[0]), (0, 0))), conv_b.reshape(1, cw))


def _s5_scan_chunk(bu_ref, a_ref, st_ref, n_steps, store):
    s_tot = a_ref.shape[1] // 2
    cb = 512
    for blk in range(s_tot // cb):
        re_sl = pl.ds(blk * cb, cb)
        im_sl = pl.ds(s_tot + blk * cb, cb)
        ar = jnp.broadcast_to(a_ref[0:1, blk * cb:(blk + 1) * cb], (SSM_SEQS, cb))
        ai = jnp.broadcast_to(a_ref[0:1, s_tot + blk * cb:s_tot + (blk + 1) * cb], (SSM_SEQS, cb))

        def step(t, carry):
            re, im = carry
            r0 = pl.multiple_of(t * SSM_SEQS, SSM_SEQS)
            br = bu_ref[pl.ds(r0, SSM_SEQS), re_sl]
            bi = bu_ref[pl.ds(r0, SSM_SEQS), im_sl]
            nre = ar * re - ai * im + br
            nim = ar * im + ai * re + bi
            if store:
                bu_ref[pl.ds(r0, SSM_SEQS), re_sl] = nre
                bu_ref[pl.ds(r0, SSM_SEQS), im_sl] = nim
            return nre, nim

        re, im = lax.fori_loop(0, n_steps, step, (st_ref[:, re_sl], st_ref[:, im_sl]), unroll=8)
        st_ref[:, re_sl] = re
        st_ref[:, im_sl] = im


def _s5_ends_kernel(u_ref, wb_ref, a_ref, e_ref, bu_ref, st_ref, *, n_steps):
    c = pl.program_id(0)

    @pl.when(c == 0)
    def _():
        st_ref[...] = jnp.zeros(st_ref.shape, F32)

    bu_ref[...] = jnp.dot(u_ref[...].astype(BF16), wb_ref[...], preferred_element_type=F32)
    _s5_scan_chunk(bu_ref, a_ref, st_ref, n_steps, store=False)

    @pl.when(c == pl.num_programs(0) - 1)
    def _():
        e_ref[...] = st_ref[...]


def _gelu_tanh(x):
    return 0.5 * x * (1.0 + jnp.tanh(math.sqrt(2.0 / math.pi) * (x + 0.044715 * (x * x * x))))


def _s5_main_kernel(u_ref, wb_ref, a_ref, ends_ref, apow_ref, wc_ref, d_ref, wg_ref, bg_ref, o_ref,
                    bu_ref, st_ref, *, n_steps, segs):
    c = pl.program_id(0)
    s_tot = a_ref.shape[1] // 2

    @pl.when(c == 0)
    def _():
        pr = apow_ref[0:1, :s_tot]
        pi = apow_ref[0:1, s_tot:]
        n_batch = SSM_SEQS // segs
        for b in range(n_batch):
            hre = jnp.zeros((1, s_tot), F32)
            him = jnp.zeros((1, s_tot), F32)
            for k in range(segs):
                row = b * segs + k
                st_ref[row:row + 1, :s_tot] = hre
                st_ref[row:row + 1, s_tot:] = him
                ere = ends_ref[row:row + 1, :s_tot]
                eim = ends_ref[row:row + 1, s_tot:]
                hre, him = pr * hre - pi * him + ere, pr * him + pi * hre + eim

    u = u_ref[...]
    bu_ref[...] = jnp.dot(u.astype(BF16), wb_ref[...], preferred_element_type=F32)
    _s5_scan_chunk(bu_ref, a_ref, st_ref, n_steps, store=True)
    y = jnp.dot(bu_ref[...].astype(BF16), wc_ref[...], preferred_element_type=F32) + d_ref[...] * u
    z = _gelu_tanh(y)
    gate = jnp.dot(z.astype(BF16), wg_ref[...], preferred_element_type=F32) + bg_ref[...]
    o_ref[...] = z * jax.nn.sigmoid(gate)


def _s5_mixer(proj, lam_re, lam_im, b_re, b_im, c_re, c_im, d_skip, log_dt, w_glu, b_glu, bsz, seq):
    n = proj.shape[0]
    g, p, nn = SSM_GROUPS, SSM_STATE, SSM_GROUP
    s_tot = g * p
    segs = SSM_SEQS // bsz
    tseg = seq // segs
    ch = min(64, tseg)
    rows = ch * SSM_SEQS
    n_chunks = tseg // ch

    dt = jnp.exp(log_dt.astype(F32))[:, None]
    lam = lax.complex(lam_re.astype(F32), lam_im.astype(F32))
    a_bar = jnp.exp(lam * dt)
    b_bar = ((a_bar - 1.0) / lam)[:, :, None] * lax.complex(b_re.astype(F32), b_im.astype(F32))
    a_pow = jnp.exp(lam * dt * tseg)
    a_vec = jnp.concatenate([jnp.real(a_bar).reshape(1, s_tot), jnp.imag(a_bar).reshape(1, s_tot)], axis=1)
    apow_vec = jnp.concatenate([jnp.real(a_pow).reshape(1, s_tot), jnp.imag(a_pow).reshape(1, s_tot)], axis=1)
    eye = jnp.eye(g, dtype=F32)
    wb = jnp.concatenate(
        [jnp.einsum('gpn,gh->gnhp', jnp.real(b_bar), eye).reshape(g * nn, s_tot),
         jnp.einsum('gpn,gh->gnhp', jnp.imag(b_bar), eye).reshape(g * nn, s_tot)], axis=1).astype(BF16)
    wc = jnp.concatenate(
        [jnp.einsum('gnp,gh->gphn', c_re.astype(F32), eye).reshape(s_tot, g * nn),
         -jnp.einsum('gnp,gh->gphn', c_im.astype(F32), eye).reshape(s_tot, g * nn)], axis=0).astype(BF16)

    u = proj[:, COL_SSM:COL_SSM + SSM_WIDTH].reshape(bsz, segs, tseg, SSM_WIDTH)
    u_perm = jnp.transpose(u, (2, 0, 1, 3)).reshape(n, SSM_WIDTH)

    ends = pl.pallas_call(
        functools.partial(_s5_ends_kernel, n_steps=ch),
        out_shape=jax.ShapeDtypeStruct((SSM_SEQS, 2 * s_tot), F32),
        grid=(n_chunks,),
        in_specs=[pl.BlockSpec((rows, SSM_WIDTH), lambda c: (c, 0)),
                  _const_spec(wb.shape), _const_spec(a_vec.shape)],
        out_specs=_const_spec((SSM_SEQS, 2 * s_tot)),
        scratch_shapes=[pltpu.VMEM((rows, 2 * s_tot), F32), pltpu.VMEM((SSM_SEQS, 2 * s_tot), F32)],
        compiler_params=_cparams(("arbitrary",)),
        name="s5_segment_ends",
    )(u_perm, wb, a_vec)

    out_perm = pl.pallas_call(
        functools.partial(_s5_main_kernel, n_steps=ch, segs=segs),
        out_shape=jax.ShapeDtypeStruct((n, SSM_WIDTH), F32),
        grid=(n_chunks,),
        in_specs=[pl.BlockSpec((rows, SSM_WIDTH), lambda c: (c, 0)),
                  _const_spec(wb.shape), _const_spec(a_vec.shape),
                  _const_spec((SSM_SEQS, 2 * s_tot)), _const_spec(apow_vec.shape),
                  _const_spec(wc.shape), _const_spec((1, SSM_WIDTH)),
                  _const_spec((SSM_WIDTH, SSM_WIDTH)), _const_spec((1, SSM_WIDTH))],
        out_specs=pl.BlockSpec((rows, SSM_WIDTH), lambda c: (c, 0)),
        scratch_shapes=[pltpu.VMEM((rows, 2 * s_tot), F32), pltpu.VMEM((SSM_SEQS, 2 * s_tot), F32)],
        compiler_params=_cparams(("arbitrary",)),
        name="s5_scan_glu",
    )(u_perm, wb, a_vec, ends, apow_vec, wc, d_skip.reshape(1, SSM_WIDTH).astype(F32),
      w_glu.astype(BF16), b_glu.reshape(1, SSM_WIDTH))

    out = out_perm.reshape(tseg, bsz, segs, SSM_WIDTH)
    return jnp.transpose(out, (1, 2, 0, 3)).reshape(n, SSM_WIDTH)


def _mem_attn_kernel(q_ref, kv_ref, o_ref):
    hw = MEM_HEADS * MEM_HEAD_DIM
    q = q_ref[...].astype(BF16)
    kv = kv_ref[...].astype(BF16)
    for h in range(MEM_HEADS):
        sl = slice(h * MEM_HEAD_DIM, (h + 1) * MEM_HEAD_DIM)
        k = kv[:, sl]
        v = kv[:, hw + h * MEM_HEAD_DIM:hw + (h + 1) * MEM_HEAD_DIM]
        lg = lax.dot_general(q[:, sl], k, (((1,), (1,)), ((), ())),
                             preferred_element_type=F32) * (MEM_HEAD_DIM ** -0.5)
        m = jnp.max(lg, axis=1, keepdims=True)
        p = jnp.exp(lg - m)
        p = p / jnp.sum(p, axis=1, keepdims=True)
        o_ref[:, sl] = jnp.dot(p.astype(BF16), v, preferred_element_type=F32)


def _memory_attention(proj, kvm, bsz, seq, mlen, tm=512):
    n = proj.shape[0]
    tm = min(tm, seq)
    nb = seq // tm
    hw = MEM_HEADS * MEM_HEAD_DIM
    return pl.pallas_call(
        _mem_attn_kernel,
        out_shape=jax.ShapeDtypeStruct((n, hw), F32),
        grid=(bsz, nb),
        in_specs=[pl.BlockSpec((tm, hw), lambda b, i: (b * nb + i, COL_MEMQ // hw)),
                  pl.BlockSpec((mlen, 2 * hw), lambda b, i: (b, 0))],
        out_specs=pl.BlockSpec((tm, hw), lambda b, i: (b * nb + i, 0)),
        compiler_params=_cparams(("parallel", "parallel")),
        name="memory_attention",
    )(proj, kvm)


def _merge_kernel(att_ref, cnv_ref, ssm_ref, mem_ref, gl_ref, h_ref, wbr_ref, wo_ref, g_ref, b_ref,
                  wr_ref, br_ref, ltri_ref, h1_ref, te_ref, tg_ref, cnt_ref, base_ref, *, alpha):
    d = h_ref.shape[1]

    @pl.when(pl.program_id(0) == 0)
    def _():
        base_ref[...] = jnp.zeros(base_ref.shape, F32)

    merged = jnp.zeros(h_ref.shape, F32)
    for r, br in enumerate((att_ref, cnv_ref, ssm_ref, mem_ref)):
        y = jnp.dot(br[...].astype(BF16), wbr_ref[r], preferred_element_type=F32)
        merged = merged + y * jax.nn.sigmoid(gl_ref[:, r * d:(r + 1) * d])
    y = alpha * h_ref[...] + jnp.dot(merged.astype(BF16), wo_ref[...], preferred_element_type=F32)
    h1 = _layer_norm(y, g_ref[...], b_ref[...])
    h1_ref[...] = h1

    logits = jnp.dot(h1, wr_ref[...], preferred_element_type=F32, precision=lax.Precision.HIGHEST) + br_ref[...]
    tm = logits.shape[0]
    lane = lax.broadcasted_iota(I32, (tm, N_EXPERTS), 1)
    out_lane = lax.broadcasted_iota(I32, (tm, LANES), 1)
    work = logits
    top_e = jnp.zeros((tm, LANES), I32)
    top_v = jnp.zeros((tm, LANES), F32)
    vals = []
    onehots = []
    for k in range(TOP_K):
        mx = jnp.max(work, axis=1, keepdims=True)
        idx = jnp.min(jnp.where(work == mx, lane, N_EXPERTS), axis=1, keepdims=True)
        hit = lane == idx
        work = jnp.where(hit, -jnp.inf, work)
        top_e = jnp.where(out_lane == k, idx, top_e)
        vals.append(mx)
        onehots.append(jnp.where(hit, 1.0, 0.0))
    den = sum(jnp.exp(vk - vals[0]) for vk in vals)
    for k in range(TOP_K):
        top_v = jnp.where(out_lane == k, jnp.exp(vals[k] - vals[0]) / den, top_v)

    per_tok = onehots[0] + onehots[1] + onehots[2] + onehots[3]
    before = jnp.dot(ltri_ref[...], per_tok.astype(BF16), preferred_element_type=F32) + base_ref[...]
    for k in range(TOP_K):
        rank = jnp.sum(onehots[k] * before, axis=1, keepdims=True)
        top_e = jnp.where(out_lane == TOP_K + k, rank.astype(I32), top_e)
    base_ref[...] = base_ref[...] + jnp.sum(per_tok, axis=0, keepdims=True)
    cnt_ref[...] = base_ref[...]
    te_ref[...] = top_e
    tg_ref[...] = top_v


def _merge_router(att, cnv, ssm, mem_o, proj, h, w_branch, w_o, ln_g, ln_b, w_router, b_router, alpha, tm=256):
    n, d = h.shape
    bw = BRANCH_WIDTH
    row = lambda w: pl.BlockSpec((tm, w), lambda i: (i, 0))
    ltri = jnp.asarray(np.tril(np.ones((tm, tm), np.float32), -1), BF16)
    return pl.pallas_call(
        functools.partial(_merge_kernel, alpha=alpha),
        out_shape=(jax.ShapeDtypeStruct((n, d), F32),
                   jax.ShapeDtypeStruct((n, LANES), I32),
                   jax.ShapeDtypeStruct((n, LANES), F32),
                   jax.ShapeDtypeStruct((1, N_EXPERTS), F32)),
        grid=(n // tm,),
        in_specs=[row(bw), row(bw), row(bw), row(bw),
                  pl.BlockSpec((tm, N_BRANCH * d), lambda i: (i, COL_GATES // (N_BRANCH * d))),
                  row(d),
                  _const_spec((N_BRANCH, bw, d)), _const_spec((d, d)), _const_spec((1, d)), _const_spec((1, d)),
                  _const_spec((d, N_EXPERTS)), _const_spec((1, N_EXPERTS)), _const_spec((tm, tm))],
        out_specs=(row(d), row(LANES), row(LANES), _const_spec((1, N_EXPERTS))),
        scratch_shapes=[pltpu.VMEM((1, N_EXPERTS), F32)],
        compiler_params=_cparams(("arbitrary",)),
        name="merge_router",
    )(att, cnv, ssm, mem_o, proj, h, w_branch.astype(BF16), w_o.astype(BF16),
      ln_g.reshape(1, d), ln_b.reshape(1, d), w_router, b_router.reshape(1, N_EXPERTS), ltri)


DISPATCH_UNROLL = 4


def _dispatch_kernel(dest_ref, h_ref, xs_in_hbm, xs_hbm, sem, *, tm):
    del xs_in_hbm

    def body(r, carry):
        for k in range(TOP_K):
            pltpu.make_async_copy(h_ref.at[pl.ds(r, 1), :],
                                  xs_hbm.at[pl.ds(dest_ref[0, r * TOP_K + k], 1), :], sem).start()
        return carry

    lax.fori_loop(0, tm, body, 0, unroll=DISPATCH_UNROLL)
    pltpu.make_async_copy(xs_hbm.at[pl.ds(0, tm * TOP_K), :], xs_hbm.at[pl.ds(0, tm * TOP_K), :], sem).wait()


def _dispatch(h1, dest, xs_prev, tm=256):
    n, d = h1.shape
    nb = n // tm
    return pl.pallas_call(
        functools.partial(_dispatch_kernel, tm=tm),
        out_shape=jax.ShapeDtypeStruct(xs_prev.shape, F32),
        grid=(nb,),
        in_specs=[pl.BlockSpec((None, 1, tm * TOP_K), lambda i: (i, 0, 0), memory_space=pltpu.SMEM),
                  pl.BlockSpec((tm, d), lambda i: (i, 0)),
                  pl.BlockSpec(memory_space=pl.ANY)],
        out_specs=pl.BlockSpec(memory_space=pl.ANY),
        scratch_shapes=[pltpu.SemaphoreType.DMA(())],
        input_output_aliases={2: 0},
        compiler_params=_cparams(("arbitrary",)),
        name="moe_dispatch",
    )(dest.reshape(nb, 1, tm * TOP_K), h1, xs_prev)


def _expert_kernel(meta_ref, x_ref, wu_ref, bu_ref, wd_ref, bd_ref, sel_ref, o_ref, wu_bf, wd_bf):
    j = pl.program_id(0)
    n_used = meta_ref[0]
    f2 = wu_ref.shape[1]

    @pl.when((j < n_used) & ((j == 0) | (meta_ref[1 + j] != meta_ref[jnp.maximum(j, 1)])))
    def _():
        wu_bf[...] = wu_ref[...].astype(BF16)
        wd_bf[...] = wd_ref[...].astype(BF16)

    @pl.when(j < n_used)
    def _():
        xb = x_ref[...].astype(BF16)
        hdn = jnp.dot(xb, wu_bf[...], preferred_element_type=F32) + bu_ref[...]
        nxt = pltpu.roll(hdn, f2 - 1, 1)
        h_glu = jnp.minimum(hdn, SWIGLU_LIMIT)
        h_lin = jnp.clip(nxt, -SWIGLU_LIMIT, SWIGLU_LIMIT)
        act = (h_glu * jax.nn.sigmoid(SWIGLU_ALPHA * h_glu) * (h_lin + 1.0)).astype(BF16)
        sel = sel_ref[...]
        cw = sel.shape[0]
        parts = [jnp.dot(act[:, c * cw:(c + 1) * cw], sel, preferred_element_type=F32).astype(BF16)
                 for c in range(f2 // cw)]
        act_c = jnp.concatenate(parts, axis=1)
        o_ref[...] = jnp.dot(act_c, wd_bf[...], preferred_element_type=F32) + bd_ref[...]

    @pl.when(j >= n_used)
    def _():
        o_ref[...] = jnp.zeros(o_ref.shape, F32)


def _expert_ffn(xs, blk_expert, n_used, layer, w_up, b_up, w_down, b_down):
    n_rows, d = xs.shape
    n_blocks = n_rows // EXPERT_BLOCK
    f = D_EXPERT
    meta = jnp.concatenate([n_used.reshape(1).astype(I32), blk_expert.astype(I32)])
    sel_np = np.zeros((2 * LANES, LANES), np.float32)
    sel_np[2 * np.arange(LANES), np.arange(LANES)] = 1.0
    sel = jnp.asarray(sel_np, BF16)
    e_of = lambda j, m: m[1 + j]
    grid_spec = pltpu.PrefetchScalarGridSpec(
        num_scalar_prefetch=1,
        grid=(n_blocks,),
        in_specs=[
            pl.BlockSpec((EXPERT_BLOCK, d), lambda j, m: (j, 0)),
            pl.BlockSpec((None, None, d, 2 * f), lambda j, m: (layer, e_of(j, m), 0, 0)),
            pl.BlockSpec((None, None, 1, 2 * f), lambda j, m: (layer, e_of(j, m), 0, 0)),
            pl.BlockSpec((None, None, f, d), lambda j, m: (layer, e_of(j, m), 0, 0)),
            pl.BlockSpec((None, None, 1, d), lambda j, m: (layer, e_of(j, m), 0, 0)),
            pl.BlockSpec(sel.shape, lambda j, m: (0, 0)),
        ],
        out_specs=pl.BlockSpec((EXPERT_BLOCK, d), lambda j, m: (j, 0)),
        scratch_shapes=[pltpu.VMEM((d, 2 * f), BF16), pltpu.VMEM((f, d), BF16)],
    )
    return pl.pallas_call(
        _expert_kernel,
        out_shape=jax.ShapeDtypeStruct((n_rows, d), F32),
        grid_spec=grid_spec,
        compiler_params=_cparams(("arbitrary",)),
        name="moe_experts",
    )(meta, xs, w_up, b_up[:, :, None, :], w_down, b_down[:, :, None, :], sel)


def _combine_kernel(cur_ref, nxt_ref, ys_hbm, g4_ref, h_ref, lg_ref, lb_ref, o_ref, ybuf, sem, *, alpha, tm):
    j = pl.program_id(0)
    nblk = pl.num_programs(0)
    slot = j % 2
    n_rows = tm * TOP_K

    def gather(idx_ref, s):
        def body(r, carry):
            pltpu.make_async_copy(ys_hbm.at[pl.ds(idx_ref[0, r], 1), :],
                                  ybuf.at[s, pl.ds(r, 1), :], sem.at[s]).start()
            return carry
        lax.fori_loop(0, n_rows, body, 0, unroll=8)

    @pl.when(j == 0)
    def _():
        gather(cur_ref, 0)

    @pl.when(j + 1 < nblk)
    def _():
        gather(nxt_ref, 1 - slot)

    pltpu.make_async_copy(ys_hbm.at[pl.ds(0, n_rows), :], ybuf.at[slot], sem.at[slot]).wait()
    g4 = g4_ref[...]
    ffn = jnp.zeros((tm, h_ref.shape[1]), F32)
    for k in range(TOP_K):
        ffn = ffn + ybuf[slot, k * tm:(k + 1) * tm, :] * g4[:, k:k + 1]
    o_ref[...] = _layer_norm(alpha * h_ref[...] + ffn, lg_ref[...], lb_ref[...])


def _combine(ys, pos_km, gates_pad, h1, ln_g, ln_b, alpha, tm=128):
    n, d = h1.shape
    nb = n // tm
    idx3 = pos_km.reshape(nb, 1, tm * TOP_K)
    return pl.pallas_call(
        functools.partial(_combine_kernel, alpha=alpha, tm=tm),
        out_shape=jax.ShapeDtypeStruct((n, d), F32),
        grid=(nb,),
        in_specs=[
            pl.BlockSpec((None, 1, tm * TOP_K), lambda j: (j, 0, 0), memory_space=pltpu.SMEM),
            pl.BlockSpec((None, 1, tm * TOP_K), lambda j: (jnp.minimum(j + 1, nb - 1), 0, 0),
                         memory_space=pltpu.SMEM),
            pl.BlockSpec(memory_space=pl.ANY),
            pl.BlockSpec((tm, LANES), lambda j: (j, 0)),
            pl.BlockSpec((tm, d), lambda j: (j, 0)),
            _const_spec((1, d)), _const_spec((1, d)),
        ],
        out_specs=pl.BlockSpec((tm, d), lambda j: (j, 0)),
        scratch_shapes=[pltpu.VMEM((2, tm * TOP_K, d), F32), pltpu.SemaphoreType.DMA((2,))],
        compiler_params=_cparams(("arbitrary",)),
        name="moe_combine",
    )(idx3, idx3, ys, gates_pad, h1, ln_g.reshape(1, d), ln_b.reshape(1, d))


def _moe_routing(top_e, rank, counts, n_tok, tm_combine):
    n_asg = n_tok * TOP_K
    counts = counts.reshape(N_EXPERTS).astype(I32)
    nblk_per = (counts + EXPERT_BLOCK - 1) // EXPERT_BLOCK
    blk_end = jnp.cumsum(nblk_per)
    pstarts = (blk_end - nblk_per) * EXPERT_BLOCK
    dest = jnp.take(pstarts, top_e, axis=0) + rank
    n_blocks = -(-n_asg // EXPERT_BLOCK) + N_EXPERTS
    blk_expert = jnp.minimum(jnp.sum(blk_end[None, :] <= jnp.arange(n_blocks, dtype=I32)[:, None], axis=1),
                             N_EXPERTS - 1)
    n_used = blk_end[-1]
    nb = n_tok // tm_combine
    pos_km = jnp.transpose(dest.reshape(nb, tm_combine, TOP_K), (0, 2, 1)).reshape(nb, TOP_K * tm_combine)
    return dest.astype(I32), blk_expert.astype(I32), n_used.astype(I32), pos_km.astype(I32)


def _permute_w_in(w):
    sizes = (N_HEADS * HEAD_DIM, KV_LATENT, IDX_HEADS * IDX_DIM, IDX_DIM, IDX_HEADS,
             CONV_WIDTH, CONV_WIDTH, CONV_WIDTH, SSM_WIDTH, MEM_HEADS * MEM_HEAD_DIM)
    offs = np.cumsum((0,) + sizes)
    q, ckv, qidx, kidx, widx, cu, cgb, cgc, ssm, memq = [w[:, offs[k]:offs[k + 1]] for k in range(len(sizes))]
    gates = w[:, offs[-1]:]
    pad = jnp.zeros((w.shape[0], LANES - IDX_DIM - IDX_HEADS), w.dtype)
    return jnp.concatenate([gates, q, cu, cgb, cgc, ssm, memq, qidx, ckv, kidx, widx, pad], axis=1).astype(BF16)


def kernel(x, mem, ln_in_g, ln_in_b, w_in, kv_norm_g, w_uk, w_uv, conv_w, conv_b, lam_re, lam_im, b_re, b_im, c_re, c_im, d_skip, log_dt, w_glu, b_glu, w_mem_kv, w_branch, w_o, ln1_g, ln1_b, w_router, b_router, w_up, b_up, w_down, b_down, ln2_g, ln2_b):
    bsz, seq, d = x.shape
    depth = w_in.shape[0]
    mlen = mem.shape[1]
    n = bsz * seq
    alpha = float((2 * depth) ** 0.25)
    tm_combine = 128

    h = _ln_in(x.reshape(n, d), ln_in_g, ln_in_b)
    n_row_blocks = -(-n * TOP_K // EXPERT_BLOCK) + N_EXPERTS
    xs = jnp.zeros((n_row_blocks * EXPERT_BLOCK, d), F32)
    mem2 = mem.reshape(bsz * mlen, d)
    for l in range(depth):
        proj = _matmul(h, _permute_w_in(w_in[l]), 256, "in_proj")
        qcat, kcat, ckvn, ckvt = _prep(proj, kv_norm_g[l], bsz, seq)
        att = _dsa_attention(proj, qcat, kcat, ckvn, ckvt, w_uk[l], w_uv[l], bsz, seq)
        cnv = _short_conv(proj, conv_w[l], conv_b[l], bsz, seq)
        ssm = _s5_mixer(proj, lam_re[l], lam_im[l], b_re[l], b_im[l], c_re[l], c_im[l], d_skip[l], log_dt[l],
                        w_glu[l], b_glu[l], bsz, seq)
        kvm = _matmul(mem2, w_mem_kv[l].astype(BF16), min(256, bsz * mlen), "mem_kv")
        mem_o = _memory_attention(proj, kvm, bsz, seq, mlen)
        h1, te_pad, tg_pad, counts = _merge_router(att, cnv, ssm, mem_o, proj, h, w_branch[l], w_o[l],
                                                   ln1_g[l], ln1_b[l], w_router[l], b_router[l], alpha)
        dest, blk_expert, n_used, pos_km = _moe_routing(te_pad[:, :TOP_K], te_pad[:, TOP_K:2 * TOP_K], counts,
                                                        n, tm_combine)
        xs = _dispatch(h1, dest, xs)
        ys = _expert_ffn(xs, blk_expert, n_used, l, w_up, b_up, w_down, b_down)
        h = _combine(ys, pos_km, tg_pad, h1, ln2_g[l], ln2_b[l], alpha, tm_combine)
    return h.reshape(bsz, seq, d)
```

```python
import functools
import math

import numpy as np
import jax
import jax.numpy as jnp
from jax import lax
from jax.experimental import pallas as pl
from jax.experimental.pallas import tpu as pltpu

F32 = jnp.float32
BF16 = jnp.bfloat16
I32 = jnp.int32

N_HEADS = 8
HEAD_DIM = 64
KV_LATENT = 128
IDX_HEADS = 8
IDX_DIM = 32
INDEX_TOPK = 256
DSA_QUERY_BLOCK = 256
CONV_WIDTH = 512
SSM_WIDTH = 512
SSM_GROUP = 16
SSM_GROUPS = SSM_WIDTH // SSM_GROUP
SSM_STATE = 64
MEM_HEADS = 4
MEM_HEAD_DIM = 128
N_BRANCH = 4
BRANCH_WIDTH = 512
N_EXPERTS = 32
TOP_K = 4
D_EXPERT = 1024
SWIGLU_LIMIT = 7.0
SWIGLU_ALPHA = 1.702
EXPERT_BLOCK = 256
LN_EPS = 1e-5

LANES = 128
SUBLANES = 8
VMEM_LIMIT_BYTES = 56 * 1024 * 1024

INT_MIN = -(2 ** 31)
NEG_BIG = -1e30
TINY_NORMAL = 1.1754943508222875e-38

COL_GATES = 0
COL_Q = 4096
COL_CONV_U = 4608
COL_CONV_GB = 5120
COL_CONV_GC = 5632
COL_SSM = 6144
COL_MEMQ = 6656
COL_QIDX = 7168
COL_CKV = 7424
COL_TAIL = 7552
D_PROJ = 7680
TAIL_W_OFF = IDX_DIM

SSM_SEQS = 8


def _cparams(sem):
    return pltpu.CompilerParams(dimension_semantics=sem, vmem_limit_bytes=VMEM_LIMIT_BYTES)


def _layer_norm(x, g, b):
    mu = jnp.mean(x, axis=-1, keepdims=True)
    xc = x - mu
    var = jnp.mean(xc * xc, axis=-1, keepdims=True)
    return xc * lax.rsqrt(var + LN_EPS) * g + b


def _const_spec(shape):
    nd = len(shape)
    return pl.BlockSpec(shape, lambda *_: (0,) * nd)


def _ln_kernel(x_ref, g_ref, b_ref, o_ref):
    o_ref[...] = _layer_norm(x_ref[...], g_ref[...], b_ref[...])


def _ln_in(x2, g, b, tm=512):
    n, d = x2.shape
    return pl.pallas_call(
        _ln_kernel,
        out_shape=jax.ShapeDtypeStruct((n, d), F32),
        grid=(n // tm,),
        in_specs=[pl.BlockSpec((tm, d), lambda i: (i, 0)), _const_spec((1, d)), _const_spec((1, d))],
        out_specs=pl.BlockSpec((tm, d), lambda i: (i, 0)),
        compiler_params=_cparams(("parallel",)),
        name="ln_in",
    )(x2, g.reshape(1, d), b.reshape(1, d))


def _matmul_kernel(x_ref, w_ref, o_ref):
    o_ref[...] = jnp.dot(x_ref[...].astype(BF16), w_ref[...], preferred_element_type=F32)


def _matmul(x, w_bf16, tm, name):
    n, k = x.shape
    m = w_bf16.shape[1]
    return pl.pallas_call(
        _matmul_kernel,
        out_shape=jax.ShapeDtypeStruct((n, m), F32),
        grid=(n // tm,),
        in_specs=[pl.BlockSpec((tm, k), lambda i: (i, 0)),
                  pl.BlockSpec((k, m), lambda i: (0, 0), pipeline_mode=pl.Buffered(1))],
        out_specs=pl.BlockSpec((tm, m), lambda i: (i, 0)),
        compiler_params=_cparams(("parallel",)),
        name=name,
    )(x, w_bf16)


def _split_hi_lo(x):
    hi = x.astype(BF16)
    lo = (x - hi.astype(F32)).astype(BF16)
    return hi, lo


def _prep_kernel(qidx_ref, ckv_ref, tail_ref, g_ref, sq_ref, sk_ref, qcat_ref, kcat_ref, ckvn_ref, ckvt_ref):
    q_hi, q_lo = _split_hi_lo(qidx_ref[...])
    qcat = jnp.dot(jnp.concatenate([q_hi, q_lo], axis=1), sq_ref[...], preferred_element_type=F32)
    qcat_ref[...] = qcat.astype(BF16)
    k_hi, k_lo = _split_hi_lo(tail_ref[...])
    kcat = jnp.dot(jnp.concatenate([k_hi, k_lo], axis=1), sk_ref[...], preferred_element_type=F32)
    kcat_ref[...] = kcat.astype(BF16)
    c = ckv_ref[...]
    ms = jnp.mean(c * c, axis=-1, keepdims=True)
    cn = c * lax.rsqrt(ms + LN_EPS) * g_ref[...]
    ckvn_ref[...] = cn.astype(BF16)
    ckvt_ref[...] = cn.T.astype(BF16)


def _selection_matrices():
    sq = np.zeros((2 * IDX_HEADS * IDX_DIM, IDX_HEADS * LANES), np.float32)
    for h in range(IDX_HEADS):
        for d in range(IDX_DIM):
            hi_in = h * IDX_DIM + d
            lo_in = IDX_HEADS * IDX_DIM + hi_in
            sq[hi_in, h * LANES + d] = 1.0
            sq[hi_in, h * LANES + IDX_DIM + d] = 1.0
            sq[lo_in, h * LANES + 2 * IDX_DIM + d] = 1.0
    sk = np.zeros((2 * LANES, LANES), np.float32)
    for d in range(IDX_DIM):
        sk[d, d] = 1.0
        sk[LANES + d, IDX_DIM + d] = 1.0
        sk[d, 2 * IDX_DIM + d] = 1.0
    return jnp.asarray(sq, BF16), jnp.asarray(sk, BF16)


def _prep(proj, kv_norm_g, bsz, seq, tm=512):
    n = proj.shape[0]
    tm = min(tm, seq)
    nbt = seq // tm
    sq, sk = _selection_matrices()
    return pl.pallas_call(
        _prep_kernel,
        out_shape=(jax.ShapeDtypeStruct((n, IDX_HEADS * LANES), BF16),
                   jax.ShapeDtypeStruct((n, LANES), BF16),
                   jax.ShapeDtypeStruct((n, KV_LATENT), BF16),
                   jax.ShapeDtypeStruct((bsz, KV_LATENT, seq), BF16)),
        grid=(n // tm,),
        in_specs=[pl.BlockSpec((tm, 256), lambda i: (i, COL_QIDX // 256)),
                  pl.BlockSpec((tm, 128), lambda i: (i, COL_CKV // 128)),
                  pl.BlockSpec((tm, 128), lambda i: (i, COL_TAIL // 128)),
                  _const_spec((1, KV_LATENT)), _const_spec(sq.shape), _const_spec(sk.shape)],
        out_specs=(pl.BlockSpec((tm, IDX_HEADS * LANES), lambda i: (i, 0)),
                   pl.BlockSpec((tm, LANES), lambda i: (i, 0)),
                   pl.BlockSpec((tm, KV_LATENT), lambda i: (i, 0)),
                   pl.BlockSpec((None, KV_LATENT, tm), lambda i: (i // nbt, 0, i % nbt))),
        compiler_params=_cparams(("parallel",)),
        name="dsa_prep",
    )(proj, proj, proj, kv_norm_g.reshape(1, KV_LATENT), sq, sk)


def _dsa_kernel(q_ref, tail_ref, qcat_ref, kcat_ref, ckv_ref, ckvt_ref, wuk_ref, wuv_ref, o_ref,
                key_ref, khi_ref, qlat_ref, m_ref, l_ref, acc_ref, *, seq, topk, cks, ck, cka, qb):
    i = pl.program_id(1)
    q0 = i * qb
    nchs = (q0 + qb + cks - 1) // cks
    nch = (q0 + qb + ck - 1) // ck
    q_pos = q0 + lax.broadcasted_iota(I32, (1, qb), 1)
    row_pos_s = lax.broadcasted_iota(I32, (cks, 1), 0)
    row_pos = lax.broadcasted_iota(I32, (ck, 1), 0)
    w_t = tail_ref[...].T
    nt_dims = (((1,), (1,)), ((), ()))
    cnt_rows = min(64, ck)

    def score_chunk(c, carry):
        k0 = pl.multiple_of(c * cks, cks)
        kc = kcat_ref[pl.ds(k0, cks), :]
        s = jnp.zeros((cks, qb), F32)
        for h in range(IDX_HEADS):
            d = lax.dot_general(kc, qcat_ref[:, h * LANES:(h + 1) * LANES], nt_dims, preferred_element_type=F32)
            s = s + jnp.maximum(d, 0.0) * w_t[TAIL_W_OFF + h:TAIL_W_OFF + h + 1, :]
        s = jnp.where(jnp.abs(s) < TINY_NORMAL, 0.0, s)
        bits = pltpu.bitcast(s, I32)
        key = bits ^ ((bits >> 31) & jnp.int32(0x7FFFFFFF))
        causal = k0 + row_pos_s <= q_pos
        key_ref[pl.ds(k0, cks), :] = jnp.where(causal, key, jnp.int32(INT_MIN))
        hi = pltpu.bitcast(bits & jnp.int32(-65536), F32)
        khi_ref[pl.ds(k0, cks), :] = jnp.where(causal, hi, -jnp.inf).astype(BF16)
        return carry

    lax.fori_loop(0, nchs, score_chunk, 0)

    def count_keys(pred_fn):
        def body(c, acc):
            k0 = pl.multiple_of(c * ck, ck)
            hit = jnp.where(pred_fn(key_ref[pl.ds(k0, ck), :], k0), 1.0, 0.0)
            return acc + jnp.sum(hit.reshape(ck // cnt_rows, cnt_rows, qb), axis=0)
        acc = lax.fori_loop(0, nch, body, jnp.zeros((cnt_rows, qb), F32))
        return jnp.sum(acc, axis=0, keepdims=True)

    kf = float(topk)

    def bit_pass(b, v):
        cand = v + jnp.left_shift(jnp.int32(1), 31 - b)
        cnt = count_keys(lambda key, k0: key >= cand)
        return jnp.where(cnt >= kf, cand, v)

    pack = 16
    n_part = ck // (2 * pack)

    def count_hi(cand16):
        cand16 = jnp.where((cand16 >= 1) & (cand16 < 128), 128, jnp.where((cand16 >= -128) & (cand16 <= 0), 0, cand16))
        pat = cand16 ^ ((cand16 >> 31) & jnp.int32(0x7FFF))
        cand = jnp.broadcast_to(pltpu.bitcast(pat << 16, F32), (pack, qb)).astype(BF16)
        one = jnp.ones((), BF16)
        zero = jnp.zeros((), BF16)

        def body(c, acc):
            k0 = pl.multiple_of(c * ck, ck)
            hit = jnp.where(khi_ref[pl.ds(k0, ck), :].reshape(ck // pack, pack, qb) >= cand[None], one, zero)
            part = hit[0:2]
            for j in range(1, n_part):
                part = part + hit[2 * j:2 * j + 2]
            return acc + part.astype(F32)
        acc = lax.fori_loop(0, nch, body, jnp.zeros((2, pack, qb), F32))
        return jnp.sum(jnp.sum(acc, axis=0), axis=0, keepdims=True)

    def hi_pass(b, v16):
        cand16 = v16 + jnp.left_shift(jnp.int32(1), 15 - b)
        return jnp.where(count_hi(cand16) >= kf, cand16, v16)

    v16 = lax.fori_loop(0, 16, hi_pass, jnp.full((1, qb), -32768, I32))
    v = lax.fori_loop(16, 32, bit_pass, v16 << 16)
    thr = jnp.maximum(v, jnp.int32(INT_MIN + 1))
    c_ge = count_keys(lambda key, k0: key >= thr)
    c_gt = count_keys(lambda key, k0: key > thr)
    need = kf - c_gt

    jlim_ref = m_ref.at[0:1]
    jlim_ref[...] = jnp.full((1, qb), float(seq), F32)
    has_tie = jnp.max(jnp.where(c_ge > kf, 1.0, 0.0)) > 0.0

    @pl.when(has_tie)
    def _():
        n_bits = max(1, int(math.ceil(math.log2(seq))))

        def pos_pass(b, j):
            cand = j + jnp.left_shift(jnp.int32(1), n_bits - 1 - b)
            cnt = count_keys(lambda key, k0: (key == thr) & (k0 + row_pos < cand))
            return jnp.where(cnt < need, cand, j)

        j = lax.fori_loop(0, n_bits, pos_pass, jnp.zeros((1, qb), I32))
        jlim_ref[...] = jnp.where(c_ge > kf, j.astype(F32), float(seq))

    jlim = jlim_ref[...].astype(I32)

    q_bf = q_ref[...].astype(BF16)
    for h in range(N_HEADS):
        ql = jnp.dot(q_bf, wuk_ref[h], preferred_element_type=F32) * (HEAD_DIM ** -0.5 * math.log2(math.e))
        qlat_ref[h] = ql.T.astype(BF16)

    m_ref[...] = jnp.full(m_ref.shape, NEG_BIG, F32)
    l_ref[...] = jnp.zeros(l_ref.shape, F32)
    acc_ref[...] = jnp.zeros(acc_ref.shape, F32)

    row_pos_a = lax.broadcasted_iota(I32, (cka, 1), 0)

    def att_chunk(c, carry):
        for sub in range(cks // cka):
            k0 = pl.multiple_of(c * cks + sub * cka, cka)
            kv = ckv_ref[pl.ds(k0, cka), :]
            kvt = ckvt_ref[:, pl.ds(k0, cka)]
            key = key_ref[pl.ds(k0, cka), :]
            sel = (key > thr) | ((key == thr) & (k0 + row_pos_a <= jlim))
            bias = jnp.where(sel, 0.0, NEG_BIG)
            for h in range(N_HEADS):
                lg = jnp.dot(kv, qlat_ref[h], preferred_element_type=F32) + bias
                m_prev = m_ref[h:h + 1, :]
                m_new = jnp.maximum(m_prev, jnp.max(lg, axis=0, keepdims=True))
                p = jnp.exp2(lg - m_new)
                alpha = jnp.exp2(m_prev - m_new)
                l_ref[h:h + 1, :] = alpha * l_ref[h:h + 1, :] + jnp.sum(p, axis=0, keepdims=True)
                acc_ref[h] = alpha * acc_ref[h] + jnp.dot(kvt, p.astype(BF16), preferred_element_type=F32)
                m_ref[h:h + 1, :] = m_new
        return carry

    lax.fori_loop(0, nchs, att_chunk, 0)

    out = jnp.zeros((qb, N_HEADS * HEAD_DIM), F32)
    for h in range(N_HEADS):
        o_lat = (acc_ref[h] / l_ref[h:h + 1, :]).T
        out = out + jnp.dot(o_lat.astype(BF16), wuv_ref[h], preferred_element_type=F32)
    o_ref[...] = out


def _dsa_attention(proj, qcat, kcat, ckvn, ckvt, w_uk, w_uv, bsz, seq):
    n = proj.shape[0]
    qb = min(DSA_QUERY_BLOCK, seq)
    n_blk = seq // qb
    topk = min(INDEX_TOPK, seq // 4)
    cks = min(512, seq)
    ck = min(256, seq)
    cka = min(128, seq)
    eye = jnp.eye(N_HEADS, dtype=F32)
    wuk = jnp.einsum('chd,hg->hgdc', w_uk, eye).reshape(N_HEADS, N_HEADS * HEAD_DIM, KV_LATENT).astype(BF16)
    wuv = jnp.einsum('chd,hg->hcgd', w_uv, eye).reshape(N_HEADS, KV_LATENT, N_HEADS * HEAD_DIM).astype(BF16)
    kernel = functools.partial(_dsa_kernel, seq=seq, topk=topk, cks=cks, ck=ck, cka=cka, qb=qb)
    return pl.pallas_call(
        kernel,
        out_shape=jax.ShapeDtypeStruct((n, N_HEADS * HEAD_DIM), F32),
        grid=(bsz, n_blk),
        in_specs=[pl.BlockSpec((qb, 512), lambda b, i: (b * n_blk + i, COL_Q // 512)),
                  pl.BlockSpec((qb, 128), lambda b, i: (b * n_blk + i, COL_TAIL // 128)),
                  pl.BlockSpec((qb, IDX_HEADS * LANES), lambda b, i: (b * n_blk + i, 0)),
                  pl.BlockSpec((seq, LANES), lambda b, i: (b, 0)),
                  pl.BlockSpec((seq, KV_LATENT), lambda b, i: (b, 0)),
                  pl.BlockSpec((None, KV_LATENT, seq), lambda b, i: (b, 0, 0)),
                  _const_spec(wuk.shape), _const_spec(wuv.shape)],
        out_specs=pl.BlockSpec((qb, N_HEADS * HEAD_DIM), lambda b, i: (b * n_blk + i, 0)),
        scratch_shapes=[pltpu.VMEM((seq, qb), I32),
                        pltpu.VMEM((seq, qb), BF16),
                        pltpu.VMEM((N_HEADS, KV_LATENT, qb), BF16),
                        pltpu.VMEM((N_HEADS, qb), F32),
                        pltpu.VMEM((N_HEADS, qb), F32),
                        pltpu.VMEM((N_HEADS, KV_LATENT, qb), F32)],
        compiler_params=_cparams(("parallel", "parallel")),
        name="dsa_attention",
    )(proj, proj, qcat, kcat, ckvn, ckvt, wuk, wuv)


def _conv_kernel(u_ref, gb_ref, gc_ref, pu_ref, pgc_ref, w_ref, b_ref, o_ref):
    i = pl.program_id(1)
    tb = u_ref.shape[0]
    v = gc_ref[...] * u_ref[...]
    halo = jnp.where(i > 0, pgc_ref[...] * pu_ref[...], 0.0)
    vfull = jnp.concatenate([halo, v], axis=0)
    v1 = pltpu.roll(vfull, 1, 0)[SUBLANES:SUBLANES + tb]
    v2 = pltpu.roll(vfull, 2, 0)[SUBLANES:SUBLANES + tb]
    w = w_ref[...]
    y = w[0:1] * v2 + w[1:2] * v1 + w[2:3] * v
    o_ref[...] = gb_ref[...] * (y + b_ref[...])


def _short_conv(proj, conv_w, conv_b, bsz, seq, tb=512):
    n = proj.shape[0]
    tb = min(tb, seq)
    nb = seq // tb
    cw = CONV_WIDTH
    blk = lambda col: pl.BlockSpec((tb, cw), lambda b, i: (b * nb + i, col // cw))
    prev = lambda col: pl.BlockSpec(
        (SUBLANES, cw), lambda b, i: (jnp.maximum((b * nb + i) * (tb // SUBLANES) - 1, 0), col // cw))
    return pl.pallas_call(
        _conv_kernel,
        out_shape=jax.ShapeDtypeStruct((n, cw), F32),
        grid=(bsz, nb),
        in_specs=[blk(COL_CONV_U), blk(COL_CONV_GB), blk(COL_CONV_GC), prev(COL_CONV_U), prev(COL_CONV_GC),
                  _const_spec((SUBLANES, cw)), _const_spec((1, cw))],
        out_specs=pl.BlockSpec((tb, cw), lambda b, i: (b * nb + i, 0)),
        compiler_params=_cparams(("parallel", "parallel")),
        name="short_conv",
    )(proj, proj, proj, proj, proj,
      jnp.pad(conv_w, ((0, SUBLANES - conv_w.shape[0]), (0, 0))), conv_b.reshape(1, cw))


def _s5_scan_chunk(bu_ref, a_ref, st_ref, n_steps, store):
    s_tot = a_ref.shape[1] // 2
    cb = 512
    for blk in range(s_tot // cb):
        re_sl = pl.ds(blk * cb, cb)
        im_sl = pl.ds(s_tot + blk * cb, cb)
        ar = jnp.broadcast_to(a_ref[0:1, blk * cb:(blk + 1) * cb], (SSM_SEQS, cb))
        ai = jnp.broadcast_to(a_ref[0:1, s_tot + blk * cb:s_tot + (blk + 1) * cb], (SSM_SEQS, cb))

        def step(t, carry):
            re, im = carry
            r0 = pl.multiple_of(t * SSM_SEQS, SSM_SEQS)
            br = bu_ref[pl.ds(r0, SSM_SEQS), re_sl]
            bi = bu_ref[pl.ds(r0, SSM_SEQS), im_sl]
            nre = ar * re - ai * im + br
            nim = ar * im + ai * re + bi
            if store:
                bu_ref[pl.ds(r0, SSM_SEQS), re_sl] = nre
                bu_ref[pl.ds(r0, SSM_SEQS), im_sl] = nim
            return nre, nim

        re, im = lax.fori_loop(0, n_steps, step, (st_ref[:, re_sl], st_ref[:, im_sl]), unroll=8)
        st_ref[:, re_sl] = re
        st_ref[:, im_sl] = im


def _gelu_tanh(x):
    return 0.5 * x * (1.0 + jnp.tanh(math.sqrt(2.0 / math.pi) * (x + 0.044715 * (x * x * x))))


S5_ROWS = 512


def _s5_kernel(u_ref, perm_ref, wb_ref, a_ref, apow_ref, wc_ref, d_ref, wg_ref, bg_ref, o_ref,
               bu_ref, st_ref, carry_ref, *, seg_len, steps_per_seq):
    c = pl.program_id(0)
    s_tot = a_ref.shape[1] // 2
    tn_dims = (((0,), (0,)), ((), ()))

    @pl.when(c % steps_per_seq == 0)
    def _():
        carry_ref[...] = jnp.zeros(carry_ref.shape, F32)

    u = u_ref[...]
    perm = perm_ref[...]
    u_il = jnp.dot(perm, u.astype(BF16), preferred_element_type=F32).astype(BF16)
    hw = u_il.shape[1] // 2
    hs = s_tot // 2
    for part in range(2):
        for q in range(2):
            c0 = part * s_tot + q * hs
            bu_ref[:, c0:c0 + hs] = jnp.dot(u_il[:, q * hw:(q + 1) * hw], wb_ref[q * hw:(q + 1) * hw, c0:c0 + hs],
                                            preferred_element_type=F32)

    st_ref[...] = jnp.zeros(st_ref.shape, F32)
    _s5_scan_chunk(bu_ref, a_ref, st_ref, seg_len, store=False)

    pr = apow_ref[0:1, :s_tot]
    pi = apow_ref[0:1, s_tot:]
    hre = carry_ref[0:1, :s_tot]
    him = carry_ref[0:1, s_tot:]
    for j in range(SSM_SEQS):
        ere = st_ref[j:j + 1, :s_tot]
        eim = st_ref[j:j + 1, s_tot:]
        st_ref[j:j + 1, :s_tot] = hre
        st_ref[j:j + 1, s_tot:] = him
        hre, him = pr * hre - pi * him + ere, pr * him + pi * hre + eim
    carry_ref[0:1, :s_tot] = hre
    carry_ref[0:1, s_tot:] = him

    _s5_scan_chunk(bu_ref, a_ref, st_ref, seg_len, store=True)
    halves = []
    for q in range(2):
        acc = None
        for part in range(2):
            c0 = part * s_tot + q * hs
            term = jnp.dot(bu_ref[:, c0:c0 + hs].astype(BF16), wc_ref[c0:c0 + hs, q * hw:(q + 1) * hw],
                           preferred_element_type=F32)
            acc = term if acc is None else acc + term
        halves.append(acc)
    ch = jnp.concatenate(halves, axis=1)
    ch_hi, ch_lo = _split_hi_lo(ch)
    ch_nat = (lax.dot_general(perm, ch_hi, tn_dims, preferred_element_type=F32)
              + lax.dot_general(perm, ch_lo, tn_dims, preferred_element_type=F32))
    z = _gelu_tanh(ch_nat + d_ref[...] * u)
    gate = jnp.dot(z.astype(BF16), wg_ref[...], preferred_element_type=F32) + bg_ref[...]
    o_ref[...] = z * jax.nn.sigmoid(gate)


def _s5_mixer(proj, lam_re, lam_im, b_re, b_im, c_re, c_im, d_skip, log_dt, w_glu, b_glu, bsz, seq):
    n = proj.shape[0]
    g, p, nn = SSM_GROUPS, SSM_STATE, SSM_GROUP
    s_tot = g * p
    rows = min(S5_ROWS, seq)
    seg_len = rows // SSM_SEQS

    dt = jnp.exp(log_dt.astype(F32))[:, None]
    lam = lax.complex(lam_re.astype(F32), lam_im.astype(F32))
    a_bar = jnp.exp(lam * dt)
    b_bar = ((a_bar - 1.0) / lam)[:, :, None] * lax.complex(b_re.astype(F32), b_im.astype(F32))
    a_pow = jnp.exp(lam * dt * seg_len)
    a_vec = jnp.concatenate([jnp.real(a_bar).reshape(1, s_tot), jnp.imag(a_bar).reshape(1, s_tot)], axis=1)
    apow_vec = jnp.concatenate([jnp.real(a_pow).reshape(1, s_tot), jnp.imag(a_pow).reshape(1, s_tot)], axis=1)
    eye = jnp.eye(g, dtype=F32)
    wb = jnp.concatenate(
        [jnp.einsum('gpn,gh->gnhp', jnp.real(b_bar), eye).reshape(g * nn, s_tot),
         jnp.einsum('gpn,gh->gnhp', jnp.imag(b_bar), eye).reshape(g * nn, s_tot)], axis=1).astype(BF16)
    wc = jnp.concatenate(
        [jnp.einsum('gnp,gh->gphn', c_re.astype(F32), eye).reshape(s_tot, g * nn),
         -jnp.einsum('gnp,gh->gphn', c_im.astype(F32), eye).reshape(s_tot, g * nn)], axis=0).astype(BF16)
    perm_np = np.zeros((rows, rows), np.float32)
    t_idx, j_idx = np.meshgrid(np.arange(seg_len), np.arange(SSM_SEQS), indexing='ij')
    perm_np[(t_idx * SSM_SEQS + j_idx).ravel(), (j_idx * seg_len + t_idx).ravel()] = 1.0
    perm = jnp.asarray(perm_np, BF16)

    return pl.pallas_call(
        functools.partial(_s5_kernel, seg_len=seg_len, steps_per_seq=seq // rows),
        out_shape=jax.ShapeDtypeStruct((n, SSM_WIDTH), F32),
        grid=(n // rows,),
        in_specs=[pl.BlockSpec((rows, SSM_WIDTH), lambda c: (c, COL_SSM // SSM_WIDTH)),
                  _const_spec(perm.shape), _const_spec(wb.shape), _const_spec(a_vec.shape),
                  _const_spec(apow_vec.shape), _const_spec(wc.shape), _const_spec((1, SSM_WIDTH)),
                  _const_spec((SSM_WIDTH, SSM_WIDTH)), _const_spec((1, SSM_WIDTH))],
        out_specs=pl.BlockSpec((rows, SSM_WIDTH), lambda c: (c, 0)),
        scratch_shapes=[pltpu.VMEM((rows, 2 * s_tot), F32), pltpu.VMEM((SSM_SEQS, 2 * s_tot), F32),
                        pltpu.VMEM((1, 2 * s_tot), F32)],
        compiler_params=_cparams(("arbitrary",)),
        name="s5_mixer",
    )(proj, perm, wb, a_vec, apow_vec, wc, d_skip.reshape(1, SSM_WIDTH).astype(F32),
      w_glu.astype(BF16), b_glu.reshape(1, SSM_WIDTH))


def _mem_attn_kernel(q_ref, kv_ref, o_ref):
    hw = MEM_HEADS * MEM_HEAD_DIM
    q = q_ref[...].astype(BF16)
    kv = kv_ref[...].astype(BF16)
    for h in range(MEM_HEADS):
        sl = slice(h * MEM_HEAD_DIM, (h + 1) * MEM_HEAD_DIM)
        k = kv[:, sl]
        v = kv[:, hw + h * MEM_HEAD_DIM:hw + (h + 1) * MEM_HEAD_DIM]
        lg = lax.dot_general(q[:, sl], k, (((1,), (1,)), ((), ())),
                             preferred_element_type=F32) * (MEM_HEAD_DIM ** -0.5)
        m = jnp.max(lg, axis=1, keepdims=True)
        p = jnp.exp(lg - m)
        p = p / jnp.sum(p, axis=1, keepdims=True)
        o_ref[:, sl] = jnp.dot(p.astype(BF16), v, preferred_element_type=F32)


def _memory_attention(proj, kvm, bsz, seq, mlen, tm=512):
    n = proj.shape[0]
    tm = min(tm, seq)
    nb = seq // tm
    hw = MEM_HEADS * MEM_HEAD_DIM
    return pl.pallas_call(
        _mem_attn_kernel,
        out_shape=jax.ShapeDtypeStruct((n, hw), F32),
        grid=(bsz, nb),
        in_specs=[pl.BlockSpec((tm, hw), lambda b, i: (b * nb + i, COL_MEMQ // hw)),
                  pl.BlockSpec((mlen, 2 * hw), lambda b, i: (b, 0))],
        out_specs=pl.BlockSpec((tm, hw), lambda b, i: (b * nb + i, 0)),
        compiler_params=_cparams(("parallel", "parallel")),
        name="memory_attention",
    )(proj, kvm)


def _merge_kernel(att_ref, cnv_ref, ssm_ref, mem_ref, gl_ref, h_ref, wbr_ref, wo_ref, g_ref, b_ref,
                  wr_ref, br_ref, ltri_ref, h1_ref, te_ref, tg_ref, cnt_ref, base_ref, *, alpha):
    d = h_ref.shape[1]

    @pl.when(pl.program_id(0) == 0)
    def _():
        base_ref[...] = jnp.zeros(base_ref.shape, F32)

    merged = jnp.zeros(h_ref.shape, F32)
    for r, br in enumerate((att_ref, cnv_ref, ssm_ref, mem_ref)):
        y = jnp.dot(br[...].astype(BF16), wbr_ref[r], preferred_element_type=F32)
        merged = merged + y * jax.nn.sigmoid(gl_ref[:, r * d:(r + 1) * d])
    y = alpha * h_ref[...] + jnp.dot(merged.astype(BF16), wo_ref[...], preferred_element_type=F32)
    h1 = _layer_norm(y, g_ref[...], b_ref[...])
    h1_ref[...] = h1

    logits = jnp.dot(h1, wr_ref[...], preferred_element_type=F32, precision=lax.Precision.HIGHEST) + br_ref[...]
    tm = logits.shape[0]
    lane = lax.broadcasted_iota(I32, (tm, N_EXPERTS), 1)
    out_lane = lax.broadcasted_iota(I32, (tm, LANES), 1)
    work = logits
    top_e = jnp.zeros((tm, LANES), I32)
    top_v = jnp.zeros((tm, LANES), F32)
    vals = []
    onehots = []
    for k in range(TOP_K):
        mx = jnp.max(work, axis=1, keepdims=True)
        idx = jnp.min(jnp.where(work == mx, lane, N_EXPERTS), axis=1, keepdims=True)
        hit = lane == idx
        work = jnp.where(hit, -jnp.inf, work)
        top_e = jnp.where(out_lane == k, idx, top_e)
        vals.append(mx)
        onehots.append(jnp.where(hit, 1.0, 0.0))
    den = sum(jnp.exp(vk - vals[0]) for vk in vals)
    for k in range(TOP_K):
        top_v = jnp.where(out_lane == k, jnp.exp(vals[k] - vals[0]) / den, top_v)

    per_tok = onehots[0] + onehots[1] + onehots[2] + onehots[3]
    before = jnp.dot(ltri_ref[...], per_tok.astype(BF16), preferred_element_type=F32) + base_ref[...]
    for k in range(TOP_K):
        rank = jnp.sum(onehots[k] * before, axis=1, keepdims=True)
        top_e = jnp.where(out_lane == TOP_K + k, rank.astype(I32), top_e)
    base_ref[...] = base_ref[...] + jnp.sum(per_tok, axis=0, keepdims=True)
    cnt_ref[...] = base_ref[...]
    te_ref[...] = top_e
    tg_ref[...] = top_v


def _merge_router(att, cnv, ssm, mem_o, proj, h, w_branch, w_o, ln_g, ln_b, w_router, b_router, alpha, tm=256):
    n, d = h.shape
    bw = BRANCH_WIDTH
    row = lambda w: pl.BlockSpec((tm, w), lambda i: (i, 0))
    ltri = jnp.asarray(np.tril(np.ones((tm, tm), np.float32), -1), BF16)
    return pl.pallas_call(
        functools.partial(_merge_kernel, alpha=alpha),
        out_shape=(jax.ShapeDtypeStruct((n, d), F32),
                   jax.ShapeDtypeStruct((n, LANES), I32),
                   jax.ShapeDtypeStruct((n, LANES), F32),
                   jax.ShapeDtypeStruct((1, N_EXPERTS), F32)),
        grid=(n // tm,),
        in_specs=[row(bw), row(bw), row(bw), row(bw),
                  pl.BlockSpec((tm, N_BRANCH * d), lambda i: (i, COL_GATES // (N_BRANCH * d))),
                  row(d),
                  _const_spec((N_BRANCH, bw, d)), _const_spec((d, d)), _const_spec((1, d)), _const_spec((1, d)),
                  _const_spec((d, N_EXPERTS)), _const_spec((1, N_EXPERTS)), _const_spec((tm, tm))],
        out_specs=(row(d), row(LANES), row(LANES), _const_spec((1, N_EXPERTS))),
        scratch_shapes=[pltpu.VMEM((1, N_EXPERTS), F32)],
        compiler_params=_cparams(("arbitrary",)),
        name="merge_router",
    )(att, cnv, ssm, mem_o, proj, h, w_branch.astype(BF16), w_o.astype(BF16),
      ln_g.reshape(1, d), ln_b.reshape(1, d), w_router, b_router.reshape(1, N_EXPERTS), ltri)


DISPATCH_UNROLL = 4


def _dispatch_kernel(dest_ref, h_ref, xs_in_hbm, xs_hbm, sem, *, tm):
    del xs_in_hbm

    def body(r, carry):
        for k in range(TOP_K):
            pltpu.make_async_copy(h_ref.at[pl.ds(r, 1), :],
                                  xs_hbm.at[pl.ds(dest_ref[0, r * TOP_K + k], 1), :], sem).start()
        return carry

    lax.fori_loop(0, tm, body, 0, unroll=DISPATCH_UNROLL)
    pltpu.make_async_copy(xs_hbm.at[pl.ds(0, tm * TOP_K), :], xs_hbm.at[pl.ds(0, tm * TOP_K), :], sem).wait()


def _dispatch(h1, dest, xs_prev, tm=256):
    n, d = h1.shape
    nb = n // tm
    return pl.pallas_call(
        functools.partial(_dispatch_kernel, tm=tm),
        out_shape=jax.ShapeDtypeStruct(xs_prev.shape, F32),
        grid=(nb,),
        in_specs=[pl.BlockSpec((None, 1, tm * TOP_K), lambda i: (i, 0, 0), memory_space=pltpu.SMEM),
                  pl.BlockSpec((tm, d), lambda i: (i, 0)),
                  pl.BlockSpec(memory_space=pl.ANY)],
        out_specs=pl.BlockSpec(memory_space=pl.ANY),
        scratch_shapes=[pltpu.SemaphoreType.DMA(())],
        input_output_aliases={2: 0},
        compiler_params=_cparams(("arbitrary",)),
        name="moe_dispatch",
    )(dest.reshape(nb, 1, tm * TOP_K), h1, xs_prev)


def _expert_kernel(meta_ref, x_ref, wu_ref, bu_ref, wd_ref, bd_ref, sel_ref, o_ref, wu_bf, wd_bf):
    j = pl.program_id(0)
    n_used = meta_ref[0]
    f2 = wu_ref.shape[1]

    @pl.when((j < n_used) & ((j == 0) | (meta_ref[1 + j] != meta_ref[jnp.maximum(j, 1)])))
    def _():
        sel = sel_ref[...]
        cw = sel.shape[0]
        for c in range(f2 // cw):
            grp = jnp.dot(wu_ref[:, c * cw:(c + 1) * cw].astype(BF16), sel, preferred_element_type=F32)
            wu_bf[:, c * cw:(c + 1) * cw] = grp.astype(BF16)
        wd_bf[...] = wd_ref[...].astype(BF16)

    @pl.when(j < n_used)
    def _():
        xb = x_ref[...].astype(BF16)
        hdn = jnp.dot(xb, wu_bf[...], preferred_element_type=F32) + bu_ref[...]
        cw = sel_ref.shape[0]
        half = cw // 2
        parts = []
        for c in range(f2 // cw):
            h_glu = jnp.minimum(hdn[:, c * cw:c * cw + half], SWIGLU_LIMIT)
            h_lin = jnp.clip(hdn[:, c * cw + half:(c + 1) * cw], -SWIGLU_LIMIT, SWIGLU_LIMIT)
            parts.append((h_glu * jax.nn.sigmoid(SWIGLU_ALPHA * h_glu) * (h_lin + 1.0)).astype(BF16))
        act_c = jnp.concatenate(parts, axis=1)
        o_ref[...] = jnp.dot(act_c, wd_bf[...], preferred_element_type=F32) + bd_ref[...]

    @pl.when(j >= n_used)
    def _():
        o_ref[...] = jnp.zeros(o_ref.shape, F32)


def _expert_ffn(xs, blk_expert, n_used, layer, w_up, b_up, w_down, b_down):
    n_rows, d = xs.shape
    n_blocks = n_rows // EXPERT_BLOCK
    f = D_EXPERT
    meta = jnp.concatenate([n_used.reshape(1).astype(I32), blk_expert.astype(I32)])
    sel_np = np.zeros((2 * LANES, 2 * LANES), np.float32)
    sel_np[2 * np.arange(LANES), np.arange(LANES)] = 1.0
    sel_np[2 * np.arange(LANES) + 1, LANES + np.arange(LANES)] = 1.0
    sel = jnp.asarray(sel_np, BF16)
    le = b_up.shape[:2]
    b_up = jnp.transpose(b_up.reshape(le + (f // LANES, LANES, 2)), (0, 1, 2, 4, 3)).reshape(le + (2 * f,))
    e_of = lambda j, m: m[1 + j]
    grid_spec = pltpu.PrefetchScalarGridSpec(
        num_scalar_prefetch=1,
        grid=(n_blocks,),
        in_specs=[
            pl.BlockSpec((EXPERT_BLOCK, d), lambda j, m: (j, 0)),
            pl.BlockSpec((None, None, d, 2 * f), lambda j, m: (layer, e_of(j, m), 0, 0)),
            pl.BlockSpec((None, None, 1, 2 * f), lambda j, m: (layer, e_of(j, m), 0, 0)),
            pl.BlockSpec((None, None, f, d), lambda j, m: (layer, e_of(j, m), 0, 0)),
            pl.BlockSpec((None, None, 1, d), lambda j, m: (layer, e_of(j, m), 0, 0)),
            pl.BlockSpec(sel.shape, lambda j, m: (0, 0)),
        ],
        out_specs=pl.BlockSpec((EXPERT_BLOCK, d), lambda j, m: (j, 0)),
        scratch_shapes=[pltpu.VMEM((d, 2 * f), BF16), pltpu.VMEM((f, d), BF16)],
    )
    return pl.pallas_call(
        _expert_kernel,
        out_shape=jax.ShapeDtypeStruct((n_rows, d), F32),
        grid_spec=grid_spec,
        compiler_params=_cparams(("arbitrary",)),
        name="moe_experts",
    )(meta, xs, w_up, b_up[:, :, None, :], w_down, b_down[:, :, None, :], sel)


def _combine_kernel(cur_ref, nxt_ref, ys_hbm, g4_ref, h_ref, lg_ref, lb_ref, o_ref, ybuf, sem, *, alpha, tm):
    j = pl.program_id(0)
    nblk = pl.num_programs(0)
    slot = j % 2
    n_rows = tm * TOP_K

    def gather(idx_ref, s):
        def body(r, carry):
            pltpu.make_async_copy(ys_hbm.at[pl.ds(idx_ref[0, r], 1), :],
                                  ybuf.at[s, pl.ds(r, 1), :], sem.at[s]).start()
            return carry
        lax.fori_loop(0, n_rows, body, 0, unroll=8)

    @pl.when(j == 0)
    def _():
        gather(cur_ref, 0)

    @pl.when(j + 1 < nblk)
    def _():
        gather(nxt_ref, 1 - slot)

    pltpu.make_async_copy(ys_hbm.at[pl.ds(0, n_rows), :], ybuf.at[slot], sem.at[slot]).wait()
    g4 = g4_ref[...]
    ffn = jnp.zeros((tm, h_ref.shape[1]), F32)
    for k in range(TOP_K):
        ffn = ffn + ybuf[slot, k * tm:(k + 1) * tm, :] * g4[:, k:k + 1]
    o_ref[...] = _layer_norm(alpha * h_ref[...] + ffn, lg_ref[...], lb_ref[...])


def _combine(ys, pos_km, gates_pad, h1, ln_g, ln_b, alpha, tm=128):
    n, d = h1.shape
    nb = n // tm
    idx3 = pos_km.reshape(nb, 1, tm * TOP_K)
    return pl.pallas_call(
        functools.partial(_combine_kernel, alpha=alpha, tm=tm),
        out_shape=jax.ShapeDtypeStruct((n, d), F32),
        grid=(nb,),
        in_specs=[
            pl.BlockSpec((None, 1, tm * TOP_K), lambda j: (j, 0, 0), memory_space=pltpu.SMEM),
            pl.BlockSpec((None, 1, tm * TOP_K), lambda j: (jnp.minimum(j + 1, nb - 1), 0, 0),
                         memory_space=pltpu.SMEM),
            pl.BlockSpec(memory_space=pl.ANY),
            pl.BlockSpec((tm, LANES), lambda j: (j, 0)),
            pl.BlockSpec((tm, d), lambda j: (j, 0)),
            _const_spec((1, d)), _const_spec((1, d)),
        ],
        out_specs=pl.BlockSpec((tm, d), lambda j: (j, 0)),
        scratch_shapes=[pltpu.VMEM((2, tm * TOP_K, d), F32), pltpu.SemaphoreType.DMA((2,))],
        compiler_params=_cparams(("arbitrary",)),
        name="moe_combine",
    )(idx3, idx3, ys, gates_pad, h1, ln_g.reshape(1, d), ln_b.reshape(1, d))


def _moe_routing(top_e, rank, counts, n_tok, tm_combine):
    n_asg = n_tok * TOP_K
    counts = counts.reshape(N_EXPERTS).astype(I32)
    nblk_per = (counts + EXPERT_BLOCK - 1) // EXPERT_BLOCK
    blk_end = jnp.cumsum(nblk_per)
    pstarts = (blk_end - nblk_per) * EXPERT_BLOCK
    dest = jnp.take(pstarts, top_e, axis=0) + rank
    n_blocks = -(-n_asg // EXPERT_BLOCK) + N_EXPERTS
    blk_expert = jnp.minimum(jnp.sum(blk_end[None, :] <= jnp.arange(n_blocks, dtype=I32)[:, None], axis=1),
                             N_EXPERTS - 1)
    n_used = blk_end[-1]
    nb = n_tok // tm_combine
    pos_km = jnp.transpose(dest.reshape(nb, tm_combine, TOP_K), (0, 2, 1)).reshape(nb, TOP_K * tm_combine)
    return dest.astype(I32), blk_expert.astype(I32), n_used.astype(I32), pos_km.astype(I32)


def _permute_w_in(w):
    sizes = (N_HEADS * HEAD_DIM, KV_LATENT, IDX_HEADS * IDX_DIM, IDX_DIM, IDX_HEADS,
             CONV_WIDTH, CONV_WIDTH, CONV_WIDTH, SSM_WIDTH, MEM_HEADS * MEM_HEAD_DIM)
    offs = np.cumsum((0,) + sizes)
    q, ckv, qidx, kidx, widx, cu, cgb, cgc, ssm, memq = [w[:, offs[k]:offs[k + 1]] for k in range(len(sizes))]
    gates = w[:, offs[-1]:]
    pad = jnp.zeros((w.shape[0], LANES - IDX_DIM - IDX_HEADS), w.dtype)
    return jnp.concatenate([gates, q, cu, cgb, cgc, ssm, memq, qidx, ckv, kidx, widx, pad], axis=1).astype(BF16)


def kernel(x, mem, ln_in_g, ln_in_b, w_in, kv_norm_g, w_uk, w_uv, conv_w, conv_b, lam_re, lam_im, b_re, b_im, c_re, c_im, d_skip, log_dt, w_glu, b_glu, w_mem_kv, w_branch, w_o, ln1_g, ln1_b, w_router, b_router, w_up, b_up, w_down, b_down, ln2_g, ln2_b):
    bsz, seq, d = x.shape
    depth = w_in.shape[0]
    mlen = mem.shape[1]
    n = bsz * seq
    alpha = float((2 * depth) ** 0.25)
    tm_combine = 128

    h = _ln_in(x.reshape(n, d), ln_in_g, ln_in_b)
    n_row_blocks = -(-n * TOP_K // EXPERT_BLOCK) + N_EXPERTS
    xs = jnp.zeros((n_row_blocks * EXPERT_BLOCK, d), F32)
    mem2 = mem.reshape(bsz * mlen, d)
    for l in range(depth):
        proj = _matmul(h, _permute_w_in(w_in[l]), 256, "in_proj")
        qcat, kcat, ckvn, ckvt = _prep(proj, kv_norm_g[l], bsz, seq)
        att = _dsa_attention(proj, qcat, kcat, ckvn, ckvt, w_uk[l], w_uv[l], bsz, seq)
        cnv = _short_conv(proj, conv_w[l], conv_b[l], bsz, seq)
        ssm = _s5_mixer(proj, lam_re[l], lam_im[l], b_re[l], b_im[l], c_re[l], c_im[l], d_skip[l], log_dt[l],
                        w_glu[l], b_glu[l], bsz, seq)
        kvm = _matmul(mem2, w_mem_kv[l].astype(BF16), min(256, bsz * mlen), "mem_kv")
        mem_o = _memory_attention(proj, kvm, bsz, seq, mlen)
        h1, te_pad, tg_pad, counts = _merge_router(att, cnv, ssm, mem_o, proj, h, w_branch[l], w_o[l],
                                                   ln1_g[l], ln1_b[l], w_router[l], b_router[l], alpha)
        dest, blk_expert, n_used, pos_km = _moe_routing(te_pad[:, :TOP_K], te_pad[:, TOP_K:2 * TOP_K], counts,
                                                        n, tm_combine)
        xs = _dispatch(h1, dest, xs)
        ys = _expert_ffn(xs, blk_expert, n_used, l, w_up, b_up, w_down, b_down)
        h = _combine(ys, pos_km, tg_pad, h1, ln2_g[l], ln2_b[l], alpha, tm_combine)
    return h.reshape(bsz, seq, d)
```

```python
import functools
import math

import numpy as np
import jax
import jax.numpy as jnp
from jax import lax
from jax.experimental import pallas as pl
from jax.experimental.pallas import tpu as pltpu

F32 = jnp.float32
BF16 = jnp.bfloat16
I32 = jnp.int32
I16 = jnp.int16

N_HEADS = 8
HEAD_DIM = 64
KV_LATENT = 128
IDX_HEADS = 8
IDX_DIM = 32
INDEX_TOPK = 256
DSA_QUERY_BLOCK = 256
CONV_WIDTH = 512
SSM_WIDTH = 512
SSM_GROUP = 16
SSM_GROUPS = SSM_WIDTH // SSM_GROUP
SSM_STATE = 64
MEM_HEADS = 4
MEM_HEAD_DIM = 128
N_BRANCH = 4
BRANCH_WIDTH = 512
N_EXPERTS = 32
TOP_K = 4
D_EXPERT = 1024
SWIGLU_LIMIT = 7.0
SWIGLU_ALPHA = 1.702
EXPERT_BLOCK = 256
LN_EPS = 1e-5

LANES = 128
SUBLANES = 8
VMEM_LIMIT_BYTES = 56 * 1024 * 1024

INT_MIN = -(2 ** 31)
NEG_BIG = -1e30

COL_GATES = 0
COL_Q = 4096
COL_CONV_U = 4608
COL_CONV_GB = 5120
COL_CONV_GC = 5632
COL_SSM = 6144
COL_MEMQ = 6656
COL_QIDX = 7168
COL_CKV = 7424
COL_TAIL = 7552
D_PROJ = 7680
TAIL_W_OFF = IDX_DIM

SSM_SEQS = 8


def _cparams(sem):
    return pltpu.CompilerParams(dimension_semantics=sem, vmem_limit_bytes=VMEM_LIMIT_BYTES)


def _layer_norm(x, g, b):
    mu = jnp.mean(x, axis=-1, keepdims=True)
    xc = x - mu
    var = jnp.mean(xc * xc, axis=-1, keepdims=True)
    return xc * lax.rsqrt(var + LN_EPS) * g + b


def _const_spec(shape):
    nd = len(shape)
    return pl.BlockSpec(shape, lambda *_: (0,) * nd)


def _ln_kernel(x_ref, g_ref, b_ref, o_ref):
    o_ref[...] = _layer_norm(x_ref[...], g_ref[...], b_ref[...])


def _ln_in(x2, g, b, tm=512):
    n, d = x2.shape
    return pl.pallas_call(
        _ln_kernel,
        out_shape=jax.ShapeDtypeStruct((n, d), F32),
        grid=(n // tm,),
        in_specs=[pl.BlockSpec((tm, d), lambda i: (i, 0)), _const_spec((1, d)), _const_spec((1, d))],
        out_specs=pl.BlockSpec((tm, d), lambda i: (i, 0)),
        compiler_params=_cparams(("parallel",)),
        name="ln_in",
    )(x2, g.reshape(1, d), b.reshape(1, d))


def _matmul_kernel(x_ref, w_ref, o_ref):
    o_ref[...] = jnp.dot(x_ref[...].astype(BF16), w_ref[...], preferred_element_type=F32)


def _matmul(x, w_bf16, tm, name):
    n, k = x.shape
    m = w_bf16.shape[1]
    return pl.pallas_call(
        _matmul_kernel,
        out_shape=jax.ShapeDtypeStruct((n, m), F32),
        grid=(n // tm,),
        in_specs=[pl.BlockSpec((tm, k), lambda i: (i, 0)),
                  pl.BlockSpec((k, m), lambda i: (0, 0), pipeline_mode=pl.Buffered(1))],
        out_specs=pl.BlockSpec((tm, m), lambda i: (i, 0)),
        compiler_params=_cparams(("parallel",)),
        name=name,
    )(x, w_bf16)


def _split_hi_lo(x):
    hi = x.astype(BF16)
    lo = (x - hi.astype(F32)).astype(BF16)
    return hi, lo


def _prep_kernel(qidx_ref, ckv_ref, tail_ref, g_ref, sq_ref, sk_ref, qcat_ref, kcat_ref, ckvn_ref, ckvt_ref):
    q_hi, q_lo = _split_hi_lo(qidx_ref[...])
    qcat = jnp.dot(jnp.concatenate([q_hi, q_lo], axis=1), sq_ref[...], preferred_element_type=F32)
    qcat_ref[...] = qcat.astype(BF16)
    k_hi, k_lo = _split_hi_lo(tail_ref[...])
    kcat = jnp.dot(jnp.concatenate([k_hi, k_lo], axis=1), sk_ref[...], preferred_element_type=F32)
    kcat_ref[...] = kcat.astype(BF16)
    c = ckv_ref[...]
    ms = jnp.mean(c * c, axis=-1, keepdims=True)
    cn = c * lax.rsqrt(ms + LN_EPS) * g_ref[...]
    ckvn_ref[...] = cn.astype(BF16)
    ckvt_ref[...] = cn.T.astype(BF16)


def _selection_matrices():
    sq = np.zeros((2 * IDX_HEADS * IDX_DIM, IDX_HEADS * LANES), np.float32)
    for h in range(IDX_HEADS):
        for d in range(IDX_DIM):
            hi_in = h * IDX_DIM + d
            lo_in = IDX_HEADS * IDX_DIM + hi_in
            sq[hi_in, h * LANES + d] = 1.0
            sq[hi_in, h * LANES + IDX_DIM + d] = 1.0
            sq[lo_in, h * LANES + 2 * IDX_DIM + d] = 1.0
    sk = np.zeros((2 * LANES, LANES), np.float32)
    for d in range(IDX_DIM):
        sk[d, d] = 1.0
        sk[LANES + d, IDX_DIM + d] = 1.0
        sk[d, 2 * IDX_DIM + d] = 1.0
    return jnp.asarray(sq, BF16), jnp.asarray(sk, BF16)


def _prep(proj, kv_norm_g, bsz, seq, tm=512):
    n = proj.shape[0]
    tm = min(tm, seq)
    nbt = seq // tm
    sq, sk = _selection_matrices()
    return pl.pallas_call(
        _prep_kernel,
        out_shape=(jax.ShapeDtypeStruct((n, IDX_HEADS * LANES), BF16),
                   jax.ShapeDtypeStruct((n, LANES), BF16),
                   jax.ShapeDtypeStruct((n, KV_LATENT), BF16),
                   jax.ShapeDtypeStruct((bsz, KV_LATENT, seq), BF16)),
        grid=(n // tm,),
        in_specs=[pl.BlockSpec((tm, 256), lambda i: (i, COL_QIDX // 256)),
                  pl.BlockSpec((tm, 128), lambda i: (i, COL_CKV // 128)),
                  pl.BlockSpec((tm, 128), lambda i: (i, COL_TAIL // 128)),
                  _const_spec((1, KV_LATENT)), _const_spec(sq.shape), _const_spec(sk.shape)],
        out_specs=(pl.BlockSpec((tm, IDX_HEADS * LANES), lambda i: (i, 0)),
                   pl.BlockSpec((tm, LANES), lambda i: (i, 0)),
                   pl.BlockSpec((tm, KV_LATENT), lambda i: (i, 0)),
                   pl.BlockSpec((None, KV_LATENT, tm), lambda i: (i // nbt, 0, i % nbt))),
        compiler_params=_cparams(("parallel",)),
        name="dsa_prep",
    )(proj, proj, proj, kv_norm_g.reshape(1, KV_LATENT), sq, sk)


def _dsa_kernel(q_ref, tail_ref, qcat_ref, kcat_ref, ckv_ref, ckvt_ref, wuk_ref, wuv_ref, o_ref,
                key_ref, khi_ref, klo_ref, qlat_ref, m_ref, l_ref, acc_ref, *, seq, topk, cks, ck, cka, qb):
    i = pl.program_id(1)
    q0 = i * qb
    nchs = (q0 + qb + cks - 1) // cks
    nch = (q0 + qb + ck - 1) // ck
    q_pos = q0 + lax.broadcasted_iota(I32, (1, qb), 1)
    row_pos_s = lax.broadcasted_iota(I32, (cks, 1), 0)
    row_pos = lax.broadcasted_iota(I32, (ck, 1), 0)
    w_t = tail_ref[...].T
    nt_dims = (((1,), (1,)), ((), ()))
    cnt_rows = min(64, ck)

    def score_chunk(c, carry):
        k0 = pl.multiple_of(c * cks, cks)
        kc = kcat_ref[pl.ds(k0, cks), :]
        s = jnp.zeros((cks, qb), F32)
        for h in range(IDX_HEADS):
            d = lax.dot_general(kc, qcat_ref[:, h * LANES:(h + 1) * LANES], nt_dims, preferred_element_type=F32)
            s = s + jnp.maximum(d, 0.0) * w_t[TAIL_W_OFF + h:TAIL_W_OFF + h + 1, :]
        s = s + 0.0
        bits = pltpu.bitcast(s, I32)
        key = bits ^ ((bits >> 31) & jnp.int32(0x7FFFFFFF))
        key = jnp.where(k0 + row_pos_s <= q_pos, key, jnp.int32(INT_MIN))
        key_ref[pl.ds(k0, cks), :] = key
        khi_ref[pl.ds(k0, cks), :] = (key >> 16).astype(I16)
        klo_ref[pl.ds(k0, cks), :] = ((key & jnp.int32(0xFFFF)) - 32768).astype(I16)
        return carry

    lax.fori_loop(0, nchs, score_chunk, 0)

    def count_keys(pred_fn):
        def body(c, acc):
            k0 = pl.multiple_of(c * ck, ck)
            hit = jnp.where(pred_fn(key_ref[pl.ds(k0, ck), :], k0), 1.0, 0.0)
            return acc + jnp.sum(hit.reshape(ck // cnt_rows, cnt_rows, qb), axis=0)
        acc = lax.fori_loop(0, nch, body, jnp.zeros((cnt_rows, qb), F32))
        return jnp.sum(acc, axis=0, keepdims=True)

    kf = float(topk)
    pack = 16
    n_part = ck // (2 * pack)
    lo16, hi16 = -32768, 32767

    def count16(ref, cand16):
        cand = jnp.broadcast_to(cand16, (pack, qb)).astype(I16)
        one = jnp.ones((), I16)
        zero = jnp.zeros((), I16)

        def body(c, acc):
            k0 = pl.multiple_of(c * ck, ck)
            hit = jnp.where(ref[pl.ds(k0, ck), :].reshape(ck // pack, pack, qb) >= cand[None], one, zero)
            part = hit[0:2]
            for j in range(1, n_part):
                part = part + hit[2 * j:2 * j + 2]
            return acc + part.astype(I32).astype(F32)
        acc = lax.fori_loop(0, nch, body, jnp.zeros((2, pack, qb), F32))
        return jnp.sum(jnp.sum(acc, axis=0), axis=0, keepdims=True)

    def count16_above(ref, x16):
        return jnp.where(x16 == hi16, 0.0, count16(ref, jnp.minimum(x16 + 1, hi16)))

    def bisect16(ref, target):
        def one_pass(b, carry):
            w, cw = carry
            cand = w + jnp.left_shift(jnp.int32(1), 15 - b)
            cnt = count16(ref, cand)
            ok = cnt >= target
            return jnp.where(ok, cand, w), jnp.where(ok, cnt, cw)
        total = jnp.full((1, qb), 1.0, F32) * (nch * ck).astype(F32)
        return lax.fori_loop(0, 16, one_pass, (jnp.full((1, qb), lo16, I32), total))

    v16, c_v16 = bisect16(khi_ref, kf)
    c_above = count16_above(khi_ref, v16)
    v16_b = jnp.broadcast_to(v16, (pack, qb)).astype(I16)

    def build_lo(c, carry):
        k0 = pl.multiple_of(c * ck, ck)
        hi = khi_ref[pl.ds(k0, ck), :].reshape(ck // pack, pack, qb)
        lo = klo_ref[pl.ds(k0, ck), :].reshape(ck // pack, pack, qb)
        khi_ref[pl.ds(k0, ck), :] = jnp.where(hi == v16_b[None], lo, jnp.full((), lo16, I16)).reshape(ck, qb)
        return carry

    lax.fori_loop(0, nch, build_lo, 0)
    w, c_w = bisect16(khi_ref, kf - c_above)
    c_w = jnp.where(w == lo16, c_v16 - c_above, c_w)
    v = (v16 << 16) + (w + 32768)
    select_all = v == jnp.int32(INT_MIN)
    thr = jnp.maximum(v, jnp.int32(INT_MIN + 1))
    c_ge = jnp.where(select_all, c_above, c_above + c_w)
    c_gt = jnp.where(select_all, c_above, c_above + count16_above(khi_ref, w))
    need = kf - c_gt

    jlim_ref = m_ref.at[0:1]
    jlim_ref[...] = jnp.full((1, qb), float(seq), F32)
    has_tie = jnp.max(jnp.where(c_ge > kf, 1.0, 0.0)) > 0.0

    @pl.when(has_tie)
    def _():
        n_bits = max(1, int(math.ceil(math.log2(seq))))

        def pos_pass(b, j):
            cand = j + jnp.left_shift(jnp.int32(1), n_bits - 1 - b)
            cnt = count_keys(lambda key, k0: (key == thr) & (k0 + row_pos < cand))
            return jnp.where(cnt < need, cand, j)

        j = lax.fori_loop(0, n_bits, pos_pass, jnp.zeros((1, qb), I32))
        jlim_ref[...] = jnp.where(c_ge > kf, j.astype(F32), float(seq))

    jlim = jlim_ref[...].astype(I32)

    q_bf = q_ref[...].astype(BF16)
    for h in range(N_HEADS):
        ql = jnp.dot(q_bf, wuk_ref[h], preferred_element_type=F32) * (HEAD_DIM ** -0.5 * math.log2(math.e))
        qlat_ref[h] = ql.T.astype(BF16)

    m_ref[...] = jnp.full(m_ref.shape, NEG_BIG, F32)
    l_ref[...] = jnp.zeros(l_ref.shape, F32)
    acc_ref[...] = jnp.zeros(acc_ref.shape, F32)

    row_pos_a = lax.broadcasted_iota(I32, (cka, 1), 0)

    def att_chunk(c, carry):
        for sub in range(cks // cka):
            k0 = pl.multiple_of(c * cks + sub * cka, cka)
            kv = ckv_ref[pl.ds(k0, cka), :]
            kvt = ckvt_ref[:, pl.ds(k0, cka)]
            key = key_ref[pl.ds(k0, cka), :]
            sel = (key > thr) | ((key == thr) & (k0 + row_pos_a <= jlim))
            bias = jnp.where(sel, 0.0, NEG_BIG)
            for h in range(N_HEADS):
                lg = jnp.dot(kv, qlat_ref[h], preferred_element_type=F32) + bias
                m_prev = m_ref[h:h + 1, :]
                m_new = jnp.maximum(m_prev, jnp.max(lg, axis=0, keepdims=True))
                p = jnp.exp2(lg - m_new)
                alpha = jnp.exp2(m_prev - m_new)
                l_ref[h:h + 1, :] = alpha * l_ref[h:h + 1, :] + jnp.sum(p, axis=0, keepdims=True)
                acc_ref[h] = alpha * acc_ref[h] + jnp.dot(kvt, p.astype(BF16), preferred_element_type=F32)
                m_ref[h:h + 1, :] = m_new
        return carry

    lax.fori_loop(0, nchs, att_chunk, 0)

    out = jnp.zeros((qb, N_HEADS * HEAD_DIM), F32)
    for h in range(N_HEADS):
        o_lat = (acc_ref[h] / l_ref[h:h + 1, :]).T
        out = out + jnp.dot(o_lat.astype(BF16), wuv_ref[h], preferred_element_type=F32)
    o_ref[...] = out


def _dsa_attention(proj, qcat, kcat, ckvn, ckvt, w_uk, w_uv, bsz, seq):
    n = proj.shape[0]
    qb = min(DSA_QUERY_BLOCK, seq)
    n_blk = seq // qb
    topk = min(INDEX_TOPK, seq // 4)
    cks = min(512, seq)
    ck = min(256, seq)
    cka = min(128, seq)
    eye = jnp.eye(N_HEADS, dtype=F32)
    wuk = jnp.einsum('chd,hg->hgdc', w_uk, eye).reshape(N_HEADS, N_HEADS * HEAD_DIM, KV_LATENT).astype(BF16)
    wuv = jnp.einsum('chd,hg->hcgd', w_uv, eye).reshape(N_HEADS, KV_LATENT, N_HEADS * HEAD_DIM).astype(BF16)
    kernel = functools.partial(_dsa_kernel, seq=seq, topk=topk, cks=cks, ck=ck, cka=cka, qb=qb)
    return pl.pallas_call(
        kernel,
        out_shape=jax.ShapeDtypeStruct((n, N_HEADS * HEAD_DIM), F32),
        grid=(bsz, n_blk),
        in_specs=[pl.BlockSpec((qb, 512), lambda b, i: (b * n_blk + i, COL_Q // 512)),
                  pl.BlockSpec((qb, 128), lambda b, i: (b * n_blk + i, COL_TAIL // 128)),
                  pl.BlockSpec((qb, IDX_HEADS * LANES), lambda b, i: (b * n_blk + i, 0)),
                  pl.BlockSpec((seq, LANES), lambda b, i: (b, 0)),
                  pl.BlockSpec((seq, KV_LATENT), lambda b, i: (b, 0)),
                  pl.BlockSpec((None, KV_LATENT, seq), lambda b, i: (b, 0, 0)),
                  _const_spec(wuk.shape), _const_spec(wuv.shape)],
        out_specs=pl.BlockSpec((qb, N_HEADS * HEAD_DIM), lambda b, i: (b * n_blk + i, 0)),
        scratch_shapes=[pltpu.VMEM((seq, qb), I32),
                        pltpu.VMEM((seq, qb), I16),
                        pltpu.VMEM((seq, qb), I16),
                        pltpu.VMEM((N_HEADS, KV_LATENT, qb), BF16),
                        pltpu.VMEM((N_HEADS, qb), F32),
                        pltpu.VMEM((N_HEADS, qb), F32),
                        pltpu.VMEM((N_HEADS, KV_LATENT, qb), F32)],
        compiler_params=_cparams(("parallel", "parallel")),
        name="dsa_attention",
    )(proj, proj, qcat, kcat, ckvn, ckvt, wuk, wuv)


def _conv_kernel(u_ref, gb_ref, gc_ref, pu_ref, pgc_ref, w_ref, b_ref, o_ref):
    i = pl.program_id(1)
    tb = u_ref.shape[0]
    v = gc_ref[...] * u_ref[...]
    halo = jnp.where(i > 0, pgc_ref[...] * pu_ref[...], 0.0)
    vfull = jnp.concatenate([halo, v], axis=0)
    v1 = pltpu.roll(vfull, 1, 0)[SUBLANES:SUBLANES + tb]
    v2 = pltpu.roll(vfull, 2, 0)[SUBLANES:SUBLANES + tb]
    w = w_ref[...]
    y = w[0:1] * v2 + w[1:2] * v1 + w[2:3] * v
    o_ref[...] = gb_ref[...] * (y + b_ref[...])


def _short_conv(proj, conv_w, conv_b, bsz, seq, tb=512):
    n = proj.shape[0]
    tb = min(tb, seq)
    nb = seq // tb
    cw = CONV_WIDTH
    blk = lambda col: pl.BlockSpec((tb, cw), lambda b, i: (b * nb + i, col // cw))
    prev = lambda col: pl.BlockSpec(
        (SUBLANES, cw), lambda b, i: (jnp.maximum((b * nb + i) * (tb // SUBLANES) - 1, 0), col // cw))
    return pl.pallas_call(
        _conv_kernel,
        out_shape=jax.ShapeDtypeStruct((n, cw), F32),
        grid=(bsz, nb),
        in_specs=[blk(COL_CONV_U), blk(COL_CONV_GB), blk(COL_CONV_GC), prev(COL_CONV_U), prev(COL_CONV_GC),
                  _const_spec((SUBLANES, cw)), _const_spec((1, cw))],
        out_specs=pl.BlockSpec((tb, cw), lambda b, i: (b * nb + i, 0)),
        compiler_params=_cparams(("parallel", "parallel")),
        name="short_conv",
    )(proj, proj, proj, proj, proj,
      jnp.pad(conv_w, ((0, SUBLANES - conv_w.shape[0]), (0, 0))), conv_b.reshape(1, cw))


def _s5_scan_chunk(bu_ref, a_ref, st_ref, n_steps, store):
    s_tot = a_ref.shape[1] // 2
    cb = 512
    for blk in range(s_tot // cb):
        re_sl = pl.ds(blk * cb, cb)
        im_sl = pl.ds(s_tot + blk * cb, cb)
        ar = jnp.broadcast_to(a_ref[0:1, blk * cb:(blk + 1) * cb], (SSM_SEQS, cb))
        ai = jnp.broadcast_to(a_ref[0:1, s_tot + blk * cb:s_tot + (blk + 1) * cb], (SSM_SEQS, cb))

        def step(t, carry):
            re, im = carry
            r0 = pl.multiple_of(t * SSM_SEQS, SSM_SEQS)
            br = bu_ref[pl.ds(r0, SSM_SEQS), re_sl]
            bi = bu_ref[pl.ds(r0, SSM_SEQS), im_sl]
            nre = ar * re - ai * im + br
            nim = ar * im + ai * re + bi
            if store:
                bu_ref[pl.ds(r0, SSM_SEQS), re_sl] = nre
                bu_ref[pl.ds(r0, SSM_SEQS), im_sl] = nim
            return nre, nim

        re, im = lax.fori_loop(0, n_steps, step, (st_ref[:, re_sl], st_ref[:, im_sl]), unroll=8)
        st_ref[:, re_sl] = re
        st_ref[:, im_sl] = im


def _gelu_tanh(x):
    return 0.5 * x * (1.0 + jnp.tanh(math.sqrt(2.0 / math.pi) * (x + 0.044715 * (x * x * x))))


S5_ROWS = 512


def _s5_kernel(u_ref, perm_ref, wb_ref, a_ref, apow_ref, wc_ref, d_ref, wg_ref, bg_ref, o_ref,
               bu_ref, st_ref, carry_ref, *, seg_len, steps_per_seq):
    c = pl.program_id(0)
    s_tot = a_ref.shape[1] // 2
    tn_dims = (((0,), (0,)), ((), ()))

    @pl.when(c % steps_per_seq == 0)
    def _():
        carry_ref[...] = jnp.zeros(carry_ref.shape, F32)

    u = u_ref[...]
    perm = perm_ref[...]
    u_il = jnp.dot(perm, u.astype(BF16), preferred_element_type=F32).astype(BF16)
    hw = u_il.shape[1] // 2
    hs = s_tot // 2
    for part in range(2):
        for q in range(2):
            c0 = part * s_tot + q * hs
            bu_ref[:, c0:c0 + hs] = jnp.dot(u_il[:, q * hw:(q + 1) * hw], wb_ref[q * hw:(q + 1) * hw, c0:c0 + hs],
                                            preferred_element_type=F32)

    st_ref[...] = jnp.zeros(st_ref.shape, F32)
    _s5_scan_chunk(bu_ref, a_ref, st_ref, seg_len, store=False)

    pr = apow_ref[0:1, :s_tot]
    pi = apow_ref[0:1, s_tot:]
    hre = carry_ref[0:1, :s_tot]
    him = carry_ref[0:1, s_tot:]
    for j in range(SSM_SEQS):
        ere = st_ref[j:j + 1, :s_tot]
        eim = st_ref[j:j + 1, s_tot:]
        st_ref[j:j + 1, :s_tot] = hre
        st_ref[j:j + 1, s_tot:] = him
        hre, him = pr * hre - pi * him + ere, pr * him + pi * hre + eim
    carry_ref[0:1, :s_tot] = hre
    carry_ref[0:1, s_tot:] = him

    _s5_scan_chunk(bu_ref, a_ref, st_ref, seg_len, store=True)
    halves = []
    for q in range(2):
        acc = None
        for part in range(2):
            c0 = part * s_tot + q * hs
            term = jnp.dot(bu_ref[:, c0:c0 + hs].astype(BF16), wc_ref[c0:c0 + hs, q * hw:(q + 1) * hw],
                           preferred_element_type=F32)
            acc = term if acc is None else acc + term
        halves.append(acc)
    ch = jnp.concatenate(halves, axis=1)
    ch_hi, ch_lo = _split_hi_lo(ch)
    ch_nat = (lax.dot_general(perm, ch_hi, tn_dims, preferred_element_type=F32)
              + lax.dot_general(perm, ch_lo, tn_dims, preferred_element_type=F32))
    z = _gelu_tanh(ch_nat + d_ref[...] * u)
    gate = jnp.dot(z.astype(BF16), wg_ref[...], preferred_element_type=F32) + bg_ref[...]
    o_ref[...] = z * jax.nn.sigmoid(gate)


def _s5_mixer(proj, lam_re, lam_im, b_re, b_im, c_re, c_im, d_skip, log_dt, w_glu, b_glu, bsz, seq):
    n = proj.shape[0]
    g, p, nn = SSM_GROUPS, SSM_STATE, SSM_GROUP
    s_tot = g * p
    rows = min(S5_ROWS, seq)
    seg_len = rows // SSM_SEQS

    dt = jnp.exp(log_dt.astype(F32))[:, None]
    lam = lax.complex(lam_re.astype(F32), lam_im.astype(F32))
    a_bar = jnp.exp(lam * dt)
    b_bar = ((a_bar - 1.0) / lam)[:, :, None] * lax.complex(b_re.astype(F32), b_im.astype(F32))
    a_pow = jnp.exp(lam * dt * seg_len)
    a_vec = jnp.concatenate([jnp.real(a_bar).reshape(1, s_tot), jnp.imag(a_bar).reshape(1, s_tot)], axis=1)
    apow_vec = jnp.concatenate([jnp.real(a_pow).reshape(1, s_tot), jnp.imag(a_pow).reshape(1, s_tot)], axis=1)
    eye = jnp.eye(g, dtype=F32)
    wb = jnp.concatenate(
        [jnp.einsum('gpn,gh->gnhp', jnp.real(b_bar), eye).reshape(g * nn, s_tot),
         jnp.einsum('gpn,gh->gnhp', jnp.imag(b_bar), eye).reshape(g * nn, s_tot)], axis=1).astype(BF16)
    wc = jnp.concatenate(
        [jnp.einsum('gnp,gh->gphn', c_re.astype(F32), eye).reshape(s_tot, g * nn),
         -jnp.einsum('gnp,gh->gphn', c_im.astype(F32), eye).reshape(s_tot, g * nn)], axis=0).astype(BF16)
    perm_np = np.zeros((rows, rows), np.float32)
    t_idx, j_idx = np.meshgrid(np.arange(seg_len), np.arange(SSM_SEQS), indexing='ij')
    perm_np[(t_idx * SSM_SEQS + j_idx).ravel(), (j_idx * seg_len + t_idx).ravel()] = 1.0
    perm = jnp.asarray(perm_np, BF16)

    return pl.pallas_call(
        functools.partial(_s5_kernel, seg_len=seg_len, steps_per_seq=seq // rows),
        out_shape=jax.ShapeDtypeStruct((n, SSM_WIDTH), F32),
        grid=(n // rows,),
        in_specs=[pl.BlockSpec((rows, SSM_WIDTH), lambda c: (c, COL_SSM // SSM_WIDTH)),
                  _const_spec(perm.shape), _const_spec(wb.shape), _const_spec(a_vec.shape),
                  _const_spec(apow_vec.shape), _const_spec(wc.shape), _const_spec((1, SSM_WIDTH)),
                  _const_spec((SSM_WIDTH, SSM_WIDTH)), _const_spec((1, SSM_WIDTH))],
        out_specs=pl.BlockSpec((rows, SSM_WIDTH), lambda c: (c, 0)),
        scratch_shapes=[pltpu.VMEM((rows, 2 * s_tot), F32), pltpu.VMEM((SSM_SEQS, 2 * s_tot), F32),
                        pltpu.VMEM((1, 2 * s_tot), F32)],
        compiler_params=_cparams(("arbitrary",)),
        name="s5_mixer",
    )(proj, perm, wb, a_vec, apow_vec, wc, d_skip.reshape(1, SSM_WIDTH).astype(F32),
      w_glu.astype(BF16), b_glu.reshape(1, SSM_WIDTH))


def _mem_attn_kernel(q_ref, kv_ref, o_ref):
    hw = MEM_HEADS * MEM_HEAD_DIM
    q = q_ref[...].astype(BF16)
    kv = kv_ref[...].astype(BF16)
    for h in range(MEM_HEADS):
        sl = slice(h * MEM_HEAD_DIM, (h + 1) * MEM_HEAD_DIM)
        k = kv[:, sl]
        v = kv[:, hw + h * MEM_HEAD_DIM:hw + (h + 1) * MEM_HEAD_DIM]
        lg = lax.dot_general(q[:, sl], k, (((1,), (1,)), ((), ())),
                             preferred_element_type=F32) * (MEM_HEAD_DIM ** -0.5)
        m = jnp.max(lg, axis=1, keepdims=True)
        p = jnp.exp(lg - m)
        p = p / jnp.sum(p, axis=1, keepdims=True)
        o_ref[:, sl] = jnp.dot(p.astype(BF16), v, preferred_element_type=F32)


def _memory_attention(proj, kvm, bsz, seq, mlen, tm=512):
    n = proj.shape[0]
    tm = min(tm, seq)
    nb = seq // tm
    hw = MEM_HEADS * MEM_HEAD_DIM
    return pl.pallas_call(
        _mem_attn_kernel,
        out_shape=jax.ShapeDtypeStruct((n, hw), F32),
        grid=(bsz, nb),
        in_specs=[pl.BlockSpec((tm, hw), lambda b, i: (b * nb + i, COL_MEMQ // hw)),
                  pl.BlockSpec((mlen, 2 * hw), lambda b, i: (b, 0))],
        out_specs=pl.BlockSpec((tm, hw), lambda b, i: (b * nb + i, 0)),
        compiler_params=_cparams(("parallel", "parallel")),
        name="memory_attention",
    )(proj, kvm)


def _merge_kernel(att_ref, cnv_ref, ssm_ref, mem_ref, gl_ref, h_ref, wbr_ref, wo_ref, g_ref, b_ref,
                  wr_ref, br_ref, ltri_ref, h1_ref, te_ref, tg_ref, cnt_ref, base_ref, *, alpha):
    d = h_ref.shape[1]

    @pl.when(pl.program_id(0) == 0)
    def _():
        base_ref[...] = jnp.zeros(base_ref.shape, F32)

    merged = jnp.zeros(h_ref.shape, F32)
    for r, br in enumerate((att_ref, cnv_ref, ssm_ref, mem_ref)):
        y = jnp.dot(br[...].astype(BF16), wbr_ref[r], preferred_element_type=F32)
        merged = merged + y * jax.nn.sigmoid(gl_ref[:, r * d:(r + 1) * d])
    y = alpha * h_ref[...] + jnp.dot(merged.astype(BF16), wo_ref[...], preferred_element_type=F32)
    h1 = _layer_norm(y, g_ref[...], b_ref[...])
    h1_ref[...] = h1

    logits = jnp.dot(h1, wr_ref[...], preferred_element_type=F32, precision=lax.Precision.HIGHEST) + br_ref[...]
    tm = logits.shape[0]
    lane = lax.broadcasted_iota(I32, (tm, N_EXPERTS), 1)
    out_lane = lax.broadcasted_iota(I32, (tm, LANES), 1)
    work = logits
    top_e = jnp.zeros((tm, LANES), I32)
    top_v = jnp.zeros((tm, LANES), F32)
    vals = []
    onehots = []
    for k in range(TOP_K):
        mx = jnp.max(work, axis=1, keepdims=True)
        idx = jnp.min(jnp.where(work == mx, lane, N_EXPERTS), axis=1, keepdims=True)
        hit = lane == idx
        work = jnp.where(hit, -jnp.inf, work)
        top_e = jnp.where(out_lane == k, idx, top_e)
        vals.append(mx)
        onehots.append(jnp.where(hit, 1.0, 0.0))
    den = sum(jnp.exp(vk - vals[0]) for vk in vals)
    for k in range(TOP_K):
        top_v = jnp.where(out_lane == k, jnp.exp(vals[k] - vals[0]) / den, top_v)

    per_tok = onehots[0] + onehots[1] + onehots[2] + onehots[3]
    before = jnp.dot(ltri_ref[...], per_tok.astype(BF16), preferred_element_type=F32) + base_ref[...]
    for k in range(TOP_K):
        rank = jnp.sum(onehots[k] * before, axis=1, keepdims=True)
        top_e = jnp.where(out_lane == TOP_K + k, rank.astype(I32), top_e)
    base_ref[...] = base_ref[...] + jnp.sum(per_tok, axis=0, keepdims=True)
    cnt_ref[...] = base_ref[...]
    te_ref[...] = top_e
    tg_ref[...] = top_v


def _merge_router(att, cnv, ssm, mem_o, proj, h, w_branch, w_o, ln_g, ln_b, w_router, b_router, alpha, tm=256):
    n, d = h.shape
    bw = BRANCH_WIDTH
    row = lambda w: pl.BlockSpec((tm, w), lambda i: (i, 0))
    ltri = jnp.asarray(np.tril(np.ones((tm, tm), np.float32), -1), BF16)
    return pl.pallas_call(
        functools.partial(_merge_kernel, alpha=alpha),
        out_shape=(jax.ShapeDtypeStruct((n, d), F32),
                   jax.ShapeDtypeStruct((n, LANES), I32),
                   jax.ShapeDtypeStruct((n, LANES), F32),
                   jax.ShapeDtypeStruct((1, N_EXPERTS), F32)),
        grid=(n // tm,),
        in_specs=[row(bw), row(bw), row(bw), row(bw),
                  pl.BlockSpec((tm, N_BRANCH * d), lambda i: (i, COL_GATES // (N_BRANCH * d))),
                  row(d),
                  _const_spec((N_BRANCH, bw, d)), _const_spec((d, d)), _const_spec((1, d)), _const_spec((1, d)),
                  _const_spec((d, N_EXPERTS)), _const_spec((1, N_EXPERTS)), _const_spec((tm, tm))],
        out_specs=(row(d), row(LANES), row(LANES), _const_spec((1, N_EXPERTS))),
        scratch_shapes=[pltpu.VMEM((1, N_EXPERTS), F32)],
        compiler_params=_cparams(("arbitrary",)),
        name="merge_router",
    )(att, cnv, ssm, mem_o, proj, h, w_branch.astype(BF16), w_o.astype(BF16),
      ln_g.reshape(1, d), ln_b.reshape(1, d), w_router, b_router.reshape(1, N_EXPERTS), ltri)


DISPATCH_UNROLL = 4


def _dispatch_kernel(dest_ref, h_ref, xs_in_hbm, xs_hbm, sem, *, tm):
    del xs_in_hbm

    def body(r, carry):
        for k in range(TOP_K):
            pltpu.make_async_copy(h_ref.at[pl.ds(r, 1), :],
                                  xs_hbm.at[pl.ds(dest_ref[0, r * TOP_K + k], 1), :], sem).start()
        return carry

    lax.fori_loop(0, tm, body, 0, unroll=DISPATCH_UNROLL)
    pltpu.make_async_copy(xs_hbm.at[pl.ds(0, tm * TOP_K), :], xs_hbm.at[pl.ds(0, tm * TOP_K), :], sem).wait()


def _dispatch(h1, dest, xs_prev, tm=256):
    n, d = h1.shape
    nb = n // tm
    return pl.pallas_call(
        functools.partial(_dispatch_kernel, tm=tm),
        out_shape=jax.ShapeDtypeStruct(xs_prev.shape, F32),
        grid=(nb,),
        in_specs=[pl.BlockSpec((None, 1, tm * TOP_K), lambda i: (i, 0, 0), memory_space=pltpu.SMEM),
                  pl.BlockSpec((tm, d), lambda i: (i, 0)),
                  pl.BlockSpec(memory_space=pl.ANY)],
        out_specs=pl.BlockSpec(memory_space=pl.ANY),
        scratch_shapes=[pltpu.SemaphoreType.DMA(())],
        input_output_aliases={2: 0},
        compiler_params=_cparams(("arbitrary",)),
        name="moe_dispatch",
    )(dest.reshape(nb, 1, tm * TOP_K), h1, xs_prev)


def _expert_kernel(meta_ref, x_ref, wu_ref, bu_ref, wd_ref, bd_ref, sel_ref, o_ref, wu_bf, wd_bf):
    j = pl.program_id(0)
    n_used = meta_ref[0]
    f2 = wu_ref.shape[1]

    @pl.when((j < n_used) & ((j == 0) | (meta_ref[1 + j] != meta_ref[jnp.maximum(j, 1)])))
    def _():
        sel = sel_ref[...]
        cw = sel.shape[0]
        for c in range(f2 // cw):
            grp = jnp.dot(wu_ref[:, c * cw:(c + 1) * cw].astype(BF16), sel, preferred_element_type=F32)
            wu_bf[:, c * cw:(c + 1) * cw] = grp.astype(BF16)
        wd_bf[...] = wd_ref[...].astype(BF16)

    @pl.when(j < n_used)
    def _():
        xb = x_ref[...].astype(BF16)
        hdn = jnp.dot(xb, wu_bf[...], preferred_element_type=F32) + bu_ref[...]
        cw = sel_ref.shape[0]
        half = cw // 2
        parts = []
        for c in range(f2 // cw):
            h_glu = jnp.minimum(hdn[:, c * cw:c * cw + half], SWIGLU_LIMIT)
            h_lin = jnp.clip(hdn[:, c * cw + half:(c + 1) * cw], -SWIGLU_LIMIT, SWIGLU_LIMIT)
            parts.append((h_glu * jax.nn.sigmoid(SWIGLU_ALPHA * h_glu) * (h_lin + 1.0)).astype(BF16))
        act_c = jnp.concatenate(parts, axis=1)
        o_ref[...] = jnp.dot(act_c, wd_bf[...], preferred_element_type=F32) + bd_ref[...]

    @pl.when(j >= n_used)
    def _():
        o_ref[...] = jnp.zeros(o_ref.shape, F32)


def _expert_ffn(xs, blk_expert, n_used, layer, w_up, b_up, w_down, b_down):
    n_rows, d = xs.shape
    n_blocks = n_rows // EXPERT_BLOCK
    f = D_EXPERT
    meta = jnp.concatenate([n_used.reshape(1).astype(I32), blk_expert.astype(I32)])
    sel_np = np.zeros((2 * LANES, 2 * LANES), np.float32)
    sel_np[2 * np.arange(LANES), np.arange(LANES)] = 1.0
    sel_np[2 * np.arange(LANES) + 1, LANES + np.arange(LANES)] = 1.0
    sel = jnp.asarray(sel_np, BF16)
    le = b_up.shape[:2]
    b_up = jnp.transpose(b_up.reshape(le + (f // LANES, LANES, 2)), (0, 1, 2, 4, 3)).reshape(le + (2 * f,))
    e_of = lambda j, m: m[1 + j]
    grid_spec = pltpu.PrefetchScalarGridSpec(
        num_scalar_prefetch=1,
        grid=(n_blocks,),
        in_specs=[
            pl.BlockSpec((EXPERT_BLOCK, d), lambda j, m: (j, 0)),
            pl.BlockSpec((None, None, d, 2 * f), lambda j, m: (layer, e_of(j, m), 0, 0)),
            pl.BlockSpec((None, None, 1, 2 * f), lambda j, m: (layer, e_of(j, m), 0, 0)),
            pl.BlockSpec((None, None, f, d), lambda j, m: (layer, e_of(j, m), 0, 0)),
            pl.BlockSpec((None, None, 1, d), lambda j, m: (layer, e_of(j, m), 0, 0)),
            pl.BlockSpec(sel.shape, lambda j, m: (0, 0)),
        ],
        out_specs=pl.BlockSpec((EXPERT_BLOCK, d), lambda j, m: (j, 0)),
        scratch_shapes=[pltpu.VMEM((d, 2 * f), BF16), pltpu.VMEM((f, d), BF16)],
    )
    return pl.pallas_call(
        _expert_kernel,
        out_shape=jax.ShapeDtypeStruct((n_rows, d), F32),
        grid_spec=grid_spec,
        compiler_params=_cparams(("arbitrary",)),
        name="moe_experts",
    )(meta, xs, w_up, b_up[:, :, None, :], w_down, b_down[:, :, None, :], sel)


def _combine_kernel(cur_ref, nxt_ref, ys_hbm, g4_ref, h_ref, lg_ref, lb_ref, o_ref, ybuf, sem, *, alpha, tm):
    j = pl.program_id(0)
    nblk = pl.num_programs(0)
    slot = j % 2
    n_rows = tm * TOP_K

    def gather(idx_ref, s):
        def body(r, carry):
            pltpu.make_async_copy(ys_hbm.at[pl.ds(idx_ref[0, r], 1), :],
                                  ybuf.at[s, pl.ds(r, 1), :], sem.at[s]).start()
            return carry
        lax.fori_loop(0, n_rows, body, 0, unroll=8)

    @pl.when(j == 0)
    def _():
        gather(cur_ref, 0)

    @pl.when(j + 1 < nblk)
    def _():
        gather(nxt_ref, 1 - slot)

    pltpu.make_async_copy(ys_hbm.at[pl.ds(0, n_rows), :], ybuf.at[slot], sem.at[slot]).wait()
    g4 = g4_ref[...]
    ffn = jnp.zeros((tm, h_ref.shape[1]), F32)
    for k in range(TOP_K):
        ffn = ffn + ybuf[slot, k * tm:(k + 1) * tm, :] * g4[:, k:k + 1]
    o_ref[...] = _layer_norm(alpha * h_ref[...] + ffn, lg_ref[...], lb_ref[...])


def _combine(ys, pos_km, gates_pad, h1, ln_g, ln_b, alpha, tm=128):
    n, d = h1.shape
    nb = n // tm
    idx3 = pos_km.reshape(nb, 1, tm * TOP_K)
    return pl.pallas_call(
        functools.partial(_combine_kernel, alpha=alpha, tm=tm),
        out_shape=jax.ShapeDtypeStruct((n, d), F32),
        grid=(nb,),
        in_specs=[
            pl.BlockSpec((None, 1, tm * TOP_K), lambda j: (j, 0, 0), memory_space=pltpu.SMEM),
            pl.BlockSpec((None, 1, tm * TOP_K), lambda j: (jnp.minimum(j + 1, nb - 1), 0, 0),
                         memory_space=pltpu.SMEM),
            pl.BlockSpec(memory_space=pl.ANY),
            pl.BlockSpec((tm, LANES), lambda j: (j, 0)),
            pl.BlockSpec((tm, d), lambda j: (j, 0)),
            _const_spec((1, d)), _const_spec((1, d)),
        ],
        out_specs=pl.BlockSpec((tm, d), lambda j: (j, 0)),
        scratch_shapes=[pltpu.VMEM((2, tm * TOP_K, d), F32), pltpu.SemaphoreType.DMA((2,))],
        compiler_params=_cparams(("arbitrary",)),
        name="moe_combine",
    )(idx3, idx3, ys, gates_pad, h1, ln_g.reshape(1, d), ln_b.reshape(1, d))


def _moe_routing(top_e, rank, counts, n_tok, tm_combine):
    n_asg = n_tok * TOP_K
    counts = counts.reshape(N_EXPERTS).astype(I32)
    nblk_per = (counts + EXPERT_BLOCK - 1) // EXPERT_BLOCK
    blk_end = jnp.cumsum(nblk_per)
    pstarts = (blk_end - nblk_per) * EXPERT_BLOCK
    dest = jnp.take(pstarts, top_e, axis=0) + rank
    n_blocks = -(-n_asg // EXPERT_BLOCK) + N_EXPERTS
    blk_expert = jnp.minimum(jnp.sum(blk_end[None, :] <= jnp.arange(n_blocks, dtype=I32)[:, None], axis=1),
                             N_EXPERTS - 1)
    n_used = blk_end[-1]
    nb = n_tok // tm_combine
    pos_km = jnp.transpose(dest.reshape(nb, tm_combine, TOP_K), (0, 2, 1)).reshape(nb, TOP_K * tm_combine)
    return dest.astype(I32), blk_expert.astype(I32), n_used.astype(I32), pos_km.astype(I32)


def _permute_w_in(w):
    sizes = (N_HEADS * HEAD_DIM, KV_LATENT, IDX_HEADS * IDX_DIM, IDX_DIM, IDX_HEADS,
             CONV_WIDTH, CONV_WIDTH, CONV_WIDTH, SSM_WIDTH, MEM_HEADS * MEM_HEAD_DIM)
    offs = np.cumsum((0,) + sizes)
    q, ckv, qidx, kidx, widx, cu, cgb, cgc, ssm, memq = [w[:, offs[k]:offs[k + 1]] for k in range(len(sizes))]
    gates = w[:, offs[-1]:]
    pad = jnp.zeros((w.shape[0], LANES - IDX_DIM - IDX_HEADS), w.dtype)
    return jnp.concatenate([gates, q, cu, cgb, cgc, ssm, memq, qidx, ckv, kidx, widx, pad], axis=1).astype(BF16)


def kernel(x, mem, ln_in_g, ln_in_b, w_in, kv_norm_g, w_uk, w_uv, conv_w, conv_b, lam_re, lam_im, b_re, b_im, c_re, c_im, d_skip, log_dt, w_glu, b_glu, w_mem_kv, w_branch, w_o, ln1_g, ln1_b, w_router, b_router, w_up, b_up, w_down, b_down, ln2_g, ln2_b):
    bsz, seq, d = x.shape
    depth = w_in.shape[0]
    mlen = mem.shape[1]
    n = bsz * seq
    alpha = float((2 * depth) ** 0.25)
    tm_combine = 128

    h = _ln_in(x.reshape(n, d), ln_in_g, ln_in_b)
    n_row_blocks = -(-n * TOP_K // EXPERT_BLOCK) + N_EXPERTS
    xs = jnp.zeros((n_row_blocks * EXPERT_BLOCK, d), F32)
    mem2 = mem.reshape(bsz * mlen, d)
    for l in range(depth):
        proj = _matmul(h, _permute_w_in(w_in[l]), 256, "in_proj")
        qcat, kcat, ckvn, ckvt = _prep(proj, kv_norm_g[l], bsz, seq)
        att = _dsa_attention(proj, qcat, kcat, ckvn, ckvt, w_uk[l], w_uv[l], bsz, seq)
        cnv = _short_conv(proj, conv_w[l], conv_b[l], bsz, seq)
        ssm = _s5_mixer(proj, lam_re[l], lam_im[l], b_re[l], b_im[l], c_re[l], c_im[l], d_skip[l], log_dt[l],
                        w_glu[l], b_glu[l], bsz, seq)
        kvm = _matmul(mem2, w_mem_kv[l].astype(BF16), min(256, bsz * mlen), "mem_kv")
        mem_o = _memory_attention(proj, kvm, bsz, seq, mlen)
        h1, te_pad, tg_pad, counts = _merge_router(att, cnv, ssm, mem_o, proj, h, w_branch[l], w_o[l],
                                                   ln1_g[l], ln1_b[l], w_router[l], b_router[l], alpha)
        dest, blk_expert, n_used, pos_km = _moe_routing(te_pad[:, :TOP_K], te_pad[:, TOP_K:2 * TOP_K], counts,
                                                        n, tm_combine)
        xs = _dispatch(h1, dest, xs)
        ys = _expert_ffn(xs, blk_expert, n_used, l, w_up, b_up, w_down, b_down)
        h = _combine(ys, pos_km, tg_pad, h1, ln2_g[l], ln2_b[l], alpha, tm_combine)
    return h.reshape(bsz, seq, d)
```

```python
import functools
import math

import numpy as np
import jax
import jax.numpy as jnp
from jax import lax
from jax.experimental import pallas as pl
from jax.experimental.pallas import tpu as pltpu

F32 = jnp.float32
BF16 = jnp.bfloat16
I32 = jnp.int32
I16 = jnp.int16

N_HEADS = 8
HEAD_DIM = 64
KV_LATENT = 128
IDX_HEADS = 8
IDX_DIM = 32
INDEX_TOPK = 256
DSA_QUERY_BLOCK = 256
CONV_WIDTH = 512
SSM_WIDTH = 512
SSM_GROUP = 16
SSM_GROUPS = SSM_WIDTH // SSM_GROUP
SSM_STATE = 64
MEM_HEADS = 4
MEM_HEAD_DIM = 128
N_BRANCH = 4
BRANCH_WIDTH = 512
N_EXPERTS = 32
TOP_K = 4
D_EXPERT = 1024
SWIGLU_LIMIT = 7.0
SWIGLU_ALPHA = 1.702
EXPERT_BLOCK = 256
LN_EPS = 1e-5

LANES = 128
SUBLANES = 8
VMEM_LIMIT_BYTES = 56 * 1024 * 1024

INT_MIN = -(2 ** 31)
NEG_BIG = -1e30

COL_GATES = 0
COL_Q = 4096
COL_CONV_U = 4608
COL_CONV_GB = 5120
COL_CONV_GC = 5632
COL_SSM = 6144
COL_MEMQ = 6656
COL_QIDX = 7168
COL_CKV = 7424
COL_TAIL = 7552
D_PROJ = 7680
TAIL_W_OFF = IDX_DIM

SSM_SEQS = 8


def _cparams(sem):
    return pltpu.CompilerParams(dimension_semantics=sem, vmem_limit_bytes=VMEM_LIMIT_BYTES)


def _layer_norm(x, g, b):
    mu = jnp.mean(x, axis=-1, keepdims=True)
    xc = x - mu
    var = jnp.mean(xc * xc, axis=-1, keepdims=True)
    return xc * lax.rsqrt(var + LN_EPS) * g + b


def _const_spec(shape):
    nd = len(shape)
    return pl.BlockSpec(shape, lambda *_: (0,) * nd)


def _ln_kernel(x_ref, g_ref, b_ref, o_ref):
    o_ref[...] = _layer_norm(x_ref[...], g_ref[...], b_ref[...])


def _ln_in(x2, g, b, tm=512):
    n, d = x2.shape
    return pl.pallas_call(
        _ln_kernel,
        out_shape=jax.ShapeDtypeStruct((n, d), F32),
        grid=(n // tm,),
        in_specs=[pl.BlockSpec((tm, d), lambda i: (i, 0)), _const_spec((1, d)), _const_spec((1, d))],
        out_specs=pl.BlockSpec((tm, d), lambda i: (i, 0)),
        compiler_params=_cparams(("parallel",)),
        name="ln_in",
    )(x2, g.reshape(1, d), b.reshape(1, d))


def _matmul_kernel(x_ref, w_ref, o_ref):
    o_ref[...] = jnp.dot(x_ref[...].astype(BF16), w_ref[...], preferred_element_type=F32)


def _matmul(x, w_bf16, tm, name):
    n, k = x.shape
    m = w_bf16.shape[1]
    return pl.pallas_call(
        _matmul_kernel,
        out_shape=jax.ShapeDtypeStruct((n, m), F32),
        grid=(n // tm,),
        in_specs=[pl.BlockSpec((tm, k), lambda i: (i, 0)),
                  pl.BlockSpec((k, m), lambda i: (0, 0), pipeline_mode=pl.Buffered(1))],
        out_specs=pl.BlockSpec((tm, m), lambda i: (i, 0)),
        compiler_params=_cparams(("parallel",)),
        name=name,
    )(x, w_bf16)


def _split_hi_lo(x):
    hi = x.astype(BF16)
    lo = (x - hi.astype(F32)).astype(BF16)
    return hi, lo


def _prep_kernel(qidx_ref, ckv_ref, tail_ref, g_ref, sq_ref, sk_ref, qcat_ref, kcat_ref, ckvn_ref, ckvt_ref):
    q_hi, q_lo = _split_hi_lo(qidx_ref[...])
    qcat = jnp.dot(jnp.concatenate([q_hi, q_lo], axis=1), sq_ref[...], preferred_element_type=F32)
    qcat_ref[...] = qcat.astype(BF16)
    k_hi, k_lo = _split_hi_lo(tail_ref[...])
    kcat = jnp.dot(jnp.concatenate([k_hi, k_lo], axis=1), sk_ref[...], preferred_element_type=F32)
    kcat_ref[...] = kcat.astype(BF16)
    c = ckv_ref[...]
    ms = jnp.mean(c * c, axis=-1, keepdims=True)
    cn = c * lax.rsqrt(ms + LN_EPS) * g_ref[...]
    ckvn_ref[...] = cn.astype(BF16)
    ckvt_ref[...] = cn.T.astype(BF16)


def _selection_matrices():
    sq = np.zeros((2 * IDX_HEADS * IDX_DIM, IDX_HEADS * LANES), np.float32)
    for h in range(IDX_HEADS):
        for d in range(IDX_DIM):
            hi_in = h * IDX_DIM + d
            lo_in = IDX_HEADS * IDX_DIM + hi_in
            sq[hi_in, h * LANES + d] = 1.0
            sq[hi_in, h * LANES + IDX_DIM + d] = 1.0
            sq[lo_in, h * LANES + 2 * IDX_DIM + d] = 1.0
    sk = np.zeros((2 * LANES, LANES), np.float32)
    for d in range(IDX_DIM):
        sk[d, d] = 1.0
        sk[LANES + d, IDX_DIM + d] = 1.0
        sk[d, 2 * IDX_DIM + d] = 1.0
    return jnp.asarray(sq, BF16), jnp.asarray(sk, BF16)


def _prep(proj, kv_norm_g, bsz, seq, tm=512):
    n = proj.shape[0]
    tm = min(tm, seq)
    nbt = seq // tm
    sq, sk = _selection_matrices()
    return pl.pallas_call(
        _prep_kernel,
        out_shape=(jax.ShapeDtypeStruct((n, IDX_HEADS * LANES), BF16),
                   jax.ShapeDtypeStruct((n, LANES), BF16),
                   jax.ShapeDtypeStruct((n, KV_LATENT), BF16),
                   jax.ShapeDtypeStruct((bsz, KV_LATENT, seq), BF16)),
        grid=(n // tm,),
        in_specs=[pl.BlockSpec((tm, 256), lambda i: (i, COL_QIDX // 256)),
                  pl.BlockSpec((tm, 128), lambda i: (i, COL_CKV // 128)),
                  pl.BlockSpec((tm, 128), lambda i: (i, COL_TAIL // 128)),
                  _const_spec((1, KV_LATENT)), _const_spec(sq.shape), _const_spec(sk.shape)],
        out_specs=(pl.BlockSpec((tm, IDX_HEADS * LANES), lambda i: (i, 0)),
                   pl.BlockSpec((tm, LANES), lambda i: (i, 0)),
                   pl.BlockSpec((tm, KV_LATENT), lambda i: (i, 0)),
                   pl.BlockSpec((None, KV_LATENT, tm), lambda i: (i // nbt, 0, i % nbt))),
        compiler_params=_cparams(("parallel",)),
        name="dsa_prep",
    )(proj, proj, proj, kv_norm_g.reshape(1, KV_LATENT), sq, sk)


def _dsa_kernel(q_ref, tail_ref, qcat_ref, kcat_ref, ckv_ref, ckvt_ref, wuk_ref, wuv_ref, o_ref,
                key_ref, khi_ref, klo_ref, qlat_ref, m_ref, l_ref, acc_ref, *, seq, topk, cks, ck, cka, qb):
    i = pl.program_id(1)
    q0 = i * qb
    nchs = (q0 + qb + cks - 1) // cks
    nch = (q0 + qb + ck - 1) // ck
    q_pos = q0 + lax.broadcasted_iota(I32, (1, qb), 1)
    row_pos_s = lax.broadcasted_iota(I32, (cks, 1), 0)
    row_pos = lax.broadcasted_iota(I32, (ck, 1), 0)
    w_t = tail_ref[...].T
    nt_dims = (((1,), (1,)), ((), ()))
    cnt_rows = min(64, ck)

    def score_chunk(c, carry):
        k0 = pl.multiple_of(c * cks, cks)
        kc = kcat_ref[pl.ds(k0, cks), :]
        s = jnp.zeros((cks, qb), F32)
        for h in range(IDX_HEADS):
            d = lax.dot_general(kc, qcat_ref[:, h * LANES:(h + 1) * LANES], nt_dims, preferred_element_type=F32)
            s = s + jnp.maximum(d, 0.0) * w_t[TAIL_W_OFF + h:TAIL_W_OFF + h + 1, :]
        s = s + 0.0
        bits = pltpu.bitcast(s, I32)
        key = bits ^ ((bits >> 31) & jnp.int32(0x7FFFFFFF))
        key = jnp.where(k0 + row_pos_s <= q_pos, key, jnp.int32(INT_MIN))
        key_ref[pl.ds(k0, cks), :] = key
        khi_ref[pl.ds(k0, cks), :] = (key >> 16).astype(I16)
        klo_ref[pl.ds(k0, cks), :] = ((key & jnp.int32(0xFFFF)) - 32768).astype(I16)
        return carry

    lax.fori_loop(0, nchs, score_chunk, 0)

    def count_keys(pred_fn):
        def body(c, acc):
            k0 = pl.multiple_of(c * ck, ck)
            hit = jnp.where(pred_fn(key_ref[pl.ds(k0, ck), :], k0), 1.0, 0.0)
            return acc + jnp.sum(hit.reshape(ck // cnt_rows, cnt_rows, qb), axis=0)
        acc = lax.fori_loop(0, nch, body, jnp.zeros((cnt_rows, qb), F32))
        return jnp.sum(acc, axis=0, keepdims=True)

    kf = float(topk)
    pack = 16
    n_part = ck // (2 * pack)
    lo16, hi16 = -32768, 32767

    def count16(ref, cand16):
        cand = jnp.broadcast_to(cand16, (pack, qb)).astype(I16)
        one = jnp.ones((), I16)
        zero = jnp.zeros((), I16)

        def body(c, acc):
            k0 = pl.multiple_of(c * ck, ck)
            hit = jnp.where(ref[pl.ds(k0, ck), :].reshape(ck // pack, pack, qb) >= cand[None], one, zero)
            part = hit[0:2]
            for j in range(1, n_part):
                part = part + hit[2 * j:2 * j + 2]
            return acc + part.astype(I32).astype(F32)
        acc = lax.fori_loop(0, nch, body, jnp.zeros((2, pack, qb), F32))
        return jnp.sum(jnp.sum(acc, axis=0), axis=0, keepdims=True)

    def count16_above(ref, x16):
        return jnp.where(x16 == hi16, 0.0, count16(ref, jnp.minimum(x16 + 1, hi16)))

    def bisect16(ref, target):
        def one_pass(b, carry):
            w, cw = carry
            cand = w + jnp.left_shift(jnp.int32(1), 15 - b)
            cnt = count16(ref, cand)
            ok = cnt >= target
            return jnp.where(ok, cand, w), jnp.where(ok, cnt, cw)
        total = jnp.full((1, qb), 1.0, F32) * (nch * ck).astype(F32)
        return lax.fori_loop(0, 16, one_pass, (jnp.full((1, qb), lo16, I32), total))

    v16, c_v16 = bisect16(khi_ref, kf)
    c_above = count16_above(khi_ref, v16)
    v16_b = jnp.broadcast_to(v16, (pack, qb)).astype(I16)

    def build_lo(c, carry):
        k0 = pl.multiple_of(c * ck, ck)
        hi = khi_ref[pl.ds(k0, ck), :].reshape(ck // pack, pack, qb)
        lo = klo_ref[pl.ds(k0, ck), :].reshape(ck // pack, pack, qb)
        khi_ref[pl.ds(k0, ck), :] = jnp.where(hi == v16_b[None], lo, jnp.full((), lo16, I16)).reshape(ck, qb)
        return carry

    lax.fori_loop(0, nch, build_lo, 0)
    w, c_w = bisect16(khi_ref, kf - c_above)
    c_w = jnp.where(w == lo16, c_v16 - c_above, c_w)
    v = (v16 << 16) + (w + 32768)
    select_all = v == jnp.int32(INT_MIN)
    thr = jnp.maximum(v, jnp.int32(INT_MIN + 1))
    c_ge = jnp.where(select_all, c_above, c_above + c_w)
    c_gt = jnp.where(select_all, c_above, c_above + count16_above(khi_ref, w))
    need = kf - c_gt

    jlim_ref = m_ref.at[0:1]
    jlim_ref[...] = jnp.full((1, qb), float(seq), F32)
    has_tie = jnp.max(jnp.where(c_ge > kf, 1.0, 0.0)) > 0.0

    @pl.when(has_tie)
    def _():
        n_bits = max(1, int(math.ceil(math.log2(seq))))

        def pos_pass(b, j):
            cand = j + jnp.left_shift(jnp.int32(1), n_bits - 1 - b)
            cnt = count_keys(lambda key, k0: (key == thr) & (k0 + row_pos < cand))
            return jnp.where(cnt < need, cand, j)

        j = lax.fori_loop(0, n_bits, pos_pass, jnp.zeros((1, qb), I32))
        jlim_ref[...] = jnp.where(c_ge > kf, j.astype(F32), float(seq))

    jlim = jlim_ref[...].astype(I32)

    q_bf = q_ref[...].astype(BF16)
    for j in range(N_HEADS // 2):
        ql = jnp.dot(q_bf[:, j * LANES:(j + 1) * LANES], wuk_ref[j],
                     preferred_element_type=F32) * (HEAD_DIM ** -0.5 * math.log2(math.e))
        qlat_ref[2 * j] = ql[:, :KV_LATENT].T.astype(BF16)
        qlat_ref[2 * j + 1] = ql[:, KV_LATENT:].T.astype(BF16)

    m_ref[...] = jnp.full(m_ref.shape, NEG_BIG, F32)
    l_ref[...] = jnp.zeros(l_ref.shape, F32)
    acc_ref[...] = jnp.zeros(acc_ref.shape, F32)

    row_pos_a = lax.broadcasted_iota(I32, (cka, 1), 0)

    def att_tiles(base, n_sub):
        for sub in range(n_sub):
            k0 = pl.multiple_of(base + sub * cka, cka)
            kv = ckv_ref[pl.ds(k0, cka), :]
            kvt = ckvt_ref[:, pl.ds(k0, cka)]
            key = key_ref[pl.ds(k0, cka), :]
            sel = (key > thr) | ((key == thr) & (k0 + row_pos_a <= jlim))
            bias = jnp.where(sel, 0.0, NEG_BIG)
            for h in range(N_HEADS):
                lg = jnp.dot(kv, qlat_ref[h], preferred_element_type=F32) + bias
                m_prev = m_ref[h:h + 1, :]
                m_new = jnp.maximum(m_prev, jnp.max(lg, axis=0, keepdims=True))
                p = jnp.exp2(lg - m_new)
                alpha = jnp.exp2(m_prev - m_new)
                l_ref[h:h + 1, :] = alpha * l_ref[h:h + 1, :] + jnp.sum(p, axis=0, keepdims=True)
                acc_ref[h] = alpha * acc_ref[h] + jnp.dot(kvt, p.astype(BF16), preferred_element_type=F32)
                m_ref[h:h + 1, :] = m_new

    def att_chunk(c, carry):
        att_tiles(c * cks, cks // cka)
        return carry

    end = q0 + qb
    n_full = end // cks
    lax.fori_loop(0, n_full, att_chunk, 0)
    if cks % qb == 0 and cks > qb:
        for parts in range(1, cks // qb):
            @pl.when(end - n_full * cks == parts * qb)
            def _():
                att_tiles(n_full * cks, parts * qb // cka)
    else:
        @pl.when(end - n_full * cks > 0)
        def _():
            att_tiles(n_full * cks, cks // cka)

    outs = []
    for j in range(N_HEADS // 2):
        o_pair = [(acc_ref[h] / l_ref[h:h + 1, :]).T.astype(BF16) for h in (2 * j, 2 * j + 1)]
        outs.append(jnp.dot(jnp.concatenate(o_pair, axis=1), wuv_ref[j], preferred_element_type=F32))
    o_ref[...] = jnp.concatenate(outs, axis=1)


def _dsa_attention(proj, qcat, kcat, ckvn, ckvt, w_uk, w_uv, bsz, seq):
    n = proj.shape[0]
    qb = min(DSA_QUERY_BLOCK, seq)
    n_blk = seq // qb
    topk = min(INDEX_TOPK, seq // 4)
    cks = min(512, seq)
    ck = min(256, seq)
    cka = min(128, seq)
    zk = jnp.zeros((HEAD_DIM, KV_LATENT), F32)
    zv = jnp.zeros((KV_LATENT, HEAD_DIM), F32)
    wuk = jnp.stack([jnp.block([[w_uk[:, 2 * j, :].T, zk], [zk, w_uk[:, 2 * j + 1, :].T]])
                     for j in range(N_HEADS // 2)]).astype(BF16)
    wuv = jnp.stack([jnp.block([[w_uv[:, 2 * j, :], zv], [zv, w_uv[:, 2 * j + 1, :]]])
                     for j in range(N_HEADS // 2)]).astype(BF16)
    kernel = functools.partial(_dsa_kernel, seq=seq, topk=topk, cks=cks, ck=ck, cka=cka, qb=qb)
    return pl.pallas_call(
        kernel,
        out_shape=jax.ShapeDtypeStruct((n, N_HEADS * HEAD_DIM), F32),
        grid=(bsz, n_blk),
        in_specs=[pl.BlockSpec((qb, 512), lambda b, i: (b * n_blk + i, COL_Q // 512)),
                  pl.BlockSpec((qb, 128), lambda b, i: (b * n_blk + i, COL_TAIL // 128)),
                  pl.BlockSpec((qb, IDX_HEADS * LANES), lambda b, i: (b * n_blk + i, 0)),
                  pl.BlockSpec((seq, LANES), lambda b, i: (b, 0)),
                  pl.BlockSpec((seq, KV_LATENT), lambda b, i: (b, 0)),
                  pl.BlockSpec((None, KV_LATENT, seq), lambda b, i: (b, 0, 0)),
                  _const_spec(wuk.shape), _const_spec(wuv.shape)],
        out_specs=pl.BlockSpec((qb, N_HEADS * HEAD_DIM), lambda b, i: (b * n_blk + i, 0)),
        scratch_shapes=[pltpu.VMEM((seq, qb), I32),
                        pltpu.VMEM((seq, qb), I16),
                        pltpu.VMEM((seq, qb), I16),
                        pltpu.VMEM((N_HEADS, KV_LATENT, qb), BF16),
                        pltpu.VMEM((N_HEADS, qb), F32),
                        pltpu.VMEM((N_HEADS, qb), F32),
                        pltpu.VMEM((N_HEADS, KV_LATENT, qb), F32)],
        compiler_params=_cparams(("parallel", "parallel")),
        name="dsa_attention",
    )(proj, proj, qcat, kcat, ckvn, ckvt, wuk, wuv)


def _conv_kernel(u_ref, gb_ref, gc_ref, pu_ref, pgc_ref, w_ref, b_ref, o_ref):
    i = pl.program_id(1)
    tb = u_ref.shape[0]
    v = gc_ref[...] * u_ref[...]
    halo = jnp.where(i > 0, pgc_ref[...] * pu_ref[...], 0.0)
    vfull = jnp.concatenate([halo, v], axis=0)
    v1 = pltpu.roll(vfull, 1, 0)[SUBLANES:SUBLANES + tb]
    v2 = pltpu.roll(vfull, 2, 0)[SUBLANES:SUBLANES + tb]
    w = w_ref[...]
    y = w[0:1] * v2 + w[1:2] * v1 + w[2:3] * v
    o_ref[...] = gb_ref[...] * (y + b_ref[...])


def _short_conv(proj, conv_w, conv_b, bsz, seq, tb=512):
    n = proj.shape[0]
    tb = min(tb, seq)
    nb = seq // tb
    cw = CONV_WIDTH
    blk = lambda col: pl.BlockSpec((tb, cw), lambda b, i: (b * nb + i, col // cw))
    prev = lambda col: pl.BlockSpec(
        (SUBLANES, cw), lambda b, i: (jnp.maximum((b * nb + i) * (tb // SUBLANES) - 1, 0), col // cw))
    return pl.pallas_call(
        _conv_kernel,
        out_shape=jax.ShapeDtypeStruct((n, cw), F32),
        grid=(bsz, nb),
        in_specs=[blk(COL_CONV_U), blk(COL_CONV_GB), blk(COL_CONV_GC), prev(COL_CONV_U), prev(COL_CONV_GC),
                  _const_spec((SUBLANES, cw)), _const_spec((1, cw))],
        out_specs=pl.BlockSpec((tb, cw), lambda b, i: (b * nb + i, 0)),
        compiler_params=_cparams(("parallel", "parallel")),
        name="short_conv",
    )(proj, proj, proj, proj, proj,
      jnp.pad(conv_w, ((0, SUBLANES - conv_w.shape[0]), (0, 0))), conv_b.reshape(1, cw))


def _s5_scan_chunk(bu_ref, a_ref, st_ref, n_steps, store):
    s_tot = a_ref.shape[1] // 2
    cb = 512
    for blk in range(s_tot // cb):
        re_sl = pl.ds(blk * cb, cb)
        im_sl = pl.ds(s_tot + blk * cb, cb)
        ar = jnp.broadcast_to(a_ref[0:1, blk * cb:(blk + 1) * cb], (SSM_SEQS, cb))
        ai = jnp.broadcast_to(a_ref[0:1, s_tot + blk * cb:s_tot + (blk + 1) * cb], (SSM_SEQS, cb))

        def step(t, carry):
            re, im = carry
            r0 = pl.multiple_of(t * SSM_SEQS, SSM_SEQS)
            br = bu_ref[pl.ds(r0, SSM_SEQS), re_sl]
            bi = bu_ref[pl.ds(r0, SSM_SEQS), im_sl]
            nre = ar * re - ai * im + br
            nim = ar * im + ai * re + bi
            if store:
                bu_ref[pl.ds(r0, SSM_SEQS), re_sl] = nre
                bu_ref[pl.ds(r0, SSM_SEQS), im_sl] = nim
            return nre, nim

        re, im = lax.fori_loop(0, n_steps, step, (st_ref[:, re_sl], st_ref[:, im_sl]), unroll=8)
        st_ref[:, re_sl] = re
        st_ref[:, im_sl] = im


def _gelu_tanh(x):
    return 0.5 * x * (1.0 + jnp.tanh(math.sqrt(2.0 / math.pi) * (x + 0.044715 * (x * x * x))))


S5_ROWS = 512


def _s5_kernel(u_ref, perm_ref, wb_ref, a_ref, apow_ref, wc_ref, d_ref, wg_ref, bg_ref, o_ref,
               bu_ref, st_ref, carry_ref, *, seg_len, steps_per_seq):
    c = pl.program_id(0)
    s_tot = a_ref.shape[1] // 2
    tn_dims = (((0,), (0,)), ((), ()))

    @pl.when(c % steps_per_seq == 0)
    def _():
        carry_ref[...] = jnp.zeros(carry_ref.shape, F32)

    u = u_ref[...]
    perm = perm_ref[...]
    u_il = jnp.dot(perm, u.astype(BF16), preferred_element_type=F32).astype(BF16)
    hw = u_il.shape[1] // 2
    hs = s_tot // 2
    for part in range(2):
        for q in range(2):
            c0 = part * s_tot + q * hs
            bu_ref[:, c0:c0 + hs] = jnp.dot(u_il[:, q * hw:(q + 1) * hw], wb_ref[q * hw:(q + 1) * hw, c0:c0 + hs],
                                            preferred_element_type=F32)

    st_ref[...] = jnp.zeros(st_ref.shape, F32)
    _s5_scan_chunk(bu_ref, a_ref, st_ref, seg_len, store=False)

    pr = apow_ref[0:1, :s_tot]
    pi = apow_ref[0:1, s_tot:]
    hre = carry_ref[0:1, :s_tot]
    him = carry_ref[0:1, s_tot:]
    for j in range(SSM_SEQS):
        ere = st_ref[j:j + 1, :s_tot]
        eim = st_ref[j:j + 1, s_tot:]
        st_ref[j:j + 1, :s_tot] = hre
        st_ref[j:j + 1, s_tot:] = him
        hre, him = pr * hre - pi * him + ere, pr * him + pi * hre + eim
    carry_ref[0:1, :s_tot] = hre
    carry_ref[0:1, s_tot:] = him

    _s5_scan_chunk(bu_ref, a_ref, st_ref, seg_len, store=True)
    halves = []
    for q in range(2):
        acc = None
        for part in range(2):
            c0 = part * s_tot + q * hs
            term = jnp.dot(bu_ref[:, c0:c0 + hs].astype(BF16), wc_ref[c0:c0 + hs, q * hw:(q + 1) * hw],
                           preferred_element_type=F32)
            acc = term if acc is None else acc + term
        halves.append(acc)
    ch = jnp.concatenate(halves, axis=1)
    ch_hi, ch_lo = _split_hi_lo(ch)
    ch_nat = (lax.dot_general(perm, ch_hi, tn_dims, preferred_element_type=F32)
              + lax.dot_general(perm, ch_lo, tn_dims, preferred_element_type=F32))
    z = _gelu_tanh(ch_nat + d_ref[...] * u)
    gate = jnp.dot(z.astype(BF16), wg_ref[...], preferred_element_type=F32) + bg_ref[...]
    o_ref[...] = z * jax.nn.sigmoid(gate)


def _s5_mixer(proj, lam_re, lam_im, b_re, b_im, c_re, c_im, d_skip, log_dt, w_glu, b_glu, bsz, seq):
    n = proj.shape[0]
    g, p, nn = SSM_GROUPS, SSM_STATE, SSM_GROUP
    s_tot = g * p
    rows = min(S5_ROWS, seq)
    seg_len = rows // SSM_SEQS

    dt = jnp.exp(log_dt.astype(F32))[:, None]
    lam = lax.complex(lam_re.astype(F32), lam_im.astype(F32))
    a_bar = jnp.exp(lam * dt)
    b_bar = ((a_bar - 1.0) / lam)[:, :, None] * lax.complex(b_re.astype(F32), b_im.astype(F32))
    a_pow = jnp.exp(lam * dt * seg_len)
    a_vec = jnp.concatenate([jnp.real(a_bar).reshape(1, s_tot), jnp.imag(a_bar).reshape(1, s_tot)], axis=1)
    apow_vec = jnp.concatenate([jnp.real(a_pow).reshape(1, s_tot), jnp.imag(a_pow).reshape(1, s_tot)], axis=1)
    eye = jnp.eye(g, dtype=F32)
    wb = jnp.concatenate(
        [jnp.einsum('gpn,gh->gnhp', jnp.real(b_bar), eye).reshape(g * nn, s_tot),
         jnp.einsum('gpn,gh->gnhp', jnp.imag(b_bar), eye).reshape(g * nn, s_tot)], axis=1).astype(BF16)
    wc = jnp.concatenate(
        [jnp.einsum('gnp,gh->gphn', c_re.astype(F32), eye).reshape(s_tot, g * nn),
         -jnp.einsum('gnp,gh->gphn', c_im.astype(F32), eye).reshape(s_tot, g * nn)], axis=0).astype(BF16)
    perm_np = np.zeros((rows, rows), np.float32)
    t_idx, j_idx = np.meshgrid(np.arange(seg_len), np.arange(SSM_SEQS), indexing='ij')
    perm_np[(t_idx * SSM_SEQS + j_idx).ravel(), (j_idx * seg_len + t_idx).ravel()] = 1.0
    perm = jnp.asarray(perm_np, BF16)

    return pl.pallas_call(
        functools.partial(_s5_kernel, seg_len=seg_len, steps_per_seq=seq // rows),
        out_shape=jax.ShapeDtypeStruct((n, SSM_WIDTH), F32),
        grid=(n // rows,),
        in_specs=[pl.BlockSpec((rows, SSM_WIDTH), lambda c: (c, COL_SSM // SSM_WIDTH)),
                  _const_spec(perm.shape), _const_spec(wb.shape), _const_spec(a_vec.shape),
                  _const_spec(apow_vec.shape), _const_spec(wc.shape), _const_spec((1, SSM_WIDTH)),
                  _const_spec((SSM_WIDTH, SSM_WIDTH)), _const_spec((1, SSM_WIDTH))],
        out_specs=pl.BlockSpec((rows, SSM_WIDTH), lambda c: (c, 0)),
        scratch_shapes=[pltpu.VMEM((rows, 2 * s_tot), F32), pltpu.VMEM((SSM_SEQS, 2 * s_tot), F32),
                        pltpu.VMEM((1, 2 * s_tot), F32)],
        compiler_params=_cparams(("arbitrary",)),
        name="s5_mixer",
    )(proj, perm, wb, a_vec, apow_vec, wc, d_skip.reshape(1, SSM_WIDTH).astype(F32),
      w_glu.astype(BF16), b_glu.reshape(1, SSM_WIDTH))


def _mem_attn_kernel(q_ref, kv_ref, o_ref):
    hw = MEM_HEADS * MEM_HEAD_DIM
    q = q_ref[...].astype(BF16)
    kv = kv_ref[...].astype(BF16)
    for h in range(MEM_HEADS):
        sl = slice(h * MEM_HEAD_DIM, (h + 1) * MEM_HEAD_DIM)
        k = kv[:, sl]
        v = kv[:, hw + h * MEM_HEAD_DIM:hw + (h + 1) * MEM_HEAD_DIM]
        lg = lax.dot_general(q[:, sl], k, (((1,), (1,)), ((), ())),
                             preferred_element_type=F32) * (MEM_HEAD_DIM ** -0.5)
        m = jnp.max(lg, axis=1, keepdims=True)
        p = jnp.exp(lg - m)
        p = p / jnp.sum(p, axis=1, keepdims=True)
        o_ref[:, sl] = jnp.dot(p.astype(BF16), v, preferred_element_type=F32)


def _memory_attention(proj, kvm, bsz, seq, mlen, tm=512):
    n = proj.shape[0]
    tm = min(tm, seq)
    nb = seq // tm
    hw = MEM_HEADS * MEM_HEAD_DIM
    return pl.pallas_call(
        _mem_attn_kernel,
        out_shape=jax.ShapeDtypeStruct((n, hw), F32),
        grid=(bsz, nb),
        in_specs=[pl.BlockSpec((tm, hw), lambda b, i: (b * nb + i, COL_MEMQ // hw)),
                  pl.BlockSpec((mlen, 2 * hw), lambda b, i: (b, 0))],
        out_specs=pl.BlockSpec((tm, hw), lambda b, i: (b * nb + i, 0)),
        compiler_params=_cparams(("parallel", "parallel")),
        name="memory_attention",
    )(proj, kvm)


def _merge_kernel(att_ref, cnv_ref, ssm_ref, mem_ref, gl_ref, h_ref, wbr_ref, wo_ref, g_ref, b_ref,
                  wr_ref, br_ref, ltri_ref, h1_ref, te_ref, tg_ref, cnt_ref, base_ref, *, alpha):
    d = h_ref.shape[1]

    @pl.when(pl.program_id(0) == 0)
    def _():
        base_ref[...] = jnp.zeros(base_ref.shape, F32)

    merged = jnp.zeros(h_ref.shape, F32)
    for r, br in enumerate((att_ref, cnv_ref, ssm_ref, mem_ref)):
        y = jnp.dot(br[...].astype(BF16), wbr_ref[r], preferred_element_type=F32)
        merged = merged + y * jax.nn.sigmoid(gl_ref[:, r * d:(r + 1) * d])
    y = alpha * h_ref[...] + jnp.dot(merged.astype(BF16), wo_ref[...], preferred_element_type=F32)
    h1 = _layer_norm(y, g_ref[...], b_ref[...])
    h1_ref[...] = h1

    logits = jnp.dot(h1, wr_ref[...], preferred_element_type=F32, precision=lax.Precision.HIGHEST) + br_ref[...]
    tm = logits.shape[0]
    lane = lax.broadcasted_iota(I32, (tm, N_EXPERTS), 1)
    out_lane = lax.broadcasted_iota(I32, (tm, LANES), 1)
    work = logits
    top_e = jnp.zeros((tm, LANES), I32)
    top_v = jnp.zeros((tm, LANES), F32)
    vals = []
    onehots = []
    for k in range(TOP_K):
        mx = jnp.max(work, axis=1, keepdims=True)
        idx = jnp.min(jnp.where(work == mx, lane, N_EXPERTS), axis=1, keepdims=True)
        hit = lane == idx
        work = jnp.where(hit, -jnp.inf, work)
        top_e = jnp.where(out_lane == k, idx, top_e)
        vals.append(mx)
        onehots.append(jnp.where(hit, 1.0, 0.0))
    den = sum(jnp.exp(vk - vals[0]) for vk in vals)
    for k in range(TOP_K):
        top_v = jnp.where(out_lane == k, jnp.exp(vals[k] - vals[0]) / den, top_v)

    per_tok = onehots[0] + onehots[1] + onehots[2] + onehots[3]
    before = jnp.dot(ltri_ref[...], per_tok.astype(BF16), preferred_element_type=F32) + base_ref[...]
    for k in range(TOP_K):
        rank = jnp.sum(onehots[k] * before, axis=1, keepdims=True)
        top_e = jnp.where(out_lane == TOP_K + k, rank.astype(I32), top_e)
    base_ref[...] = base_ref[...] + jnp.sum(per_tok, axis=0, keepdims=True)
    cnt_ref[...] = base_ref[...]
    te_ref[...] = top_e
    tg_ref[...] = top_v


def _merge_router(att, cnv, ssm, mem_o, proj, h, w_branch, w_o, ln_g, ln_b, w_router, b_router, alpha, tm=256):
    n, d = h.shape
    bw = BRANCH_WIDTH
    row = lambda w: pl.BlockSpec((tm, w), lambda i: (i, 0))
    ltri = jnp.asarray(np.tril(np.ones((tm, tm), np.float32), -1), BF16)
    return pl.pallas_call(
        functools.partial(_merge_kernel, alpha=alpha),
        out_shape=(jax.ShapeDtypeStruct((n, d), F32),
                   jax.ShapeDtypeStruct((n, LANES), I32),
                   jax.ShapeDtypeStruct((n, LANES), F32),
                   jax.ShapeDtypeStruct((1, N_EXPERTS), F32)),
        grid=(n // tm,),
        in_specs=[row(bw), row(bw), row(bw), row(bw),
                  pl.BlockSpec((tm, N_BRANCH * d), lambda i: (i, COL_GATES // (N_BRANCH * d))),
                  row(d),
                  _const_spec((N_BRANCH, bw, d)), _const_spec((d, d)), _const_spec((1, d)), _const_spec((1, d)),
                  _const_spec((d, N_EXPERTS)), _const_spec((1, N_EXPERTS)), _const_spec((tm, tm))],
        out_specs=(row(d), row(LANES), row(LANES), _const_spec((1, N_EXPERTS))),
        scratch_shapes=[pltpu.VMEM((1, N_EXPERTS), F32)],
        compiler_params=_cparams(("arbitrary",)),
        name="merge_router",
    )(att, cnv, ssm, mem_o, proj, h, w_branch.astype(BF16), w_o.astype(BF16),
      ln_g.reshape(1, d), ln_b.reshape(1, d), w_router, b_router.reshape(1, N_EXPERTS), ltri)


DISPATCH_UNROLL = 8
COMBINE_UNROLL = 16


def _dispatch_kernel(dest_ref, h_ref, xs_in_hbm, xs_hbm, sem, *, tm):
    del xs_in_hbm

    def body(r, carry):
        for k in range(TOP_K):
            pltpu.make_async_copy(h_ref.at[pl.ds(r, 1), :],
                                  xs_hbm.at[pl.ds(dest_ref[0, r * TOP_K + k], 1), :], sem).start()
        return carry

    lax.fori_loop(0, tm, body, 0, unroll=DISPATCH_UNROLL)
    pltpu.make_async_copy(xs_hbm.at[pl.ds(0, tm * TOP_K), :], xs_hbm.at[pl.ds(0, tm * TOP_K), :], sem).wait()


def _dispatch(h1, dest, xs_prev, tm=256):
    n, d = h1.shape
    nb = n // tm
    return pl.pallas_call(
        functools.partial(_dispatch_kernel, tm=tm),
        out_shape=jax.ShapeDtypeStruct(xs_prev.shape, F32),
        grid=(nb,),
        in_specs=[pl.BlockSpec((None, 1, tm * TOP_K), lambda i: (i, 0, 0), memory_space=pltpu.SMEM),
                  pl.BlockSpec((tm, d), lambda i: (i, 0)),
                  pl.BlockSpec(memory_space=pl.ANY)],
        out_specs=pl.BlockSpec(memory_space=pl.ANY),
        scratch_shapes=[pltpu.SemaphoreType.DMA(())],
        input_output_aliases={2: 0},
        compiler_params=_cparams(("arbitrary",)),
        name="moe_dispatch",
    )(dest.reshape(nb, 1, tm * TOP_K), h1, xs_prev)


def _expert_kernel(meta_ref, x_ref, wu_ref, bu_ref, wd_ref, bd_ref, sel_ref, o_ref, wu_bf, wd_bf):
    j = pl.program_id(0)
    n_used = meta_ref[0]
    f2 = wu_ref.shape[1]

    @pl.when((j < n_used) & ((j == 0) | (meta_ref[1 + j] != meta_ref[jnp.maximum(j, 1)])))
    def _():
        sel = sel_ref[...]
        cw = sel.shape[0]
        for c in range(f2 // cw):
            grp = jnp.dot(wu_ref[:, c * cw:(c + 1) * cw].astype(BF16), sel, preferred_element_type=F32)
            wu_bf[:, c * cw:(c + 1) * cw] = grp.astype(BF16)
        wd_bf[...] = wd_ref[...].astype(BF16)

    @pl.when(j < n_used)
    def _():
        xb = x_ref[...].astype(BF16)
        hdn = jnp.dot(xb, wu_bf[...], preferred_element_type=F32) + bu_ref[...]
        cw = sel_ref.shape[0]
        half = cw // 2
        parts = []
        for c in range(f2 // cw):
            h_glu = jnp.minimum(hdn[:, c * cw:c * cw + half], SWIGLU_LIMIT)
            h_lin = jnp.clip(hdn[:, c * cw + half:(c + 1) * cw], -SWIGLU_LIMIT, SWIGLU_LIMIT)
            parts.append((h_glu * jax.nn.sigmoid(SWIGLU_ALPHA * h_glu) * (h_lin + 1.0)).astype(BF16))
        act_c = jnp.concatenate(parts, axis=1)
        o_ref[...] = jnp.dot(act_c, wd_bf[...], preferred_element_type=F32) + bd_ref[...]

    @pl.when(j >= n_used)
    def _():
        o_ref[...] = jnp.zeros(o_ref.shape, F32)


def _expert_ffn(xs, blk_expert, n_used, layer, w_up, b_up, w_down, b_down):
    n_rows, d = xs.shape
    n_blocks = n_rows // EXPERT_BLOCK
    f = D_EXPERT
    meta = jnp.concatenate([n_used.reshape(1).astype(I32), blk_expert.astype(I32)])
    sel_np = np.zeros((2 * LANES, 2 * LANES), np.float32)
    sel_np[2 * np.arange(LANES), np.arange(LANES)] = 1.0
    sel_np[2 * np.arange(LANES) + 1, LANES + np.arange(LANES)] = 1.0
    sel = jnp.asarray(sel_np, BF16)
    le = b_up.shape[:2]
    b_up = jnp.transpose(b_up.reshape(le + (f // LANES, LANES, 2)), (0, 1, 2, 4, 3)).reshape(le + (2 * f,))
    e_of = lambda j, m: m[1 + j]
    grid_spec = pltpu.PrefetchScalarGridSpec(
        num_scalar_prefetch=1,
        grid=(n_blocks,),
        in_specs=[
            pl.BlockSpec((EXPERT_BLOCK, d), lambda j, m: (j, 0)),
            pl.BlockSpec((None, None, d, 2 * f), lambda j, m: (layer, e_of(j, m), 0, 0)),
            pl.BlockSpec((None, None, 1, 2 * f), lambda j, m: (layer, e_of(j, m), 0, 0)),
            pl.BlockSpec((None, None, f, d), lambda j, m: (layer, e_of(j, m), 0, 0)),
            pl.BlockSpec((None, None, 1, d), lambda j, m: (layer, e_of(j, m), 0, 0)),
            pl.BlockSpec(sel.shape, lambda j, m: (0, 0)),
        ],
        out_specs=pl.BlockSpec((EXPERT_BLOCK, d), lambda j, m: (j, 0)),
        scratch_shapes=[pltpu.VMEM((d, 2 * f), BF16), pltpu.VMEM((f, d), BF16)],
    )
    return pl.pallas_call(
        _expert_kernel,
        out_shape=jax.ShapeDtypeStruct((n_rows, d), F32),
        grid_spec=grid_spec,
        compiler_params=_cparams(("arbitrary",)),
        name="moe_experts",
    )(meta, xs, w_up, b_up[:, :, None, :], w_down, b_down[:, :, None, :], sel)


def _combine_kernel(cur_ref, nxt_ref, ys_hbm, g4_ref, h_ref, lg_ref, lb_ref, o_ref, ybuf, sem, *, alpha, tm):
    j = pl.program_id(0)
    nblk = pl.num_programs(0)
    slot = j % 2
    n_rows = tm * TOP_K

    def gather(idx_ref, s):
        def body(r, carry):
            pltpu.make_async_copy(ys_hbm.at[pl.ds(idx_ref[0, r], 1), :],
                                  ybuf.at[s, pl.ds(r, 1), :], sem.at[s]).start()
            return carry
        lax.fori_loop(0, n_rows, body, 0, unroll=COMBINE_UNROLL)

    @pl.when(j == 0)
    def _():
        gather(cur_ref, 0)

    @pl.when(j + 1 < nblk)
    def _():
        gather(nxt_ref, 1 - slot)

    pltpu.make_async_copy(ys_hbm.at[pl.ds(0, n_rows), :], ybuf.at[slot], sem.at[slot]).wait()
    g4 = g4_ref[...]
    ffn = jnp.zeros((tm, h_ref.shape[1]), F32)
    for k in range(TOP_K):
        ffn = ffn + ybuf[slot, k * tm:(k + 1) * tm, :] * g4[:, k:k + 1]
    o_ref[...] = _layer_norm(alpha * h_ref[...] + ffn, lg_ref[...], lb_ref[...])


def _combine(ys, pos_km, gates_pad, h1, ln_g, ln_b, alpha, tm=128):
    n, d = h1.shape
    nb = n // tm
    idx3 = pos_km.reshape(nb, 1, tm * TOP_K)
    return pl.pallas_call(
        functools.partial(_combine_kernel, alpha=alpha, tm=tm),
        out_shape=jax.ShapeDtypeStruct((n, d), F32),
        grid=(nb,),
        in_specs=[
            pl.BlockSpec((None, 1, tm * TOP_K), lambda j: (j, 0, 0), memory_space=pltpu.SMEM),
            pl.BlockSpec((None, 1, tm * TOP_K), lambda j: (jnp.minimum(j + 1, nb - 1), 0, 0),
                         memory_space=pltpu.SMEM),
            pl.BlockSpec(memory_space=pl.ANY),
            pl.BlockSpec((tm, LANES), lambda j: (j, 0)),
            pl.BlockSpec((tm, d), lambda j: (j, 0)),
            _const_spec((1, d)), _const_spec((1, d)),
        ],
        out_specs=pl.BlockSpec((tm, d), lambda j: (j, 0)),
        scratch_shapes=[pltpu.VMEM((2, tm * TOP_K, d), F32), pltpu.SemaphoreType.DMA((2,))],
        compiler_params=_cparams(("arbitrary",)),
        name="moe_combine",
    )(idx3, idx3, ys, gates_pad, h1, ln_g.reshape(1, d), ln_b.reshape(1, d))


def _moe_routing(top_e, rank, counts, n_tok, tm_combine):
    n_asg = n_tok * TOP_K
    counts = counts.reshape(N_EXPERTS).astype(I32)
    nblk_per = (counts + EXPERT_BLOCK - 1) // EXPERT_BLOCK
    blk_end = jnp.cumsum(nblk_per)
    pstarts = (blk_end - nblk_per) * EXPERT_BLOCK
    onehot = top_e[..., None] == jnp.arange(N_EXPERTS, dtype=I32)
    dest = jnp.sum(jnp.where(onehot, pstarts, 0), axis=-1) + rank
    n_blocks = -(-n_asg // EXPERT_BLOCK) + N_EXPERTS
    blk_expert = jnp.minimum(jnp.sum(blk_end[None, :] <= jnp.arange(n_blocks, dtype=I32)[:, None], axis=1),
                             N_EXPERTS - 1)
    n_used = blk_end[-1]
    nb = n_tok // tm_combine
    pos_km = jnp.transpose(dest.reshape(nb, tm_combine, TOP_K), (0, 2, 1)).reshape(nb, TOP_K * tm_combine)
    return dest.astype(I32), blk_expert.astype(I32), n_used.astype(I32), pos_km.astype(I32)


def _permute_w_in(w):
    sizes = (N_HEADS * HEAD_DIM, KV_LATENT, IDX_HEADS * IDX_DIM, IDX_DIM, IDX_HEADS,
             CONV_WIDTH, CONV_WIDTH, CONV_WIDTH, SSM_WIDTH, MEM_HEADS * MEM_HEAD_DIM)
    offs = np.cumsum((0,) + sizes)
    q, ckv, qidx, kidx, widx, cu, cgb, cgc, ssm, memq = [w[:, offs[k]:offs[k + 1]] for k in range(len(sizes))]
    gates = w[:, offs[-1]:]
    pad = jnp.zeros((w.shape[0], LANES - IDX_DIM - IDX_HEADS), w.dtype)
    return jnp.concatenate([gates, q, cu, cgb, cgc, ssm, memq, qidx, ckv, kidx, widx, pad], axis=1).astype(BF16)


def kernel(x, mem, ln_in_g, ln_in_b, w_in, kv_norm_g, w_uk, w_uv, conv_w, conv_b, lam_re, lam_im, b_re, b_im, c_re, c_im, d_skip, log_dt, w_glu, b_glu, w_mem_kv, w_branch, w_o, ln1_g, ln1_b, w_router, b_router, w_up, b_up, w_down, b_down, ln2_g, ln2_b):
    bsz, seq, d = x.shape
    depth = w_in.shape[0]
    mlen = mem.shape[1]
    n = bsz * seq
    alpha = float((2 * depth) ** 0.25)
    tm_combine = 256

    h = _ln_in(x.reshape(n, d), ln_in_g, ln_in_b)
    n_row_blocks = -(-n * TOP_K // EXPERT_BLOCK) + N_EXPERTS
    xs = jnp.zeros((n_row_blocks * EXPERT_BLOCK, d), F32)
    mem2 = mem.reshape(bsz * mlen, d)
    for l in range(depth):
        proj = _matmul(h, _permute_w_in(w_in[l]), 256, "in_proj")
        qcat, kcat, ckvn, ckvt = _prep(proj, kv_norm_g[l], bsz, seq)
        att = _dsa_attention(proj, qcat, kcat, ckvn, ckvt, w_uk[l], w_uv[l], bsz, seq)
        cnv = _short_conv(proj, conv_w[l], conv_b[l], bsz, seq)
        ssm = _s5_mixer(proj, lam_re[l], lam_im[l], b_re[l], b_im[l], c_re[l], c_im[l], d_skip[l], log_dt[l],
                        w_glu[l], b_glu[l], bsz, seq)
        kvm = _matmul(mem2, w_mem_kv[l].astype(BF16), min(256, bsz * mlen), "mem_kv")
        mem_o = _memory_attention(proj, kvm, bsz, seq, mlen)
        h1, te_pad, tg_pad, counts = _merge_router(att, cnv, ssm, mem_o, proj, h, w_branch[l], w_o[l],
                                                   ln1_g[l], ln1_b[l], w_router[l], b_router[l], alpha)
        dest, blk_expert, n_used, pos_km = _moe_routing(te_pad[:, :TOP_K], te_pad[:, TOP_K:2 * TOP_K], counts,
                                                        n, tm_combine)
        xs = _dispatch(h1, dest, xs)
        ys = _expert_ffn(xs, blk_expert, n_used, l, w_up, b_up, w_down, b_down)
        h = _combine(ys, pos_km, tg_pad, h1, ln2_g[l], ln2_b[l], alpha, tm_combine)
    return h.reshape(bsz, seq, d)
```

```python
import functools
import math

import numpy as np
import jax
import jax.numpy as jnp
from jax import lax
from jax.experimental import pallas as pl
from jax.experimental.pallas import tpu as pltpu

F32 = jnp.float32
BF16 = jnp.bfloat16
I32 = jnp.int32
I16 = jnp.int16

N_HEADS = 8
HEAD_DIM = 64
KV_LATENT = 128
IDX_HEADS = 8
IDX_DIM = 32
INDEX_TOPK = 256
DSA_QUERY_BLOCK = 256
CONV_WIDTH = 512
SSM_WIDTH = 512
SSM_GROUP = 16
SSM_GROUPS = SSM_WIDTH // SSM_GROUP
SSM_STATE = 64
MEM_HEADS = 4
MEM_HEAD_DIM = 128
N_BRANCH = 4
BRANCH_WIDTH = 512
N_EXPERTS = 32
TOP_K = 4
D_EXPERT = 1024
SWIGLU_LIMIT = 7.0
SWIGLU_ALPHA = 1.702
EXPERT_BLOCK = 256
LN_EPS = 1e-5

LANES = 128
SUBLANES = 8
VMEM_LIMIT_BYTES = 56 * 1024 * 1024

INT_MIN = -(2 ** 31)
NEG_BIG = -1e30
SAFE_LOGIT = 100.0

COL_GATES = 0
COL_Q = 4096
COL_CONV_U = 4608
COL_CONV_GB = 5120
COL_CONV_GC = 5632
COL_SSM = 6144
COL_MEMQ = 6656
COL_QIDX = 7168
COL_CKV = 7424
COL_TAIL = 7552
D_PROJ = 7680
TAIL_W_OFF = IDX_DIM

SSM_SEQS = 8


def _cparams(sem):
    return pltpu.CompilerParams(dimension_semantics=sem, vmem_limit_bytes=VMEM_LIMIT_BYTES)


def _layer_norm(x, g, b):
    mu = jnp.mean(x, axis=-1, keepdims=True)
    xc = x - mu
    var = jnp.mean(xc * xc, axis=-1, keepdims=True)
    return xc * lax.rsqrt(var + LN_EPS) * g + b


def _const_spec(shape):
    nd = len(shape)
    return pl.BlockSpec(shape, lambda *_: (0,) * nd)


def _ln_kernel(x_ref, g_ref, b_ref, o_ref):
    o_ref[...] = _layer_norm(x_ref[...], g_ref[...], b_ref[...])


def _ln_in(x2, g, b, tm=512):
    n, d = x2.shape
    return pl.pallas_call(
        _ln_kernel,
        out_shape=jax.ShapeDtypeStruct((n, d), F32),
        grid=(n // tm,),
        in_specs=[pl.BlockSpec((tm, d), lambda i: (i, 0)), _const_spec((1, d)), _const_spec((1, d))],
        out_specs=pl.BlockSpec((tm, d), lambda i: (i, 0)),
        compiler_params=_cparams(("parallel",)),
        name="ln_in",
    )(x2, g.reshape(1, d), b.reshape(1, d))


def _matmul_kernel(x_ref, w_ref, o_ref):
    o_ref[...] = jnp.dot(x_ref[...].astype(BF16), w_ref[...], preferred_element_type=F32)


def _matmul(x, w_bf16, tm, name):
    n, k = x.shape
    m = w_bf16.shape[1]
    return pl.pallas_call(
        _matmul_kernel,
        out_shape=jax.ShapeDtypeStruct((n, m), F32),
        grid=(n // tm,),
        in_specs=[pl.BlockSpec((tm, k), lambda i: (i, 0)),
                  pl.BlockSpec((k, m), lambda i: (0, 0), pipeline_mode=pl.Buffered(1))],
        out_specs=pl.BlockSpec((tm, m), lambda i: (i, 0)),
        compiler_params=_cparams(("parallel",)),
        name=name,
    )(x, w_bf16)


def _split_hi_lo(x):
    hi = x.astype(BF16)
    lo = (x - hi.astype(F32)).astype(BF16)
    return hi, lo


def _prep_kernel(qidx_ref, ckv_ref, tail_ref, g_ref, sq_ref, sk_ref, qcat_ref, kcat_ref, ckvn_ref, ckvt_ref):
    q_hi, q_lo = _split_hi_lo(qidx_ref[...])
    qcat = jnp.dot(jnp.concatenate([q_hi, q_lo], axis=1), sq_ref[...], preferred_element_type=F32)
    qcat_ref[...] = qcat.astype(BF16)
    k_hi, k_lo = _split_hi_lo(tail_ref[...])
    kcat = jnp.dot(jnp.concatenate([k_hi, k_lo], axis=1), sk_ref[...], preferred_element_type=F32)
    kcat_ref[...] = kcat.astype(BF16)
    c = ckv_ref[...]
    ms = jnp.mean(c * c, axis=-1, keepdims=True)
    cn = c * lax.rsqrt(ms + LN_EPS) * g_ref[...]
    ckvn_ref[...] = cn.astype(BF16)
    ckvt_ref[...] = cn.T.astype(BF16)


def _selection_matrices():
    sq = np.zeros((2 * IDX_HEADS * IDX_DIM, IDX_HEADS * LANES), np.float32)
    for h in range(IDX_HEADS):
        for d in range(IDX_DIM):
            hi_in = h * IDX_DIM + d
            lo_in = IDX_HEADS * IDX_DIM + hi_in
            sq[hi_in, h * LANES + d] = 1.0
            sq[hi_in, h * LANES + IDX_DIM + d] = 1.0
            sq[lo_in, h * LANES + 2 * IDX_DIM + d] = 1.0
    sk = np.zeros((2 * LANES, LANES), np.float32)
    for d in range(IDX_DIM):
        sk[d, d] = 1.0
        sk[LANES + d, IDX_DIM + d] = 1.0
        sk[d, 2 * IDX_DIM + d] = 1.0
    return jnp.asarray(sq, BF16), jnp.asarray(sk, BF16)


def _prep(proj, kv_norm_g, bsz, seq, tm=512):
    n = proj.shape[0]
    tm = min(tm, seq)
    nbt = seq // tm
    sq, sk = _selection_matrices()
    return pl.pallas_call(
        _prep_kernel,
        out_shape=(jax.ShapeDtypeStruct((n, IDX_HEADS * LANES), BF16),
                   jax.ShapeDtypeStruct((n, LANES), BF16),
                   jax.ShapeDtypeStruct((n, KV_LATENT), BF16),
                   jax.ShapeDtypeStruct((bsz, KV_LATENT, seq), BF16)),
        grid=(n // tm,),
        in_specs=[pl.BlockSpec((tm, 256), lambda i: (i, COL_QIDX // 256)),
                  pl.BlockSpec((tm, 128), lambda i: (i, COL_CKV // 128)),
                  pl.BlockSpec((tm, 128), lambda i: (i, COL_TAIL // 128)),
                  _const_spec((1, KV_LATENT)), _const_spec(sq.shape), _const_spec(sk.shape)],
        out_specs=(pl.BlockSpec((tm, IDX_HEADS * LANES), lambda i: (i, 0)),
                   pl.BlockSpec((tm, LANES), lambda i: (i, 0)),
                   pl.BlockSpec((tm, KV_LATENT), lambda i: (i, 0)),
                   pl.BlockSpec((None, KV_LATENT, tm), lambda i: (i // nbt, 0, i % nbt))),
        compiler_params=_cparams(("parallel",)),
        name="dsa_prep",
    )(proj, proj, proj, kv_norm_g.reshape(1, KV_LATENT), sq, sk)


def _dsa_kernel(q_ref, tail_ref, qcat_ref, kcat_ref, ckv_ref, ckvt_ref, wuk_ref, wuv_ref, kbound_ref, o_ref,
                key_ref, khi_ref, klo_ref, qlat_ref, m_ref, l_ref, acc_ref, *, seq, topk, cks, ck, cka, qb):
    i = pl.program_id(1)
    q0 = i * qb
    nchs = (q0 + qb + cks - 1) // cks
    nch = (q0 + qb + ck - 1) // ck
    q_pos = q0 + lax.broadcasted_iota(I32, (1, qb), 1)
    row_pos_s = lax.broadcasted_iota(I32, (cks, 1), 0)
    row_pos = lax.broadcasted_iota(I32, (ck, 1), 0)
    w_t = tail_ref[...].T
    nt_dims = (((1,), (1,)), ((), ()))
    cnt_rows = min(64, ck)

    def score_chunk(c, carry):
        k0 = pl.multiple_of(c * cks, cks)
        kc = kcat_ref[pl.ds(k0, cks), :]
        s = jnp.zeros((cks, qb), F32)
        for h in range(IDX_HEADS):
            d = lax.dot_general(kc, qcat_ref[:, h * LANES:(h + 1) * LANES], nt_dims, preferred_element_type=F32)
            s = s + jnp.maximum(d, 0.0) * w_t[TAIL_W_OFF + h:TAIL_W_OFF + h + 1, :]
        s = s + 0.0
        bits = pltpu.bitcast(s, I32)
        key = bits ^ ((bits >> 31) & jnp.int32(0x7FFFFFFF))
        key = jnp.where(k0 + row_pos_s <= q_pos, key, jnp.int32(INT_MIN))
        key_ref[pl.ds(k0, cks), :] = key
        khi_ref[pl.ds(k0, cks), :] = (key >> 16).astype(I16)
        klo_ref[pl.ds(k0, cks), :] = ((key & jnp.int32(0xFFFF)) - 32768).astype(I16)
        return carry

    lax.fori_loop(0, nchs, score_chunk, 0)

    def count_keys(pred_fn):
        def body(c, acc):
            k0 = pl.multiple_of(c * ck, ck)
            hit = jnp.where(pred_fn(key_ref[pl.ds(k0, ck), :], k0), 1.0, 0.0)
            return acc + jnp.sum(hit.reshape(ck // cnt_rows, cnt_rows, qb), axis=0)
        acc = lax.fori_loop(0, nch, body, jnp.zeros((cnt_rows, qb), F32))
        return jnp.sum(acc, axis=0, keepdims=True)

    kf = float(topk)
    pack = 16
    n_part = ck // (2 * pack)
    lo16, hi16 = -32768, 32767

    def count16(ref, cand16):
        cand = jnp.broadcast_to(cand16, (pack, qb)).astype(I16)
        one = jnp.ones((), I16)
        zero = jnp.zeros((), I16)

        def body(c, acc):
            k0 = pl.multiple_of(c * ck, ck)
            hit = jnp.where(ref[pl.ds(k0, ck), :].reshape(ck // pack, pack, qb) >= cand[None], one, zero)
            part = hit[0:2]
            for j in range(1, n_part):
                part = part + hit[2 * j:2 * j + 2]
            return acc + part
        acc = lax.fori_loop(0, nch, body, jnp.zeros((2, pack, qb), I16)).astype(I32).astype(F32)
        return jnp.sum(jnp.sum(acc, axis=0), axis=0, keepdims=True)

    def count16_above(ref, x16):
        return jnp.where(x16 == hi16, 0.0, count16(ref, jnp.minimum(x16 + 1, hi16)))

    def bisect16(ref, target):
        def one_pass(b, carry):
            w, cw = carry
            cand = w + jnp.left_shift(jnp.int32(1), 15 - b)
            cnt = count16(ref, cand)
            ok = cnt >= target
            return jnp.where(ok, cand, w), jnp.where(ok, cnt, cw)
        total = jnp.full((1, qb), 1.0, F32) * (nch * ck).astype(F32)
        return lax.fori_loop(0, 16, one_pass, (jnp.full((1, qb), lo16, I32), total))

    v16, c_v16 = bisect16(khi_ref, kf)
    c_above = count16_above(khi_ref, v16)
    v16_b = jnp.broadcast_to(v16, (pack, qb)).astype(I16)

    def build_lo(c, carry):
        k0 = pl.multiple_of(c * ck, ck)
        hi = khi_ref[pl.ds(k0, ck), :].reshape(ck // pack, pack, qb)
        lo = klo_ref[pl.ds(k0, ck), :].reshape(ck // pack, pack, qb)
        khi_ref[pl.ds(k0, ck), :] = jnp.where(hi == v16_b[None], lo, jnp.full((), lo16, I16)).reshape(ck, qb)
        return carry

    lax.fori_loop(0, nch, build_lo, 0)
    w, c_w = bisect16(khi_ref, kf - c_above)
    c_w = jnp.where(w == lo16, c_v16 - c_above, c_w)
    v = (v16 << 16) + (w + 32768)
    select_all = v == jnp.int32(INT_MIN)
    thr = jnp.maximum(v, jnp.int32(INT_MIN + 1))
    c_ge = jnp.where(select_all, c_above, c_above + c_w)
    c_gt = jnp.where(select_all, c_above, c_above + count16_above(khi_ref, w))
    need = kf - c_gt

    jlim_ref = m_ref.at[0:1]
    jlim_ref[...] = jnp.full((1, qb), float(seq), F32)
    has_tie = jnp.max(jnp.where(c_ge > kf, 1.0, 0.0)) > 0.0

    @pl.when(has_tie)
    def _():
        n_bits = max(1, int(math.ceil(math.log2(seq))))

        def pos_pass(b, j):
            cand = j + jnp.left_shift(jnp.int32(1), n_bits - 1 - b)
            cnt = count_keys(lambda key, k0: (key == thr) & (k0 + row_pos < cand))
            return jnp.where(cnt < need, cand, j)

        j = lax.fori_loop(0, n_bits, pos_pass, jnp.zeros((1, qb), I32))
        jlim_ref[...] = jnp.where(c_ge > kf, j.astype(F32), float(seq))

    jlim = jlim_ref[...].astype(I32)

    q_bf = q_ref[...].astype(BF16)
    qn2_max = jnp.zeros((1, qb), F32)
    for j in range(N_HEADS // 2):
        ql = jnp.dot(q_bf[:, j * LANES:(j + 1) * LANES], wuk_ref[j],
                     preferred_element_type=F32) * (HEAD_DIM ** -0.5 * math.log2(math.e))
        for half in range(2):
            qt = ql[:, half * KV_LATENT:(half + 1) * KV_LATENT].T.astype(BF16)
            qlat_ref[2 * j + half] = qt
            qf = qt.astype(F32)
            qn2_max = jnp.maximum(qn2_max, jnp.sum(qf * qf, axis=0, keepdims=True))
    logit_bound = jnp.sqrt(jnp.max(qn2_max)) * kbound_ref[0, 0]
    row_pos_a = lax.broadcasted_iota(I32, (cka, 1), 0)

    def att_tiles(base, n_sub, online):
        for sub in range(n_sub):
            k0 = pl.multiple_of(base + sub * cka, cka)
            kv = ckv_ref[pl.ds(k0, cka), :]
            kvt = ckvt_ref[:, pl.ds(k0, cka)]
            key = key_ref[pl.ds(k0, cka), :]
            sel = (key > thr) | ((key == thr) & (k0 + row_pos_a <= jlim))
            bias = jnp.where(sel, 0.0, NEG_BIG)
            for h in range(N_HEADS):
                lg = jnp.dot(kv, qlat_ref[h], preferred_element_type=F32) + bias
                if not online:
                    p = jnp.exp2(lg)
                    l_ref[h:h + 1, :] = l_ref[h:h + 1, :] + jnp.sum(p, axis=0, keepdims=True)
                    acc_ref[h] = acc_ref[h] + jnp.dot(kvt, p.astype(BF16), preferred_element_type=F32)
                else:
                    m_prev = m_ref[h:h + 1, :]
                    m_new = jnp.maximum(m_prev, jnp.max(lg, axis=0, keepdims=True))
                    p = jnp.exp2(lg - m_new)
                    alpha = jnp.exp2(m_prev - m_new)
                    l_ref[h:h + 1, :] = alpha * l_ref[h:h + 1, :] + jnp.sum(p, axis=0, keepdims=True)
                    acc_ref[h] = alpha * acc_ref[h] + jnp.dot(kvt, p.astype(BF16), preferred_element_type=F32)
                    m_ref[h:h + 1, :] = m_new

    def attend(online):
        def att_chunk(c, carry):
            att_tiles(c * cks, cks // cka, online)
            return carry

        end = q0 + qb
        n_full = end // cks
        lax.fori_loop(0, n_full, att_chunk, 0)
        if cks % qb == 0 and cks > qb:
            for parts in range(1, cks // qb):
                @pl.when(end - n_full * cks == parts * qb)
                def _():
                    att_tiles(n_full * cks, parts * qb // cka, online)
        else:
            @pl.when(end - n_full * cks > 0)
            def _():
                att_tiles(n_full * cks, cks // cka, online)

    m_ref[...] = jnp.full(m_ref.shape, NEG_BIG, F32)
    l_ref[...] = jnp.zeros(l_ref.shape, F32)
    acc_ref[...] = jnp.zeros(acc_ref.shape, F32)

    @pl.when(logit_bound < SAFE_LOGIT)
    def _():
        attend(online=False)

    @pl.when(jnp.logical_not(logit_bound < SAFE_LOGIT))
    def _():
        attend(online=True)

    outs = []
    for j in range(N_HEADS // 2):
        o_pair = [(acc_ref[h] / l_ref[h:h + 1, :]).T.astype(BF16) for h in (2 * j, 2 * j + 1)]
        outs.append(jnp.dot(jnp.concatenate(o_pair, axis=1), wuv_ref[j], preferred_element_type=F32))
    o_ref[...] = jnp.concatenate(outs, axis=1)


def _dsa_attention(proj, qcat, kcat, ckvn, ckvt, w_uk, w_uv, kv_norm_g, bsz, seq):
    n = proj.shape[0]
    qb = min(DSA_QUERY_BLOCK, seq)
    n_blk = seq // qb
    topk = min(INDEX_TOPK, seq // 4)
    cks = min(512, seq)
    ck = min(512, seq)
    cka = min(128, seq)
    zk = jnp.zeros((HEAD_DIM, KV_LATENT), F32)
    zv = jnp.zeros((KV_LATENT, HEAD_DIM), F32)
    wuk = jnp.stack([jnp.block([[w_uk[:, 2 * j, :].T, zk], [zk, w_uk[:, 2 * j + 1, :].T]])
                     for j in range(N_HEADS // 2)]).astype(BF16)
    wuv = jnp.stack([jnp.block([[w_uv[:, 2 * j, :], zv], [zv, w_uv[:, 2 * j + 1, :]]])
                     for j in range(N_HEADS // 2)]).astype(BF16)
    kbound = (1.01 * math.sqrt(KV_LATENT) * jnp.max(jnp.abs(kv_norm_g))).reshape(1, 1).astype(F32)
    kernel = functools.partial(_dsa_kernel, seq=seq, topk=topk, cks=cks, ck=ck, cka=cka, qb=qb)
    return pl.pallas_call(
        kernel,
        out_shape=jax.ShapeDtypeStruct((n, N_HEADS * HEAD_DIM), F32),
        grid=(bsz, n_blk),
        in_specs=[pl.BlockSpec((qb, 512), lambda b, i: (b * n_blk + i, COL_Q // 512)),
                  pl.BlockSpec((qb, 128), lambda b, i: (b * n_blk + i, COL_TAIL // 128)),
                  pl.BlockSpec((qb, IDX_HEADS * LANES), lambda b, i: (b * n_blk + i, 0)),
                  pl.BlockSpec((seq, LANES), lambda b, i: (b, 0)),
                  pl.BlockSpec((seq, KV_LATENT), lambda b, i: (b, 0)),
                  pl.BlockSpec((None, KV_LATENT, seq), lambda b, i: (b, 0, 0)),
                  _const_spec(wuk.shape), _const_spec(wuv.shape),
                  pl.BlockSpec(memory_space=pltpu.SMEM)],
        out_specs=pl.BlockSpec((qb, N_HEADS * HEAD_DIM), lambda b, i: (b * n_blk + i, 0)),
        scratch_shapes=[pltpu.VMEM((seq, qb), I32),
                        pltpu.VMEM((seq, qb), I16),
                        pltpu.VMEM((seq, qb), I16),
                        pltpu.VMEM((N_HEADS, KV_LATENT, qb), BF16),
                        pltpu.VMEM((N_HEADS, qb), F32),
                        pltpu.VMEM((N_HEADS, qb), F32),
                        pltpu.VMEM((N_HEADS, KV_LATENT, qb), F32)],
        compiler_params=_cparams(("parallel", "parallel")),
        name="dsa_attention",
    )(proj, proj, qcat, kcat, ckvn, ckvt, wuk, wuv, kbound)


def _conv_kernel(u_ref, gb_ref, gc_ref, pu_ref, pgc_ref, w_ref, b_ref, o_ref):
    i = pl.program_id(1)
    tb = u_ref.shape[0]
    v = gc_ref[...] * u_ref[...]
    halo = jnp.where(i > 0, pgc_ref[...] * pu_ref[...], 0.0)
    vfull = jnp.concatenate([halo, v], axis=0)
    v1 = pltpu.roll(vfull, 1, 0)[SUBLANES:SUBLANES + tb]
    v2 = pltpu.roll(vfull, 2, 0)[SUBLANES:SUBLANES + tb]
    w = w_ref[...]
    y = w[0:1] * v2 + w[1:2] * v1 + w[2:3] * v
    o_ref[...] = gb_ref[...] * (y + b_ref[...])


def _short_conv(proj, conv_w, conv_b, bsz, seq, tb=512):
    n = proj.shape[0]
    tb = min(tb, seq)
    nb = seq // tb
    cw = CONV_WIDTH
    blk = lambda col: pl.BlockSpec((tb, cw), lambda b, i: (b * nb + i, col // cw))
    prev = lambda col: pl.BlockSpec(
        (SUBLANES, cw), lambda b, i: (jnp.maximum((b * nb + i) * (tb // SUBLANES) - 1, 0), col // cw))
    return pl.pallas_call(
        _conv_kernel,
        out_shape=jax.ShapeDtypeStruct((n, cw), F32),
        grid=(bsz, nb),
        in_specs=[blk(COL_CONV_U), blk(COL_CONV_GB), blk(COL_CONV_GC), prev(COL_CONV_U), prev(COL_CONV_GC),
                  _const_spec((SUBLANES, cw)), _const_spec((1, cw))],
        out_specs=pl.BlockSpec((tb, cw), lambda b, i: (b * nb + i, 0)),
        compiler_params=_cparams(("parallel", "parallel")),
        name="short_conv",
    )(proj, proj, proj, proj, proj,
      jnp.pad(conv_w, ((0, SUBLANES - conv_w.shape[0]), (0, 0))), conv_b.reshape(1, cw))


def _s5_scan_chunk(bu_ref, a_ref, st_ref, n_steps, store):
    s_tot = a_ref.shape[1] // 2
    cb = 512
    for blk in range(s_tot // cb):
        re_sl = pl.ds(blk * cb, cb)
        im_sl = pl.ds(s_tot + blk * cb, cb)
        ar = jnp.broadcast_to(a_ref[0:1, blk * cb:(blk + 1) * cb], (SSM_SEQS, cb))
        ai = jnp.broadcast_to(a_ref[0:1, s_tot + blk * cb:s_tot + (blk + 1) * cb], (SSM_SEQS, cb))

        def step(t, carry):
            re, im = carry
            r0 = pl.multiple_of(t * SSM_SEQS, SSM_SEQS)
            br = bu_ref[pl.ds(r0, SSM_SEQS), re_sl]
            bi = bu_ref[pl.ds(r0, SSM_SEQS), im_sl]
            nre = ar * re - ai * im + br
            nim = ar * im + ai * re + bi
            if store:
                bu_ref[pl.ds(r0, SSM_SEQS), re_sl] = nre
                bu_ref[pl.ds(r0, SSM_SEQS), im_sl] = nim
            return nre, nim

        re, im = lax.fori_loop(0, n_steps, step, (st_ref[:, re_sl], st_ref[:, im_sl]), unroll=8)
        st_ref[:, re_sl] = re
        st_ref[:, im_sl] = im


def _gelu_tanh(x):
    return 0.5 * x * (1.0 + jnp.tanh(math.sqrt(2.0 / math.pi) * (x + 0.044715 * (x * x * x))))


S5_ROWS = 512


def _s5_kernel(u_ref, perm_ref, wb_ref, a_ref, apow_ref, wc_ref, d_ref, wg_ref, bg_ref, o_ref,
               bu_ref, st_ref, carry_ref, *, seg_len, steps_per_seq):
    c = pl.program_id(0)
    s_tot = a_ref.shape[1] // 2
    tn_dims = (((0,), (0,)), ((), ()))

    @pl.when(c % steps_per_seq == 0)
    def _():
        carry_ref[...] = jnp.zeros(carry_ref.shape, F32)

    u = u_ref[...]
    perm = perm_ref[...]
    u_il = jnp.dot(perm, u.astype(BF16), preferred_element_type=F32).astype(BF16)
    hw = u_il.shape[1] // 2
    hs = s_tot // 2
    for part in range(2):
        for q in range(2):
            c0 = part * s_tot + q * hs
            bu_ref[:, c0:c0 + hs] = jnp.dot(u_il[:, q * hw:(q + 1) * hw], wb_ref[q * hw:(q + 1) * hw, c0:c0 + hs],
                                            preferred_element_type=F32)

    st_ref[...] = jnp.zeros(st_ref.shape, F32)
    _s5_scan_chunk(bu_ref, a_ref, st_ref, seg_len, store=False)

    pr = apow_ref[0:1, :s_tot]
    pi = apow_ref[0:1, s_tot:]
    hre = carry_ref[0:1, :s_tot]
    him = carry_ref[0:1, s_tot:]
    for j in range(SSM_SEQS):
        ere = st_ref[j:j + 1, :s_tot]
        eim = st_ref[j:j + 1, s_tot:]
        st_ref[j:j + 1, :s_tot] = hre
        st_ref[j:j + 1, s_tot:] = him
        hre, him = pr * hre - pi * him + ere, pr * him + pi * hre + eim
    carry_ref[0:1, :s_tot] = hre
    carry_ref[0:1, s_tot:] = him

    _s5_scan_chunk(bu_ref, a_ref, st_ref, seg_len, store=True)
    halves = []
    for q in range(2):
        acc = None
        for part in range(2):
            c0 = part * s_tot + q * hs
            term = jnp.dot(bu_ref[:, c0:c0 + hs].astype(BF16), wc_ref[c0:c0 + hs, q * hw:(q + 1) * hw],
                           preferred_element_type=F32)
            acc = term if acc is None else acc + term
        halves.append(acc)
    ch = jnp.concatenate(halves, axis=1)
    ch_hi, ch_lo = _split_hi_lo(ch)
    ch_nat = (lax.dot_general(perm, ch_hi, tn_dims, preferred_element_type=F32)
              + lax.dot_general(perm, ch_lo, tn_dims, preferred_element_type=F32))
    z = _gelu_tanh(ch_nat + d_ref[...] * u)
    gate = jnp.dot(z.astype(BF16), wg_ref[...], preferred_element_type=F32) + bg_ref[...]
    o_ref[...] = z * jax.nn.sigmoid(gate)


def _s5_mixer(proj, lam_re, lam_im, b_re, b_im, c_re, c_im, d_skip, log_dt, w_glu, b_glu, bsz, seq):
    n = proj.shape[0]
    g, p, nn = SSM_GROUPS, SSM_STATE, SSM_GROUP
    s_tot = g * p
    rows = min(S5_ROWS, seq)
    seg_len = rows // SSM_SEQS

    dt = jnp.exp(log_dt.astype(F32))[:, None]
    lam = lax.complex(lam_re.astype(F32), lam_im.astype(F32))
    a_bar = jnp.exp(lam * dt)
    b_bar = ((a_bar - 1.0) / lam)[:, :, None] * lax.complex(b_re.astype(F32), b_im.astype(F32))
    a_pow = jnp.exp(lam * dt * seg_len)
    a_vec = jnp.concatenate([jnp.real(a_bar).reshape(1, s_tot), jnp.imag(a_bar).reshape(1, s_tot)], axis=1)
    apow_vec = jnp.concatenate([jnp.real(a_pow).reshape(1, s_tot), jnp.imag(a_pow).reshape(1, s_tot)], axis=1)
    eye = jnp.eye(g, dtype=F32)
    wb = jnp.concatenate(
        [jnp.einsum('gpn,gh->gnhp', jnp.real(b_bar), eye).reshape(g * nn, s_tot),
         jnp.einsum('gpn,gh->gnhp', jnp.imag(b_bar), eye).reshape(g * nn, s_tot)], axis=1).astype(BF16)
    wc = jnp.concatenate(
        [jnp.einsum('gnp,gh->gphn', c_re.astype(F32), eye).reshape(s_tot, g * nn),
         -jnp.einsum('gnp,gh->gphn', c_im.astype(F32), eye).reshape(s_tot, g * nn)], axis=0).astype(BF16)
    perm_np = np.zeros((rows, rows), np.float32)
    t_idx, j_idx = np.meshgrid(np.arange(seg_len), np.arange(SSM_SEQS), indexing='ij')
    perm_np[(t_idx * SSM_SEQS + j_idx).ravel(), (j_idx * seg_len + t_idx).ravel()] = 1.0
    perm = jnp.asarray(perm_np, BF16)

    return pl.pallas_call(
        functools.partial(_s5_kernel, seg_len=seg_len, steps_per_seq=seq // rows),
        out_shape=jax.ShapeDtypeStruct((n, SSM_WIDTH), F32),
        grid=(n // rows,),
        in_specs=[pl.BlockSpec((rows, SSM_WIDTH), lambda c: (c, COL_SSM // SSM_WIDTH)),
                  _const_spec(perm.shape), _const_spec(wb.shape), _const_spec(a_vec.shape),
                  _const_spec(apow_vec.shape), _const_spec(wc.shape), _const_spec((1, SSM_WIDTH)),
                  _const_spec((SSM_WIDTH, SSM_WIDTH)), _const_spec((1, SSM_WIDTH))],
        out_specs=pl.BlockSpec((rows, SSM_WIDTH), lambda c: (c, 0)),
        scratch_shapes=[pltpu.VMEM((rows, 2 * s_tot), F32), pltpu.VMEM((SSM_SEQS, 2 * s_tot), F32),
                        pltpu.VMEM((1, 2 * s_tot), F32)],
        compiler_params=_cparams(("arbitrary",)),
        name="s5_mixer",
    )(proj, perm, wb, a_vec, apow_vec, wc, d_skip.reshape(1, SSM_WIDTH).astype(F32),
      w_glu.astype(BF16), b_glu.reshape(1, SSM_WIDTH))


def _mem_attn_kernel(q_ref, kv_ref, o_ref):
    hw = MEM_HEADS * MEM_HEAD_DIM
    q = q_ref[...].astype(BF16)
    kv = kv_ref[...].astype(BF16)
    for h in range(MEM_HEADS):
        sl = slice(h * MEM_HEAD_DIM, (h + 1) * MEM_HEAD_DIM)
        k = kv[:, sl]
        v = kv[:, hw + h * MEM_HEAD_DIM:hw + (h + 1) * MEM_HEAD_DIM]
        lg = lax.dot_general(q[:, sl], k, (((1,), (1,)), ((), ())),
                             preferred_element_type=F32) * (MEM_HEAD_DIM ** -0.5)
        m = jnp.max(lg, axis=1, keepdims=True)
        p = jnp.exp(lg - m)
        p = p / jnp.sum(p, axis=1, keepdims=True)
        o_ref[:, sl] = jnp.dot(p.astype(BF16), v, preferred_element_type=F32)


def _memory_attention(proj, kvm, bsz, seq, mlen, tm=512):
    n = proj.shape[0]
    tm = min(tm, seq)
    nb = seq // tm
    hw = MEM_HEADS * MEM_HEAD_DIM
    return pl.pallas_call(
        _mem_attn_kernel,
        out_shape=jax.ShapeDtypeStruct((n, hw), F32),
        grid=(bsz, nb),
        in_specs=[pl.BlockSpec((tm, hw), lambda b, i: (b * nb + i, COL_MEMQ // hw)),
                  pl.BlockSpec((mlen, 2 * hw), lambda b, i: (b, 0))],
        out_specs=pl.BlockSpec((tm, hw), lambda b, i: (b * nb + i, 0)),
        compiler_params=_cparams(("parallel", "parallel")),
        name="memory_attention",
    )(proj, kvm)


def _merge_kernel(att_ref, cnv_ref, ssm_ref, mem_ref, gl_ref, h_ref, wbr_ref, wo_ref, g_ref, b_ref,
                  wr_ref, br_ref, ltri_ref, h1_ref, te_ref, tg_ref, cnt_ref, base_ref, *, alpha):
    d = h_ref.shape[1]

    @pl.when(pl.program_id(0) == 0)
    def _():
        base_ref[...] = jnp.zeros(base_ref.shape, F32)

    merged = jnp.zeros(h_ref.shape, F32)
    for r, br in enumerate((att_ref, cnv_ref, ssm_ref, mem_ref)):
        y = jnp.dot(br[...].astype(BF16), wbr_ref[r], preferred_element_type=F32)
        merged = merged + y * jax.nn.sigmoid(gl_ref[:, r * d:(r + 1) * d])
    y = alpha * h_ref[...] + jnp.dot(merged.astype(BF16), wo_ref[...], preferred_element_type=F32)
    h1 = _layer_norm(y, g_ref[...], b_ref[...])
    h1_ref[...] = h1

    logits = jnp.dot(h1, wr_ref[...], preferred_element_type=F32, precision=lax.Precision.HIGHEST) + br_ref[...]
    tm = logits.shape[0]
    lane = lax.broadcasted_iota(I32, (tm, N_EXPERTS), 1)
    out_lane = lax.broadcasted_iota(I32, (tm, LANES), 1)
    work = logits
    top_e = jnp.zeros((tm, LANES), I32)
    top_v = jnp.zeros((tm, LANES), F32)
    vals = []
    onehots = []
    for k in range(TOP_K):
        mx = jnp.max(work, axis=1, keepdims=True)
        idx = jnp.min(jnp.where(work == mx, lane, N_EXPERTS), axis=1, keepdims=True)
        hit = lane == idx
        work = jnp.where(hit, -jnp.inf, work)
        top_e = jnp.where(out_lane == k, idx, top_e)
        vals.append(mx)
        onehots.append(jnp.where(hit, 1.0, 0.0))
    den = sum(jnp.exp(vk - vals[0]) for vk in vals)
    for k in range(TOP_K):
        top_v = jnp.where(out_lane == k, jnp.exp(vals[k] - vals[0]) / den, top_v)

    per_tok = onehots[0] + onehots[1] + onehots[2] + onehots[3]
    before = jnp.dot(ltri_ref[...], per_tok.astype(BF16), preferred_element_type=F32) + base_ref[...]
    for k in range(TOP_K):
        rank = jnp.sum(onehots[k] * before, axis=1, keepdims=True)
        top_e = jnp.where(out_lane == TOP_K + k, rank.astype(I32), top_e)
    base_ref[...] = base_ref[...] + jnp.sum(per_tok, axis=0, keepdims=True)
    cnt_ref[...] = base_ref[...]
    te_ref[...] = top_e
    tg_ref[...] = top_v


def _merge_router(att, cnv, ssm, mem_o, proj, h, w_branch, w_o, ln_g, ln_b, w_router, b_router, alpha, tm=256):
    n, d = h.shape
    bw = BRANCH_WIDTH
    row = lambda w: pl.BlockSpec((tm, w), lambda i: (i, 0))
    ltri = jnp.asarray(np.tril(np.ones((tm, tm), np.float32), -1), BF16)
    return pl.pallas_call(
        functools.partial(_merge_kernel, alpha=alpha),
        out_shape=(jax.ShapeDtypeStruct((n, d), F32),
                   jax.ShapeDtypeStruct((n, LANES), I32),
                   jax.ShapeDtypeStruct((n, LANES), F32),
                   jax.ShapeDtypeStruct((1, N_EXPERTS), F32)),
        grid=(n // tm,),
        in_specs=[row(bw), row(bw), row(bw), row(bw),
                  pl.BlockSpec((tm, N_BRANCH * d), lambda i: (i, COL_GATES // (N_BRANCH * d))),
                  row(d),
                  _const_spec((N_BRANCH, bw, d)), _const_spec((d, d)), _const_spec((1, d)), _const_spec((1, d)),
                  _const_spec((d, N_EXPERTS)), _const_spec((1, N_EXPERTS)), _const_spec((tm, tm))],
        out_specs=(row(d), row(LANES), row(LANES), _const_spec((1, N_EXPERTS))),
        scratch_shapes=[pltpu.VMEM((1, N_EXPERTS), F32)],
        compiler_params=_cparams(("arbitrary",)),
        name="merge_router",
    )(att, cnv, ssm, mem_o, proj, h, w_branch.astype(BF16), w_o.astype(BF16),
      ln_g.reshape(1, d), ln_b.reshape(1, d), w_router, b_router.reshape(1, N_EXPERTS), ltri)


DISPATCH_UNROLL = 8
COMBINE_UNROLL = 16


def _dispatch_kernel(dest_ref, h_ref, xs_in_hbm, xs_hbm, sem, *, tm):
    del xs_in_hbm

    def body(r, carry):
        for k in range(TOP_K):
            pltpu.make_async_copy(h_ref.at[pl.ds(r, 1), :],
                                  xs_hbm.at[pl.ds(dest_ref[0, r * TOP_K + k], 1), :], sem).start()
        return carry

    lax.fori_loop(0, tm, body, 0, unroll=DISPATCH_UNROLL)
    pltpu.make_async_copy(xs_hbm.at[pl.ds(0, tm * TOP_K), :], xs_hbm.at[pl.ds(0, tm * TOP_K), :], sem).wait()


def _dispatch(h1, dest, xs_prev, tm=256):
    n, d = h1.shape
    nb = n // tm
    return pl.pallas_call(
        functools.partial(_dispatch_kernel, tm=tm),
        out_shape=jax.ShapeDtypeStruct(xs_prev.shape, F32),
        grid=(nb,),
        in_specs=[pl.BlockSpec((None, 1, tm * TOP_K), lambda i: (i, 0, 0), memory_space=pltpu.SMEM),
                  pl.BlockSpec((tm, d), lambda i: (i, 0)),
                  pl.BlockSpec(memory_space=pl.ANY)],
        out_specs=pl.BlockSpec(memory_space=pl.ANY),
        scratch_shapes=[pltpu.SemaphoreType.DMA(())],
        input_output_aliases={2: 0},
        compiler_params=_cparams(("arbitrary",)),
        name="moe_dispatch",
    )(dest.reshape(nb, 1, tm * TOP_K), h1, xs_prev)


def _expert_kernel(meta_ref, x_ref, wu_ref, bu_ref, wd_ref, bd_ref, sel_ref, o_ref, wu_bf, wd_bf):
    j = pl.program_id(0)
    n_used = meta_ref[0]
    f2 = wu_ref.shape[1]

    @pl.when((j < n_used) & ((j == 0) | (meta_ref[1 + j] != meta_ref[jnp.maximum(j, 1)])))
    def _():
        sel = sel_ref[...]
        cw = sel.shape[0]
        for c in range(f2 // cw):
            grp = jnp.dot(wu_ref[:, c * cw:(c + 1) * cw].astype(BF16), sel, preferred_element_type=F32)
            wu_bf[:, c * cw:(c + 1) * cw] = grp.astype(BF16)
        wd_bf[...] = wd_ref[...].astype(BF16)

    @pl.when(j < n_used)
    def _():
        xb = x_ref[...].astype(BF16)
        hdn = jnp.dot(xb, wu_bf[...], preferred_element_type=F32) + bu_ref[...]
        cw = sel_ref.shape[0]
        half = cw // 2
        parts = []
        for c in range(f2 // cw):
            h_glu = jnp.minimum(hdn[:, c * cw:c * cw + half], SWIGLU_LIMIT)
            h_lin = jnp.clip(hdn[:, c * cw + half:(c + 1) * cw], -SWIGLU_LIMIT, SWIGLU_LIMIT)
            parts.append((h_glu * jax.nn.sigmoid(SWIGLU_ALPHA * h_glu) * (h_lin + 1.0)).astype(BF16))
        act_c = jnp.concatenate(parts, axis=1)
        o_ref[...] = jnp.dot(act_c, wd_bf[...], preferred_element_type=F32) + bd_ref[...]

    @pl.when(j >= n_used)
    def _():
        o_ref[...] = jnp.zeros(o_ref.shape, F32)


def _expert_ffn(xs, blk_expert, n_used, layer, w_up, b_up, w_down, b_down):
    n_rows, d = xs.shape
    n_blocks = n_rows // EXPERT_BLOCK
    f = D_EXPERT
    meta = jnp.concatenate([n_used.reshape(1).astype(I32), blk_expert.astype(I32)])
    sel_np = np.zeros((2 * LANES, 2 * LANES), np.float32)
    sel_np[2 * np.arange(LANES), np.arange(LANES)] = 1.0
    sel_np[2 * np.arange(LANES) + 1, LANES + np.arange(LANES)] = 1.0
    sel = jnp.asarray(sel_np, BF16)
    le = b_up.shape[:2]
    b_up = jnp.transpose(b_up.reshape(le + (f // LANES, LANES, 2)), (0, 1, 2, 4, 3)).reshape(le + (2 * f,))
    e_of = lambda j, m: m[1 + j]
    grid_spec = pltpu.PrefetchScalarGridSpec(
        num_scalar_prefetch=1,
        grid=(n_blocks,),
        in_specs=[
            pl.BlockSpec((EXPERT_BLOCK, d), lambda j, m: (j, 0)),
            pl.BlockSpec((None, None, d, 2 * f), lambda j, m: (layer, e_of(j, m), 0, 0)),
            pl.BlockSpec((None, None, 1, 2 * f), lambda j, m: (layer, e_of(j, m), 0, 0)),
            pl.BlockSpec((None, None, f, d), lambda j, m: (layer, e_of(j, m), 0, 0)),
            pl.BlockSpec((None, None, 1, d), lambda j, m: (layer, e_of(j, m), 0, 0)),
            pl.BlockSpec(sel.shape, lambda j, m: (0, 0)),
        ],
        out_specs=pl.BlockSpec((EXPERT_BLOCK, d), lambda j, m: (j, 0)),
        scratch_shapes=[pltpu.VMEM((d, 2 * f), BF16), pltpu.VMEM((f, d), BF16)],
    )
    return pl.pallas_call(
        _expert_kernel,
        out_shape=jax.ShapeDtypeStruct((n_rows, d), F32),
        grid_spec=grid_spec,
        compiler_params=_cparams(("arbitrary",)),
        name="moe_experts",
    )(meta, xs, w_up, b_up[:, :, None, :], w_down, b_down[:, :, None, :], sel)


def _combine_kernel(cur_ref, nxt_ref, ys_hbm, g4_ref, h_ref, lg_ref, lb_ref, o_ref, ybuf, sem, *, alpha, tm):
    j = pl.program_id(0)
    nblk = pl.num_programs(0)
    slot = j % 2
    n_rows = tm * TOP_K

    def gather(idx_ref, s):
        def body(r, carry):
            pltpu.make_async_copy(ys_hbm.at[pl.ds(idx_ref[0, r], 1), :],
                                  ybuf.at[s, pl.ds(r, 1), :], sem.at[s]).start()
            return carry
        lax.fori_loop(0, n_rows, body, 0, unroll=COMBINE_UNROLL)

    @pl.when(j == 0)
    def _():
        gather(cur_ref, 0)

    @pl.when(j + 1 < nblk)
    def _():
        gather(nxt_ref, 1 - slot)

    pltpu.make_async_copy(ys_hbm.at[pl.ds(0, n_rows), :], ybuf.at[slot], sem.at[slot]).wait()
    g4 = g4_ref[...]
    ffn = jnp.zeros((tm, h_ref.shape[1]), F32)
    for k in range(TOP_K):
        ffn = ffn + ybuf[slot, k * tm:(k + 1) * tm, :] * g4[:, k:k + 1]
    o_ref[...] = _layer_norm(alpha * h_ref[...] + ffn, lg_ref[...], lb_ref[...])


def _combine(ys, pos_km, gates_pad, h1, ln_g, ln_b, alpha, tm=128):
    n, d = h1.shape
    nb = n // tm
    idx3 = pos_km.reshape(nb, 1, tm * TOP_K)
    return pl.pallas_call(
        functools.partial(_combine_kernel, alpha=alpha, tm=tm),
        out_shape=jax.ShapeDtypeStruct((n, d), F32),
        grid=(nb,),
        in_specs=[
            pl.BlockSpec((None, 1, tm * TOP_K), lambda j: (j, 0, 0), memory_space=pltpu.SMEM),
            pl.BlockSpec((None, 1, tm * TOP_K), lambda j: (jnp.minimum(j + 1, nb - 1), 0, 0),
                         memory_space=pltpu.SMEM),
            pl.BlockSpec(memory_space=pl.ANY),
            pl.BlockSpec((tm, LANES), lambda j: (j, 0)),
            pl.BlockSpec((tm, d), lambda j: (j, 0)),
            _const_spec((1, d)), _const_spec((1, d)),
        ],
        out_specs=pl.BlockSpec((tm, d), lambda j: (j, 0)),
        scratch_shapes=[pltpu.VMEM((2, tm * TOP_K, d), F32), pltpu.SemaphoreType.DMA((2,))],
        compiler_params=_cparams(("arbitrary",)),
        name="moe_combine",
    )(idx3, idx3, ys, gates_pad, h1, ln_g.reshape(1, d), ln_b.reshape(1, d))


def _moe_routing(top_e, rank, counts, n_tok, tm_combine):
    n_asg = n_tok * TOP_K
    counts = counts.reshape(N_EXPERTS).astype(I32)
    nblk_per = (counts + EXPERT_BLOCK - 1) // EXPERT_BLOCK
    blk_end = jnp.cumsum(nblk_per)
    pstarts = (blk_end - nblk_per) * EXPERT_BLOCK
    onehot = top_e[..., None] == jnp.arange(N_EXPERTS, dtype=I32)
    dest = jnp.sum(jnp.where(onehot, pstarts, 0), axis=-1) + rank
    n_blocks = -(-n_asg // EXPERT_BLOCK) + N_EXPERTS
    blk_expert = jnp.minimum(jnp.sum(blk_end[None, :] <= jnp.arange(n_blocks, dtype=I32)[:, None], axis=1),
                             N_EXPERTS - 1)
    n_used = blk_end[-1]
    nb = n_tok // tm_combine
    pos_km = jnp.transpose(dest.reshape(nb, tm_combine, TOP_K), (0, 2, 1)).reshape(nb, TOP_K * tm_combine)
    return dest.astype(I32), blk_expert.astype(I32), n_used.astype(I32), pos_km.astype(I32)


def _permute_w_in(w):
    sizes = (N_HEADS * HEAD_DIM, KV_LATENT, IDX_HEADS * IDX_DIM, IDX_DIM, IDX_HEADS,
             CONV_WIDTH, CONV_WIDTH, CONV_WIDTH, SSM_WIDTH, MEM_HEADS * MEM_HEAD_DIM)
    offs = np.cumsum((0,) + sizes)
    q, ckv, qidx, kidx, widx, cu, cgb, cgc, ssm, memq = [w[:, offs[k]:offs[k + 1]] for k in range(len(sizes))]
    gates = w[:, offs[-1]:]
    pad = jnp.zeros((w.shape[0], LANES - IDX_DIM - IDX_HEADS), w.dtype)
    return jnp.concatenate([gates, q, cu, cgb, cgc, ssm, memq, qidx, ckv, kidx, widx, pad], axis=1).astype(BF16)


def kernel(x, mem, ln_in_g, ln_in_b, w_in, kv_norm_g, w_uk, w_uv, conv_w, conv_b, lam_re, lam_im, b_re, b_im, c_re, c_im, d_skip, log_dt, w_glu, b_glu, w_mem_kv, w_branch, w_o, ln1_g, ln1_b, w_router, b_router, w_up, b_up, w_down, b_down, ln2_g, ln2_b):
    bsz, seq, d = x.shape
    depth = w_in.shape[0]
    mlen = mem.shape[1]
    n = bsz * seq
    alpha = float((2 * depth) ** 0.25)
    tm_combine = 256

    h = _ln_in(x.reshape(n, d), ln_in_g, ln_in_b)
    n_row_blocks = -(-n * TOP_K // EXPERT_BLOCK) + N_EXPERTS
    xs = jnp.zeros((n_row_blocks * EXPERT_BLOCK, d), F32)
    mem2 = mem.reshape(bsz * mlen, d)
    for l in range(depth):
        proj = _matmul(h, _permute_w_in(w_in[l]), 256, "in_proj")
        qcat, kcat, ckvn, ckvt = _prep(proj, kv_norm_g[l], bsz, seq)
        att = _dsa_attention(proj, qcat, kcat, ckvn, ckvt, w_uk[l], w_uv[l], kv_norm_g[l], bsz, seq)
        cnv = _short_conv(proj, conv_w[l], conv_b[l], bsz, seq)
        ssm = _s5_mixer(proj, lam_re[l], lam_im[l], b_re[l], b_im[l], c_re[l], c_im[l], d_skip[l], log_dt[l],
                        w_glu[l], b_glu[l], bsz, seq)
        kvm = _matmul(mem2, w_mem_kv[l].astype(BF16), min(256, bsz * mlen), "mem_kv")
        mem_o = _memory_attention(proj, kvm, bsz, seq, mlen)
        h1, te_pad, tg_pad, counts = _merge_router(att, cnv, ssm, mem_o, proj, h, w_branch[l], w_o[l],
                                                   ln1_g[l], ln1_b[l], w_router[l], b_router[l], alpha)
        dest, blk_expert, n_used, pos_km = _moe_routing(te_pad[:, :TOP_K], te_pad[:, TOP_K:2 * TOP_K], counts,
                                                        n, tm_combine)
        xs = _dispatch(h1, dest, xs)
        ys = _expert_ffn(xs, blk_expert, n_used, l, w_up, b_up, w_down, b_down)
        h = _combine(ys, pos_km, tg_pad, h1, ln2_g[l], ln2_b[l], alpha, tm_combine)
    return h.reshape(bsz, seq, d)
```

```python
import functools
import math

import numpy as np
import jax
import jax.numpy as jnp
from jax import lax
from jax.experimental import pallas as pl
from jax.experimental.pallas import tpu as pltpu

F32 = jnp.float32
BF16 = jnp.bfloat16
I32 = jnp.int32
I16 = jnp.int16

N_HEADS = 8
HEAD_DIM = 64
KV_LATENT = 128
IDX_HEADS = 8
IDX_DIM = 32
INDEX_TOPK = 256
DSA_QUERY_BLOCK = 256
CONV_WIDTH = 512
SSM_WIDTH = 512
SSM_GROUP = 16
SSM_GROUPS = SSM_WIDTH // SSM_GROUP
SSM_STATE = 64
MEM_HEADS = 4
MEM_HEAD_DIM = 128
N_BRANCH = 4
BRANCH_WIDTH = 512
N_EXPERTS = 32
TOP_K = 4
D_EXPERT = 1024
SWIGLU_LIMIT = 7.0
SWIGLU_ALPHA = 1.702
EXPERT_BLOCK = 256
LN_EPS = 1e-5

LANES = 128
SUBLANES = 8
VMEM_LIMIT_BYTES = 56 * 1024 * 1024

INT_MIN = -(2 ** 31)
NEG_BIG = -1e30
SAFE_LOGIT = 100.0

COL_GATES = 0
COL_Q = 4096
COL_CONV_U = 4608
COL_CONV_GB = 5120
COL_CONV_GC = 5632
COL_SSM = 6144
COL_MEMQ = 6656
COL_QIDX = 7168
COL_CKV = 7424
COL_TAIL = 7552
D_PROJ = 7680
TAIL_W_OFF = IDX_DIM

SSM_SEQS = 8


def _cparams(sem):
    return pltpu.CompilerParams(dimension_semantics=sem, vmem_limit_bytes=VMEM_LIMIT_BYTES)


def _layer_norm(x, g, b):
    mu = jnp.mean(x, axis=-1, keepdims=True)
    xc = x - mu
    var = jnp.mean(xc * xc, axis=-1, keepdims=True)
    return xc * lax.rsqrt(var + LN_EPS) * g + b


def _const_spec(shape):
    nd = len(shape)
    return pl.BlockSpec(shape, lambda *_: (0,) * nd)


def _ln_kernel(x_ref, g_ref, b_ref, o_ref):
    o_ref[...] = _layer_norm(x_ref[...], g_ref[...], b_ref[...])


def _ln_in(x2, g, b, tm=512):
    n, d = x2.shape
    return pl.pallas_call(
        _ln_kernel,
        out_shape=jax.ShapeDtypeStruct((n, d), F32),
        grid=(n // tm,),
        in_specs=[pl.BlockSpec((tm, d), lambda i: (i, 0)), _const_spec((1, d)), _const_spec((1, d))],
        out_specs=pl.BlockSpec((tm, d), lambda i: (i, 0)),
        compiler_params=_cparams(("parallel",)),
        name="ln_in",
    )(x2, g.reshape(1, d), b.reshape(1, d))


def _matmul_kernel(x_ref, w_ref, o_ref):
    o_ref[...] = jnp.dot(x_ref[...].astype(BF16), w_ref[...], preferred_element_type=F32)


def _matmul(x, w_bf16, tm, name):
    n, k = x.shape
    m = w_bf16.shape[1]
    return pl.pallas_call(
        _matmul_kernel,
        out_shape=jax.ShapeDtypeStruct((n, m), F32),
        grid=(n // tm,),
        in_specs=[pl.BlockSpec((tm, k), lambda i: (i, 0)),
                  pl.BlockSpec((k, m), lambda i: (0, 0), pipeline_mode=pl.Buffered(1))],
        out_specs=pl.BlockSpec((tm, m), lambda i: (i, 0)),
        compiler_params=_cparams(("parallel",)),
        name=name,
    )(x, w_bf16)


def _split_hi_lo(x):
    hi = x.astype(BF16)
    lo = (x - hi.astype(F32)).astype(BF16)
    return hi, lo


def _prep_kernel(qidx_ref, ckv_ref, tail_ref, g_ref, sq_ref, sk_ref, qcat_ref, kcat_ref, ckvn_ref, ckvt_ref):
    q_hi, q_lo = _split_hi_lo(qidx_ref[...])
    qcat = jnp.dot(jnp.concatenate([q_hi, q_lo], axis=1), sq_ref[...], preferred_element_type=F32)
    qcat_ref[...] = qcat.astype(BF16)
    k_hi, k_lo = _split_hi_lo(tail_ref[...])
    kcat = jnp.dot(jnp.concatenate([k_hi, k_lo], axis=1), sk_ref[...], preferred_element_type=F32)
    kcat_ref[...] = kcat.astype(BF16)
    c = ckv_ref[...]
    ms = jnp.mean(c * c, axis=-1, keepdims=True)
    cn = c * lax.rsqrt(ms + LN_EPS) * g_ref[...]
    ckvn_ref[...] = cn.astype(BF16)
    ckvt_ref[...] = cn.T.astype(BF16)


def _selection_matrices():
    sq = np.zeros((2 * IDX_HEADS * IDX_DIM, IDX_HEADS * LANES), np.float32)
    for h in range(IDX_HEADS):
        for d in range(IDX_DIM):
            hi_in = h * IDX_DIM + d
            lo_in = IDX_HEADS * IDX_DIM + hi_in
            sq[hi_in, h * LANES + d] = 1.0
            sq[hi_in, h * LANES + IDX_DIM + d] = 1.0
            sq[lo_in, h * LANES + 2 * IDX_DIM + d] = 1.0
    sk = np.zeros((2 * LANES, LANES), np.float32)
    for d in range(IDX_DIM):
        sk[d, d] = 1.0
        sk[LANES + d, IDX_DIM + d] = 1.0
        sk[d, 2 * IDX_DIM + d] = 1.0
    return jnp.asarray(sq, BF16), jnp.asarray(sk, BF16)


def _prep(proj, kv_norm_g, bsz, seq, tm=512):
    n = proj.shape[0]
    tm = min(tm, seq)
    nbt = seq // tm
    sq, sk = _selection_matrices()
    return pl.pallas_call(
        _prep_kernel,
        out_shape=(jax.ShapeDtypeStruct((n, IDX_HEADS * LANES), BF16),
                   jax.ShapeDtypeStruct((n, LANES), BF16),
                   jax.ShapeDtypeStruct((n, KV_LATENT), BF16),
                   jax.ShapeDtypeStruct((bsz, KV_LATENT, seq), BF16)),
        grid=(n // tm,),
        in_specs=[pl.BlockSpec((tm, 256), lambda i: (i, COL_QIDX // 256)),
                  pl.BlockSpec((tm, 128), lambda i: (i, COL_CKV // 128)),
                  pl.BlockSpec((tm, 128), lambda i: (i, COL_TAIL // 128)),
                  _const_spec((1, KV_LATENT)), _const_spec(sq.shape), _const_spec(sk.shape)],
        out_specs=(pl.BlockSpec((tm, IDX_HEADS * LANES), lambda i: (i, 0)),
                   pl.BlockSpec((tm, LANES), lambda i: (i, 0)),
                   pl.BlockSpec((tm, KV_LATENT), lambda i: (i, 0)),
                   pl.BlockSpec((None, KV_LATENT, tm), lambda i: (i // nbt, 0, i % nbt))),
        compiler_params=_cparams(("parallel",)),
        name="dsa_prep",
    )(proj, proj, proj, kv_norm_g.reshape(1, KV_LATENT), sq, sk)


def _dsa_kernel(q_ref, tail_ref, qcat_ref, kcat_ref, ckv_ref, ckvt_ref, wuk_ref, wuv_ref, kbound_ref, o_ref,
                key_ref, khi_ref, klo_ref, qlat_ref, m_ref, l_ref, acc_ref, *, seq, topk, cks, ck, cka, qb):
    i = pl.program_id(1)
    q0 = i * qb
    nchs = (q0 + qb + cks - 1) // cks
    nch = (q0 + qb + ck - 1) // ck
    q_pos = q0 + lax.broadcasted_iota(I32, (1, qb), 1)
    row_pos_s = lax.broadcasted_iota(I32, (cks, 1), 0)
    row_pos = lax.broadcasted_iota(I32, (ck, 1), 0)
    w_t = tail_ref[...].T
    nt_dims = (((1,), (1,)), ((), ()))
    cnt_rows = min(64, ck)

    def score_chunk(c, carry):
        k0 = pl.multiple_of(c * cks, cks)
        kc = kcat_ref[pl.ds(k0, cks), :]
        s = jnp.zeros((cks, qb), F32)
        for h in range(IDX_HEADS):
            d = lax.dot_general(kc, qcat_ref[:, h * LANES:(h + 1) * LANES], nt_dims, preferred_element_type=F32)
            s = s + jnp.maximum(d, 0.0) * w_t[TAIL_W_OFF + h:TAIL_W_OFF + h + 1, :]
        s = s + 0.0
        bits = pltpu.bitcast(s, I32)
        key = bits ^ ((bits >> 31) & jnp.int32(0x7FFFFFFF))
        key = jnp.where(k0 + row_pos_s <= q_pos, key, jnp.int32(INT_MIN))
        key_ref[pl.ds(k0, cks), :] = key
        khi_ref[pl.ds(k0, cks), :] = (key >> 16).astype(I16)
        klo_ref[pl.ds(k0, cks), :] = ((key & jnp.int32(0xFFFF)) - 32768).astype(I16)
        return carry

    lax.fori_loop(0, nchs, score_chunk, 0)

    def count_keys(pred_fn):
        def body(c, acc):
            k0 = pl.multiple_of(c * ck, ck)
            hit = jnp.where(pred_fn(key_ref[pl.ds(k0, ck), :], k0), 1.0, 0.0)
            return acc + jnp.sum(hit.reshape(ck // cnt_rows, cnt_rows, qb), axis=0)
        acc = lax.fori_loop(0, nch, body, jnp.zeros((cnt_rows, qb), F32))
        return jnp.sum(acc, axis=0, keepdims=True)

    kf = float(topk)
    pack = 16
    n_part = ck // (2 * pack)
    lo16, hi16 = -32768, 32767

    def count16(ref, cand16):
        cand = jnp.broadcast_to(cand16, (pack, qb)).astype(I16)
        one = jnp.ones((), I16)
        zero = jnp.zeros((), I16)

        def body(c, acc):
            k0 = pl.multiple_of(c * ck, ck)
            hit = jnp.where(ref[pl.ds(k0, ck), :].reshape(ck // pack, pack, qb) >= cand[None], one, zero)
            part = hit[0:2]
            for j in range(1, n_part):
                part = part + hit[2 * j:2 * j + 2]
            return acc + part
        acc = lax.fori_loop(0, nch, body, jnp.zeros((2, pack, qb), I16)).astype(I32).astype(F32)
        return jnp.sum(jnp.sum(acc, axis=0), axis=0, keepdims=True)

    def bisect16(ref, target):
        def one_pass(b, carry):
            w, c_ge_w, c_gt_w = carry
            cand = w + jnp.left_shift(jnp.int32(1), 15 - b)
            cnt = count16(ref, cand)
            ok = cnt >= target
            return jnp.where(ok, cand, w), jnp.where(ok, cnt, c_ge_w), jnp.where(ok, c_gt_w, cnt)
        total = jnp.full((1, qb), 1.0, F32) * (nch * ck).astype(F32)
        return lax.fori_loop(0, 16, one_pass, (jnp.full((1, qb), lo16, I32), total, jnp.zeros((1, qb), F32)))

    v16, c_v16, c_above = bisect16(khi_ref, kf)
    v16_b = jnp.broadcast_to(v16, (pack, qb)).astype(I16)

    def build_lo(c, carry):
        k0 = pl.multiple_of(c * ck, ck)
        hi = khi_ref[pl.ds(k0, ck), :].reshape(ck // pack, pack, qb)
        lo = klo_ref[pl.ds(k0, ck), :].reshape(ck // pack, pack, qb)
        khi_ref[pl.ds(k0, ck), :] = jnp.where(hi == v16_b[None], lo, jnp.full((), lo16, I16)).reshape(ck, qb)
        return carry

    lax.fori_loop(0, nch, build_lo, 0)
    w, c_w, c_w_above = bisect16(khi_ref, kf - c_above)
    c_w = jnp.where(w == lo16, c_v16 - c_above, c_w)
    v = (v16 << 16) + (w + 32768)
    select_all = v == jnp.int32(INT_MIN)
    thr = jnp.maximum(v, jnp.int32(INT_MIN + 1))
    c_ge = jnp.where(select_all, c_above, c_above + c_w)
    c_gt = jnp.where(select_all, c_above, c_above + c_w_above)
    need = kf - c_gt

    jlim_ref = m_ref.at[0:1]
    jlim_ref[...] = jnp.full((1, qb), float(seq), F32)
    has_tie = jnp.max(jnp.where(c_ge > kf, 1.0, 0.0)) > 0.0

    @pl.when(has_tie)
    def _():
        n_bits = max(1, int(math.ceil(math.log2(seq))))

        def pos_pass(b, j):
            cand = j + jnp.left_shift(jnp.int32(1), n_bits - 1 - b)
            cnt = count_keys(lambda key, k0: (key == thr) & (k0 + row_pos < cand))
            return jnp.where(cnt < need, cand, j)

        j = lax.fori_loop(0, n_bits, pos_pass, jnp.zeros((1, qb), I32))
        jlim_ref[...] = jnp.where(c_ge > kf, j.astype(F32), float(seq))

    jlim = jlim_ref[...].astype(I32)

    q_bf = q_ref[...].astype(BF16)
    qn2_max = jnp.zeros((1, qb), F32)
    for j in range(N_HEADS // 2):
        ql = jnp.dot(q_bf[:, j * LANES:(j + 1) * LANES], wuk_ref[j],
                     preferred_element_type=F32) * (HEAD_DIM ** -0.5 * math.log2(math.e))
        for half in range(2):
            qt = ql[:, half * KV_LATENT:(half + 1) * KV_LATENT].T.astype(BF16)
            qlat_ref[2 * j + half] = qt
            qf = qt.astype(F32)
            qn2_max = jnp.maximum(qn2_max, jnp.sum(qf * qf, axis=0, keepdims=True))
    logit_bound = jnp.sqrt(jnp.max(qn2_max)) * kbound_ref[0, 0]
    row_pos_a = lax.broadcasted_iota(I32, (cka, 1), 0)

    def att_tiles(base, n_sub, online):
        for sub in range(n_sub):
            k0 = pl.multiple_of(base + sub * cka, cka)
            kv = ckv_ref[pl.ds(k0, cka), :]
            kvt = ckvt_ref[:, pl.ds(k0, cka)]
            key = key_ref[pl.ds(k0, cka), :]
            sel = (key > thr) | ((key == thr) & (k0 + row_pos_a <= jlim))
            bias = jnp.where(sel, 0.0, NEG_BIG)
            for h in range(N_HEADS):
                lg = jnp.dot(kv, qlat_ref[h], preferred_element_type=F32) + bias
                if not online:
                    p = jnp.exp2(lg)
                    l_ref[h:h + 1, :] = l_ref[h:h + 1, :] + jnp.sum(p, axis=0, keepdims=True)
                    acc_ref[h] = acc_ref[h] + jnp.dot(kvt, p.astype(BF16), preferred_element_type=F32)
                else:
                    m_prev = m_ref[h:h + 1, :]
                    m_new = jnp.maximum(m_prev, jnp.max(lg, axis=0, keepdims=True))
                    p = jnp.exp2(lg - m_new)
                    alpha = jnp.exp2(m_prev - m_new)
                    l_ref[h:h + 1, :] = alpha * l_ref[h:h + 1, :] + jnp.sum(p, axis=0, keepdims=True)
                    acc_ref[h] = alpha * acc_ref[h] + jnp.dot(kvt, p.astype(BF16), preferred_element_type=F32)
                    m_ref[h:h + 1, :] = m_new

    def attend(online):
        def att_chunk(c, carry):
            att_tiles(c * cks, cks // cka, online)
            return carry

        end = q0 + qb
        n_full = end // cks
        lax.fori_loop(0, n_full, att_chunk, 0)
        if cks % qb == 0 and cks > qb:
            for parts in range(1, cks // qb):
                @pl.when(end - n_full * cks == parts * qb)
                def _():
                    att_tiles(n_full * cks, parts * qb // cka, online)
        else:
            @pl.when(end - n_full * cks > 0)
            def _():
                att_tiles(n_full * cks, cks // cka, online)

    m_ref[...] = jnp.full(m_ref.shape, NEG_BIG, F32)
    l_ref[...] = jnp.zeros(l_ref.shape, F32)
    acc_ref[...] = jnp.zeros(acc_ref.shape, F32)

    @pl.when(logit_bound < SAFE_LOGIT)
    def _():
        attend(online=False)

    @pl.when(jnp.logical_not(logit_bound < SAFE_LOGIT))
    def _():
        attend(online=True)

    outs = []
    for j in range(N_HEADS // 2):
        o_pair = [(acc_ref[h] / l_ref[h:h + 1, :]).T.astype(BF16) for h in (2 * j, 2 * j + 1)]
        outs.append(jnp.dot(jnp.concatenate(o_pair, axis=1), wuv_ref[j], preferred_element_type=F32))
    o_ref[...] = jnp.concatenate(outs, axis=1)


def _dsa_attention(proj, qcat, kcat, ckvn, ckvt, w_uk, w_uv, kv_norm_g, bsz, seq):
    n = proj.shape[0]
    qb = min(DSA_QUERY_BLOCK, seq)
    n_blk = seq // qb
    topk = min(INDEX_TOPK, seq // 4)
    cks = min(512, seq)
    ck = min(512, seq)
    cka = min(128, seq)
    zk = jnp.zeros((HEAD_DIM, KV_LATENT), F32)
    zv = jnp.zeros((KV_LATENT, HEAD_DIM), F32)
    wuk = jnp.stack([jnp.block([[w_uk[:, 2 * j, :].T, zk], [zk, w_uk[:, 2 * j + 1, :].T]])
                     for j in range(N_HEADS // 2)]).astype(BF16)
    wuv = jnp.stack([jnp.block([[w_uv[:, 2 * j, :], zv], [zv, w_uv[:, 2 * j + 1, :]]])
                     for j in range(N_HEADS // 2)]).astype(BF16)
    kbound = (1.01 * math.sqrt(KV_LATENT) * jnp.max(jnp.abs(kv_norm_g))).reshape(1, 1).astype(F32)
    kernel = functools.partial(_dsa_kernel, seq=seq, topk=topk, cks=cks, ck=ck, cka=cka, qb=qb)
    return pl.pallas_call(
        kernel,
        out_shape=jax.ShapeDtypeStruct((n, N_HEADS * HEAD_DIM), F32),
        grid=(bsz, n_blk),
        in_specs=[pl.BlockSpec((qb, 512), lambda b, i: (b * n_blk + i, COL_Q // 512)),
                  pl.BlockSpec((qb, 128), lambda b, i: (b * n_blk + i, COL_TAIL // 128)),
                  pl.BlockSpec((qb, IDX_HEADS * LANES), lambda b, i: (b * n_blk + i, 0)),
                  pl.BlockSpec((seq, LANES), lambda b, i: (b, 0)),
                  pl.BlockSpec((seq, KV_LATENT), lambda b, i: (b, 0)),
                  pl.BlockSpec((None, KV_LATENT, seq), lambda b, i: (b, 0, 0)),
                  _const_spec(wuk.shape), _const_spec(wuv.shape),
                  pl.BlockSpec(memory_space=pltpu.SMEM)],
        out_specs=pl.BlockSpec((qb, N_HEADS * HEAD_DIM), lambda b, i: (b * n_blk + i, 0)),
        scratch_shapes=[pltpu.VMEM((seq, qb), I32),
                        pltpu.VMEM((seq, qb), I16),
                        pltpu.VMEM((seq, qb), I16),
                        pltpu.VMEM((N_HEADS, KV_LATENT, qb), BF16),
                        pltpu.VMEM((N_HEADS, qb), F32),
                        pltpu.VMEM((N_HEADS, qb), F32),
                        pltpu.VMEM((N_HEADS, KV_LATENT, qb), F32)],
        compiler_params=_cparams(("parallel", "parallel")),
        name="dsa_attention",
    )(proj, proj, qcat, kcat, ckvn, ckvt, wuk, wuv, kbound)


def _conv_kernel(u_ref, gb_ref, gc_ref, pu_ref, pgc_ref, w_ref, b_ref, o_ref):
    i = pl.program_id(1)
    tb = u_ref.shape[0]
    v = gc_ref[...] * u_ref[...]
    halo = jnp.where(i > 0, pgc_ref[...] * pu_ref[...], 0.0)
    vfull = jnp.concatenate([halo, v], axis=0)
    v1 = pltpu.roll(vfull, 1, 0)[SUBLANES:SUBLANES + tb]
    v2 = pltpu.roll(vfull, 2, 0)[SUBLANES:SUBLANES + tb]
    w = w_ref[...]
    y = w[0:1] * v2 + w[1:2] * v1 + w[2:3] * v
    o_ref[...] = gb_ref[...] * (y + b_ref[...])


def _short_conv(proj, conv_w, conv_b, bsz, seq, tb=512):
    n = proj.shape[0]
    tb = min(tb, seq)
    nb = seq // tb
    cw = CONV_WIDTH
    blk = lambda col: pl.BlockSpec((tb, cw), lambda b, i: (b * nb + i, col // cw))
    prev = lambda col: pl.BlockSpec(
        (SUBLANES, cw), lambda b, i: (jnp.maximum((b * nb + i) * (tb // SUBLANES) - 1, 0), col // cw))
    return pl.pallas_call(
        _conv_kernel,
        out_shape=jax.ShapeDtypeStruct((n, cw), F32),
        grid=(bsz, nb),
        in_specs=[blk(COL_CONV_U), blk(COL_CONV_GB), blk(COL_CONV_GC), prev(COL_CONV_U), prev(COL_CONV_GC),
                  _const_spec((SUBLANES, cw)), _const_spec((1, cw))],
        out_specs=pl.BlockSpec((tb, cw), lambda b, i: (b * nb + i, 0)),
        compiler_params=_cparams(("parallel", "parallel")),
        name="short_conv",
    )(proj, proj, proj, proj, proj,
      jnp.pad(conv_w, ((0, SUBLANES - conv_w.shape[0]), (0, 0))), conv_b.reshape(1, cw))


def _s5_scan_chunk(bu_ref, a_ref, st_ref, n_steps, store):
    s_tot = a_ref.shape[1] // 2
    cb = 512
    for blk in range(s_tot // cb):
        re_sl = pl.ds(blk * cb, cb)
        im_sl = pl.ds(s_tot + blk * cb, cb)
        ar = jnp.broadcast_to(a_ref[0:1, blk * cb:(blk + 1) * cb], (SSM_SEQS, cb))
        ai = jnp.broadcast_to(a_ref[0:1, s_tot + blk * cb:s_tot + (blk + 1) * cb], (SSM_SEQS, cb))

        def step(t, carry):
            re, im = carry
            r0 = pl.multiple_of(t * SSM_SEQS, SSM_SEQS)
            br = bu_ref[pl.ds(r0, SSM_SEQS), re_sl]
            bi = bu_ref[pl.ds(r0, SSM_SEQS), im_sl]
            nre = ar * re - ai * im + br
            nim = ar * im + ai * re + bi
            if store:
                bu_ref[pl.ds(r0, SSM_SEQS), re_sl] = nre
                bu_ref[pl.ds(r0, SSM_SEQS), im_sl] = nim
            return nre, nim

        re, im = lax.fori_loop(0, n_steps, step, (st_ref[:, re_sl], st_ref[:, im_sl]), unroll=8)
        st_ref[:, re_sl] = re
        st_ref[:, im_sl] = im


def _gelu_tanh(x):
    return 0.5 * x * (1.0 + jnp.tanh(math.sqrt(2.0 / math.pi) * (x + 0.044715 * (x * x * x))))


S5_ROWS = 512


def _s5_kernel(u_ref, perm_ref, wb_ref, a_ref, apow_ref, wc_ref, d_ref, wg_ref, bg_ref, o_ref,
               bu_ref, st_ref, carry_ref, *, seg_len, steps_per_seq):
    c = pl.program_id(0)
    s_tot = a_ref.shape[1] // 2
    tn_dims = (((0,), (0,)), ((), ()))

    @pl.when(c % steps_per_seq == 0)
    def _():
        carry_ref[...] = jnp.zeros(carry_ref.shape, F32)

    u = u_ref[...]
    perm = perm_ref[...]
    u_il = jnp.dot(perm, u.astype(BF16), preferred_element_type=F32).astype(BF16)
    hw = u_il.shape[1] // 2
    hs = s_tot // 2
    for part in range(2):
        for q in range(2):
            c0 = part * s_tot + q * hs
            bu_ref[:, c0:c0 + hs] = jnp.dot(u_il[:, q * hw:(q + 1) * hw], wb_ref[q * hw:(q + 1) * hw, c0:c0 + hs],
                                            preferred_element_type=F32)

    st_ref[...] = jnp.zeros(st_ref.shape, F32)
    _s5_scan_chunk(bu_ref, a_ref, st_ref, seg_len, store=False)

    pr = apow_ref[0:1, :s_tot]
    pi = apow_ref[0:1, s_tot:]
    hre = carry_ref[0:1, :s_tot]
    him = carry_ref[0:1, s_tot:]
    for j in range(SSM_SEQS):
        ere = st_ref[j:j + 1, :s_tot]
        eim = st_ref[j:j + 1, s_tot:]
        st_ref[j:j + 1, :s_tot] = hre
        st_ref[j:j + 1, s_tot:] = him
        hre, him = pr * hre - pi * him + ere, pr * him + pi * hre + eim
    carry_ref[0:1, :s_tot] = hre
    carry_ref[0:1, s_tot:] = him

    _s5_scan_chunk(bu_ref, a_ref, st_ref, seg_len, store=True)
    halves = []
    for q in range(2):
        acc = None
        for part in range(2):
            c0 = part * s_tot + q * hs
            term = jnp.dot(bu_ref[:, c0:c0 + hs].astype(BF16), wc_ref[c0:c0 + hs, q * hw:(q + 1) * hw],
                           preferred_element_type=F32)
            acc = term if acc is None else acc + term
        halves.append(acc)
    ch = jnp.concatenate(halves, axis=1)
    ch_hi, ch_lo = _split_hi_lo(ch)
    ch_nat = (lax.dot_general(perm, ch_hi, tn_dims, preferred_element_type=F32)
              + lax.dot_general(perm, ch_lo, tn_dims, preferred_element_type=F32))
    z = _gelu_tanh(ch_nat + d_ref[...] * u)
    gate = jnp.dot(z.astype(BF16), wg_ref[...], preferred_element_type=F32) + bg_ref[...]
    o_ref[...] = z * jax.nn.sigmoid(gate)


def _s5_mixer(proj, lam_re, lam_im, b_re, b_im, c_re, c_im, d_skip, log_dt, w_glu, b_glu, bsz, seq):
    n = proj.shape[0]
    g, p, nn = SSM_GROUPS, SSM_STATE, SSM_GROUP
    s_tot = g * p
    rows = min(S5_ROWS, seq)
    seg_len = rows // SSM_SEQS

    dt = jnp.exp(log_dt.astype(F32))[:, None]
    lam = lax.complex(lam_re.astype(F32), lam_im.astype(F32))
    a_bar = jnp.exp(lam * dt)
    b_bar = ((a_bar - 1.0) / lam)[:, :, None] * lax.complex(b_re.astype(F32), b_im.astype(F32))
    a_pow = jnp.exp(lam * dt * seg_len)
    a_vec = jnp.concatenate([jnp.real(a_bar).reshape(1, s_tot), jnp.imag(a_bar).reshape(1, s_tot)], axis=1)
    apow_vec = jnp.concatenate([jnp.real(a_pow).reshape(1, s_tot), jnp.imag(a_pow).reshape(1, s_tot)], axis=1)
    eye = jnp.eye(g, dtype=F32)
    wb = jnp.concatenate(
        [jnp.einsum('gpn,gh->gnhp', jnp.real(b_bar), eye).reshape(g * nn, s_tot),
         jnp.einsum('gpn,gh->gnhp', jnp.imag(b_bar), eye).reshape(g * nn, s_tot)], axis=1).astype(BF16)
    wc = jnp.concatenate(
        [jnp.einsum('gnp,gh->gphn', c_re.astype(F32), eye).reshape(s_tot, g * nn),
         -jnp.einsum('gnp,gh->gphn', c_im.astype(F32), eye).reshape(s_tot, g * nn)], axis=0).astype(BF16)
    perm_np = np.zeros((rows, rows), np.float32)
    t_idx, j_idx = np.meshgrid(np.arange(seg_len), np.arange(SSM_SEQS), indexing='ij')
    perm_np[(t_idx * SSM_SEQS + j_idx).ravel(), (j_idx * seg_len + t_idx).ravel()] = 1.0
    perm = jnp.asarray(perm_np, BF16)

    return pl.pallas_call(
        functools.partial(_s5_kernel, seg_len=seg_len, steps_per_seq=seq // rows),
        out_shape=jax.ShapeDtypeStruct((n, SSM_WIDTH), F32),
        grid=(n // rows,),
        in_specs=[pl.BlockSpec((rows, SSM_WIDTH), lambda c: (c, COL_SSM // SSM_WIDTH)),
                  _const_spec(perm.shape), _const_spec(wb.shape), _const_spec(a_vec.shape),
                  _const_spec(apow_vec.shape), _const_spec(wc.shape), _const_spec((1, SSM_WIDTH)),
                  _const_spec((SSM_WIDTH, SSM_WIDTH)), _const_spec((1, SSM_WIDTH))],
        out_specs=pl.BlockSpec((rows, SSM_WIDTH), lambda c: (c, 0)),
        scratch_shapes=[pltpu.VMEM((rows, 2 * s_tot), F32), pltpu.VMEM((SSM_SEQS, 2 * s_tot), F32),
                        pltpu.VMEM((1, 2 * s_tot), F32)],
        compiler_params=_cparams(("arbitrary",)),
        name="s5_mixer",
    )(proj, perm, wb, a_vec, apow_vec, wc, d_skip.reshape(1, SSM_WIDTH).astype(F32),
      w_glu.astype(BF16), b_glu.reshape(1, SSM_WIDTH))


def _mem_attn_kernel(q_ref, kv_ref, o_ref):
    hw = MEM_HEADS * MEM_HEAD_DIM
    q = q_ref[...].astype(BF16)
    kv = kv_ref[...].astype(BF16)
    for h in range(MEM_HEADS):
        sl = slice(h * MEM_HEAD_DIM, (h + 1) * MEM_HEAD_DIM)
        k = kv[:, sl]
        v = kv[:, hw + h * MEM_HEAD_DIM:hw + (h + 1) * MEM_HEAD_DIM]
        lg = lax.dot_general(q[:, sl], k, (((1,), (1,)), ((), ())),
                             preferred_element_type=F32) * (MEM_HEAD_DIM ** -0.5)
        m = jnp.max(lg, axis=1, keepdims=True)
        p = jnp.exp(lg - m)
        p = p / jnp.sum(p, axis=1, keepdims=True)
        o_ref[:, sl] = jnp.dot(p.astype(BF16), v, preferred_element_type=F32)


def _memory_attention(proj, kvm, bsz, seq, mlen, tm=512):
    n = proj.shape[0]
    tm = min(tm, seq)
    nb = seq // tm
    hw = MEM_HEADS * MEM_HEAD_DIM
    return pl.pallas_call(
        _mem_attn_kernel,
        out_shape=jax.ShapeDtypeStruct((n, hw), F32),
        grid=(bsz, nb),
        in_specs=[pl.BlockSpec((tm, hw), lambda b, i: (b * nb + i, COL_MEMQ // hw)),
                  pl.BlockSpec((mlen, 2 * hw), lambda b, i: (b, 0))],
        out_specs=pl.BlockSpec((tm, hw), lambda b, i: (b * nb + i, 0)),
        compiler_params=_cparams(("parallel", "parallel")),
        name="memory_attention",
    )(proj, kvm)


def _merge_kernel(att_ref, cnv_ref, ssm_ref, mem_ref, gl_ref, h_ref, wbr_ref, wo_ref, g_ref, b_ref,
                  wr_ref, br_ref, ltri_ref, h1_ref, te_ref, tg_ref, cnt_ref, base_ref, *, alpha):
    d = h_ref.shape[1]

    @pl.when(pl.program_id(0) == 0)
    def _():
        base_ref[...] = jnp.zeros(base_ref.shape, F32)

    merged = jnp.zeros(h_ref.shape, F32)
    for r, br in enumerate((att_ref, cnv_ref, ssm_ref, mem_ref)):
        y = jnp.dot(br[...].astype(BF16), wbr_ref[r], preferred_element_type=F32)
        merged = merged + y * jax.nn.sigmoid(gl_ref[:, r * d:(r + 1) * d])
    y = alpha * h_ref[...] + jnp.dot(merged.astype(BF16), wo_ref[...], preferred_element_type=F32)
    h1 = _layer_norm(y, g_ref[...], b_ref[...])
    h1_ref[...] = h1

    logits = jnp.dot(h1, wr_ref[...], preferred_element_type=F32, precision=lax.Precision.HIGHEST) + br_ref[...]
    tm = logits.shape[0]
    lane = lax.broadcasted_iota(I32, (tm, N_EXPERTS), 1)
    out_lane = lax.broadcasted_iota(I32, (tm, LANES), 1)
    work = logits
    top_e = jnp.zeros((tm, LANES), I32)
    top_v = jnp.zeros((tm, LANES), F32)
    vals = []
    onehots = []
    for k in range(TOP_K):
        mx = jnp.max(work, axis=1, keepdims=True)
        idx = jnp.min(jnp.where(work == mx, lane, N_EXPERTS), axis=1, keepdims=True)
        hit = lane == idx
        work = jnp.where(hit, -jnp.inf, work)
        top_e = jnp.where(out_lane == k, idx, top_e)
        vals.append(mx)
        onehots.append(jnp.where(hit, 1.0, 0.0))
    den = sum(jnp.exp(vk - vals[0]) for vk in vals)
    for k in range(TOP_K):
        top_v = jnp.where(out_lane == k, jnp.exp(vals[k] - vals[0]) / den, top_v)

    per_tok = onehots[0] + onehots[1] + onehots[2] + onehots[3]
    before = jnp.dot(ltri_ref[...], per_tok.astype(BF16), preferred_element_type=F32) + base_ref[...]
    for k in range(TOP_K):
        rank = jnp.sum(onehots[k] * before, axis=1, keepdims=True)
        top_e = jnp.where(out_lane == TOP_K + k, rank.astype(I32), top_e)
    base_ref[...] = base_ref[...] + jnp.sum(per_tok, axis=0, keepdims=True)
    cnt_ref[...] = base_ref[...]
    te_ref[...] = top_e
    tg_ref[...] = top_v


def _merge_router(att, cnv, ssm, mem_o, proj, h, w_branch, w_o, ln_g, ln_b, w_router, b_router, alpha, tm=512):
    n, d = h.shape
    bw = BRANCH_WIDTH
    row = lambda w: pl.BlockSpec((tm, w), lambda i: (i, 0))
    ltri = jnp.asarray(np.tril(np.ones((tm, tm), np.float32), -1), BF16)
    return pl.pallas_call(
        functools.partial(_merge_kernel, alpha=alpha),
        out_shape=(jax.ShapeDtypeStruct((n, d), F32),
                   jax.ShapeDtypeStruct((n, LANES), I32),
                   jax.ShapeDtypeStruct((n, LANES), F32),
                   jax.ShapeDtypeStruct((1, N_EXPERTS), F32)),
        grid=(n // tm,),
        in_specs=[row(bw), row(bw), row(bw), row(bw),
                  pl.BlockSpec((tm, N_BRANCH * d), lambda i: (i, COL_GATES // (N_BRANCH * d))),
                  row(d),
                  _const_spec((N_BRANCH, bw, d)), _const_spec((d, d)), _const_spec((1, d)), _const_spec((1, d)),
                  _const_spec((d, N_EXPERTS)), _const_spec((1, N_EXPERTS)), _const_spec((tm, tm))],
        out_specs=(row(d), row(LANES), row(LANES), _const_spec((1, N_EXPERTS))),
        scratch_shapes=[pltpu.VMEM((1, N_EXPERTS), F32)],
        compiler_params=_cparams(("arbitrary",)),
        name="merge_router",
    )(att, cnv, ssm, mem_o, proj, h, w_branch.astype(BF16), w_o.astype(BF16),
      ln_g.reshape(1, d), ln_b.reshape(1, d), w_router, b_router.reshape(1, N_EXPERTS), ltri)


DISPATCH_UNROLL = 8
COMBINE_UNROLL = 16


def _dispatch_kernel(dest_ref, h_ref, xs_in_hbm, xs_hbm, sem, *, tm):
    del xs_in_hbm

    def body(r, carry):
        for k in range(TOP_K):
            pltpu.make_async_copy(h_ref.at[pl.ds(r, 1), :],
                                  xs_hbm.at[pl.ds(dest_ref[0, r * TOP_K + k], 1), :], sem).start()
        return carry

    lax.fori_loop(0, tm, body, 0, unroll=DISPATCH_UNROLL)
    pltpu.make_async_copy(xs_hbm.at[pl.ds(0, tm * TOP_K), :], xs_hbm.at[pl.ds(0, tm * TOP_K), :], sem).wait()


def _dispatch(h1, dest, xs_prev, tm=256):
    n, d = h1.shape
    nb = n // tm
    return pl.pallas_call(
        functools.partial(_dispatch_kernel, tm=tm),
        out_shape=jax.ShapeDtypeStruct(xs_prev.shape, F32),
        grid=(nb,),
        in_specs=[pl.BlockSpec((None, 1, tm * TOP_K), lambda i: (i, 0, 0), memory_space=pltpu.SMEM),
                  pl.BlockSpec((tm, d), lambda i: (i, 0)),
                  pl.BlockSpec(memory_space=pl.ANY)],
        out_specs=pl.BlockSpec(memory_space=pl.ANY),
        scratch_shapes=[pltpu.SemaphoreType.DMA(())],
        input_output_aliases={2: 0},
        compiler_params=_cparams(("arbitrary",)),
        name="moe_dispatch",
    )(dest.reshape(nb, 1, tm * TOP_K), h1, xs_prev)


def _expert_kernel(meta_ref, x_ref, wu_ref, bu_ref, wd_ref, bd_ref, sel_ref, o_ref, wu_bf, wd_bf):
    j = pl.program_id(0)
    n_used = meta_ref[0]
    f2 = wu_ref.shape[1]

    @pl.when((j < n_used) & ((j == 0) | (meta_ref[1 + j] != meta_ref[jnp.maximum(j, 1)])))
    def _():
        sel = sel_ref[...]
        cw = sel.shape[0]
        for c in range(f2 // cw):
            grp = jnp.dot(wu_ref[:, c * cw:(c + 1) * cw].astype(BF16), sel, preferred_element_type=F32)
            wu_bf[:, c * cw:(c + 1) * cw] = grp.astype(BF16)
        wd_bf[...] = wd_ref[...].astype(BF16)

    @pl.when(j < n_used)
    def _():
        xb = x_ref[...].astype(BF16)
        hdn = jnp.dot(xb, wu_bf[...], preferred_element_type=F32) + bu_ref[...]
        cw = sel_ref.shape[0]
        half = cw // 2
        parts = []
        for c in range(f2 // cw):
            h_glu = jnp.minimum(hdn[:, c * cw:c * cw + half], SWIGLU_LIMIT)
            h_lin = jnp.clip(hdn[:, c * cw + half:(c + 1) * cw], -SWIGLU_LIMIT, SWIGLU_LIMIT)
            parts.append((h_glu * jax.nn.sigmoid(SWIGLU_ALPHA * h_glu) * (h_lin + 1.0)).astype(BF16))
        act_c = jnp.concatenate(parts, axis=1)
        o_ref[...] = jnp.dot(act_c, wd_bf[...], preferred_element_type=F32) + bd_ref[...]

    @pl.when(j >= n_used)
    def _():
        o_ref[...] = jnp.zeros(o_ref.shape, F32)


def _expert_ffn(xs, blk_expert, n_used, layer, w_up, b_up, w_down, b_down):
    n_rows, d = xs.shape
    n_blocks = n_rows // EXPERT_BLOCK
    f = D_EXPERT
    meta = jnp.concatenate([n_used.reshape(1).astype(I32), blk_expert.astype(I32)])
    sel_np = np.zeros((2 * LANES, 2 * LANES), np.float32)
    sel_np[2 * np.arange(LANES), np.arange(LANES)] = 1.0
    sel_np[2 * np.arange(LANES) + 1, LANES + np.arange(LANES)] = 1.0
    sel = jnp.asarray(sel_np, BF16)
    le = b_up.shape[:2]
    b_up = jnp.transpose(b_up.reshape(le + (f // LANES, LANES, 2)), (0, 1, 2, 4, 3)).reshape(le + (2 * f,))
    e_of = lambda j, m: m[1 + j]
    grid_spec = pltpu.PrefetchScalarGridSpec(
        num_scalar_prefetch=1,
        grid=(n_blocks,),
        in_specs=[
            pl.BlockSpec((EXPERT_BLOCK, d), lambda j, m: (j, 0)),
            pl.BlockSpec((None, None, d, 2 * f), lambda j, m: (layer, e_of(j, m), 0, 0)),
            pl.BlockSpec((None, None, 1, 2 * f), lambda j, m: (layer, e_of(j, m), 0, 0)),
            pl.BlockSpec((None, None, f, d), lambda j, m: (layer, e_of(j, m), 0, 0)),
            pl.BlockSpec((None, None, 1, d), lambda j, m: (layer, e_of(j, m), 0, 0)),
            pl.BlockSpec(sel.shape, lambda j, m: (0, 0)),
        ],
        out_specs=pl.BlockSpec((EXPERT_BLOCK, d), lambda j, m: (j, 0)),
        scratch_shapes=[pltpu.VMEM((d, 2 * f), BF16), pltpu.VMEM((f, d), BF16)],
    )
    return pl.pallas_call(
        _expert_kernel,
        out_shape=jax.ShapeDtypeStruct((n_rows, d), F32),
        grid_spec=grid_spec,
        compiler_params=_cparams(("arbitrary",)),
        name="moe_experts",
    )(meta, xs, w_up, b_up[:, :, None, :], w_down, b_down[:, :, None, :], sel)


def _combine_kernel(cur_ref, nxt_ref, ys_hbm, g4_ref, h_ref, lg_ref, lb_ref, o_ref, ybuf, sem, *, alpha, tm):
    j = pl.program_id(0)
    nblk = pl.num_programs(0)
    slot = j % 2
    n_rows = tm * TOP_K

    def gather(idx_ref, s):
        def body(r, carry):
            pltpu.make_async_copy(ys_hbm.at[pl.ds(idx_ref[0, r], 1), :],
                                  ybuf.at[s, pl.ds(r, 1), :], sem.at[s]).start()
            return carry
        lax.fori_loop(0, n_rows, body, 0, unroll=COMBINE_UNROLL)

    @pl.when(j == 0)
    def _():
        gather(cur_ref, 0)

    @pl.when(j + 1 < nblk)
    def _():
        gather(nxt_ref, 1 - slot)

    pltpu.make_async_copy(ys_hbm.at[pl.ds(0, n_rows), :], ybuf.at[slot], sem.at[slot]).wait()
    g4 = g4_ref[...]
    ffn = jnp.zeros((tm, h_ref.shape[1]), F32)
    for k in range(TOP_K):
        ffn = ffn + ybuf[slot, k * tm:(k + 1) * tm, :] * g4[:, k:k + 1]
    o_ref[...] = _layer_norm(alpha * h_ref[...] + ffn, lg_ref[...], lb_ref[...])


def _combine(ys, pos_km, gates_pad, h1, ln_g, ln_b, alpha, tm=128):
    n, d = h1.shape
    nb = n // tm
    idx3 = pos_km.reshape(nb, 1, tm * TOP_K)
    return pl.pallas_call(
        functools.partial(_combine_kernel, alpha=alpha, tm=tm),
        out_shape=jax.ShapeDtypeStruct((n, d), F32),
        grid=(nb,),
        in_specs=[
            pl.BlockSpec((None, 1, tm * TOP_K), lambda j: (j, 0, 0), memory_space=pltpu.SMEM),
            pl.BlockSpec((None, 1, tm * TOP_K), lambda j: (jnp.minimum(j + 1, nb - 1), 0, 0),
                         memory_space=pltpu.SMEM),
            pl.BlockSpec(memory_space=pl.ANY),
            pl.BlockSpec((tm, LANES), lambda j: (j, 0)),
            pl.BlockSpec((tm, d), lambda j: (j, 0)),
            _const_spec((1, d)), _const_spec((1, d)),
        ],
        out_specs=pl.BlockSpec((tm, d), lambda j: (j, 0)),
        scratch_shapes=[pltpu.VMEM((2, tm * TOP_K, d), F32), pltpu.SemaphoreType.DMA((2,))],
        compiler_params=_cparams(("arbitrary",)),
        name="moe_combine",
    )(idx3, idx3, ys, gates_pad, h1, ln_g.reshape(1, d), ln_b.reshape(1, d))


def _moe_routing(top_e, rank, counts, n_tok, tm_combine):
    n_asg = n_tok * TOP_K
    counts = counts.reshape(N_EXPERTS).astype(I32)
    nblk_per = (counts + EXPERT_BLOCK - 1) // EXPERT_BLOCK
    blk_end = jnp.cumsum(nblk_per)
    pstarts = (blk_end - nblk_per) * EXPERT_BLOCK
    onehot = top_e[..., None] == jnp.arange(N_EXPERTS, dtype=I32)
    dest = jnp.sum(jnp.where(onehot, pstarts, 0), axis=-1) + rank
    n_blocks = -(-n_asg // EXPERT_BLOCK) + N_EXPERTS
    blk_expert = jnp.minimum(jnp.sum(blk_end[None, :] <= jnp.arange(n_blocks, dtype=I32)[:, None], axis=1),
                             N_EXPERTS - 1)
    n_used = blk_end[-1]
    nb = n_tok // tm_combine
    pos_km = jnp.transpose(dest.reshape(nb, tm_combine, TOP_K), (0, 2, 1)).reshape(nb, TOP_K * tm_combine)
    return dest.astype(I32), blk_expert.astype(I32), n_used.astype(I32), pos_km.astype(I32)


def _permute_w_in(w):
    sizes = (N_HEADS * HEAD_DIM, KV_LATENT, IDX_HEADS * IDX_DIM, IDX_DIM, IDX_HEADS,
             CONV_WIDTH, CONV_WIDTH, CONV_WIDTH, SSM_WIDTH, MEM_HEADS * MEM_HEAD_DIM)
    offs = np.cumsum((0,) + sizes)
    q, ckv, qidx, kidx, widx, cu, cgb, cgc, ssm, memq = [w[:, offs[k]:offs[k + 1]] for k in range(len(sizes))]
    gates = w[:, offs[-1]:]
    pad = jnp.zeros((w.shape[0], LANES - IDX_DIM - IDX_HEADS), w.dtype)
    return jnp.concatenate([gates, q, cu, cgb, cgc, ssm, memq, qidx, ckv, kidx, widx, pad], axis=1).astype(BF16)


def kernel(x, mem, ln_in_g, ln_in_b, w_in, kv_norm_g, w_uk, w_uv, conv_w, conv_b, lam_re, lam_im, b_re, b_im, c_re, c_im, d_skip, log_dt, w_glu, b_glu, w_mem_kv, w_branch, w_o, ln1_g, ln1_b, w_router, b_router, w_up, b_up, w_down, b_down, ln2_g, ln2_b):
    bsz, seq, d = x.shape
    depth = w_in.shape[0]
    mlen = mem.shape[1]
    n = bsz * seq
    alpha = float((2 * depth) ** 0.25)
    tm_combine = 256

    h = _ln_in(x.reshape(n, d), ln_in_g, ln_in_b)
    n_row_blocks = -(-n * TOP_K // EXPERT_BLOCK) + N_EXPERTS
    xs = jnp.zeros((n_row_blocks * EXPERT_BLOCK, d), F32)
    mem2 = mem.reshape(bsz * mlen, d)
    for l in range(depth):
        proj = _matmul(h, _permute_w_in(w_in[l]), 256, "in_proj")
        qcat, kcat, ckvn, ckvt = _prep(proj, kv_norm_g[l], bsz, seq)
        att = _dsa_attention(proj, qcat, kcat, ckvn, ckvt, w_uk[l], w_uv[l], kv_norm_g[l], bsz, seq)
        cnv = _short_conv(proj, conv_w[l], conv_b[l], bsz, seq)
        ssm = _s5_mixer(proj, lam_re[l], lam_im[l], b_re[l], b_im[l], c_re[l], c_im[l], d_skip[l], log_dt[l],
                        w_glu[l], b_glu[l], bsz, seq)
        kvm = _matmul(mem2, w_mem_kv[l].astype(BF16), min(256, bsz * mlen), "mem_kv")
        mem_o = _memory_attention(proj, kvm, bsz, seq, mlen)
        h1, te_pad, tg_pad, counts = _merge_router(att, cnv, ssm, mem_o, proj, h, w_branch[l], w_o[l],
                                                   ln1_g[l], ln1_b[l], w_router[l], b_router[l], alpha)
        dest, blk_expert, n_used, pos_km = _moe_routing(te_pad[:, :TOP_K], te_pad[:, TOP_K:2 * TOP_K], counts,
                                                        n, tm_combine)
        xs = _dispatch(h1, dest, xs)
        ys = _expert_ffn(xs, blk_expert, n_used, l, w_up, b_up, w_down, b_down)
        h = _combine(ys, pos_km, tg_pad, h1, ln2_g[l], ln2_b[l], alpha, tm_combine)
    return h.reshape(bsz, seq, d)
```

```python
import functools
import math

import numpy as np
import jax
import jax.numpy as jnp
from jax import lax
from jax.experimental import pallas as pl
from jax.experimental.pallas import tpu as pltpu

F32 = jnp.float32
BF16 = jnp.bfloat16
I32 = jnp.int32
I16 = jnp.int16

N_HEADS = 8
HEAD_DIM = 64
KV_LATENT = 128
IDX_HEADS = 8
IDX_DIM = 32
INDEX_TOPK = 256
DSA_QUERY_BLOCK = 256
CONV_WIDTH = 512
SSM_WIDTH = 512
SSM_GROUP = 16
SSM_GROUPS = SSM_WIDTH // SSM_GROUP
SSM_STATE = 64
MEM_HEADS = 4
MEM_HEAD_DIM = 128
N_BRANCH = 4
BRANCH_WIDTH = 512
N_EXPERTS = 32
TOP_K = 4
D_EXPERT = 1024
SWIGLU_LIMIT = 7.0
SWIGLU_ALPHA = 1.702
EXPERT_BLOCK = 256
LN_EPS = 1e-5

LANES = 128
SUBLANES = 8
VMEM_LIMIT_BYTES = 56 * 1024 * 1024

INT_MIN = -(2 ** 31)
NEG_BIG = -1e30
SAFE_LOGIT = 100.0

COL_GATES = 0
COL_Q = 4096
COL_CONV_U = 4608
COL_CONV_GB = 5120
COL_CONV_GC = 5632
COL_SSM = 6144
COL_MEMQ = 6656
COL_QIDX = 7168
COL_CKV = 7424
COL_TAIL = 7552
D_PROJ = 7680
TAIL_W_OFF = IDX_DIM

SSM_SEQS = 8


def _cparams(sem):
    return pltpu.CompilerParams(dimension_semantics=sem, vmem_limit_bytes=VMEM_LIMIT_BYTES)


def _layer_norm(x, g, b):
    mu = jnp.mean(x, axis=-1, keepdims=True)
    xc = x - mu
    var = jnp.mean(xc * xc, axis=-1, keepdims=True)
    return xc * lax.rsqrt(var + LN_EPS) * g + b


def _const_spec(shape):
    nd = len(shape)
    return pl.BlockSpec(shape, lambda *_: (0,) * nd)


def _ln_kernel(x_ref, g_ref, b_ref, o_ref):
    o_ref[...] = _layer_norm(x_ref[...], g_ref[...], b_ref[...])


def _ln_in(x2, g, b, tm=512):
    n, d = x2.shape
    return pl.pallas_call(
        _ln_kernel,
        out_shape=jax.ShapeDtypeStruct((n, d), F32),
        grid=(n // tm,),
        in_specs=[pl.BlockSpec((tm, d), lambda i: (i, 0)), _const_spec((1, d)), _const_spec((1, d))],
        out_specs=pl.BlockSpec((tm, d), lambda i: (i, 0)),
        compiler_params=_cparams(("parallel",)),
        name="ln_in",
    )(x2, g.reshape(1, d), b.reshape(1, d))


def _matmul_kernel(x_ref, w_ref, o_ref):
    o_ref[...] = jnp.dot(x_ref[...].astype(BF16), w_ref[...], preferred_element_type=F32)


def _matmul(x, w_bf16, tm, name):
    n, k = x.shape
    m = w_bf16.shape[1]
    return pl.pallas_call(
        _matmul_kernel,
        out_shape=jax.ShapeDtypeStruct((n, m), F32),
        grid=(n // tm,),
        in_specs=[pl.BlockSpec((tm, k), lambda i: (i, 0)),
                  pl.BlockSpec((k, m), lambda i: (0, 0), pipeline_mode=pl.Buffered(1))],
        out_specs=pl.BlockSpec((tm, m), lambda i: (i, 0)),
        compiler_params=_cparams(("parallel",)),
        name=name,
    )(x, w_bf16)


def _split_hi_lo(x):
    hi = x.astype(BF16)
    lo = (x - hi.astype(F32)).astype(BF16)
    return hi, lo


def _prep_kernel(qidx_ref, ckv_ref, tail_ref, g_ref, sq_ref, sk_ref, qcat_ref, kcat_ref, ckvn_ref, ckvt_ref):
    q_hi, q_lo = _split_hi_lo(qidx_ref[...])
    qcat = jnp.dot(jnp.concatenate([q_hi, q_lo], axis=1), sq_ref[...], preferred_element_type=F32)
    qcat_ref[...] = qcat.astype(BF16)
    k_hi, k_lo = _split_hi_lo(tail_ref[...])
    kcat = jnp.dot(jnp.concatenate([k_hi, k_lo], axis=1), sk_ref[...], preferred_element_type=F32)
    kcat_ref[...] = kcat.astype(BF16)
    c = ckv_ref[...]
    ms = jnp.mean(c * c, axis=-1, keepdims=True)
    cn = c * lax.rsqrt(ms + LN_EPS) * g_ref[...]
    ckvn_ref[...] = cn.astype(BF16)
    ckvt_ref[...] = cn.T.astype(BF16)


def _selection_matrices():
    sq = np.zeros((2 * IDX_HEADS * IDX_DIM, IDX_HEADS * LANES), np.float32)
    for h in range(IDX_HEADS):
        for d in range(IDX_DIM):
            hi_in = h * IDX_DIM + d
            lo_in = IDX_HEADS * IDX_DIM + hi_in
            sq[hi_in, h * LANES + d] = 1.0
            sq[hi_in, h * LANES + IDX_DIM + d] = 1.0
            sq[lo_in, h * LANES + 2 * IDX_DIM + d] = 1.0
    sk = np.zeros((2 * LANES, LANES), np.float32)
    for d in range(IDX_DIM):
        sk[d, d] = 1.0
        sk[LANES + d, IDX_DIM + d] = 1.0
        sk[d, 2 * IDX_DIM + d] = 1.0
    return jnp.asarray(sq, BF16), jnp.asarray(sk, BF16)


def _prep(proj, kv_norm_g, bsz, seq, tm=512):
    n = proj.shape[0]
    tm = min(tm, seq)
    nbt = seq // tm
    sq, sk = _selection_matrices()
    return pl.pallas_call(
        _prep_kernel,
        out_shape=(jax.ShapeDtypeStruct((n, IDX_HEADS * LANES), BF16),
                   jax.ShapeDtypeStruct((n, LANES), BF16),
                   jax.ShapeDtypeStruct((n, KV_LATENT), BF16),
                   jax.ShapeDtypeStruct((bsz, KV_LATENT, seq), BF16)),
        grid=(n // tm,),
        in_specs=[pl.BlockSpec((tm, 256), lambda i: (i, COL_QIDX // 256)),
                  pl.BlockSpec((tm, 128), lambda i: (i, COL_CKV // 128)),
                  pl.BlockSpec((tm, 128), lambda i: (i, COL_TAIL // 128)),
                  _const_spec((1, KV_LATENT)), _const_spec(sq.shape), _const_spec(sk.shape)],
        out_specs=(pl.BlockSpec((tm, IDX_HEADS * LANES), lambda i: (i, 0)),
                   pl.BlockSpec((tm, LANES), lambda i: (i, 0)),
                   pl.BlockSpec((tm, KV_LATENT), lambda i: (i, 0)),
                   pl.BlockSpec((None, KV_LATENT, tm), lambda i: (i // nbt, 0, i % nbt))),
        compiler_params=_cparams(("parallel",)),
        name="dsa_prep",
    )(proj, proj, proj, kv_norm_g.reshape(1, KV_LATENT), sq, sk)


def _dsa_kernel(q_ref, tail_ref, qcat_ref, kcat_ref, ckv_ref, ckvt_ref, wuk_ref, wuv_ref, kbound_ref, o_ref,
                key_ref, khi_ref, klo_ref, qct_ref, qlat_ref, m_ref, l_ref, acc_ref, *, seq, topk, cks, ck, cka, qb):
    i = pl.program_id(1)
    q0 = i * qb
    nchs = (q0 + qb + cks - 1) // cks
    nch = (q0 + qb + ck - 1) // ck
    q_pos = q0 + lax.broadcasted_iota(I32, (1, qb), 1)
    row_pos = lax.broadcasted_iota(I32, (ck, 1), 0)
    w_t = tail_ref[...].T
    cnt_rows = min(64, ck)

    for h in range(IDX_HEADS):
        qct_ref[h] = qcat_ref[:, h * LANES:(h + 1) * LANES].astype(F32).T.astype(BF16)
    row_pos_t = lax.broadcasted_iota(I32, (cka, 1), 0)

    def score_chunk(c, carry):
        for sub in range(cks // cka):
            k0 = pl.multiple_of(c * cks + sub * cka, cka)
            kc = kcat_ref[pl.ds(k0, cka), :]
            s = jnp.zeros((cka, qb), F32)
            for h in range(IDX_HEADS):
                d = jnp.dot(kc, qct_ref[h], preferred_element_type=F32)
                s = s + jnp.maximum(d, 0.0) * w_t[TAIL_W_OFF + h:TAIL_W_OFF + h + 1, :]
            s = s + 0.0
            bits = pltpu.bitcast(s, I32)
            key = bits ^ ((bits >> 31) & jnp.int32(0x7FFFFFFF))
            key = jnp.where(k0 + row_pos_t <= q_pos, key, jnp.int32(INT_MIN))
            key_ref[pl.ds(k0, cka), :] = key
            khi_ref[pl.ds(k0, cka), :] = (key >> 16).astype(I16)
            klo_ref[pl.ds(k0, cka), :] = ((key & jnp.int32(0xFFFF)) - 32768).astype(I16)
        return carry

    lax.fori_loop(0, nchs, score_chunk, 0)

    def count_keys(pred_fn):
        def body(c, acc):
            k0 = pl.multiple_of(c * ck, ck)
            hit = jnp.where(pred_fn(key_ref[pl.ds(k0, ck), :], k0), 1.0, 0.0)
            return acc + jnp.sum(hit.reshape(ck // cnt_rows, cnt_rows, qb), axis=0)
        acc = lax.fori_loop(0, nch, body, jnp.zeros((cnt_rows, qb), F32))
        return jnp.sum(acc, axis=0, keepdims=True)

    kf = float(topk)
    pack = 16
    n_part = ck // (2 * pack)
    lo16, hi16 = -32768, 32767

    def count16(ref, cand16):
        cand = jnp.broadcast_to(cand16, (pack, qb)).astype(I16)
        one = jnp.ones((), I16)
        zero = jnp.zeros((), I16)

        def body(c, acc):
            k0 = pl.multiple_of(c * ck, ck)
            hit = jnp.where(ref[pl.ds(k0, ck), :].reshape(ck // pack, pack, qb) >= cand[None], one, zero)
            part = hit[0:2]
            for j in range(1, n_part):
                part = part + hit[2 * j:2 * j + 2]
            return acc + part
        acc = lax.fori_loop(0, nch, body, jnp.zeros((2, pack, qb), I16)).astype(I32).astype(F32)
        return jnp.sum(jnp.sum(acc, axis=0), axis=0, keepdims=True)

    def bisect16(ref, target):
        def one_pass(b, carry):
            w, c_ge_w, c_gt_w = carry
            cand = w + jnp.left_shift(jnp.int32(1), 15 - b)
            cnt = count16(ref, cand)
            ok = cnt >= target
            return jnp.where(ok, cand, w), jnp.where(ok, cnt, c_ge_w), jnp.where(ok, c_gt_w, cnt)
        total = jnp.full((1, qb), 1.0, F32) * (nch * ck).astype(F32)
        return lax.fori_loop(0, 16, one_pass, (jnp.full((1, qb), lo16, I32), total, jnp.zeros((1, qb), F32)))

    v16, c_v16, c_above = bisect16(khi_ref, kf)
    v16_b = jnp.broadcast_to(v16, (pack, qb)).astype(I16)

    def build_lo(c, carry):
        k0 = pl.multiple_of(c * ck, ck)
        hi = khi_ref[pl.ds(k0, ck), :].reshape(ck // pack, pack, qb)
        lo = klo_ref[pl.ds(k0, ck), :].reshape(ck // pack, pack, qb)
        khi_ref[pl.ds(k0, ck), :] = jnp.where(hi == v16_b[None], lo, jnp.full((), lo16, I16)).reshape(ck, qb)
        return carry

    lax.fori_loop(0, nch, build_lo, 0)
    w, c_w, c_w_above = bisect16(khi_ref, kf - c_above)
    c_w = jnp.where(w == lo16, c_v16 - c_above, c_w)
    v = (v16 << 16) + (w + 32768)
    select_all = v == jnp.int32(INT_MIN)
    thr = jnp.maximum(v, jnp.int32(INT_MIN + 1))
    c_ge = jnp.where(select_all, c_above, c_above + c_w)
    c_gt = jnp.where(select_all, c_above, c_above + c_w_above)
    need = kf - c_gt

    jlim_ref = m_ref.at[0:1]
    jlim_ref[...] = jnp.full((1, qb), float(seq), F32)
    has_tie = jnp.max(jnp.where(c_ge > kf, 1.0, 0.0)) > 0.0

    @pl.when(has_tie)
    def _():
        n_bits = max(1, int(math.ceil(math.log2(seq))))

        def pos_pass(b, j):
            cand = j + jnp.left_shift(jnp.int32(1), n_bits - 1 - b)
            cnt = count_keys(lambda key, k0: (key == thr) & (k0 + row_pos < cand))
            return jnp.where(cnt < need, cand, j)

        j = lax.fori_loop(0, n_bits, pos_pass, jnp.zeros((1, qb), I32))
        jlim_ref[...] = jnp.where(c_ge > kf, j.astype(F32), float(seq))

    jlim = jlim_ref[...].astype(I32)

    q_bf = q_ref[...].astype(BF16)
    qn2_max = jnp.zeros((1, qb), F32)
    for j in range(N_HEADS // 2):
        ql = jnp.dot(q_bf[:, j * LANES:(j + 1) * LANES], wuk_ref[j],
                     preferred_element_type=F32) * (HEAD_DIM ** -0.5 * math.log2(math.e))
        for half in range(2):
            qt = ql[:, half * KV_LATENT:(half + 1) * KV_LATENT].T.astype(BF16)
            qlat_ref[2 * j + half] = qt
            qf = qt.astype(F32)
            qn2_max = jnp.maximum(qn2_max, jnp.sum(qf * qf, axis=0, keepdims=True))
    logit_bound = jnp.sqrt(jnp.max(qn2_max)) * kbound_ref[0, 0]
    row_pos_a = lax.broadcasted_iota(I32, (cka, 1), 0)

    def att_tiles(base, n_sub, online):
        for sub in range(n_sub):
            k0 = pl.multiple_of(base + sub * cka, cka)
            kv = ckv_ref[pl.ds(k0, cka), :]
            kvt = ckvt_ref[:, pl.ds(k0, cka)]
            key = key_ref[pl.ds(k0, cka), :]
            sel = (key > thr) | ((key == thr) & (k0 + row_pos_a <= jlim))
            bias = jnp.where(sel, 0.0, NEG_BIG)
            for h in range(N_HEADS):
                lg = jnp.dot(kv, qlat_ref[h], preferred_element_type=F32) + bias
                if not online:
                    p = jnp.exp2(lg)
                    l_ref[h:h + 1, :] = l_ref[h:h + 1, :] + jnp.sum(p, axis=0, keepdims=True)
                    acc_ref[h] = acc_ref[h] + jnp.dot(kvt, p.astype(BF16), preferred_element_type=F32)
                else:
                    m_prev = m_ref[h:h + 1, :]
                    m_new = jnp.maximum(m_prev, jnp.max(lg, axis=0, keepdims=True))
                    p = jnp.exp2(lg - m_new)
                    alpha = jnp.exp2(m_prev - m_new)
                    l_ref[h:h + 1, :] = alpha * l_ref[h:h + 1, :] + jnp.sum(p, axis=0, keepdims=True)
                    acc_ref[h] = alpha * acc_ref[h] + jnp.dot(kvt, p.astype(BF16), preferred_element_type=F32)
                    m_ref[h:h + 1, :] = m_new

    def attend(online):
        def att_chunk(c, carry):
            att_tiles(c * cks, cks // cka, online)
            return carry

        end = q0 + qb
        n_full = end // cks
        lax.fori_loop(0, n_full, att_chunk, 0)
        if cks % qb == 0 and cks > qb:
            for parts in range(1, cks // qb):
                @pl.when(end - n_full * cks == parts * qb)
                def _():
                    att_tiles(n_full * cks, parts * qb // cka, online)
        else:
            @pl.when(end - n_full * cks > 0)
            def _():
                att_tiles(n_full * cks, cks // cka, online)

    m_ref[...] = jnp.full(m_ref.shape, NEG_BIG, F32)
    l_ref[...] = jnp.zeros(l_ref.shape, F32)
    acc_ref[...] = jnp.zeros(acc_ref.shape, F32)

    @pl.when(logit_bound < SAFE_LOGIT)
    def _():
        attend(online=False)

    @pl.when(jnp.logical_not(logit_bound < SAFE_LOGIT))
    def _():
        attend(online=True)

    outs = []
    for j in range(N_HEADS // 2):
        o_pair = [(acc_ref[h] / l_ref[h:h + 1, :]).T.astype(BF16) for h in (2 * j, 2 * j + 1)]
        outs.append(jnp.dot(jnp.concatenate(o_pair, axis=1), wuv_ref[j], preferred_element_type=F32))
    o_ref[...] = jnp.concatenate(outs, axis=1)


def _dsa_attention(proj, qcat, kcat, ckvn, ckvt, w_uk, w_uv, kv_norm_g, bsz, seq):
    n = proj.shape[0]
    qb = min(DSA_QUERY_BLOCK, seq)
    n_blk = seq // qb
    topk = min(INDEX_TOPK, seq // 4)
    cks = min(512, seq)
    ck = min(512, seq)
    cka = min(128, seq)
    zk = jnp.zeros((HEAD_DIM, KV_LATENT), F32)
    zv = jnp.zeros((KV_LATENT, HEAD_DIM), F32)
    wuk = jnp.stack([jnp.block([[w_uk[:, 2 * j, :].T, zk], [zk, w_uk[:, 2 * j + 1, :].T]])
                     for j in range(N_HEADS // 2)]).astype(BF16)
    wuv = jnp.stack([jnp.block([[w_uv[:, 2 * j, :], zv], [zv, w_uv[:, 2 * j + 1, :]]])
                     for j in range(N_HEADS // 2)]).astype(BF16)
    kbound = (1.01 * math.sqrt(KV_LATENT) * jnp.max(jnp.abs(kv_norm_g))).reshape(1, 1).astype(F32)
    kernel = functools.partial(_dsa_kernel, seq=seq, topk=topk, cks=cks, ck=ck, cka=cka, qb=qb)
    return pl.pallas_call(
        kernel,
        out_shape=jax.ShapeDtypeStruct((n, N_HEADS * HEAD_DIM), F32),
        grid=(bsz, n_blk),
        in_specs=[pl.BlockSpec((qb, 512), lambda b, i: (b * n_blk + i, COL_Q // 512)),
                  pl.BlockSpec((qb, 128), lambda b, i: (b * n_blk + i, COL_TAIL // 128)),
                  pl.BlockSpec((qb, IDX_HEADS * LANES), lambda b, i: (b * n_blk + i, 0)),
                  pl.BlockSpec((seq, LANES), lambda b, i: (b, 0)),
                  pl.BlockSpec((seq, KV_LATENT), lambda b, i: (b, 0)),
                  pl.BlockSpec((None, KV_LATENT, seq), lambda b, i: (b, 0, 0)),
                  _const_spec(wuk.shape), _const_spec(wuv.shape),
                  pl.BlockSpec(memory_space=pltpu.SMEM)],
        out_specs=pl.BlockSpec((qb, N_HEADS * HEAD_DIM), lambda b, i: (b * n_blk + i, 0)),
        scratch_shapes=[pltpu.VMEM((seq, qb), I32),
                        pltpu.VMEM((seq, qb), I16),
                        pltpu.VMEM((seq, qb), I16),
                        pltpu.VMEM((IDX_HEADS, LANES, qb), BF16),
                        pltpu.VMEM((N_HEADS, KV_LATENT, qb), BF16),
                        pltpu.VMEM((N_HEADS, qb), F32),
                        pltpu.VMEM((N_HEADS, qb), F32),
                        pltpu.VMEM((N_HEADS, KV_LATENT, qb), F32)],
        compiler_params=_cparams(("parallel", "parallel")),
        name="dsa_attention",
    )(proj, proj, qcat, kcat, ckvn, ckvt, wuk, wuv, kbound)


def _conv_kernel(u_ref, gb_ref, gc_ref, pu_ref, pgc_ref, w_ref, b_ref, o_ref):
    i = pl.program_id(1)
    tb = u_ref.shape[0]
    v = gc_ref[...] * u_ref[...]
    halo = jnp.where(i > 0, pgc_ref[...] * pu_ref[...], 0.0)
    vfull = jnp.concatenate([halo, v], axis=0)
    v1 = pltpu.roll(vfull, 1, 0)[SUBLANES:SUBLANES + tb]
    v2 = pltpu.roll(vfull, 2, 0)[SUBLANES:SUBLANES + tb]
    w = w_ref[...]
    y = w[0:1] * v2 + w[1:2] * v1 + w[2:3] * v
    o_ref[...] = gb_ref[...] * (y + b_ref[...])


def _short_conv(proj, conv_w, conv_b, bsz, seq, tb=512):
    n = proj.shape[0]
    tb = min(tb, seq)
    nb = seq // tb
    cw = CONV_WIDTH
    blk = lambda col: pl.BlockSpec((tb, cw), lambda b, i: (b * nb + i, col // cw))
    prev = lambda col: pl.BlockSpec(
        (SUBLANES, cw), lambda b, i: (jnp.maximum((b * nb + i) * (tb // SUBLANES) - 1, 0), col // cw))
    return pl.pallas_call(
        _conv_kernel,
        out_shape=jax.ShapeDtypeStruct((n, cw), F32),
        grid=(bsz, nb),
        in_specs=[blk(COL_CONV_U), blk(COL_CONV_GB), blk(COL_CONV_GC), prev(COL_CONV_U), prev(COL_CONV_GC),
                  _const_spec((SUBLANES, cw)), _const_spec((1, cw))],
        out_specs=pl.BlockSpec((tb, cw), lambda b, i: (b * nb + i, 0)),
        compiler_params=_cparams(("parallel", "parallel")),
        name="short_conv",
    )(proj, proj, proj, proj, proj,
      jnp.pad(conv_w, ((0, SUBLANES - conv_w.shape[0]), (0, 0))), conv_b.reshape(1, cw))


def _s5_scan_chunk(bu_ref, a_ref, st_ref, n_steps, store):
    s_tot = a_ref.shape[1] // 2
    cb = 512
    for blk in range(s_tot // cb):
        re_sl = pl.ds(blk * cb, cb)
        im_sl = pl.ds(s_tot + blk * cb, cb)
        ar = jnp.broadcast_to(a_ref[0:1, blk * cb:(blk + 1) * cb], (SSM_SEQS, cb))
        ai = jnp.broadcast_to(a_ref[0:1, s_tot + blk * cb:s_tot + (blk + 1) * cb], (SSM_SEQS, cb))

        def step(t, carry):
            re, im = carry
            r0 = pl.multiple_of(t * SSM_SEQS, SSM_SEQS)
            br = bu_ref[pl.ds(r0, SSM_SEQS), re_sl]
            bi = bu_ref[pl.ds(r0, SSM_SEQS), im_sl]
            nre = ar * re - ai * im + br
            nim = ar * im + ai * re + bi
            if store:
                bu_ref[pl.ds(r0, SSM_SEQS), re_sl] = nre
                bu_ref[pl.ds(r0, SSM_SEQS), im_sl] = nim
            return nre, nim

        re, im = lax.fori_loop(0, n_steps, step, (st_ref[:, re_sl], st_ref[:, im_sl]), unroll=8)
        st_ref[:, re_sl] = re
        st_ref[:, im_sl] = im


def _gelu_tanh(x):
    return 0.5 * x * (1.0 + jnp.tanh(math.sqrt(2.0 / math.pi) * (x + 0.044715 * (x * x * x))))


S5_ROWS = 512


def _s5_kernel(u_ref, perm_ref, wb_ref, a_ref, apow_ref, wc_ref, d_ref, wg_ref, bg_ref, o_ref,
               bu_ref, st_ref, carry_ref, *, seg_len, steps_per_seq):
    c = pl.program_id(0)
    s_tot = a_ref.shape[1] // 2
    tn_dims = (((0,), (0,)), ((), ()))

    @pl.when(c % steps_per_seq == 0)
    def _():
        carry_ref[...] = jnp.zeros(carry_ref.shape, F32)

    u = u_ref[...]
    perm = perm_ref[...]
    u_il = jnp.dot(perm, u.astype(BF16), preferred_element_type=F32).astype(BF16)
    hw = u_il.shape[1] // 2
    hs = s_tot // 2
    for part in range(2):
        for q in range(2):
            c0 = part * s_tot + q * hs
            bu_ref[:, c0:c0 + hs] = jnp.dot(u_il[:, q * hw:(q + 1) * hw], wb_ref[q * hw:(q + 1) * hw, c0:c0 + hs],
                                            preferred_element_type=F32)

    st_ref[...] = jnp.zeros(st_ref.shape, F32)
    _s5_scan_chunk(bu_ref, a_ref, st_ref, seg_len, store=False)

    pr = apow_ref[0:1, :s_tot]
    pi = apow_ref[0:1, s_tot:]
    hre = carry_ref[0:1, :s_tot]
    him = carry_ref[0:1, s_tot:]
    for j in range(SSM_SEQS):
        ere = st_ref[j:j + 1, :s_tot]
        eim = st_ref[j:j + 1, s_tot:]
        st_ref[j:j + 1, :s_tot] = hre
        st_ref[j:j + 1, s_tot:] = him
        hre, him = pr * hre - pi * him + ere, pr * him + pi * hre + eim
    carry_ref[0:1, :s_tot] = hre
    carry_ref[0:1, s_tot:] = him

    _s5_scan_chunk(bu_ref, a_ref, st_ref, seg_len, store=True)
    halves = []
    for q in range(2):
        acc = None
        for part in range(2):
            c0 = part * s_tot + q * hs
            term = jnp.dot(bu_ref[:, c0:c0 + hs].astype(BF16), wc_ref[c0:c0 + hs, q * hw:(q + 1) * hw],
                           preferred_element_type=F32)
            acc = term if acc is None else acc + term
        halves.append(acc)
    ch = jnp.concatenate(halves, axis=1)
    ch_hi, ch_lo = _split_hi_lo(ch)
    ch_nat = (lax.dot_general(perm, ch_hi, tn_dims, preferred_element_type=F32)
              + lax.dot_general(perm, ch_lo, tn_dims, preferred_element_type=F32))
    z = _gelu_tanh(ch_nat + d_ref[...] * u)
    gate = jnp.dot(z.astype(BF16), wg_ref[...], preferred_element_type=F32) + bg_ref[...]
    o_ref[...] = z * jax.nn.sigmoid(gate)


def _s5_mixer(proj, lam_re, lam_im, b_re, b_im, c_re, c_im, d_skip, log_dt, w_glu, b_glu, bsz, seq):
    n = proj.shape[0]
    g, p, nn = SSM_GROUPS, SSM_STATE, SSM_GROUP
    s_tot = g * p
    rows = min(S5_ROWS, seq)
    seg_len = rows // SSM_SEQS

    dt = jnp.exp(log_dt.astype(F32))[:, None]
    lam = lax.complex(lam_re.astype(F32), lam_im.astype(F32))
    a_bar = jnp.exp(lam * dt)
    b_bar = ((a_bar - 1.0) / lam)[:, :, None] * lax.complex(b_re.astype(F32), b_im.astype(F32))
    a_pow = jnp.exp(lam * dt * seg_len)
    a_vec = jnp.concatenate([jnp.real(a_bar).reshape(1, s_tot), jnp.imag(a_bar).reshape(1, s_tot)], axis=1)
    apow_vec = jnp.concatenate([jnp.real(a_pow).reshape(1, s_tot), jnp.imag(a_pow).reshape(1, s_tot)], axis=1)
    eye = jnp.eye(g, dtype=F32)
    wb = jnp.concatenate(
        [jnp.einsum('gpn,gh->gnhp', jnp.real(b_bar), eye).reshape(g * nn, s_tot),
         jnp.einsum('gpn,gh->gnhp', jnp.imag(b_bar), eye).reshape(g * nn, s_tot)], axis=1).astype(BF16)
    wc = jnp.concatenate(
        [jnp.einsum('gnp,gh->gphn', c_re.astype(F32), eye).reshape(s_tot, g * nn),
         -jnp.einsum('gnp,gh->gphn', c_im.astype(F32), eye).reshape(s_tot, g * nn)], axis=0).astype(BF16)
    perm_np = np.zeros((rows, rows), np.float32)
    t_idx, j_idx = np.meshgrid(np.arange(seg_len), np.arange(SSM_SEQS), indexing='ij')
    perm_np[(t_idx * SSM_SEQS + j_idx).ravel(), (j_idx * seg_len + t_idx).ravel()] = 1.0
    perm = jnp.asarray(perm_np, BF16)

    return pl.pallas_call(
        functools.partial(_s5_kernel, seg_len=seg_len, steps_per_seq=seq // rows),
        out_shape=jax.ShapeDtypeStruct((n, SSM_WIDTH), F32),
        grid=(n // rows,),
        in_specs=[pl.BlockSpec((rows, SSM_WIDTH), lambda c: (c, COL_SSM // SSM_WIDTH)),
                  _const_spec(perm.shape), _const_spec(wb.shape), _const_spec(a_vec.shape),
                  _const_spec(apow_vec.shape), _const_spec(wc.shape), _const_spec((1, SSM_WIDTH)),
                  _const_spec((SSM_WIDTH, SSM_WIDTH)), _const_spec((1, SSM_WIDTH))],
        out_specs=pl.BlockSpec((rows, SSM_WIDTH), lambda c: (c, 0)),
        scratch_shapes=[pltpu.VMEM((rows, 2 * s_tot), F32), pltpu.VMEM((SSM_SEQS, 2 * s_tot), F32),
                        pltpu.VMEM((1, 2 * s_tot), F32)],
        compiler_params=_cparams(("arbitrary",)),
        name="s5_mixer",
    )(proj, perm, wb, a_vec, apow_vec, wc, d_skip.reshape(1, SSM_WIDTH).astype(F32),
      w_glu.astype(BF16), b_glu.reshape(1, SSM_WIDTH))


def _mem_attn_kernel(q_ref, kv_ref, o_ref):
    hw = MEM_HEADS * MEM_HEAD_DIM
    q = q_ref[...].astype(BF16)
    kv = kv_ref[...].astype(BF16)
    for h in range(MEM_HEADS):
        sl = slice(h * MEM_HEAD_DIM, (h + 1) * MEM_HEAD_DIM)
        k = kv[:, sl]
        v = kv[:, hw + h * MEM_HEAD_DIM:hw + (h + 1) * MEM_HEAD_DIM]
        lg = lax.dot_general(q[:, sl], k, (((1,), (1,)), ((), ())),
                             preferred_element_type=F32) * (MEM_HEAD_DIM ** -0.5)
        m = jnp.max(lg, axis=1, keepdims=True)
        p = jnp.exp(lg - m)
        p = p / jnp.sum(p, axis=1, keepdims=True)
        o_ref[:, sl] = jnp.dot(p.astype(BF16), v, preferred_element_type=F32)


def _memory_attention(proj, kvm, bsz, seq, mlen, tm=512):
    n = proj.shape[0]
    tm = min(tm, seq)
    nb = seq // tm
    hw = MEM_HEADS * MEM_HEAD_DIM
    return pl.pallas_call(
        _mem_attn_kernel,
        out_shape=jax.ShapeDtypeStruct((n, hw), F32),
        grid=(bsz, nb),
        in_specs=[pl.BlockSpec((tm, hw), lambda b, i: (b * nb + i, COL_MEMQ // hw)),
                  pl.BlockSpec((mlen, 2 * hw), lambda b, i: (b, 0))],
        out_specs=pl.BlockSpec((tm, hw), lambda b, i: (b * nb + i, 0)),
        compiler_params=_cparams(("parallel", "parallel")),
        name="memory_attention",
    )(proj, kvm)


def _merge_kernel(att_ref, cnv_ref, ssm_ref, mem_ref, gl_ref, h_ref, wbr_ref, wo_ref, g_ref, b_ref,
                  wr_ref, br_ref, ltri_ref, h1_ref, te_ref, tg_ref, cnt_ref, base_ref, *, alpha):
    d = h_ref.shape[1]

    @pl.when(pl.program_id(0) == 0)
    def _():
        base_ref[...] = jnp.zeros(base_ref.shape, F32)

    merged = jnp.zeros(h_ref.shape, F32)
    for r, br in enumerate((att_ref, cnv_ref, ssm_ref, mem_ref)):
        y = jnp.dot(br[...].astype(BF16), wbr_ref[r], preferred_element_type=F32)
        merged = merged + y * jax.nn.sigmoid(gl_ref[:, r * d:(r + 1) * d])
    y = alpha * h_ref[...] + jnp.dot(merged.astype(BF16), wo_ref[...], preferred_element_type=F32)
    h1 = _layer_norm(y, g_ref[...], b_ref[...])
    h1_ref[...] = h1

    logits = jnp.dot(h1, wr_ref[...], preferred_element_type=F32, precision=lax.Precision.HIGHEST) + br_ref[...]
    tm = logits.shape[0]
    lane = lax.broadcasted_iota(I32, (tm, N_EXPERTS), 1)
    out_lane = lax.broadcasted_iota(I32, (tm, LANES), 1)
    work = logits
    top_e = jnp.zeros((tm, LANES), I32)
    top_v = jnp.zeros((tm, LANES), F32)
    vals = []
    onehots = []
    for k in range(TOP_K):
        mx = jnp.max(work, axis=1, keepdims=True)
        idx = jnp.min(jnp.where(work == mx, lane, N_EXPERTS), axis=1, keepdims=True)
        hit = lane == idx
        work = jnp.where(hit, -jnp.inf, work)
        top_e = jnp.where(out_lane == k, idx, top_e)
        vals.append(mx)
        onehots.append(jnp.where(hit, 1.0, 0.0))
    den = sum(jnp.exp(vk - vals[0]) for vk in vals)
    for k in range(TOP_K):
        top_v = jnp.where(out_lane == k, jnp.exp(vals[k] - vals[0]) / den, top_v)

    per_tok = onehots[0] + onehots[1] + onehots[2] + onehots[3]
    before = jnp.dot(ltri_ref[...], per_tok.astype(BF16), preferred_element_type=F32) + base_ref[...]
    for k in range(TOP_K):
        rank = jnp.sum(onehots[k] * before, axis=1, keepdims=True)
        top_e = jnp.where(out_lane == TOP_K + k, rank.astype(I32), top_e)
    base_ref[...] = base_ref[...] + jnp.sum(per_tok, axis=0, keepdims=True)
    cnt_ref[...] = base_ref[...]
    te_ref[...] = top_e
    tg_ref[...] = top_v


def _merge_router(att, cnv, ssm, mem_o, proj, h, w_branch, w_o, ln_g, ln_b, w_router, b_router, alpha, tm=512):
    n, d = h.shape
    bw = BRANCH_WIDTH
    row = lambda w: pl.BlockSpec((tm, w), lambda i: (i, 0))
    ltri = jnp.asarray(np.tril(np.ones((tm, tm), np.float32), -1), BF16)
    return pl.pallas_call(
        functools.partial(_merge_kernel, alpha=alpha),
        out_shape=(jax.ShapeDtypeStruct((n, d), F32),
                   jax.ShapeDtypeStruct((n, LANES), I32),
                   jax.ShapeDtypeStruct((n, LANES), F32),
                   jax.ShapeDtypeStruct((1, N_EXPERTS), F32)),
        grid=(n // tm,),
        in_specs=[row(bw), row(bw), row(bw), row(bw),
                  pl.BlockSpec((tm, N_BRANCH * d), lambda i: (i, COL_GATES // (N_BRANCH * d))),
                  row(d),
                  _const_spec((N_BRANCH, bw, d)), _const_spec((d, d)), _const_spec((1, d)), _const_spec((1, d)),
                  _const_spec((d, N_EXPERTS)), _const_spec((1, N_EXPERTS)), _const_spec((tm, tm))],
        out_specs=(row(d), row(LANES), row(LANES), _const_spec((1, N_EXPERTS))),
        scratch_shapes=[pltpu.VMEM((1, N_EXPERTS), F32)],
        compiler_params=_cparams(("arbitrary",)),
        name="merge_router",
    )(att, cnv, ssm, mem_o, proj, h, w_branch.astype(BF16), w_o.astype(BF16),
      ln_g.reshape(1, d), ln_b.reshape(1, d), w_router, b_router.reshape(1, N_EXPERTS), ltri)


DISPATCH_UNROLL = 8
COMBINE_UNROLL = 16


def _dispatch_kernel(dest_ref, h_ref, xs_in_hbm, xs_hbm, sem, *, tm):
    del xs_in_hbm

    def body(r, carry):
        for k in range(TOP_K):
            pltpu.make_async_copy(h_ref.at[pl.ds(r, 1), :],
                                  xs_hbm.at[pl.ds(dest_ref[0, r * TOP_K + k], 1), :], sem).start()
        return carry

    lax.fori_loop(0, tm, body, 0, unroll=DISPATCH_UNROLL)
    pltpu.make_async_copy(xs_hbm.at[pl.ds(0, tm * TOP_K), :], xs_hbm.at[pl.ds(0, tm * TOP_K), :], sem).wait()


def _dispatch(h1, dest, xs_prev, tm=256):
    n, d = h1.shape
    nb = n // tm
    return pl.pallas_call(
        functools.partial(_dispatch_kernel, tm=tm),
        out_shape=jax.ShapeDtypeStruct(xs_prev.shape, F32),
        grid=(nb,),
        in_specs=[pl.BlockSpec((None, 1, tm * TOP_K), lambda i: (i, 0, 0), memory_space=pltpu.SMEM),
                  pl.BlockSpec((tm, d), lambda i: (i, 0)),
                  pl.BlockSpec(memory_space=pl.ANY)],
        out_specs=pl.BlockSpec(memory_space=pl.ANY),
        scratch_shapes=[pltpu.SemaphoreType.DMA(())],
        input_output_aliases={2: 0},
        compiler_params=_cparams(("arbitrary",)),
        name="moe_dispatch",
    )(dest.reshape(nb, 1, tm * TOP_K), h1, xs_prev)


def _expert_kernel(meta_ref, x_ref, wu_ref, bu_ref, wd_ref, bd_ref, sel_ref, o_ref, wu_bf, wd_bf):
    j = pl.program_id(0)
    n_used = meta_ref[0]
    f2 = wu_ref.shape[1]

    @pl.when((j < n_used) & ((j == 0) | (meta_ref[1 + j] != meta_ref[jnp.maximum(j, 1)])))
    def _():
        sel = sel_ref[...]
        cw = sel.shape[0]
        for c in range(f2 // cw):
            grp = jnp.dot(wu_ref[:, c * cw:(c + 1) * cw].astype(BF16), sel, preferred_element_type=F32)
            wu_bf[:, c * cw:(c + 1) * cw] = grp.astype(BF16)
        wd_bf[...] = wd_ref[...].astype(BF16)

    @pl.when(j < n_used)
    def _():
        xb = x_ref[...].astype(BF16)
        hdn = jnp.dot(xb, wu_bf[...], preferred_element_type=F32) + bu_ref[...]
        cw = sel_ref.shape[0]
        half = cw // 2
        parts = []
        for c in range(f2 // cw):
            h_glu = jnp.minimum(hdn[:, c * cw:c * cw + half], SWIGLU_LIMIT)
            h_lin = jnp.clip(hdn[:, c * cw + half:(c + 1) * cw], -SWIGLU_LIMIT, SWIGLU_LIMIT)
            parts.append((h_glu * jax.nn.sigmoid(SWIGLU_ALPHA * h_glu) * (h_lin + 1.0)).astype(BF16))
        act_c = jnp.concatenate(parts, axis=1)
        o_ref[...] = jnp.dot(act_c, wd_bf[...], preferred_element_type=F32) + bd_ref[...]

    @pl.when(j >= n_used)
    def _():
        o_ref[...] = jnp.zeros(o_ref.shape, F32)


def _expert_ffn(xs, blk_expert, n_used, layer, w_up, b_up, w_down, b_down):
    n_rows, d = xs.shape
    n_blocks = n_rows // EXPERT_BLOCK
    f = D_EXPERT
    meta = jnp.concatenate([n_used.reshape(1).astype(I32), blk_expert.astype(I32)])
    sel_np = np.zeros((2 * LANES, 2 * LANES), np.float32)
    sel_np[2 * np.arange(LANES), np.arange(LANES)] = 1.0
    sel_np[2 * np.arange(LANES) + 1, LANES + np.arange(LANES)] = 1.0
    sel = jnp.asarray(sel_np, BF16)
    le = b_up.shape[:2]
    b_up = jnp.transpose(b_up.reshape(le + (f // LANES, LANES, 2)), (0, 1, 2, 4, 3)).reshape(le + (2 * f,))
    e_of = lambda j, m: m[1 + j]
    grid_spec = pltpu.PrefetchScalarGridSpec(
        num_scalar_prefetch=1,
        grid=(n_blocks,),
        in_specs=[
            pl.BlockSpec((EXPERT_BLOCK, d), lambda j, m: (j, 0)),
            pl.BlockSpec((None, None, d, 2 * f), lambda j, m: (layer, e_of(j, m), 0, 0)),
            pl.BlockSpec((None, None, 1, 2 * f), lambda j, m: (layer, e_of(j, m), 0, 0)),
            pl.BlockSpec((None, None, f, d), lambda j, m: (layer, e_of(j, m), 0, 0)),
            pl.BlockSpec((None, None, 1, d), lambda j, m: (layer, e_of(j, m), 0, 0)),
            pl.BlockSpec(sel.shape, lambda j, m: (0, 0)),
        ],
        out_specs=pl.BlockSpec((EXPERT_BLOCK, d), lambda j, m: (j, 0)),
        scratch_shapes=[pltpu.VMEM((d, 2 * f), BF16), pltpu.VMEM((f, d), BF16)],
    )
    return pl.pallas_call(
        _expert_kernel,
        out_shape=jax.ShapeDtypeStruct((n_rows, d), F32),
        grid_spec=grid_spec,
        compiler_params=_cparams(("arbitrary",)),
        name="moe_experts",
    )(meta, xs, w_up, b_up[:, :, None, :], w_down, b_down[:, :, None, :], sel)


def _combine_kernel(cur_ref, nxt_ref, ys_hbm, g4_ref, h_ref, lg_ref, lb_ref, o_ref, ybuf, sem, *, alpha, tm):
    j = pl.program_id(0)
    nblk = pl.num_programs(0)
    slot = j % 2
    n_rows = tm * TOP_K

    def gather(idx_ref, s):
        def body(r, carry):
            pltpu.make_async_copy(ys_hbm.at[pl.ds(idx_ref[0, r], 1), :],
                                  ybuf.at[s, pl.ds(r, 1), :], sem.at[s]).start()
            return carry
        lax.fori_loop(0, n_rows, body, 0, unroll=COMBINE_UNROLL)

    @pl.when(j == 0)
    def _():
        gather(cur_ref, 0)

    @pl.when(j + 1 < nblk)
    def _():
        gather(nxt_ref, 1 - slot)

    pltpu.make_async_copy(ys_hbm.at[pl.ds(0, n_rows), :], ybuf.at[slot], sem.at[slot]).wait()
    g4 = g4_ref[...]
    ffn = jnp.zeros((tm, h_ref.shape[1]), F32)
    for k in range(TOP_K):
        ffn = ffn + ybuf[slot, k * tm:(k + 1) * tm, :] * g4[:, k:k + 1]
    o_ref[...] = _layer_norm(alpha * h_ref[...] + ffn, lg_ref[...], lb_ref[...])


def _combine(ys, pos_km, gates_pad, h1, ln_g, ln_b, alpha, tm=128):
    n, d = h1.shape
    nb = n // tm
    idx3 = pos_km.reshape(nb, 1, tm * TOP_K)
    return pl.pallas_call(
        functools.partial(_combine_kernel, alpha=alpha, tm=tm),
        out_shape=jax.ShapeDtypeStruct((n, d), F32),
        grid=(nb,),
        in_specs=[
            pl.BlockSpec((None, 1, tm * TOP_K), lambda j: (j, 0, 0), memory_space=pltpu.SMEM),
            pl.BlockSpec((None, 1, tm * TOP_K), lambda j: (jnp.minimum(j + 1, nb - 1), 0, 0),
                         memory_space=pltpu.SMEM),
            pl.BlockSpec(memory_space=pl.ANY),
            pl.BlockSpec((tm, LANES), lambda j: (j, 0)),
            pl.BlockSpec((tm, d), lambda j: (j, 0)),
            _const_spec((1, d)), _const_spec((1, d)),
        ],
        out_specs=pl.BlockSpec((tm, d), lambda j: (j, 0)),
        scratch_shapes=[pltpu.VMEM((2, tm * TOP_K, d), F32), pltpu.SemaphoreType.DMA((2,))],
        compiler_params=_cparams(("arbitrary",)),
        name="moe_combine",
    )(idx3, idx3, ys, gates_pad, h1, ln_g.reshape(1, d), ln_b.reshape(1, d))


def _moe_routing(top_e, rank, counts, n_tok, tm_combine):
    n_asg = n_tok * TOP_K
    counts = counts.reshape(N_EXPERTS).astype(I32)
    nblk_per = (counts + EXPERT_BLOCK - 1) // EXPERT_BLOCK
    blk_end = jnp.cumsum(nblk_per)
    pstarts = (blk_end - nblk_per) * EXPERT_BLOCK
    onehot = top_e[..., None] == jnp.arange(N_EXPERTS, dtype=I32)
    dest = jnp.sum(jnp.where(onehot, pstarts, 0), axis=-1) + rank
    n_blocks = -(-n_asg // EXPERT_BLOCK) + N_EXPERTS
    blk_expert = jnp.minimum(jnp.sum(blk_end[None, :] <= jnp.arange(n_blocks, dtype=I32)[:, None], axis=1),
                             N_EXPERTS - 1)
    n_used = blk_end[-1]
    nb = n_tok // tm_combine
    pos_km = jnp.transpose(dest.reshape(nb, tm_combine, TOP_K), (0, 2, 1)).reshape(nb, TOP_K * tm_combine)
    return dest.astype(I32), blk_expert.astype(I32), n_used.astype(I32), pos_km.astype(I32)


def _permute_w_in(w):
    sizes = (N_HEADS * HEAD_DIM, KV_LATENT, IDX_HEADS * IDX_DIM, IDX_DIM, IDX_HEADS,
             CONV_WIDTH, CONV_WIDTH, CONV_WIDTH, SSM_WIDTH, MEM_HEADS * MEM_HEAD_DIM)
    offs = np.cumsum((0,) + sizes)
    q, ckv, qidx, kidx, widx, cu, cgb, cgc, ssm, memq = [w[:, offs[k]:offs[k + 1]] for k in range(len(sizes))]
    gates = w[:, offs[-1]:]
    pad = jnp.zeros((w.shape[0], LANES - IDX_DIM - IDX_HEADS), w.dtype)
    return jnp.concatenate([gates, q, cu, cgb, cgc, ssm, memq, qidx, ckv, kidx, widx, pad], axis=1).astype(BF16)


def kernel(x, mem, ln_in_g, ln_in_b, w_in, kv_norm_g, w_uk, w_uv, conv_w, conv_b, lam_re, lam_im, b_re, b_im, c_re, c_im, d_skip, log_dt, w_glu, b_glu, w_mem_kv, w_branch, w_o, ln1_g, ln1_b, w_router, b_router, w_up, b_up, w_down, b_down, ln2_g, ln2_b):
    bsz, seq, d = x.shape
    depth = w_in.shape[0]
    mlen = mem.shape[1]
    n = bsz * seq
    alpha = float((2 * depth) ** 0.25)
    tm_combine = 256

    h = _ln_in(x.reshape(n, d), ln_in_g, ln_in_b)
    n_row_blocks = -(-n * TOP_K // EXPERT_BLOCK) + N_EXPERTS
    xs = jnp.zeros((n_row_blocks * EXPERT_BLOCK, d), F32)
    mem2 = mem.reshape(bsz * mlen, d)
    for l in range(depth):
        proj = _matmul(h, _permute_w_in(w_in[l]), 256, "in_proj")
        qcat, kcat, ckvn, ckvt = _prep(proj, kv_norm_g[l], bsz, seq)
        att = _dsa_attention(proj, qcat, kcat, ckvn, ckvt, w_uk[l], w_uv[l], kv_norm_g[l], bsz, seq)
        cnv = _short_conv(proj, conv_w[l], conv_b[l], bsz, seq)
        ssm = _s5_mixer(proj, lam_re[l], lam_im[l], b_re[l], b_im[l], c_re[l], c_im[l], d_skip[l], log_dt[l],
                        w_glu[l], b_glu[l], bsz, seq)
        kvm = _matmul(mem2, w_mem_kv[l].astype(BF16), min(256, bsz * mlen), "mem_kv")
        mem_o = _memory_attention(proj, kvm, bsz, seq, mlen)
        h1, te_pad, tg_pad, counts = _merge_router(att, cnv, ssm, mem_o, proj, h, w_branch[l], w_o[l],
                                                   ln1_g[l], ln1_b[l], w_router[l], b_router[l], alpha)
        dest, blk_expert, n_used, pos_km = _moe_routing(te_pad[:, :TOP_K], te_pad[:, TOP_K:2 * TOP_K], counts,
                                                        n, tm_combine)
        xs = _dispatch(h1, dest, xs)
        ys = _expert_ffn(xs, blk_expert, n_used, l, w_up, b_up, w_down, b_down)
        h = _combine(ys, pos_km, tg_pad, h1, ln2_g[l], ln2_b[l], alpha, tm_combine)
    return h.reshape(bsz, seq, d)
```

```python
import functools
import math

import numpy as np
import jax
import jax.numpy as jnp
from jax import lax
from jax.experimental import pallas as pl
from jax.experimental.pallas import tpu as pltpu

F32 = jnp.float32
BF16 = jnp.bfloat16
I32 = jnp.int32
I16 = jnp.int16

N_HEADS = 8
HEAD_DIM = 64
KV_LATENT = 128
IDX_HEADS = 8
IDX_DIM = 32
INDEX_TOPK = 256

ROWS_LN = 512
ROWS_IN_PROJ = 256
ROWS_PREP = 512
ROWS_CONV = 512
ROWS_MEM_ATT = 512
ROWS_MERGE = 512
ROWS_DISPATCH = 256
ROWS_COMBINE = 256
DSA_QUERY_BLOCK = 256
DSA_SCORE_CHUNK = 512
DSA_COUNT_CHUNK = 512
DSA_TILE_KEYS = 128
CONV_WIDTH = 512
SSM_WIDTH = 512
SSM_GROUP = 16
SSM_GROUPS = SSM_WIDTH // SSM_GROUP
SSM_STATE = 64
MEM_HEADS = 4
MEM_HEAD_DIM = 128
N_BRANCH = 4
BRANCH_WIDTH = 512
N_EXPERTS = 32
TOP_K = 4
D_EXPERT = 1024
SWIGLU_LIMIT = 7.0
SWIGLU_ALPHA = 1.702
EXPERT_BLOCK = 256
LN_EPS = 1e-5

LANES = 128
SUBLANES = 8
VMEM_LIMIT_BYTES = 56 * 1024 * 1024

INT_MIN = -(2 ** 31)
NEG_BIG = -1e30
SAFE_LOGIT = 100.0

COL_GATES = 0
COL_Q = 4096
COL_CONV_U = 4608
COL_CONV_GB = 5120
COL_CONV_GC = 5632
COL_SSM = 6144
COL_MEMQ = 6656
COL_QIDX = 7168
COL_CKV = 7424
COL_TAIL = 7552
D_PROJ = 7680
TAIL_W_OFF = IDX_DIM

SSM_SEQS = 8


def _cparams(sem):
    return pltpu.CompilerParams(dimension_semantics=sem, vmem_limit_bytes=VMEM_LIMIT_BYTES)


def _layer_norm(x, g, b):
    mu = jnp.mean(x, axis=-1, keepdims=True)
    xc = x - mu
    var = jnp.mean(xc * xc, axis=-1, keepdims=True)
    return xc * lax.rsqrt(var + LN_EPS) * g + b


def _const_spec(shape):
    nd = len(shape)
    return pl.BlockSpec(shape, lambda *_: (0,) * nd)


def _ln_kernel(x_ref, g_ref, b_ref, o_ref):
    o_ref[...] = _layer_norm(x_ref[...], g_ref[...], b_ref[...])


def _ln_in(x2, g, b, tm=ROWS_LN):
    n, d = x2.shape
    return pl.pallas_call(
        _ln_kernel,
        out_shape=jax.ShapeDtypeStruct((n, d), F32),
        grid=(n // tm,),
        in_specs=[pl.BlockSpec((tm, d), lambda i: (i, 0)), _const_spec((1, d)), _const_spec((1, d))],
        out_specs=pl.BlockSpec((tm, d), lambda i: (i, 0)),
        compiler_params=_cparams(("parallel",)),
        name="ln_in",
    )(x2, g.reshape(1, d), b.reshape(1, d))


def _matmul_kernel(x_ref, w_ref, o_ref):
    o_ref[...] = jnp.dot(x_ref[...].astype(BF16), w_ref[...], preferred_element_type=F32)


def _matmul(x, w_bf16, tm, name):
    n, k = x.shape
    m = w_bf16.shape[1]
    return pl.pallas_call(
        _matmul_kernel,
        out_shape=jax.ShapeDtypeStruct((n, m), F32),
        grid=(n // tm,),
        in_specs=[pl.BlockSpec((tm, k), lambda i: (i, 0)),
                  pl.BlockSpec((k, m), lambda i: (0, 0), pipeline_mode=pl.Buffered(1))],
        out_specs=pl.BlockSpec((tm, m), lambda i: (i, 0)),
        compiler_params=_cparams(("parallel",)),
        name=name,
    )(x, w_bf16)


def _split_hi_lo(x):
    hi = x.astype(BF16)
    lo = (x - hi.astype(F32)).astype(BF16)
    return hi, lo


def _prep_kernel(qidx_ref, ckv_ref, tail_ref, g_ref, sq_ref, sk_ref, qcat_ref, kcat_ref, ckvn_ref, ckvt_ref):
    q_hi, q_lo = _split_hi_lo(qidx_ref[...])
    qcat = jnp.dot(jnp.concatenate([q_hi, q_lo], axis=1), sq_ref[...], preferred_element_type=F32)
    qcat_ref[...] = qcat.astype(BF16)
    k_hi, k_lo = _split_hi_lo(tail_ref[...])
    kcat = jnp.dot(jnp.concatenate([k_hi, k_lo], axis=1), sk_ref[...], preferred_element_type=F32)
    kcat_ref[...] = kcat.astype(BF16)
    c = ckv_ref[...]
    ms = jnp.mean(c * c, axis=-1, keepdims=True)
    cn = c * lax.rsqrt(ms + LN_EPS) * g_ref[...]
    ckvn_ref[...] = cn.astype(BF16)
    ckvt_ref[...] = cn.T.astype(BF16)


def _selection_matrices():
    sq = np.zeros((2 * IDX_HEADS * IDX_DIM, IDX_HEADS * LANES), np.float32)
    for h in range(IDX_HEADS):
        for d in range(IDX_DIM):
            hi_in = h * IDX_DIM + d
            lo_in = IDX_HEADS * IDX_DIM + hi_in
            sq[hi_in, h * LANES + d] = 1.0
            sq[hi_in, h * LANES + IDX_DIM + d] = 1.0
            sq[lo_in, h * LANES + 2 * IDX_DIM + d] = 1.0
    sk = np.zeros((2 * LANES, LANES), np.float32)
    for d in range(IDX_DIM):
        sk[d, d] = 1.0
        sk[LANES + d, IDX_DIM + d] = 1.0
        sk[d, 2 * IDX_DIM + d] = 1.0
    return jnp.asarray(sq, BF16), jnp.asarray(sk, BF16)


def _prep(proj, kv_norm_g, bsz, seq, tm=ROWS_PREP):
    n = proj.shape[0]
    tm = min(tm, seq)
    nbt = seq // tm
    sq, sk = _selection_matrices()
    return pl.pallas_call(
        _prep_kernel,
        out_shape=(jax.ShapeDtypeStruct((n, IDX_HEADS * LANES), BF16),
                   jax.ShapeDtypeStruct((n, LANES), BF16),
                   jax.ShapeDtypeStruct((n, KV_LATENT), BF16),
                   jax.ShapeDtypeStruct((bsz, KV_LATENT, seq), BF16)),
        grid=(n // tm,),
        in_specs=[pl.BlockSpec((tm, 256), lambda i: (i, COL_QIDX // 256)),
                  pl.BlockSpec((tm, 128), lambda i: (i, COL_CKV // 128)),
                  pl.BlockSpec((tm, 128), lambda i: (i, COL_TAIL // 128)),
                  _const_spec((1, KV_LATENT)), _const_spec(sq.shape), _const_spec(sk.shape)],
        out_specs=(pl.BlockSpec((tm, IDX_HEADS * LANES), lambda i: (i, 0)),
                   pl.BlockSpec((tm, LANES), lambda i: (i, 0)),
                   pl.BlockSpec((tm, KV_LATENT), lambda i: (i, 0)),
                   pl.BlockSpec((None, KV_LATENT, tm), lambda i: (i // nbt, 0, i % nbt))),
        compiler_params=_cparams(("parallel",)),
        name="dsa_prep",
    )(proj, proj, proj, kv_norm_g.reshape(1, KV_LATENT), sq, sk)


def _dsa_kernel(q_ref, tail_ref, qcat_ref, kcat_ref, ckv_ref, ckvt_ref, wuk_ref, wuv_ref, kbound_ref, o_ref,
                key_ref, khi_ref, klo_ref, qct_ref, qlat_ref, m_ref, l_ref, acc_ref, *, seq, topk, cks, ck, cka, qb):
    i = pl.program_id(1)
    q0 = i * qb
    nchs = (q0 + qb + cks - 1) // cks
    nch = (q0 + qb + ck - 1) // ck
    q_pos = q0 + lax.broadcasted_iota(I32, (1, qb), 1)
    row_pos = lax.broadcasted_iota(I32, (ck, 1), 0)
    w_t = tail_ref[...].T
    cnt_rows = min(64, ck)

    for h in range(IDX_HEADS):
        qct_ref[h] = qcat_ref[:, h * LANES:(h + 1) * LANES].astype(F32).T.astype(BF16)
    row_pos_t = lax.broadcasted_iota(I32, (cka, 1), 0)

    def score_chunk(c, carry):
        for sub in range(cks // cka):
            k0 = pl.multiple_of(c * cks + sub * cka, cka)
            kc = kcat_ref[pl.ds(k0, cka), :]
            s = jnp.zeros((cka, qb), F32)
            for h in range(IDX_HEADS):
                d = jnp.dot(kc, qct_ref[h], preferred_element_type=F32)
                s = s + jnp.maximum(d, 0.0) * w_t[TAIL_W_OFF + h:TAIL_W_OFF + h + 1, :]
            s = s + 0.0
            bits = pltpu.bitcast(s, I32)
            key = bits ^ ((bits >> 31) & jnp.int32(0x7FFFFFFF))
            key = jnp.where(k0 + row_pos_t <= q_pos, key, jnp.int32(INT_MIN))
            key_ref[pl.ds(k0, cka), :] = key
            khi_ref[pl.ds(k0, cka), :] = (key >> 16).astype(I16)
            klo_ref[pl.ds(k0, cka), :] = ((key & jnp.int32(0xFFFF)) - 32768).astype(I16)
        return carry

    lax.fori_loop(0, nchs, score_chunk, 0)

    def count_keys(pred_fn):
        def body(c, acc):
            k0 = pl.multiple_of(c * ck, ck)
            hit = jnp.where(pred_fn(key_ref[pl.ds(k0, ck), :], k0), 1.0, 0.0)
            return acc + jnp.sum(hit.reshape(ck // cnt_rows, cnt_rows, qb), axis=0)
        acc = lax.fori_loop(0, nch, body, jnp.zeros((cnt_rows, qb), F32))
        return jnp.sum(acc, axis=0, keepdims=True)

    kf = float(topk)
    pack = 16
    n_part = ck // (2 * pack)
    lo16 = -32768

    def count16(ref, cand16):
        cand = jnp.broadcast_to(cand16, (pack, qb)).astype(I16)
        one = jnp.ones((), I16)
        zero = jnp.zeros((), I16)

        def body(c, acc):
            k0 = pl.multiple_of(c * ck, ck)
            hit = jnp.where(ref[pl.ds(k0, ck), :].reshape(ck // pack, pack, qb) >= cand[None], one, zero)
            part = hit[0:2]
            for j in range(1, n_part):
                part = part + hit[2 * j:2 * j + 2]
            return acc + part
        acc = lax.fori_loop(0, nch, body, jnp.zeros((2, pack, qb), I16)).astype(I32).astype(F32)
        return jnp.sum(jnp.sum(acc, axis=0), axis=0, keepdims=True)

    def bisect16(ref, target):
        def one_pass(b, carry):
            w, c_ge_w, c_gt_w = carry
            cand = w + jnp.left_shift(jnp.int32(1), 15 - b)
            cnt = count16(ref, cand)
            ok = cnt >= target
            return jnp.where(ok, cand, w), jnp.where(ok, cnt, c_ge_w), jnp.where(ok, c_gt_w, cnt)
        total = jnp.full((1, qb), 1.0, F32) * (nch * ck).astype(F32)
        return lax.fori_loop(0, 16, one_pass, (jnp.full((1, qb), lo16, I32), total, jnp.zeros((1, qb), F32)))

    v16, c_v16, c_above = bisect16(khi_ref, kf)
    v16_b = jnp.broadcast_to(v16, (pack, qb)).astype(I16)

    def build_lo(c, carry):
        k0 = pl.multiple_of(c * ck, ck)
        hi = khi_ref[pl.ds(k0, ck), :].reshape(ck // pack, pack, qb)
        lo = klo_ref[pl.ds(k0, ck), :].reshape(ck // pack, pack, qb)
        khi_ref[pl.ds(k0, ck), :] = jnp.where(hi == v16_b[None], lo, jnp.full((), lo16, I16)).reshape(ck, qb)
        return carry

    lax.fori_loop(0, nch, build_lo, 0)
    w, c_w, c_w_above = bisect16(khi_ref, kf - c_above)
    c_w = jnp.where(w == lo16, c_v16 - c_above, c_w)
    v = (v16 << 16) + (w + 32768)
    select_all = v == jnp.int32(INT_MIN)
    thr = jnp.maximum(v, jnp.int32(INT_MIN + 1))
    c_ge = jnp.where(select_all, c_above, c_above + c_w)
    c_gt = jnp.where(select_all, c_above, c_above + c_w_above)
    need = kf - c_gt

    jlim_ref = m_ref.at[0:1]
    jlim_ref[...] = jnp.full((1, qb), float(seq), F32)
    has_tie = jnp.max(jnp.where(c_ge > kf, 1.0, 0.0)) > 0.0

    @pl.when(has_tie)
    def _():
        n_bits = max(1, int(math.ceil(math.log2(seq))))

        def pos_pass(b, j):
            cand = j + jnp.left_shift(jnp.int32(1), n_bits - 1 - b)
            cnt = count_keys(lambda key, k0: (key == thr) & (k0 + row_pos < cand))
            return jnp.where(cnt < need, cand, j)

        j = lax.fori_loop(0, n_bits, pos_pass, jnp.zeros((1, qb), I32))
        jlim_ref[...] = jnp.where(c_ge > kf, j.astype(F32), float(seq))

    jlim = jlim_ref[...].astype(I32)

    q_bf = q_ref[...].astype(BF16)
    qn2_max = jnp.zeros((1, qb), F32)
    for j in range(N_HEADS // 2):
        ql = jnp.dot(q_bf[:, j * LANES:(j + 1) * LANES], wuk_ref[j],
                     preferred_element_type=F32) * (HEAD_DIM ** -0.5 * math.log2(math.e))
        for half in range(2):
            qt = ql[:, half * KV_LATENT:(half + 1) * KV_LATENT].T.astype(BF16)
            qlat_ref[2 * j + half] = qt
            qf = qt.astype(F32)
            qn2_max = jnp.maximum(qn2_max, jnp.sum(qf * qf, axis=0, keepdims=True))
    logit_bound = jnp.sqrt(jnp.max(qn2_max)) * kbound_ref[0, 0]
    row_pos_a = lax.broadcasted_iota(I32, (cka, 1), 0)

    def att_tiles(base, n_sub, online):
        for sub in range(n_sub):
            k0 = pl.multiple_of(base + sub * cka, cka)
            kv = ckv_ref[pl.ds(k0, cka), :]
            kvt = ckvt_ref[:, pl.ds(k0, cka)]
            key = key_ref[pl.ds(k0, cka), :]
            sel = (key > thr) | ((key == thr) & (k0 + row_pos_a <= jlim))
            bias = jnp.where(sel, 0.0, NEG_BIG)
            for h in range(N_HEADS):
                lg = jnp.dot(kv, qlat_ref[h], preferred_element_type=F32) + bias
                if not online:
                    p = jnp.exp2(lg)
                    l_ref[h:h + 1, :] = l_ref[h:h + 1, :] + jnp.sum(p, axis=0, keepdims=True)
                    acc_ref[h] = acc_ref[h] + jnp.dot(kvt, p.astype(BF16), preferred_element_type=F32)
                else:
                    m_prev = m_ref[h:h + 1, :]
                    m_new = jnp.maximum(m_prev, jnp.max(lg, axis=0, keepdims=True))
                    p = jnp.exp2(lg - m_new)
                    alpha = jnp.exp2(m_prev - m_new)
                    l_ref[h:h + 1, :] = alpha * l_ref[h:h + 1, :] + jnp.sum(p, axis=0, keepdims=True)
                    acc_ref[h] = alpha * acc_ref[h] + jnp.dot(kvt, p.astype(BF16), preferred_element_type=F32)
                    m_ref[h:h + 1, :] = m_new

    def attend(online):
        def att_chunk(c, carry):
            att_tiles(c * cks, cks // cka, online)
            return carry

        end = q0 + qb
        n_full = end // cks
        lax.fori_loop(0, n_full, att_chunk, 0)
        if cks % qb == 0 and cks > qb:
            for parts in range(1, cks // qb):
                @pl.when(end - n_full * cks == parts * qb)
                def _():
                    att_tiles(n_full * cks, parts * qb // cka, online)
        else:
            @pl.when(end - n_full * cks > 0)
            def _():
                att_tiles(n_full * cks, cks // cka, online)

    m_ref[...] = jnp.full(m_ref.shape, NEG_BIG, F32)
    l_ref[...] = jnp.zeros(l_ref.shape, F32)
    acc_ref[...] = jnp.zeros(acc_ref.shape, F32)

    @pl.when(logit_bound < SAFE_LOGIT)
    def _():
        attend(online=False)

    @pl.when(jnp.logical_not(logit_bound < SAFE_LOGIT))
    def _():
        attend(online=True)

    outs = []
    for j in range(N_HEADS // 2):
        o_pair = [(acc_ref[h] / l_ref[h:h + 1, :]).T.astype(BF16) for h in (2 * j, 2 * j + 1)]
        outs.append(jnp.dot(jnp.concatenate(o_pair, axis=1), wuv_ref[j], preferred_element_type=F32))
    o_ref[...] = jnp.concatenate(outs, axis=1)


def _dsa_attention(proj, qcat, kcat, ckvn, ckvt, w_uk, w_uv, kv_norm_g, bsz, seq):
    n = proj.shape[0]
    qb = min(DSA_QUERY_BLOCK, seq)
    n_blk = seq // qb
    topk = min(INDEX_TOPK, seq // 4)
    cks = min(DSA_SCORE_CHUNK, seq)
    ck = min(DSA_COUNT_CHUNK, seq)
    cka = min(DSA_TILE_KEYS, seq)
    zk = jnp.zeros((HEAD_DIM, KV_LATENT), F32)
    zv = jnp.zeros((KV_LATENT, HEAD_DIM), F32)
    wuk = jnp.stack([jnp.block([[w_uk[:, 2 * j, :].T, zk], [zk, w_uk[:, 2 * j + 1, :].T]])
                     for j in range(N_HEADS // 2)]).astype(BF16)
    wuv = jnp.stack([jnp.block([[w_uv[:, 2 * j, :], zv], [zv, w_uv[:, 2 * j + 1, :]]])
                     for j in range(N_HEADS // 2)]).astype(BF16)
    kbound = (1.01 * math.sqrt(KV_LATENT) * jnp.max(jnp.abs(kv_norm_g))).reshape(1, 1).astype(F32)
    kernel = functools.partial(_dsa_kernel, seq=seq, topk=topk, cks=cks, ck=ck, cka=cka, qb=qb)
    return pl.pallas_call(
        kernel,
        out_shape=jax.ShapeDtypeStruct((n, N_HEADS * HEAD_DIM), F32),
        grid=(bsz, n_blk),
        in_specs=[pl.BlockSpec((qb, 512), lambda b, i: (b * n_blk + i, COL_Q // 512)),
                  pl.BlockSpec((qb, 128), lambda b, i: (b * n_blk + i, COL_TAIL // 128)),
                  pl.BlockSpec((qb, IDX_HEADS * LANES), lambda b, i: (b * n_blk + i, 0)),
                  pl.BlockSpec((seq, LANES), lambda b, i: (b, 0)),
                  pl.BlockSpec((seq, KV_LATENT), lambda b, i: (b, 0)),
                  pl.BlockSpec((None, KV_LATENT, seq), lambda b, i: (b, 0, 0)),
                  _const_spec(wuk.shape), _const_spec(wuv.shape),
                  pl.BlockSpec(memory_space=pltpu.SMEM)],
        out_specs=pl.BlockSpec((qb, N_HEADS * HEAD_DIM), lambda b, i: (b * n_blk + i, 0)),
        scratch_shapes=[pltpu.VMEM((seq, qb), I32),
                        pltpu.VMEM((seq, qb), I16),
                        pltpu.VMEM((seq, qb), I16),
                        pltpu.VMEM((IDX_HEADS, LANES, qb), BF16),
                        pltpu.VMEM((N_HEADS, KV_LATENT, qb), BF16),
                        pltpu.VMEM((N_HEADS, qb), F32),
                        pltpu.VMEM((N_HEADS, qb), F32),
                        pltpu.VMEM((N_HEADS, KV_LATENT, qb), F32)],
        compiler_params=_cparams(("parallel", "parallel")),
        name="dsa_attention",
    )(proj, proj, qcat, kcat, ckvn, ckvt, wuk, wuv, kbound)


def _conv_kernel(u_ref, gb_ref, gc_ref, pu_ref, pgc_ref, w_ref, b_ref, o_ref):
    i = pl.program_id(1)
    tb = u_ref.shape[0]
    v = gc_ref[...] * u_ref[...]
    halo = jnp.where(i > 0, pgc_ref[...] * pu_ref[...], 0.0)
    vfull = jnp.concatenate([halo, v], axis=0)
    v1 = pltpu.roll(vfull, 1, 0)[SUBLANES:SUBLANES + tb]
    v2 = pltpu.roll(vfull, 2, 0)[SUBLANES:SUBLANES + tb]
    w = w_ref[...]
    y = w[0:1] * v2 + w[1:2] * v1 + w[2:3] * v
    o_ref[...] = gb_ref[...] * (y + b_ref[...])


def _short_conv(proj, conv_w, conv_b, bsz, seq, tb=ROWS_CONV):
    n = proj.shape[0]
    tb = min(tb, seq)
    nb = seq // tb
    cw = CONV_WIDTH
    blk = lambda col: pl.BlockSpec((tb, cw), lambda b, i: (b * nb + i, col // cw))
    prev = lambda col: pl.BlockSpec(
        (SUBLANES, cw), lambda b, i: (jnp.maximum((b * nb + i) * (tb // SUBLANES) - 1, 0), col // cw))
    return pl.pallas_call(
        _conv_kernel,
        out_shape=jax.ShapeDtypeStruct((n, cw), F32),
        grid=(bsz, nb),
        in_specs=[blk(COL_CONV_U), blk(COL_CONV_GB), blk(COL_CONV_GC), prev(COL_CONV_U), prev(COL_CONV_GC),
                  _const_spec((SUBLANES, cw)), _const_spec((1, cw))],
        out_specs=pl.BlockSpec((tb, cw), lambda b, i: (b * nb + i, 0)),
        compiler_params=_cparams(("parallel", "parallel")),
        name="short_conv",
    )(proj, proj, proj, proj, proj,
      jnp.pad(conv_w, ((0, SUBLANES - conv_w.shape[0]), (0, 0))), conv_b.reshape(1, cw))


def _s5_scan_chunk(bu_ref, a_ref, st_ref, n_steps, store):
    s_tot = a_ref.shape[1] // 2
    cb = 512
    for blk in range(s_tot // cb):
        re_sl = pl.ds(blk * cb, cb)
        im_sl = pl.ds(s_tot + blk * cb, cb)
        ar = jnp.broadcast_to(a_ref[0:1, blk * cb:(blk + 1) * cb], (SSM_SEQS, cb))
        ai = jnp.broadcast_to(a_ref[0:1, s_tot + blk * cb:s_tot + (blk + 1) * cb], (SSM_SEQS, cb))

        def step(t, carry):
            re, im = carry
            r0 = pl.multiple_of(t * SSM_SEQS, SSM_SEQS)
            br = bu_ref[pl.ds(r0, SSM_SEQS), re_sl]
            bi = bu_ref[pl.ds(r0, SSM_SEQS), im_sl]
            nre = ar * re - ai * im + br
            nim = ar * im + ai * re + bi
            if store:
                bu_ref[pl.ds(r0, SSM_SEQS), re_sl] = nre
                bu_ref[pl.ds(r0, SSM_SEQS), im_sl] = nim
            return nre, nim

        re, im = lax.fori_loop(0, n_steps, step, (st_ref[:, re_sl], st_ref[:, im_sl]), unroll=8)
        st_ref[:, re_sl] = re
        st_ref[:, im_sl] = im


def _gelu_tanh(x):
    return 0.5 * x * (1.0 + jnp.tanh(math.sqrt(2.0 / math.pi) * (x + 0.044715 * (x * x * x))))


S5_ROWS = 512


def _s5_kernel(u_ref, perm_ref, wb_ref, a_ref, apow_ref, wc_ref, d_ref, wg_ref, bg_ref, o_ref,
               bu_ref, st_ref, carry_ref, *, seg_len, steps_per_seq):
    c = pl.program_id(0)
    s_tot = a_ref.shape[1] // 2
    tn_dims = (((0,), (0,)), ((), ()))

    @pl.when(c % steps_per_seq == 0)
    def _():
        carry_ref[...] = jnp.zeros(carry_ref.shape, F32)

    u = u_ref[...]
    perm = perm_ref[...]
    u_il = jnp.dot(perm, u.astype(BF16), preferred_element_type=F32).astype(BF16)
    hw = u_il.shape[1] // 2
    hs = s_tot // 2
    for part in range(2):
        for q in range(2):
            c0 = part * s_tot + q * hs
            bu_ref[:, c0:c0 + hs] = jnp.dot(u_il[:, q * hw:(q + 1) * hw], wb_ref[q * hw:(q + 1) * hw, c0:c0 + hs],
                                            preferred_element_type=F32)

    st_ref[...] = jnp.zeros(st_ref.shape, F32)
    _s5_scan_chunk(bu_ref, a_ref, st_ref, seg_len, store=False)

    pr = apow_ref[0:1, :s_tot]
    pi = apow_ref[0:1, s_tot:]
    hre = carry_ref[0:1, :s_tot]
    him = carry_ref[0:1, s_tot:]
    for j in range(SSM_SEQS):
        ere = st_ref[j:j + 1, :s_tot]
        eim = st_ref[j:j + 1, s_tot:]
        st_ref[j:j + 1, :s_tot] = hre
        st_ref[j:j + 1, s_tot:] = him
        hre, him = pr * hre - pi * him + ere, pr * him + pi * hre + eim
    carry_ref[0:1, :s_tot] = hre
    carry_ref[0:1, s_tot:] = him

    _s5_scan_chunk(bu_ref, a_ref, st_ref, seg_len, store=True)
    halves = []
    for q in range(2):
        acc = None
        for part in range(2):
            c0 = part * s_tot + q * hs
            term = jnp.dot(bu_ref[:, c0:c0 + hs].astype(BF16), wc_ref[c0:c0 + hs, q * hw:(q + 1) * hw],
                           preferred_element_type=F32)
            acc = term if acc is None else acc + term
        halves.append(acc)
    ch = jnp.concatenate(halves, axis=1)
    ch_hi, ch_lo = _split_hi_lo(ch)
    ch_nat = (lax.dot_general(perm, ch_hi, tn_dims, preferred_element_type=F32)
              + lax.dot_general(perm, ch_lo, tn_dims, preferred_element_type=F32))
    z = _gelu_tanh(ch_nat + d_ref[...] * u)
    gate = jnp.dot(z.astype(BF16), wg_ref[...], preferred_element_type=F32) + bg_ref[...]
    o_ref[...] = z * jax.nn.sigmoid(gate)


def _s5_mixer(proj, lam_re, lam_im, b_re, b_im, c_re, c_im, d_skip, log_dt, w_glu, b_glu, bsz, seq):
    n = proj.shape[0]
    g, p, nn = SSM_GROUPS, SSM_STATE, SSM_GROUP
    s_tot = g * p
    rows = min(S5_ROWS, seq)
    seg_len = rows // SSM_SEQS

    dt = jnp.exp(log_dt.astype(F32))[:, None]
    lam = lax.complex(lam_re.astype(F32), lam_im.astype(F32))
    a_bar = jnp.exp(lam * dt)
    b_bar = ((a_bar - 1.0) / lam)[:, :, None] * lax.complex(b_re.astype(F32), b_im.astype(F32))
    a_pow = jnp.exp(lam * dt * seg_len)
    a_vec = jnp.concatenate([jnp.real(a_bar).reshape(1, s_tot), jnp.imag(a_bar).reshape(1, s_tot)], axis=1)
    apow_vec = jnp.concatenate([jnp.real(a_pow).reshape(1, s_tot), jnp.imag(a_pow).reshape(1, s_tot)], axis=1)
    eye = jnp.eye(g, dtype=F32)
    wb = jnp.concatenate(
        [jnp.einsum('gpn,gh->gnhp', jnp.real(b_bar), eye).reshape(g * nn, s_tot),
         jnp.einsum('gpn,gh->gnhp', jnp.imag(b_bar), eye).reshape(g * nn, s_tot)], axis=1).astype(BF16)
    wc = jnp.concatenate(
        [jnp.einsum('gnp,gh->gphn', c_re.astype(F32), eye).reshape(s_tot, g * nn),
         -jnp.einsum('gnp,gh->gphn', c_im.astype(F32), eye).reshape(s_tot, g * nn)], axis=0).astype(BF16)
    perm_np = np.zeros((rows, rows), np.float32)
    t_idx, j_idx = np.meshgrid(np.arange(seg_len), np.arange(SSM_SEQS), indexing='ij')
    perm_np[(t_idx * SSM_SEQS + j_idx).ravel(), (j_idx * seg_len + t_idx).ravel()] = 1.0
    perm = jnp.asarray(perm_np, BF16)

    return pl.pallas_call(
        functools.partial(_s5_kernel, seg_len=seg_len, steps_per_seq=seq // rows),
        out_shape=jax.ShapeDtypeStruct((n, SSM_WIDTH), F32),
        grid=(n // rows,),
        in_specs=[pl.BlockSpec((rows, SSM_WIDTH), lambda c: (c, COL_SSM // SSM_WIDTH)),
                  _const_spec(perm.shape), _const_spec(wb.shape), _const_spec(a_vec.shape),
                  _const_spec(apow_vec.shape), _const_spec(wc.shape), _const_spec((1, SSM_WIDTH)),
                  _const_spec((SSM_WIDTH, SSM_WIDTH)), _const_spec((1, SSM_WIDTH))],
        out_specs=pl.BlockSpec((rows, SSM_WIDTH), lambda c: (c, 0)),
        scratch_shapes=[pltpu.VMEM((rows, 2 * s_tot), F32), pltpu.VMEM((SSM_SEQS, 2 * s_tot), F32),
                        pltpu.VMEM((1, 2 * s_tot), F32)],
        compiler_params=_cparams(("arbitrary",)),
        name="s5_mixer",
    )(proj, perm, wb, a_vec, apow_vec, wc, d_skip.reshape(1, SSM_WIDTH).astype(F32),
      w_glu.astype(BF16), b_glu.reshape(1, SSM_WIDTH))


def _mem_attn_kernel(q_ref, kv_ref, o_ref):
    hw = MEM_HEADS * MEM_HEAD_DIM
    q = q_ref[...].astype(BF16)
    kv = kv_ref[...].astype(BF16)
    for h in range(MEM_HEADS):
        sl = slice(h * MEM_HEAD_DIM, (h + 1) * MEM_HEAD_DIM)
        k = kv[:, sl]
        v = kv[:, hw + h * MEM_HEAD_DIM:hw + (h + 1) * MEM_HEAD_DIM]
        lg = lax.dot_general(q[:, sl], k, (((1,), (1,)), ((), ())),
                             preferred_element_type=F32) * (MEM_HEAD_DIM ** -0.5)
        m = jnp.max(lg, axis=1, keepdims=True)
        p = jnp.exp(lg - m)
        p = p / jnp.sum(p, axis=1, keepdims=True)
        o_ref[:, sl] = jnp.dot(p.astype(BF16), v, preferred_element_type=F32)


def _memory_attention(proj, kvm, bsz, seq, mlen, tm=ROWS_MEM_ATT):
    n = proj.shape[0]
    tm = min(tm, seq)
    nb = seq // tm
    hw = MEM_HEADS * MEM_HEAD_DIM
    return pl.pallas_call(
        _mem_attn_kernel,
        out_shape=jax.ShapeDtypeStruct((n, hw), F32),
        grid=(bsz, nb),
        in_specs=[pl.BlockSpec((tm, hw), lambda b, i: (b * nb + i, COL_MEMQ // hw)),
                  pl.BlockSpec((mlen, 2 * hw), lambda b, i: (b, 0))],
        out_specs=pl.BlockSpec((tm, hw), lambda b, i: (b * nb + i, 0)),
        compiler_params=_cparams(("parallel", "parallel")),
        name="memory_attention",
    )(proj, kvm)


def _merge_kernel(att_ref, cnv_ref, ssm_ref, mem_ref, gl_ref, h_ref, wbr_ref, wo_ref, g_ref, b_ref,
                  wr_ref, br_ref, ltri_ref, h1_ref, te_ref, tg_ref, cnt_ref, base_ref, *, alpha):
    d = h_ref.shape[1]

    @pl.when(pl.program_id(0) == 0)
    def _():
        base_ref[...] = jnp.zeros(base_ref.shape, F32)

    merged = jnp.zeros(h_ref.shape, F32)
    for r, br in enumerate((att_ref, cnv_ref, ssm_ref, mem_ref)):
        y = jnp.dot(br[...].astype(BF16), wbr_ref[r], preferred_element_type=F32)
        merged = merged + y * jax.nn.sigmoid(gl_ref[:, r * d:(r + 1) * d])
    y = alpha * h_ref[...] + jnp.dot(merged.astype(BF16), wo_ref[...], preferred_element_type=F32)
    h1 = _layer_norm(y, g_ref[...], b_ref[...])
    h1_ref[...] = h1

    logits = jnp.dot(h1, wr_ref[...], preferred_element_type=F32, precision=lax.Precision.HIGHEST) + br_ref[...]
    tm = logits.shape[0]
    lane = lax.broadcasted_iota(I32, (tm, N_EXPERTS), 1)
    out_lane = lax.broadcasted_iota(I32, (tm, LANES), 1)
    work = logits
    top_e = jnp.zeros((tm, LANES), I32)
    top_v = jnp.zeros((tm, LANES), F32)
    vals = []
    onehots = []
    for k in range(TOP_K):
        mx = jnp.max(work, axis=1, keepdims=True)
        idx = jnp.min(jnp.where(work == mx, lane, N_EXPERTS), axis=1, keepdims=True)
        hit = lane == idx
        work = jnp.where(hit, -jnp.inf, work)
        top_e = jnp.where(out_lane == k, idx, top_e)
        vals.append(mx)
        onehots.append(jnp.where(hit, 1.0, 0.0))
    den = sum(jnp.exp(vk - vals[0]) for vk in vals)
    for k in range(TOP_K):
        top_v = jnp.where(out_lane == k, jnp.exp(vals[k] - vals[0]) / den, top_v)

    per_tok = onehots[0] + onehots[1] + onehots[2] + onehots[3]
    before = jnp.dot(ltri_ref[...], per_tok.astype(BF16), preferred_element_type=F32) + base_ref[...]
    for k in range(TOP_K):
        rank = jnp.sum(onehots[k] * before, axis=1, keepdims=True)
        top_e = jnp.where(out_lane == TOP_K + k, rank.astype(I32), top_e)
    base_ref[...] = base_ref[...] + jnp.sum(per_tok, axis=0, keepdims=True)
    cnt_ref[...] = base_ref[...]
    te_ref[...] = top_e
    tg_ref[...] = top_v


def _merge_router(att, cnv, ssm, mem_o, proj, h, w_branch, w_o, ln_g, ln_b, w_router, b_router, alpha, tm=ROWS_MERGE):
    n, d = h.shape
    bw = BRANCH_WIDTH
    row = lambda w: pl.BlockSpec((tm, w), lambda i: (i, 0))
    ltri = jnp.asarray(np.tril(np.ones((tm, tm), np.float32), -1), BF16)
    return pl.pallas_call(
        functools.partial(_merge_kernel, alpha=alpha),
        out_shape=(jax.ShapeDtypeStruct((n, d), F32),
                   jax.ShapeDtypeStruct((n, LANES), I32),
                   jax.ShapeDtypeStruct((n, LANES), F32),
                   jax.ShapeDtypeStruct((1, N_EXPERTS), F32)),
        grid=(n // tm,),
        in_specs=[row(bw), row(bw), row(bw), row(bw),
                  pl.BlockSpec((tm, N_BRANCH * d), lambda i: (i, COL_GATES // (N_BRANCH * d))),
                  row(d),
                  _const_spec((N_BRANCH, bw, d)), _const_spec((d, d)), _const_spec((1, d)), _const_spec((1, d)),
                  _const_spec((d, N_EXPERTS)), _const_spec((1, N_EXPERTS)), _const_spec((tm, tm))],
        out_specs=(row(d), row(LANES), row(LANES), _const_spec((1, N_EXPERTS))),
        scratch_shapes=[pltpu.VMEM((1, N_EXPERTS), F32)],
        compiler_params=_cparams(("arbitrary",)),
        name="merge_router",
    )(att, cnv, ssm, mem_o, proj, h, w_branch.astype(BF16), w_o.astype(BF16),
      ln_g.reshape(1, d), ln_b.reshape(1, d), w_router, b_router.reshape(1, N_EXPERTS), ltri)


DISPATCH_UNROLL = 8
COMBINE_UNROLL = 16


def _dispatch_kernel(dest_ref, h_ref, xs_in_hbm, xs_hbm, sem, *, tm):
    del xs_in_hbm

    def body(r, carry):
        for k in range(TOP_K):
            pltpu.make_async_copy(h_ref.at[pl.ds(r, 1), :],
                                  xs_hbm.at[pl.ds(dest_ref[0, r * TOP_K + k], 1), :], sem).start()
        return carry

    lax.fori_loop(0, tm, body, 0, unroll=DISPATCH_UNROLL)
    pltpu.make_async_copy(xs_hbm.at[pl.ds(0, tm * TOP_K), :], xs_hbm.at[pl.ds(0, tm * TOP_K), :], sem).wait()


def _dispatch(h1, dest, xs_prev, tm=ROWS_DISPATCH):
    n, d = h1.shape
    nb = n // tm
    return pl.pallas_call(
        functools.partial(_dispatch_kernel, tm=tm),
        out_shape=jax.ShapeDtypeStruct(xs_prev.shape, F32),
        grid=(nb,),
        in_specs=[pl.BlockSpec((None, 1, tm * TOP_K), lambda i: (i, 0, 0), memory_space=pltpu.SMEM),
                  pl.BlockSpec((tm, d), lambda i: (i, 0)),
                  pl.BlockSpec(memory_space=pl.ANY)],
        out_specs=pl.BlockSpec(memory_space=pl.ANY),
        scratch_shapes=[pltpu.SemaphoreType.DMA(())],
        input_output_aliases={2: 0},
        compiler_params=_cparams(("arbitrary",)),
        name="moe_dispatch",
    )(dest.reshape(nb, 1, tm * TOP_K), h1, xs_prev)


def _expert_kernel(meta_ref, x_ref, wu_ref, bu_ref, wd_ref, bd_ref, sel_ref, o_ref, wu_bf, wd_bf):
    j = pl.program_id(0)
    n_used = meta_ref[0]
    f2 = wu_ref.shape[1]

    @pl.when((j < n_used) & ((j == 0) | (meta_ref[1 + j] != meta_ref[jnp.maximum(j, 1)])))
    def _():
        sel = sel_ref[...]
        cw = sel.shape[0]
        for c in range(f2 // cw):
            grp = jnp.dot(wu_ref[:, c * cw:(c + 1) * cw].astype(BF16), sel, preferred_element_type=F32)
            wu_bf[:, c * cw:(c + 1) * cw] = grp.astype(BF16)
        wd_bf[...] = wd_ref[...].astype(BF16)

    @pl.when(j < n_used)
    def _():
        xb = x_ref[...].astype(BF16)
        hdn = jnp.dot(xb, wu_bf[...], preferred_element_type=F32) + bu_ref[...]
        cw = sel_ref.shape[0]
        half = cw // 2
        parts = []
        for c in range(f2 // cw):
            h_glu = jnp.minimum(hdn[:, c * cw:c * cw + half], SWIGLU_LIMIT)
            h_lin = jnp.clip(hdn[:, c * cw + half:(c + 1) * cw], -SWIGLU_LIMIT, SWIGLU_LIMIT)
            parts.append((h_glu * jax.nn.sigmoid(SWIGLU_ALPHA * h_glu) * (h_lin + 1.0)).astype(BF16))
        act_c = jnp.concatenate(parts, axis=1)
        o_ref[...] = jnp.dot(act_c, wd_bf[...], preferred_element_type=F32) + bd_ref[...]

    @pl.when(j >= n_used)
    def _():
        o_ref[...] = jnp.zeros(o_ref.shape, F32)


def _expert_ffn(xs, blk_expert, n_used, layer, w_up, b_up, w_down, b_down):
    n_rows, d = xs.shape
    n_blocks = n_rows // EXPERT_BLOCK
    f = D_EXPERT
    meta = jnp.concatenate([n_used.reshape(1).astype(I32), blk_expert.astype(I32)])
    sel_np = np.zeros((2 * LANES, 2 * LANES), np.float32)
    sel_np[2 * np.arange(LANES), np.arange(LANES)] = 1.0
    sel_np[2 * np.arange(LANES) + 1, LANES + np.arange(LANES)] = 1.0
    sel = jnp.asarray(sel_np, BF16)
    le = b_up.shape[:2]
    b_up = jnp.transpose(b_up.reshape(le + (f // LANES, LANES, 2)), (0, 1, 2, 4, 3)).reshape(le + (2 * f,))
    e_of = lambda j, m: m[1 + j]
    grid_spec = pltpu.PrefetchScalarGridSpec(
        num_scalar_prefetch=1,
        grid=(n_blocks,),
        in_specs=[
            pl.BlockSpec((EXPERT_BLOCK, d), lambda j, m: (j, 0)),
            pl.BlockSpec((None, None, d, 2 * f), lambda j, m: (layer, e_of(j, m), 0, 0)),
            pl.BlockSpec((None, None, 1, 2 * f), lambda j, m: (layer, e_of(j, m), 0, 0)),
            pl.BlockSpec((None, None, f, d), lambda j, m: (layer, e_of(j, m), 0, 0)),
            pl.BlockSpec((None, None, 1, d), lambda j, m: (layer, e_of(j, m), 0, 0)),
            pl.BlockSpec(sel.shape, lambda j, m: (0, 0)),
        ],
        out_specs=pl.BlockSpec((EXPERT_BLOCK, d), lambda j, m: (j, 0)),
        scratch_shapes=[pltpu.VMEM((d, 2 * f), BF16), pltpu.VMEM((f, d), BF16)],
    )
    return pl.pallas_call(
        _expert_kernel,
        out_shape=jax.ShapeDtypeStruct((n_rows, d), F32),
        grid_spec=grid_spec,
        compiler_params=_cparams(("arbitrary",)),
        name="moe_experts",
    )(meta, xs, w_up, b_up[:, :, None, :], w_down, b_down[:, :, None, :], sel)


def _combine_kernel(cur_ref, nxt_ref, ys_hbm, g4_ref, h_ref, lg_ref, lb_ref, o_ref, ybuf, sem, *, alpha, tm):
    j = pl.program_id(0)
    nblk = pl.num_programs(0)
    slot = j % 2
    n_rows = tm * TOP_K

    def gather(idx_ref, s):
        def body(r, carry):
            pltpu.make_async_copy(ys_hbm.at[pl.ds(idx_ref[0, r], 1), :],
                                  ybuf.at[s, pl.ds(r, 1), :], sem.at[s]).start()
            return carry
        lax.fori_loop(0, n_rows, body, 0, unroll=COMBINE_UNROLL)

    @pl.when(j == 0)
    def _():
        gather(cur_ref, 0)

    @pl.when(j + 1 < nblk)
    def _():
        gather(nxt_ref, 1 - slot)

    pltpu.make_async_copy(ys_hbm.at[pl.ds(0, n_rows), :], ybuf.at[slot], sem.at[slot]).wait()
    g4 = g4_ref[...]
    ffn = jnp.zeros((tm, h_ref.shape[1]), F32)
    for k in range(TOP_K):
        ffn = ffn + ybuf[slot, k * tm:(k + 1) * tm, :] * g4[:, k:k + 1]
    o_ref[...] = _layer_norm(alpha * h_ref[...] + ffn, lg_ref[...], lb_ref[...])


def _combine(ys, pos_km, gates_pad, h1, ln_g, ln_b, alpha, tm=ROWS_COMBINE):
    n, d = h1.shape
    nb = n // tm
    idx3 = pos_km.reshape(nb, 1, tm * TOP_K)
    return pl.pallas_call(
        functools.partial(_combine_kernel, alpha=alpha, tm=tm),
        out_shape=jax.ShapeDtypeStruct((n, d), F32),
        grid=(nb,),
        in_specs=[
            pl.BlockSpec((None, 1, tm * TOP_K), lambda j: (j, 0, 0), memory_space=pltpu.SMEM),
            pl.BlockSpec((None, 1, tm * TOP_K), lambda j: (jnp.minimum(j + 1, nb - 1), 0, 0),
                         memory_space=pltpu.SMEM),
            pl.BlockSpec(memory_space=pl.ANY),
            pl.BlockSpec((tm, LANES), lambda j: (j, 0)),
            pl.BlockSpec((tm, d), lambda j: (j, 0)),
            _const_spec((1, d)), _const_spec((1, d)),
        ],
        out_specs=pl.BlockSpec((tm, d), lambda j: (j, 0)),
        scratch_shapes=[pltpu.VMEM((2, tm * TOP_K, d), F32), pltpu.SemaphoreType.DMA((2,))],
        compiler_params=_cparams(("arbitrary",)),
        name="moe_combine",
    )(idx3, idx3, ys, gates_pad, h1, ln_g.reshape(1, d), ln_b.reshape(1, d))


def _moe_routing(top_e, rank, counts, n_tok, tm_combine):
    n_asg = n_tok * TOP_K
    counts = counts.reshape(N_EXPERTS).astype(I32)
    nblk_per = (counts + EXPERT_BLOCK - 1) // EXPERT_BLOCK
    blk_end = jnp.cumsum(nblk_per)
    pstarts = (blk_end - nblk_per) * EXPERT_BLOCK
    onehot = top_e[..., None] == jnp.arange(N_EXPERTS, dtype=I32)
    dest = jnp.sum(jnp.where(onehot, pstarts, 0), axis=-1) + rank
    n_blocks = -(-n_asg // EXPERT_BLOCK) + N_EXPERTS
    blk_expert = jnp.minimum(jnp.sum(blk_end[None, :] <= jnp.arange(n_blocks, dtype=I32)[:, None], axis=1),
                             N_EXPERTS - 1)
    n_used = blk_end[-1]
    nb = n_tok // tm_combine
    pos_km = jnp.transpose(dest.reshape(nb, tm_combine, TOP_K), (0, 2, 1)).reshape(nb, TOP_K * tm_combine)
    return dest.astype(I32), blk_expert.astype(I32), n_used.astype(I32), pos_km.astype(I32)


def _permute_w_in(w):
    sizes = (N_HEADS * HEAD_DIM, KV_LATENT, IDX_HEADS * IDX_DIM, IDX_DIM, IDX_HEADS,
             CONV_WIDTH, CONV_WIDTH, CONV_WIDTH, SSM_WIDTH, MEM_HEADS * MEM_HEAD_DIM)
    offs = np.cumsum((0,) + sizes)
    q, ckv, qidx, kidx, widx, cu, cgb, cgc, ssm, memq = [w[:, offs[k]:offs[k + 1]] for k in range(len(sizes))]
    gates = w[:, offs[-1]:]
    pad = jnp.zeros((w.shape[0], LANES - IDX_DIM - IDX_HEADS), w.dtype)
    return jnp.concatenate([gates, q, cu, cgb, cgc, ssm, memq, qidx, ckv, kidx, widx, pad], axis=1).astype(BF16)


def kernel(x, mem, ln_in_g, ln_in_b, w_in, kv_norm_g, w_uk, w_uv, conv_w, conv_b, lam_re, lam_im, b_re, b_im, c_re, c_im, d_skip, log_dt, w_glu, b_glu, w_mem_kv, w_branch, w_o, ln1_g, ln1_b, w_router, b_router, w_up, b_up, w_down, b_down, ln2_g, ln2_b):
    bsz, seq, d = x.shape
    depth = w_in.shape[0]
    mlen = mem.shape[1]
    n = bsz * seq
    alpha = float((2 * depth) ** 0.25)
    tm_combine = ROWS_COMBINE

    h = _ln_in(x.reshape(n, d), ln_in_g, ln_in_b)
    n_row_blocks = -(-n * TOP_K // EXPERT_BLOCK) + N_EXPERTS
    xs = jnp.zeros((n_row_blocks * EXPERT_BLOCK, d), F32)
    mem2 = mem.reshape(bsz * mlen, d)
    for l in range(depth):
        proj = _matmul(h, _permute_w_in(w_in[l]), ROWS_IN_PROJ, "in_proj")
        qcat, kcat, ckvn, ckvt = _prep(proj, kv_norm_g[l], bsz, seq)
        att = _dsa_attention(proj, qcat, kcat, ckvn, ckvt, w_uk[l], w_uv[l], kv_norm_g[l], bsz, seq)
        cnv = _short_conv(proj, conv_w[l], conv_b[l], bsz, seq)
        ssm = _s5_mixer(proj, lam_re[l], lam_im[l], b_re[l], b_im[l], c_re[l], c_im[l], d_skip[l], log_dt[l],
                        w_glu[l], b_glu[l], bsz, seq)
        kvm = _matmul(mem2, w_mem_kv[l].astype(BF16), min(256, bsz * mlen), "mem_kv")
        mem_o = _memory_attention(proj, kvm, bsz, seq, mlen)
        h1, te_pad, tg_pad, counts = _merge_router(att, cnv, ssm, mem_o, proj, h, w_branch[l], w_o[l],
                                                   ln1_g[l], ln1_b[l], w_router[l], b_router[l], alpha)
        dest, blk_expert, n_used, pos_km = _moe_routing(te_pad[:, :TOP_K], te_pad[:, TOP_K:2 * TOP_K], counts,
                                                        n, tm_combine)
        xs = _dispatch(h1, dest, xs)
        ys = _expert_ffn(xs, blk_expert, n_used, l, w_up, b_up, w_down, b_down)
        h = _combine(ys, pos_km, tg_pad, h1, ln2_g[l], ln2_b[l], alpha, tm_combine)
    return h.reshape(bsz, seq, d)
```

```python
import functools
import math

import numpy as np
import jax
import jax.numpy as jnp
from jax import lax
from jax.experimental import pallas as pl
from jax.experimental.pallas import tpu as pltpu

F32 = jnp.float32
BF16 = jnp.bfloat16
I32 = jnp.int32
I16 = jnp.int16

N_HEADS = 8
HEAD_DIM = 64
KV_LATENT = 128
IDX_HEADS = 8
IDX_DIM = 32
INDEX_TOPK = 256

ROWS_LN = 512
ROWS_IN_PROJ = 256
ROWS_PREP = 512
ROWS_CONV = 512
ROWS_MEM_ATT = 512
ROWS_MERGE = 512
ROWS_DISPATCH = 256
ROWS_COMBINE = 256
DSA_QUERY_BLOCK = 256
DSA_SCORE_CHUNK = 512
DSA_COUNT_CHUNK = 512
DSA_TILE_KEYS = 128
CONV_WIDTH = 512
SSM_WIDTH = 512
SSM_GROUP = 16
SSM_GROUPS = SSM_WIDTH // SSM_GROUP
SSM_STATE = 64
MEM_HEADS = 4
MEM_HEAD_DIM = 128
N_BRANCH = 4
BRANCH_WIDTH = 512
N_EXPERTS = 32
TOP_K = 4
D_EXPERT = 1024
SWIGLU_LIMIT = 7.0
SWIGLU_ALPHA = 1.702
EXPERT_BLOCK = 256
LN_EPS = 1e-5

LANES = 128
SUBLANES = 8
VMEM_LIMIT_BYTES = 56 * 1024 * 1024

INT_MIN = -(2 ** 31)
NEG_BIG = -1e30
SAFE_LOGIT = 100.0

COL_GATES = 0
COL_Q = 4096
COL_CONV_U = 4608
COL_CONV_GB = 5120
COL_CONV_GC = 5632
COL_SSM = 6144
COL_MEMQ = 6656
COL_QIDX = 7168
COL_CKV = 7424
COL_TAIL = 7552
D_PROJ = 7680
TAIL_W_OFF = IDX_DIM

SSM_SEQS = 8


def _cparams(sem):
    return pltpu.CompilerParams(dimension_semantics=sem, vmem_limit_bytes=VMEM_LIMIT_BYTES)


def _layer_norm(x, g, b):
    mu = jnp.mean(x, axis=-1, keepdims=True)
    xc = x - mu
    var = jnp.mean(xc * xc, axis=-1, keepdims=True)
    return xc * lax.rsqrt(var + LN_EPS) * g + b


def _const_spec(shape):
    nd = len(shape)
    return pl.BlockSpec(shape, lambda *_: (0,) * nd)


def _ln_kernel(x_ref, g_ref, b_ref, o_ref):
    o_ref[...] = _layer_norm(x_ref[...], g_ref[...], b_ref[...])


def _ln_in(x2, g, b, tm=ROWS_LN):
    n, d = x2.shape
    return pl.pallas_call(
        _ln_kernel,
        out_shape=jax.ShapeDtypeStruct((n, d), F32),
        grid=(n // tm,),
        in_specs=[pl.BlockSpec((tm, d), lambda i: (i, 0)), _const_spec((1, d)), _const_spec((1, d))],
        out_specs=pl.BlockSpec((tm, d), lambda i: (i, 0)),
        compiler_params=_cparams(("parallel",)),
        name="ln_in",
    )(x2, g.reshape(1, d), b.reshape(1, d))


def _matmul_kernel(x_ref, w_ref, o_ref):
    o_ref[...] = jnp.dot(x_ref[...].astype(BF16), w_ref[...], preferred_element_type=F32)


def _matmul(x, w_stack_bf16, layer, tm, name):
    n, k = x.shape
    m = w_stack_bf16.shape[2]
    return pl.pallas_call(
        _matmul_kernel,
        out_shape=jax.ShapeDtypeStruct((n, m), F32),
        grid=(n // tm,),
        in_specs=[pl.BlockSpec((tm, k), lambda i: (i, 0)),
                  pl.BlockSpec((None, k, m), lambda i: (layer, 0, 0), pipeline_mode=pl.Buffered(1))],
        out_specs=pl.BlockSpec((tm, m), lambda i: (i, 0)),
        compiler_params=_cparams(("parallel",)),
        name=name,
    )(x, w_stack_bf16)


def _split_hi_lo(x):
    hi = x.astype(BF16)
    lo = (x - hi.astype(F32)).astype(BF16)
    return hi, lo


def _prep_kernel(qidx_ref, ckv_ref, tail_ref, g_ref, sq_ref, sk_ref, qcat_ref, kcat_ref, ckvn_ref, ckvt_ref):
    q_hi, q_lo = _split_hi_lo(qidx_ref[...])
    qcat = jnp.dot(jnp.concatenate([q_hi, q_lo], axis=1), sq_ref[...], preferred_element_type=F32)
    qcat_ref[...] = qcat.astype(BF16)
    k_hi, k_lo = _split_hi_lo(tail_ref[...])
    kcat = jnp.dot(jnp.concatenate([k_hi, k_lo], axis=1), sk_ref[...], preferred_element_type=F32)
    kcat_ref[...] = kcat.astype(BF16)
    c = ckv_ref[...]
    ms = jnp.mean(c * c, axis=-1, keepdims=True)
    cn = c * lax.rsqrt(ms + LN_EPS) * g_ref[...]
    ckvn_ref[...] = cn.astype(BF16)
    ckvt_ref[...] = cn.T.astype(BF16)


def _selection_matrices():
    sq = np.zeros((2 * IDX_HEADS * IDX_DIM, IDX_HEADS * LANES), np.float32)
    for h in range(IDX_HEADS):
        for d in range(IDX_DIM):
            hi_in = h * IDX_DIM + d
            lo_in = IDX_HEADS * IDX_DIM + hi_in
            sq[hi_in, h * LANES + d] = 1.0
            sq[hi_in, h * LANES + IDX_DIM + d] = 1.0
            sq[lo_in, h * LANES + 2 * IDX_DIM + d] = 1.0
    sk = np.zeros((2 * LANES, LANES), np.float32)
    for d in range(IDX_DIM):
        sk[d, d] = 1.0
        sk[LANES + d, IDX_DIM + d] = 1.0
        sk[d, 2 * IDX_DIM + d] = 1.0
    return jnp.asarray(sq, BF16), jnp.asarray(sk, BF16)


def _prep(proj, kv_norm_g, bsz, seq, tm=ROWS_PREP):
    n = proj.shape[0]
    tm = min(tm, seq)
    nbt = seq // tm
    sq, sk = _selection_matrices()
    return pl.pallas_call(
        _prep_kernel,
        out_shape=(jax.ShapeDtypeStruct((n, IDX_HEADS * LANES), BF16),
                   jax.ShapeDtypeStruct((n, LANES), BF16),
                   jax.ShapeDtypeStruct((n, KV_LATENT), BF16),
                   jax.ShapeDtypeStruct((bsz, KV_LATENT, seq), BF16)),
        grid=(n // tm,),
        in_specs=[pl.BlockSpec((tm, 256), lambda i: (i, COL_QIDX // 256)),
                  pl.BlockSpec((tm, 128), lambda i: (i, COL_CKV // 128)),
                  pl.BlockSpec((tm, 128), lambda i: (i, COL_TAIL // 128)),
                  _const_spec((1, KV_LATENT)), _const_spec(sq.shape), _const_spec(sk.shape)],
        out_specs=(pl.BlockSpec((tm, IDX_HEADS * LANES), lambda i: (i, 0)),
                   pl.BlockSpec((tm, LANES), lambda i: (i, 0)),
                   pl.BlockSpec((tm, KV_LATENT), lambda i: (i, 0)),
                   pl.BlockSpec((None, KV_LATENT, tm), lambda i: (i // nbt, 0, i % nbt))),
        compiler_params=_cparams(("parallel",)),
        name="dsa_prep",
    )(proj, proj, proj, kv_norm_g.reshape(1, KV_LATENT), sq, sk)


def _dsa_kernel(q_ref, tail_ref, qcat_ref, kcat_ref, ckv_ref, ckvt_ref, wuk_ref, wuv_ref, kbound_ref, o_ref,
                key_ref, khi_ref, klo_ref, qct_ref, qlat_ref, m_ref, l_ref, acc_ref, *, seq, topk, cks, ck, cka, qb):
    i = pl.program_id(1)
    q0 = i * qb
    nchs = (q0 + qb + cks - 1) // cks
    nch = (q0 + qb + ck - 1) // ck
    q_pos = q0 + lax.broadcasted_iota(I32, (1, qb), 1)
    row_pos = lax.broadcasted_iota(I32, (ck, 1), 0)
    w_t = tail_ref[...].T
    cnt_rows = min(64, ck)

    for h in range(IDX_HEADS):
        qct_ref[h] = qcat_ref[:, h * LANES:(h + 1) * LANES].astype(F32).T.astype(BF16)
    row_pos_t = lax.broadcasted_iota(I32, (cka, 1), 0)

    def score_chunk(c, carry):
        for sub in range(cks // cka):
            k0 = pl.multiple_of(c * cks + sub * cka, cka)
            kc = kcat_ref[pl.ds(k0, cka), :]
            s = jnp.zeros((cka, qb), F32)
            for h in range(IDX_HEADS):
                d = jnp.dot(kc, qct_ref[h], preferred_element_type=F32)
                s = s + jnp.maximum(d, 0.0) * w_t[TAIL_W_OFF + h:TAIL_W_OFF + h + 1, :]
            s = s + 0.0
            bits = pltpu.bitcast(s, I32)
            key = bits ^ ((bits >> 31) & jnp.int32(0x7FFFFFFF))
            key = jnp.where(k0 + row_pos_t <= q_pos, key, jnp.int32(INT_MIN))
            key_ref[pl.ds(k0, cka), :] = key
            khi_ref[pl.ds(k0, cka), :] = (key >> 16).astype(I16)
            klo_ref[pl.ds(k0, cka), :] = ((key & jnp.int32(0xFFFF)) - 32768).astype(I16)
        return carry

    lax.fori_loop(0, nchs, score_chunk, 0)

    def count_keys(pred_fn):
        def body(c, acc):
            k0 = pl.multiple_of(c * ck, ck)
            hit = jnp.where(pred_fn(key_ref[pl.ds(k0, ck), :], k0), 1.0, 0.0)
            return acc + jnp.sum(hit.reshape(ck // cnt_rows, cnt_rows, qb), axis=0)
        acc = lax.fori_loop(0, nch, body, jnp.zeros((cnt_rows, qb), F32))
        return jnp.sum(acc, axis=0, keepdims=True)

    kf = float(topk)
    pack = 16
    n_part = ck // (2 * pack)
    lo16 = -32768

    def count16(ref, cand16):
        cand = jnp.broadcast_to(cand16, (pack, qb)).astype(I16)
        one = jnp.ones((), I16)
        zero = jnp.zeros((), I16)

        def body(c, acc):
            k0 = pl.multiple_of(c * ck, ck)
            hit = jnp.where(ref[pl.ds(k0, ck), :].reshape(ck // pack, pack, qb) >= cand[None], one, zero)
            part = hit[0:2]
            for j in range(1, n_part):
                part = part + hit[2 * j:2 * j + 2]
            return acc + part
        acc = lax.fori_loop(0, nch, body, jnp.zeros((2, pack, qb), I16)).astype(I32).astype(F32)
        return jnp.sum(jnp.sum(acc, axis=0), axis=0, keepdims=True)

    def bisect16(ref, target):
        def one_pass(b, carry):
            w, c_ge_w, c_gt_w = carry
            cand = w + jnp.left_shift(jnp.int32(1), 15 - b)
            cnt = count16(ref, cand)
            ok = cnt >= target
            return jnp.where(ok, cand, w), jnp.where(ok, cnt, c_ge_w), jnp.where(ok, c_gt_w, cnt)
        total = jnp.full((1, qb), 1.0, F32) * (nch * ck).astype(F32)
        return lax.fori_loop(0, 16, one_pass, (jnp.full((1, qb), lo16, I32), total, jnp.zeros((1, qb), F32)))

    v16, c_v16, c_above = bisect16(khi_ref, kf)
    v16_b = jnp.broadcast_to(v16, (pack, qb)).astype(I16)

    def build_lo(c, carry):
        k0 = pl.multiple_of(c * ck, ck)
        hi = khi_ref[pl.ds(k0, ck), :].reshape(ck // pack, pack, qb)
        lo = klo_ref[pl.ds(k0, ck), :].reshape(ck // pack, pack, qb)
        khi_ref[pl.ds(k0, ck), :] = jnp.where(hi == v16_b[None], lo, jnp.full((), lo16, I16)).reshape(ck, qb)
        return carry

    lax.fori_loop(0, nch, build_lo, 0)
    w, c_w, c_w_above = bisect16(khi_ref, kf - c_above)
    c_w = jnp.where(w == lo16, c_v16 - c_above, c_w)
    v = (v16 << 16) + (w + 32768)
    select_all = v == jnp.int32(INT_MIN)
    thr = jnp.maximum(v, jnp.int32(INT_MIN + 1))
    c_ge = jnp.where(select_all, c_above, c_above + c_w)
    c_gt = jnp.where(select_all, c_above, c_above + c_w_above)
    need = kf - c_gt

    jlim_ref = m_ref.at[0:1]
    jlim_ref[...] = jnp.full((1, qb), float(seq), F32)
    has_tie = jnp.max(jnp.where(c_ge > kf, 1.0, 0.0)) > 0.0

    @pl.when(has_tie)
    def _():
        n_bits = max(1, int(math.ceil(math.log2(seq))))

        def pos_pass(b, j):
            cand = j + jnp.left_shift(jnp.int32(1), n_bits - 1 - b)
            cnt = count_keys(lambda key, k0: (key == thr) & (k0 + row_pos < cand))
            return jnp.where(cnt < need, cand, j)

        j = lax.fori_loop(0, n_bits, pos_pass, jnp.zeros((1, qb), I32))
        jlim_ref[...] = jnp.where(c_ge > kf, j.astype(F32), float(seq))

    jlim = jlim_ref[...].astype(I32)

    q_bf = q_ref[...].astype(BF16)
    qn2_max = jnp.zeros((1, qb), F32)
    for j in range(N_HEADS // 2):
        ql = jnp.dot(q_bf[:, j * LANES:(j + 1) * LANES], wuk_ref[j],
                     preferred_element_type=F32) * (HEAD_DIM ** -0.5 * math.log2(math.e))
        for half in range(2):
            qt = ql[:, half * KV_LATENT:(half + 1) * KV_LATENT].T.astype(BF16)
            qlat_ref[2 * j + half] = qt
            qf = qt.astype(F32)
            qn2_max = jnp.maximum(qn2_max, jnp.sum(qf * qf, axis=0, keepdims=True))
    logit_bound = jnp.sqrt(jnp.max(qn2_max)) * kbound_ref[0, 0]
    row_pos_a = lax.broadcasted_iota(I32, (cka, 1), 0)

    def att_tiles(base, n_sub, online):
        for sub in range(n_sub):
            k0 = pl.multiple_of(base + sub * cka, cka)
            kv = ckv_ref[pl.ds(k0, cka), :]
            kvt = ckvt_ref[:, pl.ds(k0, cka)]
            key = key_ref[pl.ds(k0, cka), :]
            sel = (key > thr) | ((key == thr) & (k0 + row_pos_a <= jlim))
            bias = jnp.where(sel, 0.0, NEG_BIG)
            for h in range(N_HEADS):
                lg = jnp.dot(kv, qlat_ref[h], preferred_element_type=F32) + bias
                if not online:
                    p = jnp.exp2(lg)
                    l_ref[h:h + 1, :] = l_ref[h:h + 1, :] + jnp.sum(p, axis=0, keepdims=True)
                    acc_ref[h] = acc_ref[h] + jnp.dot(kvt, p.astype(BF16), preferred_element_type=F32)
                else:
                    m_prev = m_ref[h:h + 1, :]
                    m_new = jnp.maximum(m_prev, jnp.max(lg, axis=0, keepdims=True))
                    p = jnp.exp2(lg - m_new)
                    alpha = jnp.exp2(m_prev - m_new)
                    l_ref[h:h + 1, :] = alpha * l_ref[h:h + 1, :] + jnp.sum(p, axis=0, keepdims=True)
                    acc_ref[h] = alpha * acc_ref[h] + jnp.dot(kvt, p.astype(BF16), preferred_element_type=F32)
                    m_ref[h:h + 1, :] = m_new

    def attend(online):
        def att_chunk(c, carry):
            att_tiles(c * cks, cks // cka, online)
            return carry

        end = q0 + qb
        n_full = end // cks
        lax.fori_loop(0, n_full, att_chunk, 0)
        if cks % qb == 0 and cks > qb:
            for parts in range(1, cks // qb):
                @pl.when(end - n_full * cks == parts * qb)
                def _():
                    att_tiles(n_full * cks, parts * qb // cka, online)
        else:
            @pl.when(end - n_full * cks > 0)
            def _():
                att_tiles(n_full * cks, cks // cka, online)

    m_ref[...] = jnp.full(m_ref.shape, NEG_BIG, F32)
    l_ref[...] = jnp.zeros(l_ref.shape, F32)
    acc_ref[...] = jnp.zeros(acc_ref.shape, F32)

    @pl.when(logit_bound < SAFE_LOGIT)
    def _():
        attend(online=False)

    @pl.when(jnp.logical_not(logit_bound < SAFE_LOGIT))
    def _():
        attend(online=True)

    outs = []
    for j in range(N_HEADS // 2):
        o_pair = [(acc_ref[h] / l_ref[h:h + 1, :]).T.astype(BF16) for h in (2 * j, 2 * j + 1)]
        outs.append(jnp.dot(jnp.concatenate(o_pair, axis=1), wuv_ref[j], preferred_element_type=F32))
    o_ref[...] = jnp.concatenate(outs, axis=1)


def _dsa_attention(proj, qcat, kcat, ckvn, ckvt, w_uk, w_uv, kv_norm_g, bsz, seq):
    n = proj.shape[0]
    qb = min(DSA_QUERY_BLOCK, seq)
    n_blk = seq // qb
    topk = min(INDEX_TOPK, seq // 4)
    cks = min(DSA_SCORE_CHUNK, seq)
    ck = min(DSA_COUNT_CHUNK, seq)
    cka = min(DSA_TILE_KEYS, seq)
    zk = jnp.zeros((HEAD_DIM, KV_LATENT), F32)
    zv = jnp.zeros((KV_LATENT, HEAD_DIM), F32)
    wuk = jnp.stack([jnp.block([[w_uk[:, 2 * j, :].T, zk], [zk, w_uk[:, 2 * j + 1, :].T]])
                     for j in range(N_HEADS // 2)]).astype(BF16)
    wuv = jnp.stack([jnp.block([[w_uv[:, 2 * j, :], zv], [zv, w_uv[:, 2 * j + 1, :]]])
                     for j in range(N_HEADS // 2)]).astype(BF16)
    kbound = (1.01 * math.sqrt(KV_LATENT) * jnp.max(jnp.abs(kv_norm_g))).reshape(1, 1).astype(F32)
    kernel = functools.partial(_dsa_kernel, seq=seq, topk=topk, cks=cks, ck=ck, cka=cka, qb=qb)
    return pl.pallas_call(
        kernel,
        out_shape=jax.ShapeDtypeStruct((n, N_HEADS * HEAD_DIM), F32),
        grid=(bsz, n_blk),
        in_specs=[pl.BlockSpec((qb, 512), lambda b, i: (b * n_blk + i, COL_Q // 512)),
                  pl.BlockSpec((qb, 128), lambda b, i: (b * n_blk + i, COL_TAIL // 128)),
                  pl.BlockSpec((qb, IDX_HEADS * LANES), lambda b, i: (b * n_blk + i, 0)),
                  pl.BlockSpec((seq, LANES), lambda b, i: (b, 0)),
                  pl.BlockSpec((seq, KV_LATENT), lambda b, i: (b, 0)),
                  pl.BlockSpec((None, KV_LATENT, seq), lambda b, i: (b, 0, 0)),
                  _const_spec(wuk.shape), _const_spec(wuv.shape),
                  pl.BlockSpec(memory_space=pltpu.SMEM)],
        out_specs=pl.BlockSpec((qb, N_HEADS * HEAD_DIM), lambda b, i: (b * n_blk + i, 0)),
        scratch_shapes=[pltpu.VMEM((seq, qb), I32),
                        pltpu.VMEM((seq, qb), I16),
                        pltpu.VMEM((seq, qb), I16),
                        pltpu.VMEM((IDX_HEADS, LANES, qb), BF16),
                        pltpu.VMEM((N_HEADS, KV_LATENT, qb), BF16),
                        pltpu.VMEM((N_HEADS, qb), F32),
                        pltpu.VMEM((N_HEADS, qb), F32),
                        pltpu.VMEM((N_HEADS, KV_LATENT, qb), F32)],
        compiler_params=_cparams(("parallel", "parallel")),
        name="dsa_attention",
    )(proj, proj, qcat, kcat, ckvn, ckvt, wuk, wuv, kbound)


def _conv_kernel(u_ref, gb_ref, gc_ref, pu_ref, pgc_ref, w_ref, b_ref, o_ref):
    i = pl.program_id(1)
    tb = u_ref.shape[0]
    v = gc_ref[...] * u_ref[...]
    halo = jnp.where(i > 0, pgc_ref[...] * pu_ref[...], 0.0)
    vfull = jnp.concatenate([halo, v], axis=0)
    v1 = pltpu.roll(vfull, 1, 0)[SUBLANES:SUBLANES + tb]
    v2 = pltpu.roll(vfull, 2, 0)[SUBLANES:SUBLANES + tb]
    w = w_ref[...]
    y = w[0:1] * v2 + w[1:2] * v1 + w[2:3] * v
    o_ref[...] = gb_ref[...] * (y + b_ref[...])


def _short_conv(proj, conv_w, conv_b, bsz, seq, tb=ROWS_CONV):
    n = proj.shape[0]
    tb = min(tb, seq)
    nb = seq // tb
    cw = CONV_WIDTH
    blk = lambda col: pl.BlockSpec((tb, cw), lambda b, i: (b * nb + i, col // cw))
    prev = lambda col: pl.BlockSpec(
        (SUBLANES, cw), lambda b, i: (jnp.maximum((b * nb + i) * (tb // SUBLANES) - 1, 0), col // cw))
    return pl.pallas_call(
        _conv_kernel,
        out_shape=jax.ShapeDtypeStruct((n, cw), F32),
        grid=(bsz, nb),
        in_specs=[blk(COL_CONV_U), blk(COL_CONV_GB), blk(COL_CONV_GC), prev(COL_CONV_U), prev(COL_CONV_GC),
                  _const_spec((SUBLANES, cw)), _const_spec((1, cw))],
        out_specs=pl.BlockSpec((tb, cw), lambda b, i: (b * nb + i, 0)),
        compiler_params=_cparams(("parallel", "parallel")),
        name="short_conv",
    )(proj, proj, proj, proj, proj,
      jnp.pad(conv_w, ((0, SUBLANES - conv_w.shape[0]), (0, 0))), conv_b.reshape(1, cw))


def _s5_scan_chunk(bu_ref, a_ref, st_ref, n_steps, store):
    s_tot = a_ref.shape[1] // 2
    cb = 512
    for blk in range(s_tot // cb):
        re_sl = pl.ds(blk * cb, cb)
        im_sl = pl.ds(s_tot + blk * cb, cb)
        ar = jnp.broadcast_to(a_ref[0:1, blk * cb:(blk + 1) * cb], (SSM_SEQS, cb))
        ai = jnp.broadcast_to(a_ref[0:1, s_tot + blk * cb:s_tot + (blk + 1) * cb], (SSM_SEQS, cb))

        def step(t, carry):
            re, im = carry
            r0 = pl.multiple_of(t * SSM_SEQS, SSM_SEQS)
            br = bu_ref[pl.ds(r0, SSM_SEQS), re_sl]
            bi = bu_ref[pl.ds(r0, SSM_SEQS), im_sl]
            nre = ar * re - ai * im + br
            nim = ar * im + ai * re + bi
            if store:
                bu_ref[pl.ds(r0, SSM_SEQS), re_sl] = nre
                bu_ref[pl.ds(r0, SSM_SEQS), im_sl] = nim
            return nre, nim

        re, im = lax.fori_loop(0, n_steps, step, (st_ref[:, re_sl], st_ref[:, im_sl]), unroll=8)
        st_ref[:, re_sl] = re
        st_ref[:, im_sl] = im


def _gelu_tanh(x):
    return 0.5 * x * (1.0 + jnp.tanh(math.sqrt(2.0 / math.pi) * (x + 0.044715 * (x * x * x))))


S5_ROWS = 512


def _s5_kernel(u_ref, perm_ref, wb_ref, a_ref, apow_ref, wc_ref, d_ref, wg_ref, bg_ref, o_ref,
               bu_ref, st_ref, carry_ref, *, seg_len, steps_per_seq):
    c = pl.program_id(0)
    s_tot = a_ref.shape[1] // 2
    tn_dims = (((0,), (0,)), ((), ()))

    @pl.when(c % steps_per_seq == 0)
    def _():
        carry_ref[...] = jnp.zeros(carry_ref.shape, F32)

    u = u_ref[...]
    perm = perm_ref[...]
    u_il = jnp.dot(perm, u.astype(BF16), preferred_element_type=F32).astype(BF16)
    hw = u_il.shape[1] // 2
    hs = s_tot // 2
    for part in range(2):
        for q in range(2):
            c0 = part * s_tot + q * hs
            bu_ref[:, c0:c0 + hs] = jnp.dot(u_il[:, q * hw:(q + 1) * hw], wb_ref[q * hw:(q + 1) * hw, c0:c0 + hs],
                                            preferred_element_type=F32)

    st_ref[...] = jnp.zeros(st_ref.shape, F32)
    _s5_scan_chunk(bu_ref, a_ref, st_ref, seg_len, store=False)

    pr = apow_ref[0:1, :s_tot]
    pi = apow_ref[0:1, s_tot:]
    hre = carry_ref[0:1, :s_tot]
    him = carry_ref[0:1, s_tot:]
    for j in range(SSM_SEQS):
        ere = st_ref[j:j + 1, :s_tot]
        eim = st_ref[j:j + 1, s_tot:]
        st_ref[j:j + 1, :s_tot] = hre
        st_ref[j:j + 1, s_tot:] = him
        hre, him = pr * hre - pi * him + ere, pr * him + pi * hre + eim
    carry_ref[0:1, :s_tot] = hre
    carry_ref[0:1, s_tot:] = him

    _s5_scan_chunk(bu_ref, a_ref, st_ref, seg_len, store=True)
    halves = []
    for q in range(2):
        acc = None
        for part in range(2):
            c0 = part * s_tot + q * hs
            term = jnp.dot(bu_ref[:, c0:c0 + hs].astype(BF16), wc_ref[c0:c0 + hs, q * hw:(q + 1) * hw],
                           preferred_element_type=F32)
            acc = term if acc is None else acc + term
        halves.append(acc)
    ch = jnp.concatenate(halves, axis=1)
    ch_hi, ch_lo = _split_hi_lo(ch)
    ch_nat = (lax.dot_general(perm, ch_hi, tn_dims, preferred_element_type=F32)
              + lax.dot_general(perm, ch_lo, tn_dims, preferred_element_type=F32))
    z = _gelu_tanh(ch_nat + d_ref[...] * u)
    gate = jnp.dot(z.astype(BF16), wg_ref[...], preferred_element_type=F32) + bg_ref[...]
    o_ref[...] = z * jax.nn.sigmoid(gate)


def _s5_mixer(proj, lam_re, lam_im, b_re, b_im, c_re, c_im, d_skip, log_dt, w_glu, b_glu, bsz, seq):
    n = proj.shape[0]
    g, p, nn = SSM_GROUPS, SSM_STATE, SSM_GROUP
    s_tot = g * p
    rows = min(S5_ROWS, seq)
    seg_len = rows // SSM_SEQS

    dt = jnp.exp(log_dt.astype(F32))[:, None]
    lam = lax.complex(lam_re.astype(F32), lam_im.astype(F32))
    a_bar = jnp.exp(lam * dt)
    b_bar = ((a_bar - 1.0) / lam)[:, :, None] * lax.complex(b_re.astype(F32), b_im.astype(F32))
    a_pow = jnp.exp(lam * dt * seg_len)
    a_vec = jnp.concatenate([jnp.real(a_bar).reshape(1, s_tot), jnp.imag(a_bar).reshape(1, s_tot)], axis=1)
    apow_vec = jnp.concatenate([jnp.real(a_pow).reshape(1, s_tot), jnp.imag(a_pow).reshape(1, s_tot)], axis=1)
    eye = jnp.eye(g, dtype=F32)
    wb = jnp.concatenate(
        [jnp.einsum('gpn,gh->gnhp', jnp.real(b_bar), eye).reshape(g * nn, s_tot),
         jnp.einsum('gpn,gh->gnhp', jnp.imag(b_bar), eye).reshape(g * nn, s_tot)], axis=1).astype(BF16)
    wc = jnp.concatenate(
        [jnp.einsum('gnp,gh->gphn', c_re.astype(F32), eye).reshape(s_tot, g * nn),
         -jnp.einsum('gnp,gh->gphn', c_im.astype(F32), eye).reshape(s_tot, g * nn)], axis=0).astype(BF16)
    perm_np = np.zeros((rows, rows), np.float32)
    t_idx, j_idx = np.meshgrid(np.arange(seg_len), np.arange(SSM_SEQS), indexing='ij')
    perm_np[(t_idx * SSM_SEQS + j_idx).ravel(), (j_idx * seg_len + t_idx).ravel()] = 1.0
    perm = jnp.asarray(perm_np, BF16)

    return pl.pallas_call(
        functools.partial(_s5_kernel, seg_len=seg_len, steps_per_seq=seq // rows),
        out_shape=jax.ShapeDtypeStruct((n, SSM_WIDTH), F32),
        grid=(n // rows,),
        in_specs=[pl.BlockSpec((rows, SSM_WIDTH), lambda c: (c, COL_SSM // SSM_WIDTH)),
                  _const_spec(perm.shape), _const_spec(wb.shape), _const_spec(a_vec.shape),
                  _const_spec(apow_vec.shape), _const_spec(wc.shape), _const_spec((1, SSM_WIDTH)),
                  _const_spec((SSM_WIDTH, SSM_WIDTH)), _const_spec((1, SSM_WIDTH))],
        out_specs=pl.BlockSpec((rows, SSM_WIDTH), lambda c: (c, 0)),
        scratch_shapes=[pltpu.VMEM((rows, 2 * s_tot), F32), pltpu.VMEM((SSM_SEQS, 2 * s_tot), F32),
                        pltpu.VMEM((1, 2 * s_tot), F32)],
        compiler_params=_cparams(("arbitrary",)),
        name="s5_mixer",
    )(proj, perm, wb, a_vec, apow_vec, wc, d_skip.reshape(1, SSM_WIDTH).astype(F32),
      w_glu.astype(BF16), b_glu.reshape(1, SSM_WIDTH))


def _mem_attn_kernel(q_ref, kv_ref, o_ref):
    hw = MEM_HEADS * MEM_HEAD_DIM
    q = q_ref[...].astype(BF16)
    kv = kv_ref[...].astype(BF16)
    for h in range(MEM_HEADS):
        sl = slice(h * MEM_HEAD_DIM, (h + 1) * MEM_HEAD_DIM)
        k = kv[:, sl]
        v = kv[:, hw + h * MEM_HEAD_DIM:hw + (h + 1) * MEM_HEAD_DIM]
        lg = lax.dot_general(q[:, sl], k, (((1,), (1,)), ((), ())),
                             preferred_element_type=F32) * (MEM_HEAD_DIM ** -0.5)
        m = jnp.max(lg, axis=1, keepdims=True)
        p = jnp.exp(lg - m)
        p = p / jnp.sum(p, axis=1, keepdims=True)
        o_ref[:, sl] = jnp.dot(p.astype(BF16), v, preferred_element_type=F32)


def _memory_attention(proj, kvm, bsz, seq, mlen, tm=ROWS_MEM_ATT):
    n = proj.shape[0]
    tm = min(tm, seq)
    nb = seq // tm
    hw = MEM_HEADS * MEM_HEAD_DIM
    return pl.pallas_call(
        _mem_attn_kernel,
        out_shape=jax.ShapeDtypeStruct((n, hw), F32),
        grid=(bsz, nb),
        in_specs=[pl.BlockSpec((tm, hw), lambda b, i: (b * nb + i, COL_MEMQ // hw)),
                  pl.BlockSpec((mlen, 2 * hw), lambda b, i: (b, 0))],
        out_specs=pl.BlockSpec((tm, hw), lambda b, i: (b * nb + i, 0)),
        compiler_params=_cparams(("parallel", "parallel")),
        name="memory_attention",
    )(proj, kvm)


def _merge_kernel(att_ref, cnv_ref, ssm_ref, mem_ref, gl_ref, h_ref, wbr_ref, wo_ref, g_ref, b_ref,
                  wr_ref, br_ref, ltri_ref, h1_ref, te_ref, tg_ref, cnt_ref, base_ref, *, alpha):
    d = h_ref.shape[1]

    @pl.when(pl.program_id(0) == 0)
    def _():
        base_ref[...] = jnp.zeros(base_ref.shape, F32)

    merged = jnp.zeros(h_ref.shape, F32)
    for r, br in enumerate((att_ref, cnv_ref, ssm_ref, mem_ref)):
        y = jnp.dot(br[...].astype(BF16), wbr_ref[r], preferred_element_type=F32)
        merged = merged + y * jax.nn.sigmoid(gl_ref[:, r * d:(r + 1) * d])
    y = alpha * h_ref[...] + jnp.dot(merged.astype(BF16), wo_ref[...], preferred_element_type=F32)
    h1 = _layer_norm(y, g_ref[...], b_ref[...])
    h1_ref[...] = h1

    logits = jnp.dot(h1, wr_ref[...], preferred_element_type=F32, precision=lax.Precision.HIGHEST) + br_ref[...]
    tm = logits.shape[0]
    lane = lax.broadcasted_iota(I32, (tm, N_EXPERTS), 1)
    out_lane = lax.broadcasted_iota(I32, (tm, LANES), 1)
    work = logits
    top_e = jnp.zeros((tm, LANES), I32)
    top_v = jnp.zeros((tm, LANES), F32)
    vals = []
    onehots = []
    for k in range(TOP_K):
        mx = jnp.max(work, axis=1, keepdims=True)
        idx = jnp.min(jnp.where(work == mx, lane, N_EXPERTS), axis=1, keepdims=True)
        hit = lane == idx
        work = jnp.where(hit, -jnp.inf, work)
        top_e = jnp.where(out_lane == k, idx, top_e)
        vals.append(mx)
        onehots.append(jnp.where(hit, 1.0, 0.0))
    den = sum(jnp.exp(vk - vals[0]) for vk in vals)
    for k in range(TOP_K):
        top_v = jnp.where(out_lane == k, jnp.exp(vals[k] - vals[0]) / den, top_v)

    per_tok = onehots[0] + onehots[1] + onehots[2] + onehots[3]
    before = jnp.dot(ltri_ref[...], per_tok.astype(BF16), preferred_element_type=F32) + base_ref[...]
    for k in range(TOP_K):
        rank = jnp.sum(onehots[k] * before, axis=1, keepdims=True)
        top_e = jnp.where(out_lane == TOP_K + k, rank.astype(I32), top_e)
    base_ref[...] = base_ref[...] + jnp.sum(per_tok, axis=0, keepdims=True)
    cnt_ref[...] = base_ref[...]
    te_ref[...] = top_e
    tg_ref[...] = top_v


def _merge_router(att, cnv, ssm, mem_o, proj, h, w_branch, w_o, ln_g, ln_b, w_router, b_router, alpha, tm=ROWS_MERGE):
    n, d = h.shape
    bw = BRANCH_WIDTH
    row = lambda w: pl.BlockSpec((tm, w), lambda i: (i, 0))
    ltri = jnp.asarray(np.tril(np.ones((tm, tm), np.float32), -1), BF16)
    return pl.pallas_call(
        functools.partial(_merge_kernel, alpha=alpha),
        out_shape=(jax.ShapeDtypeStruct((n, d), F32),
                   jax.ShapeDtypeStruct((n, LANES), I32),
                   jax.ShapeDtypeStruct((n, LANES), F32),
                   jax.ShapeDtypeStruct((1, N_EXPERTS), F32)),
        grid=(n // tm,),
        in_specs=[row(bw), row(bw), row(bw), row(bw),
                  pl.BlockSpec((tm, N_BRANCH * d), lambda i: (i, COL_GATES // (N_BRANCH * d))),
                  row(d),
                  _const_spec((N_BRANCH, bw, d)), _const_spec((d, d)), _const_spec((1, d)), _const_spec((1, d)),
                  _const_spec((d, N_EXPERTS)), _const_spec((1, N_EXPERTS)), _const_spec((tm, tm))],
        out_specs=(row(d), row(LANES), row(LANES), _const_spec((1, N_EXPERTS))),
        scratch_shapes=[pltpu.VMEM((1, N_EXPERTS), F32)],
        compiler_params=_cparams(("arbitrary",)),
        name="merge_router",
    )(att, cnv, ssm, mem_o, proj, h, w_branch.astype(BF16), w_o.astype(BF16),
      ln_g.reshape(1, d), ln_b.reshape(1, d), w_router, b_router.reshape(1, N_EXPERTS), ltri)


DISPATCH_UNROLL = 8
COMBINE_UNROLL = 16


def _dispatch_kernel(dest_ref, h_ref, xs_in_hbm, xs_hbm, sem, *, tm):
    del xs_in_hbm

    def body(r, carry):
        for k in range(TOP_K):
            pltpu.make_async_copy(h_ref.at[pl.ds(r, 1), :],
                                  xs_hbm.at[pl.ds(dest_ref[0, r * TOP_K + k], 1), :], sem).start()
        return carry

    lax.fori_loop(0, tm, body, 0, unroll=DISPATCH_UNROLL)
    pltpu.make_async_copy(xs_hbm.at[pl.ds(0, tm * TOP_K), :], xs_hbm.at[pl.ds(0, tm * TOP_K), :], sem).wait()


def _dispatch(h1, dest, xs_prev, tm=ROWS_DISPATCH):
    n, d = h1.shape
    nb = n // tm
    return pl.pallas_call(
        functools.partial(_dispatch_kernel, tm=tm),
        out_shape=jax.ShapeDtypeStruct(xs_prev.shape, F32),
        grid=(nb,),
        in_specs=[pl.BlockSpec((None, 1, tm * TOP_K), lambda i: (i, 0, 0), memory_space=pltpu.SMEM),
                  pl.BlockSpec((tm, d), lambda i: (i, 0)),
                  pl.BlockSpec(memory_space=pl.ANY)],
        out_specs=pl.BlockSpec(memory_space=pl.ANY),
        scratch_shapes=[pltpu.SemaphoreType.DMA(())],
        input_output_aliases={2: 0},
        compiler_params=_cparams(("arbitrary",)),
        name="moe_dispatch",
    )(dest.reshape(nb, 1, tm * TOP_K), h1, xs_prev)


def _expert_kernel(meta_ref, x_ref, wu_ref, bu_ref, wd_ref, bd_ref, sel_ref, o_ref, wu_bf, wd_bf):
    j = pl.program_id(0)
    n_used = meta_ref[0]
    f2 = wu_ref.shape[1]

    @pl.when((j < n_used) & ((j == 0) | (meta_ref[1 + j] != meta_ref[jnp.maximum(j, 1)])))
    def _():
        sel = sel_ref[...]
        cw = sel.shape[0]
        for c in range(f2 // cw):
            grp = jnp.dot(wu_ref[:, c * cw:(c + 1) * cw].astype(BF16), sel, preferred_element_type=F32)
            wu_bf[:, c * cw:(c + 1) * cw] = grp.astype(BF16)
        wd_bf[...] = wd_ref[...].astype(BF16)

    @pl.when(j < n_used)
    def _():
        xb = x_ref[...].astype(BF16)
        hdn = jnp.dot(xb, wu_bf[...], preferred_element_type=F32) + bu_ref[...]
        cw = sel_ref.shape[0]
        half = cw // 2
        parts = []
        for c in range(f2 // cw):
            h_glu = jnp.minimum(hdn[:, c * cw:c * cw + half], SWIGLU_LIMIT)
            h_lin = jnp.clip(hdn[:, c * cw + half:(c + 1) * cw], -SWIGLU_LIMIT, SWIGLU_LIMIT)
            parts.append((h_glu * jax.nn.sigmoid(SWIGLU_ALPHA * h_glu) * (h_lin + 1.0)).astype(BF16))
        act_c = jnp.concatenate(parts, axis=1)
        o_ref[...] = jnp.dot(act_c, wd_bf[...], preferred_element_type=F32) + bd_ref[...]

    @pl.when(j >= n_used)
    def _():
        o_ref[...] = jnp.zeros(o_ref.shape, F32)


def _expert_ffn(xs, blk_expert, n_used, layer, w_up, b_up, w_down, b_down):
    n_rows, d = xs.shape
    n_blocks = n_rows // EXPERT_BLOCK
    f = D_EXPERT
    meta = jnp.concatenate([n_used.reshape(1).astype(I32), blk_expert.astype(I32)])
    sel_np = np.zeros((2 * LANES, 2 * LANES), np.float32)
    sel_np[2 * np.arange(LANES), np.arange(LANES)] = 1.0
    sel_np[2 * np.arange(LANES) + 1, LANES + np.arange(LANES)] = 1.0
    sel = jnp.asarray(sel_np, BF16)
    le = b_up.shape[:2]
    b_up = jnp.transpose(b_up.reshape(le + (f // LANES, LANES, 2)), (0, 1, 2, 4, 3)).reshape(le + (2 * f,))
    e_of = lambda j, m: m[1 + j]
    grid_spec = pltpu.PrefetchScalarGridSpec(
        num_scalar_prefetch=1,
        grid=(n_blocks,),
        in_specs=[
            pl.BlockSpec((EXPERT_BLOCK, d), lambda j, m: (j, 0)),
            pl.BlockSpec((None, None, d, 2 * f), lambda j, m: (layer, e_of(j, m), 0, 0)),
            pl.BlockSpec((None, None, 1, 2 * f), lambda j, m: (layer, e_of(j, m), 0, 0)),
            pl.BlockSpec((None, None, f, d), lambda j, m: (layer, e_of(j, m), 0, 0)),
            pl.BlockSpec((None, None, 1, d), lambda j, m: (layer, e_of(j, m), 0, 0)),
            pl.BlockSpec(sel.shape, lambda j, m: (0, 0)),
        ],
        out_specs=pl.BlockSpec((EXPERT_BLOCK, d), lambda j, m: (j, 0)),
        scratch_shapes=[pltpu.VMEM((d, 2 * f), BF16), pltpu.VMEM((f, d), BF16)],
    )
    return pl.pallas_call(
        _expert_kernel,
        out_shape=jax.ShapeDtypeStruct((n_rows, d), F32),
        grid_spec=grid_spec,
        compiler_params=_cparams(("arbitrary",)),
        name="moe_experts",
    )(meta, xs, w_up, b_up[:, :, None, :], w_down, b_down[:, :, None, :], sel)


def _combine_kernel(cur_ref, nxt_ref, ys_hbm, g4_ref, h_ref, lg_ref, lb_ref, o_ref, ybuf, sem, *, alpha, tm):
    j = pl.program_id(0)
    nblk = pl.num_programs(0)
    slot = j % 2
    n_rows = tm * TOP_K

    def gather(idx_ref, s):
        def body(r, carry):
            pltpu.make_async_copy(ys_hbm.at[pl.ds(idx_ref[0, r], 1), :],
                                  ybuf.at[s, pl.ds(r, 1), :], sem.at[s]).start()
            return carry
        lax.fori_loop(0, n_rows, body, 0, unroll=COMBINE_UNROLL)

    @pl.when(j == 0)
    def _():
        gather(cur_ref, 0)

    @pl.when(j + 1 < nblk)
    def _():
        gather(nxt_ref, 1 - slot)

    pltpu.make_async_copy(ys_hbm.at[pl.ds(0, n_rows), :], ybuf.at[slot], sem.at[slot]).wait()
    g4 = g4_ref[...]
    ffn = jnp.zeros((tm, h_ref.shape[1]), F32)
    for k in range(TOP_K):
        ffn = ffn + ybuf[slot, k * tm:(k + 1) * tm, :] * g4[:, k:k + 1]
    o_ref[...] = _layer_norm(alpha * h_ref[...] + ffn, lg_ref[...], lb_ref[...])


def _combine(ys, pos_km, gates_pad, h1, ln_g, ln_b, alpha, tm=ROWS_COMBINE):
    n, d = h1.shape
    nb = n // tm
    idx3 = pos_km.reshape(nb, 1, tm * TOP_K)
    return pl.pallas_call(
        functools.partial(_combine_kernel, alpha=alpha, tm=tm),
        out_shape=jax.ShapeDtypeStruct((n, d), F32),
        grid=(nb,),
        in_specs=[
            pl.BlockSpec((None, 1, tm * TOP_K), lambda j: (j, 0, 0), memory_space=pltpu.SMEM),
            pl.BlockSpec((None, 1, tm * TOP_K), lambda j: (jnp.minimum(j + 1, nb - 1), 0, 0),
                         memory_space=pltpu.SMEM),
            pl.BlockSpec(memory_space=pl.ANY),
            pl.BlockSpec((tm, LANES), lambda j: (j, 0)),
            pl.BlockSpec((tm, d), lambda j: (j, 0)),
            _const_spec((1, d)), _const_spec((1, d)),
        ],
        out_specs=pl.BlockSpec((tm, d), lambda j: (j, 0)),
        scratch_shapes=[pltpu.VMEM((2, tm * TOP_K, d), F32), pltpu.SemaphoreType.DMA((2,))],
        compiler_params=_cparams(("arbitrary",)),
        name="moe_combine",
    )(idx3, idx3, ys, gates_pad, h1, ln_g.reshape(1, d), ln_b.reshape(1, d))


def _moe_routing(top_e, rank, counts, n_tok, tm_combine):
    n_asg = n_tok * TOP_K
    counts = counts.reshape(N_EXPERTS).astype(I32)
    nblk_per = (counts + EXPERT_BLOCK - 1) // EXPERT_BLOCK
    blk_end = jnp.cumsum(nblk_per)
    pstarts = (blk_end - nblk_per) * EXPERT_BLOCK
    onehot = top_e[..., None] == jnp.arange(N_EXPERTS, dtype=I32)
    dest = jnp.sum(jnp.where(onehot, pstarts, 0), axis=-1) + rank
    n_blocks = -(-n_asg // EXPERT_BLOCK) + N_EXPERTS
    blk_expert = jnp.minimum(jnp.sum(blk_end[None, :] <= jnp.arange(n_blocks, dtype=I32)[:, None], axis=1),
                             N_EXPERTS - 1)
    n_used = blk_end[-1]
    nb = n_tok // tm_combine
    pos_km = jnp.transpose(dest.reshape(nb, tm_combine, TOP_K), (0, 2, 1)).reshape(nb, TOP_K * tm_combine)
    return dest.astype(I32), blk_expert.astype(I32), n_used.astype(I32), pos_km.astype(I32)


def _permute_w_in(w):
    sizes = (N_HEADS * HEAD_DIM, KV_LATENT, IDX_HEADS * IDX_DIM, IDX_DIM, IDX_HEADS,
             CONV_WIDTH, CONV_WIDTH, CONV_WIDTH, SSM_WIDTH, MEM_HEADS * MEM_HEAD_DIM)
    offs = np.cumsum((0,) + sizes)
    q, ckv, qidx, kidx, widx, cu, cgb, cgc, ssm, memq = [w[..., offs[k]:offs[k + 1]] for k in range(len(sizes))]
    gates = w[..., offs[-1]:]
    pad = jnp.zeros(w.shape[:-1] + (LANES - IDX_DIM - IDX_HEADS,), w.dtype)
    return jnp.concatenate([gates, q, cu, cgb, cgc, ssm, memq, qidx, ckv, kidx, widx, pad], axis=-1).astype(BF16)


def kernel(x, mem, ln_in_g, ln_in_b, w_in, kv_norm_g, w_uk, w_uv, conv_w, conv_b, lam_re, lam_im, b_re, b_im, c_re, c_im, d_skip, log_dt, w_glu, b_glu, w_mem_kv, w_branch, w_o, ln1_g, ln1_b, w_router, b_router, w_up, b_up, w_down, b_down, ln2_g, ln2_b):
    bsz, seq, d = x.shape
    depth = w_in.shape[0]
    mlen = mem.shape[1]
    n = bsz * seq
    alpha = float((2 * depth) ** 0.25)
    tm_combine = ROWS_COMBINE

    h = _ln_in(x.reshape(n, d), ln_in_g, ln_in_b)
    n_row_blocks = -(-n * TOP_K // EXPERT_BLOCK) + N_EXPERTS
    xs = jnp.zeros((n_row_blocks * EXPERT_BLOCK, d), F32)
    mem2 = mem.reshape(bsz * mlen, d)
    w_in_p = _permute_w_in(w_in)
    w_mem_kv_b = w_mem_kv.astype(BF16)
    for l in range(depth):
        proj = _matmul(h, w_in_p, l, ROWS_IN_PROJ, "in_proj")
        qcat, kcat, ckvn, ckvt = _prep(proj, kv_norm_g[l], bsz, seq)
        att = _dsa_attention(proj, qcat, kcat, ckvn, ckvt, w_uk[l], w_uv[l], kv_norm_g[l], bsz, seq)
        cnv = _short_conv(proj, conv_w[l], conv_b[l], bsz, seq)
        ssm = _s5_mixer(proj, lam_re[l], lam_im[l], b_re[l], b_im[l], c_re[l], c_im[l], d_skip[l], log_dt[l],
                        w_glu[l], b_glu[l], bsz, seq)
        kvm = _matmul(mem2, w_mem_kv_b, l, min(256, bsz * mlen), "mem_kv")
        mem_o = _memory_attention(proj, kvm, bsz, seq, mlen)
        h1, te_pad, tg_pad, counts = _merge_router(att, cnv, ssm, mem_o, proj, h, w_branch[l], w_o[l],
                                                   ln1_g[l], ln1_b[l], w_router[l], b_router[l], alpha)
        dest, blk_expert, n_used, pos_km = _moe_routing(te_pad[:, :TOP_K], te_pad[:, TOP_K:2 * TOP_K], counts,
                                                        n, tm_combine)
        xs = _dispatch(h1, dest, xs)
        ys = _expert_ffn(xs, blk_expert, n_used, l, w_up, b_up, w_down, b_down)
        h = _combine(ys, pos_km, tg_pad, h1, ln2_g[l], ln2_b[l], alpha, tm_combine)
    return h.reshape(bsz, seq, d)
```

```python
import functools
import math

import numpy as np
import jax
import jax.numpy as jnp
from jax import lax
from jax.experimental import pallas as pl
from jax.experimental.pallas import tpu as pltpu

F32 = jnp.float32
BF16 = jnp.bfloat16
I32 = jnp.int32
I16 = jnp.int16

N_HEADS = 8
HEAD_DIM = 64
KV_LATENT = 128
IDX_HEADS = 8
IDX_DIM = 32
INDEX_TOPK = 256

ROWS_LN = 512
ROWS_IN_PROJ = 256
ROWS_PREP = 512
ROWS_CONV = 512
ROWS_MEM_ATT = 512
ROWS_MERGE = 512
ROWS_DISPATCH = 256
ROWS_COMBINE = 256
DSA_QUERY_BLOCK = 256
DSA_SCORE_CHUNK = 512
DSA_COUNT_CHUNK = 512
DSA_TILE_KEYS = 128
CONV_WIDTH = 512
SSM_WIDTH = 512
SSM_GROUP = 16
SSM_GROUPS = SSM_WIDTH // SSM_GROUP
SSM_STATE = 64
MEM_HEADS = 4
MEM_HEAD_DIM = 128
N_BRANCH = 4
BRANCH_WIDTH = 512
N_EXPERTS = 32
TOP_K = 4
D_EXPERT = 1024
SWIGLU_LIMIT = 7.0
SWIGLU_ALPHA = 1.702
EXPERT_BLOCK = 256
LN_EPS = 1e-5

LANES = 128
SUBLANES = 8
VMEM_LIMIT_BYTES = 56 * 1024 * 1024

INT_MIN = -(2 ** 31)
NEG_BIG = -1e30
SAFE_LOGIT = 100.0

COL_GATES = 0
COL_Q = 4096
COL_CONV_U = 4608
COL_CONV_GB = 5120
COL_CONV_GC = 5632
COL_SSM = 6144
COL_MEMQ = 6656
COL_QIDX = 7168
COL_CKV = 7424
COL_TAIL = 7552
D_PROJ = 7680
TAIL_W_OFF = IDX_DIM

SSM_SEQS = 8


def _cparams(sem):
    return pltpu.CompilerParams(dimension_semantics=sem, vmem_limit_bytes=VMEM_LIMIT_BYTES)


def _layer_norm(x, g, b):
    mu = jnp.mean(x, axis=-1, keepdims=True)
    xc = x - mu
    var = jnp.mean(xc * xc, axis=-1, keepdims=True)
    return xc * lax.rsqrt(var + LN_EPS) * g + b


def _const_spec(shape):
    nd = len(shape)
    return pl.BlockSpec(shape, lambda *_: (0,) * nd)


def _ln_kernel(x_ref, g_ref, b_ref, o_ref):
    o_ref[...] = _layer_norm(x_ref[...], g_ref[...], b_ref[...])


def _ln_in(x2, g, b, tm=ROWS_LN):
    n, d = x2.shape
    return pl.pallas_call(
        _ln_kernel,
        out_shape=jax.ShapeDtypeStruct((n, d), F32),
        grid=(n // tm,),
        in_specs=[pl.BlockSpec((tm, d), lambda i: (i, 0)), _const_spec((1, d)), _const_spec((1, d))],
        out_specs=pl.BlockSpec((tm, d), lambda i: (i, 0)),
        compiler_params=_cparams(("parallel",)),
        name="ln_in",
    )(x2, g.reshape(1, d), b.reshape(1, d))


def _matmul_kernel(x_ref, w_ref, o_ref):
    o_ref[...] = jnp.dot(x_ref[...].astype(BF16), w_ref[...], preferred_element_type=F32)


def _matmul(x, w_stack_bf16, layer, tm, name):
    n, k = x.shape
    m = w_stack_bf16.shape[2]
    return pl.pallas_call(
        _matmul_kernel,
        out_shape=jax.ShapeDtypeStruct((n, m), F32),
        grid=(n // tm,),
        in_specs=[pl.BlockSpec((tm, k), lambda i: (i, 0)),
                  pl.BlockSpec((None, k, m), lambda i: (layer, 0, 0), pipeline_mode=pl.Buffered(1))],
        out_specs=pl.BlockSpec((tm, m), lambda i: (i, 0)),
        compiler_params=_cparams(("parallel",)),
        name=name,
    )(x, w_stack_bf16)


def _split_hi_lo(x):
    hi = x.astype(BF16)
    lo = (x - hi.astype(F32)).astype(BF16)
    return hi, lo


def _prep_kernel(qidx_ref, ckv_ref, tail_ref, g_ref, sq_ref, sk_ref, qcat_ref, kcat_ref, ckvn_ref, ckvt_ref):
    q_hi, q_lo = _split_hi_lo(qidx_ref[...])
    qcat = jnp.dot(jnp.concatenate([q_hi, q_lo], axis=1), sq_ref[...], preferred_element_type=F32)
    qcat_ref[...] = qcat.astype(BF16)
    k_hi, k_lo = _split_hi_lo(tail_ref[...])
    kcat = jnp.dot(jnp.concatenate([k_hi, k_lo], axis=1), sk_ref[...], preferred_element_type=F32)
    kcat_ref[...] = kcat.astype(BF16)
    c = ckv_ref[...]
    ms = jnp.mean(c * c, axis=-1, keepdims=True)
    cn = c * lax.rsqrt(ms + LN_EPS) * g_ref[...]
    ckvn_ref[...] = cn.astype(BF16)
    ckvt_ref[...] = cn.T.astype(BF16)


def _selection_matrices():
    sq = np.zeros((2 * IDX_HEADS * IDX_DIM, IDX_HEADS * LANES), np.float32)
    for h in range(IDX_HEADS):
        for d in range(IDX_DIM):
            hi_in = h * IDX_DIM + d
            lo_in = IDX_HEADS * IDX_DIM + hi_in
            sq[hi_in, h * LANES + d] = 1.0
            sq[hi_in, h * LANES + IDX_DIM + d] = 1.0
            sq[lo_in, h * LANES + 2 * IDX_DIM + d] = 1.0
    sk = np.zeros((2 * LANES, LANES), np.float32)
    for d in range(IDX_DIM):
        sk[d, d] = 1.0
        sk[LANES + d, IDX_DIM + d] = 1.0
        sk[d, 2 * IDX_DIM + d] = 1.0
    return jnp.asarray(sq, BF16), jnp.asarray(sk, BF16)


def _prep(proj, kv_norm_g, bsz, seq, tm=ROWS_PREP):
    n = proj.shape[0]
    tm = min(tm, seq)
    nbt = seq // tm
    sq, sk = _selection_matrices()
    return pl.pallas_call(
        _prep_kernel,
        out_shape=(jax.ShapeDtypeStruct((n, IDX_HEADS * LANES), BF16),
                   jax.ShapeDtypeStruct((n, LANES), BF16),
                   jax.ShapeDtypeStruct((n, KV_LATENT), BF16),
                   jax.ShapeDtypeStruct((bsz, KV_LATENT, seq), BF16)),
        grid=(n // tm,),
        in_specs=[pl.BlockSpec((tm, 256), lambda i: (i, COL_QIDX // 256)),
                  pl.BlockSpec((tm, 128), lambda i: (i, COL_CKV // 128)),
                  pl.BlockSpec((tm, 128), lambda i: (i, COL_TAIL // 128)),
                  _const_spec((1, KV_LATENT)), _const_spec(sq.shape), _const_spec(sk.shape)],
        out_specs=(pl.BlockSpec((tm, IDX_HEADS * LANES), lambda i: (i, 0)),
                   pl.BlockSpec((tm, LANES), lambda i: (i, 0)),
                   pl.BlockSpec((tm, KV_LATENT), lambda i: (i, 0)),
                   pl.BlockSpec((None, KV_LATENT, tm), lambda i: (i // nbt, 0, i % nbt))),
        compiler_params=_cparams(("parallel",)),
        name="dsa_prep",
    )(proj, proj, proj, kv_norm_g.reshape(1, KV_LATENT), sq, sk)


def _dsa_kernel(q_ref, tail_ref, qcat_ref, kcat_ref, ckv_ref, ckvt_ref, wuk_ref, wuv_ref, kbound_ref, o_ref,
                key_ref, khi_ref, klo_ref, qct_ref, qlat_ref, m_ref, l_ref, acc_ref, *, seq, topk, cks, ck, cka, qb):
    i = pl.program_id(1)
    q0 = i * qb
    nchs = (q0 + qb + cks - 1) // cks
    nch = (q0 + qb + ck - 1) // ck
    q_pos = q0 + lax.broadcasted_iota(I32, (1, qb), 1)
    row_pos = lax.broadcasted_iota(I32, (ck, 1), 0)
    w_t = tail_ref[...].T
    cnt_rows = min(64, ck)

    for h in range(IDX_HEADS):
        qct_ref[h] = qcat_ref[:, h * LANES:(h + 1) * LANES].astype(F32).T.astype(BF16)
    row_pos_t = lax.broadcasted_iota(I32, (cka, 1), 0)

    def score_chunk(c, carry):
        for sub in range(cks // cka):
            k0 = pl.multiple_of(c * cks + sub * cka, cka)
            kc = kcat_ref[pl.ds(k0, cka), :]
            s = jnp.zeros((cka, qb), F32)
            for h in range(IDX_HEADS):
                d = jnp.dot(kc, qct_ref[h], preferred_element_type=F32)
                s = s + jnp.maximum(d, 0.0) * w_t[TAIL_W_OFF + h:TAIL_W_OFF + h + 1, :]
            s = s + 0.0
            bits = pltpu.bitcast(s, I32)
            key = bits ^ ((bits >> 31) & jnp.int32(0x7FFFFFFF))
            key = jnp.where(k0 + row_pos_t <= q_pos, key, jnp.int32(INT_MIN))
            key_ref[pl.ds(k0, cka), :] = key
            khi_ref[pl.ds(k0, cka), :] = (key >> 16).astype(I16)
            klo_ref[pl.ds(k0, cka), :] = ((key & jnp.int32(0xFFFF)) - 32768).astype(I16)
        return carry

    lax.fori_loop(0, nchs, score_chunk, 0)

    def count_keys(pred_fn):
        def body(c, acc):
            k0 = pl.multiple_of(c * ck, ck)
            hit = jnp.where(pred_fn(key_ref[pl.ds(k0, ck), :], k0), 1.0, 0.0)
            return acc + jnp.sum(hit.reshape(ck // cnt_rows, cnt_rows, qb), axis=0)
        acc = lax.fori_loop(0, nch, body, jnp.zeros((cnt_rows, qb), F32))
        return jnp.sum(acc, axis=0, keepdims=True)

    kf = float(topk)
    pack = 16
    n_part = ck // (2 * pack)
    lo16 = -32768

    def count16(ref, cand16):
        cand = jnp.broadcast_to(cand16, (pack, qb)).astype(I16)
        one = jnp.ones((), I16)
        zero = jnp.zeros((), I16)

        def body(c, acc):
            k0 = pl.multiple_of(c * ck, ck)
            hit = jnp.where(ref[pl.ds(k0, ck), :].reshape(ck // pack, pack, qb) >= cand[None], one, zero)
            part = hit[0:2]
            for j in range(1, n_part):
                part = part + hit[2 * j:2 * j + 2]
            return acc + part
        acc = lax.fori_loop(0, nch, body, jnp.zeros((2, pack, qb), I16)).astype(I32).astype(F32)
        return jnp.sum(jnp.sum(acc, axis=0), axis=0, keepdims=True)

    def bisect16(ref, target):
        def one_pass(b, carry):
            w, c_ge_w, c_gt_w = carry
            cand = w + jnp.left_shift(jnp.int32(1), 15 - b)
            cnt = count16(ref, cand)
            ok = cnt >= target
            return jnp.where(ok, cand, w), jnp.where(ok, cnt, c_ge_w), jnp.where(ok, c_gt_w, cnt)
        total = jnp.full((1, qb), 1.0, F32) * (nch * ck).astype(F32)
        return lax.fori_loop(0, 16, one_pass, (jnp.full((1, qb), lo16, I32), total, jnp.zeros((1, qb), F32)))

    v16, c_v16, c_above = bisect16(khi_ref, kf)
    v16_b = jnp.broadcast_to(v16, (pack, qb)).astype(I16)

    def build_lo(c, carry):
        k0 = pl.multiple_of(c * ck, ck)
        hi = khi_ref[pl.ds(k0, ck), :].reshape(ck // pack, pack, qb)
        lo = klo_ref[pl.ds(k0, ck), :].reshape(ck // pack, pack, qb)
        khi_ref[pl.ds(k0, ck), :] = jnp.where(hi == v16_b[None], lo, jnp.full((), lo16, I16)).reshape(ck, qb)
        return carry

    lax.fori_loop(0, nch, build_lo, 0)
    w, c_w, c_w_above = bisect16(khi_ref, kf - c_above)
    c_w = jnp.where(w == lo16, c_v16 - c_above, c_w)
    v = (v16 << 16) + (w + 32768)
    select_all = v == jnp.int32(INT_MIN)
    thr = jnp.maximum(v, jnp.int32(INT_MIN + 1))
    c_ge = jnp.where(select_all, c_above, c_above + c_w)
    c_gt = jnp.where(select_all, c_above, c_above + c_w_above)
    need = kf - c_gt

    jlim_ref = m_ref.at[0:1]
    jlim_ref[...] = jnp.full((1, qb), float(seq), F32)
    has_tie = jnp.max(jnp.where(c_ge > kf, 1.0, 0.0)) > 0.0

    @pl.when(has_tie)
    def _():
        n_bits = max(1, int(math.ceil(math.log2(seq))))

        def pos_pass(b, j):
            cand = j + jnp.left_shift(jnp.int32(1), n_bits - 1 - b)
            cnt = count_keys(lambda key, k0: (key == thr) & (k0 + row_pos < cand))
            return jnp.where(cnt < need, cand, j)

        j = lax.fori_loop(0, n_bits, pos_pass, jnp.zeros((1, qb), I32))
        jlim_ref[...] = jnp.where(c_ge > kf, j.astype(F32), float(seq))

    jlim = jlim_ref[...].astype(I32)

    q_bf = q_ref[...].astype(BF16)
    qn2_max = jnp.zeros((1, qb), F32)
    for j in range(N_HEADS // 2):
        ql = jnp.dot(q_bf[:, j * LANES:(j + 1) * LANES], wuk_ref[j],
                     preferred_element_type=F32) * (HEAD_DIM ** -0.5 * math.log2(math.e))
        for half in range(2):
            qt = ql[:, half * KV_LATENT:(half + 1) * KV_LATENT].T.astype(BF16)
            qlat_ref[2 * j + half] = qt
            qf = qt.astype(F32)
            qn2_max = jnp.maximum(qn2_max, jnp.sum(qf * qf, axis=0, keepdims=True))
    logit_bound = jnp.sqrt(jnp.max(qn2_max)) * kbound_ref[0, 0]
    row_pos_a = lax.broadcasted_iota(I32, (cka, 1), 0)

    def att_tiles(base, n_sub, online):
        for sub in range(n_sub):
            k0 = pl.multiple_of(base + sub * cka, cka)
            kv = ckv_ref[pl.ds(k0, cka), :]
            kvt = ckvt_ref[:, pl.ds(k0, cka)]
            key = key_ref[pl.ds(k0, cka), :]
            sel = (key > thr) | ((key == thr) & (k0 + row_pos_a <= jlim))
            bias = jnp.where(sel, 0.0, NEG_BIG)
            for h in range(N_HEADS):
                lg = jnp.dot(kv, qlat_ref[h], preferred_element_type=F32) + bias
                if not online:
                    p = jnp.exp2(lg)
                    l_ref[h:h + 1, :] = l_ref[h:h + 1, :] + jnp.sum(p, axis=0, keepdims=True)
                    acc_ref[h] = acc_ref[h] + jnp.dot(kvt, p.astype(BF16), preferred_element_type=F32)
                else:
                    m_prev = m_ref[h:h + 1, :]
                    m_new = jnp.maximum(m_prev, jnp.max(lg, axis=0, keepdims=True))
                    p = jnp.exp2(lg - m_new)
                    alpha = jnp.exp2(m_prev - m_new)
                    l_ref[h:h + 1, :] = alpha * l_ref[h:h + 1, :] + jnp.sum(p, axis=0, keepdims=True)
                    acc_ref[h] = alpha * acc_ref[h] + jnp.dot(kvt, p.astype(BF16), preferred_element_type=F32)
                    m_ref[h:h + 1, :] = m_new

    def attend(online):
        def att_chunk(c, carry):
            att_tiles(c * cks, cks // cka, online)
            return carry

        end = q0 + qb
        n_full = end // cks
        lax.fori_loop(0, n_full, att_chunk, 0)
        if cks % qb == 0 and cks > qb:
            for parts in range(1, cks // qb):
                @pl.when(end - n_full * cks == parts * qb)
                def _():
                    att_tiles(n_full * cks, parts * qb // cka, online)
        else:
            @pl.when(end - n_full * cks > 0)
            def _():
                att_tiles(n_full * cks, cks // cka, online)

    m_ref[...] = jnp.full(m_ref.shape, NEG_BIG, F32)
    l_ref[...] = jnp.zeros(l_ref.shape, F32)
    acc_ref[...] = jnp.zeros(acc_ref.shape, F32)

    @pl.when(logit_bound < SAFE_LOGIT)
    def _():
        attend(online=False)

    @pl.when(jnp.logical_not(logit_bound < SAFE_LOGIT))
    def _():
        attend(online=True)

    outs = []
    for j in range(N_HEADS // 2):
        o_pair = [(acc_ref[h] / l_ref[h:h + 1, :]).T.astype(BF16) for h in (2 * j, 2 * j + 1)]
        outs.append(jnp.dot(jnp.concatenate(o_pair, axis=1), wuv_ref[j], preferred_element_type=F32))
    o_ref[...] = jnp.concatenate(outs, axis=1)


def _dsa_attention(proj, qcat, kcat, ckvn, ckvt, w_uk, w_uv, kv_norm_g, bsz, seq):
    n = proj.shape[0]
    qb = min(DSA_QUERY_BLOCK, seq)
    n_blk = seq // qb
    topk = min(INDEX_TOPK, seq // 4)
    cks = min(DSA_SCORE_CHUNK, seq)
    ck = min(DSA_COUNT_CHUNK, seq)
    cka = min(DSA_TILE_KEYS, seq)
    zk = jnp.zeros((HEAD_DIM, KV_LATENT), F32)
    zv = jnp.zeros((KV_LATENT, HEAD_DIM), F32)
    wuk = jnp.stack([jnp.block([[w_uk[:, 2 * j, :].T, zk], [zk, w_uk[:, 2 * j + 1, :].T]])
                     for j in range(N_HEADS // 2)]).astype(BF16)
    wuv = jnp.stack([jnp.block([[w_uv[:, 2 * j, :], zv], [zv, w_uv[:, 2 * j + 1, :]]])
                     for j in range(N_HEADS // 2)]).astype(BF16)
    kbound = (1.01 * math.sqrt(KV_LATENT) * jnp.max(jnp.abs(kv_norm_g))).reshape(1, 1).astype(F32)
    kernel = functools.partial(_dsa_kernel, seq=seq, topk=topk, cks=cks, ck=ck, cka=cka, qb=qb)
    return pl.pallas_call(
        kernel,
        out_shape=jax.ShapeDtypeStruct((n, N_HEADS * HEAD_DIM), F32),
        grid=(bsz, n_blk),
        in_specs=[pl.BlockSpec((qb, 512), lambda b, i: (b * n_blk + i, COL_Q // 512)),
                  pl.BlockSpec((qb, 128), lambda b, i: (b * n_blk + i, COL_TAIL // 128)),
                  pl.BlockSpec((qb, IDX_HEADS * LANES), lambda b, i: (b * n_blk + i, 0)),
                  pl.BlockSpec((seq, LANES), lambda b, i: (b, 0)),
                  pl.BlockSpec((seq, KV_LATENT), lambda b, i: (b, 0)),
                  pl.BlockSpec((None, KV_LATENT, seq), lambda b, i: (b, 0, 0)),
                  _const_spec(wuk.shape), _const_spec(wuv.shape),
                  pl.BlockSpec(memory_space=pltpu.SMEM)],
        out_specs=pl.BlockSpec((qb, N_HEADS * HEAD_DIM), lambda b, i: (b * n_blk + i, 0)),
        scratch_shapes=[pltpu.VMEM((seq, qb), I32),
                        pltpu.VMEM((seq, qb), I16),
                        pltpu.VMEM((seq, qb), I16),
                        pltpu.VMEM((IDX_HEADS, LANES, qb), BF16),
                        pltpu.VMEM((N_HEADS, KV_LATENT, qb), BF16),
                        pltpu.VMEM((N_HEADS, qb), F32),
                        pltpu.VMEM((N_HEADS, qb), F32),
                        pltpu.VMEM((N_HEADS, KV_LATENT, qb), F32)],
        compiler_params=_cparams(("parallel", "parallel")),
        name="dsa_attention",
    )(proj, proj, qcat, kcat, ckvn, ckvt, wuk, wuv, kbound)


def _conv_kernel(u_ref, gb_ref, gc_ref, pu_ref, pgc_ref, w_ref, b_ref, o_ref):
    i = pl.program_id(1)
    tb = u_ref.shape[0]
    v = gc_ref[...] * u_ref[...]
    halo = jnp.where(i > 0, pgc_ref[...] * pu_ref[...], 0.0)
    vfull = jnp.concatenate([halo, v], axis=0)
    v1 = pltpu.roll(vfull, 1, 0)[SUBLANES:SUBLANES + tb]
    v2 = pltpu.roll(vfull, 2, 0)[SUBLANES:SUBLANES + tb]
    w = w_ref[...]
    y = w[0:1] * v2 + w[1:2] * v1 + w[2:3] * v
    o_ref[...] = gb_ref[...] * (y + b_ref[...])


def _short_conv(proj, conv_w, conv_b, bsz, seq, tb=ROWS_CONV):
    n = proj.shape[0]
    tb = min(tb, seq)
    nb = seq // tb
    cw = CONV_WIDTH
    blk = lambda col: pl.BlockSpec((tb, cw), lambda b, i: (b * nb + i, col // cw))
    prev = lambda col: pl.BlockSpec(
        (SUBLANES, cw), lambda b, i: (jnp.maximum((b * nb + i) * (tb // SUBLANES) - 1, 0), col // cw))
    return pl.pallas_call(
        _conv_kernel,
        out_shape=jax.ShapeDtypeStruct((n, cw), F32),
        grid=(bsz, nb),
        in_specs=[blk(COL_CONV_U), blk(COL_CONV_GB), blk(COL_CONV_GC), prev(COL_CONV_U), prev(COL_CONV_GC),
                  _const_spec((SUBLANES, cw)), _const_spec((1, cw))],
        out_specs=pl.BlockSpec((tb, cw), lambda b, i: (b * nb + i, 0)),
        compiler_params=_cparams(("parallel", "parallel")),
        name="short_conv",
    )(proj, proj, proj, proj, proj,
      jnp.pad(conv_w, ((0, SUBLANES - conv_w.shape[0]), (0, 0))), conv_b.reshape(1, cw))


def _s5_scan_chunk(bu_ref, a_ref, st_ref, n_steps, store):
    s_tot = a_ref.shape[1] // 2
    cb = 512
    for blk in range(s_tot // cb):
        re_sl = pl.ds(blk * cb, cb)
        im_sl = pl.ds(s_tot + blk * cb, cb)
        ar = jnp.broadcast_to(a_ref[0:1, blk * cb:(blk + 1) * cb], (SSM_SEQS, cb))
        ai = jnp.broadcast_to(a_ref[0:1, s_tot + blk * cb:s_tot + (blk + 1) * cb], (SSM_SEQS, cb))

        def step(t, carry):
            re, im = carry
            r0 = pl.multiple_of(t * SSM_SEQS, SSM_SEQS)
            br = bu_ref[pl.ds(r0, SSM_SEQS), re_sl]
            bi = bu_ref[pl.ds(r0, SSM_SEQS), im_sl]
            nre = ar * re - ai * im + br
            nim = ar * im + ai * re + bi
            if store:
                bu_ref[pl.ds(r0, SSM_SEQS), re_sl] = nre
                bu_ref[pl.ds(r0, SSM_SEQS), im_sl] = nim
            return nre, nim

        re, im = lax.fori_loop(0, n_steps, step, (st_ref[:, re_sl], st_ref[:, im_sl]), unroll=8)
        st_ref[:, re_sl] = re
        st_ref[:, im_sl] = im


def _gelu_tanh(x):
    return 0.5 * x * (1.0 + jnp.tanh(math.sqrt(2.0 / math.pi) * (x + 0.044715 * (x * x * x))))


S5_ROWS = 512


def _s5_kernel(u_ref, perm_ref, wb_ref, a_ref, apow_ref, wc_ref, d_ref, wg_ref, bg_ref, o_ref,
               bu_ref, st_ref, carry_ref, *, seg_len, steps_per_seq):
    c = pl.program_id(0)
    s_tot = a_ref.shape[1] // 2
    tn_dims = (((0,), (0,)), ((), ()))

    @pl.when(c % steps_per_seq == 0)
    def _():
        carry_ref[...] = jnp.zeros(carry_ref.shape, F32)

    u = u_ref[...]
    perm = perm_ref[...]
    u_il = jnp.dot(perm, u.astype(BF16), preferred_element_type=F32).astype(BF16)
    hw = u_il.shape[1] // 2
    hs = s_tot // 2
    for part in range(2):
        for q in range(2):
            c0 = part * s_tot + q * hs
            bu_ref[:, c0:c0 + hs] = jnp.dot(u_il[:, q * hw:(q + 1) * hw], wb_ref[q * hw:(q + 1) * hw, c0:c0 + hs],
                                            preferred_element_type=F32)

    st_ref[...] = jnp.zeros(st_ref.shape, F32)
    _s5_scan_chunk(bu_ref, a_ref, st_ref, seg_len, store=False)

    pr = apow_ref[0:1, :s_tot]
    pi = apow_ref[0:1, s_tot:]
    hre = carry_ref[0:1, :s_tot]
    him = carry_ref[0:1, s_tot:]
    for j in range(SSM_SEQS):
        ere = st_ref[j:j + 1, :s_tot]
        eim = st_ref[j:j + 1, s_tot:]
        st_ref[j:j + 1, :s_tot] = hre
        st_ref[j:j + 1, s_tot:] = him
        hre, him = pr * hre - pi * him + ere, pr * him + pi * hre + eim
    carry_ref[0:1, :s_tot] = hre
    carry_ref[0:1, s_tot:] = him

    _s5_scan_chunk(bu_ref, a_ref, st_ref, seg_len, store=True)
    halves = []
    for q in range(2):
        acc = None
        for part in range(2):
            c0 = part * s_tot + q * hs
            term = jnp.dot(bu_ref[:, c0:c0 + hs].astype(BF16), wc_ref[c0:c0 + hs, q * hw:(q + 1) * hw],
                           preferred_element_type=F32)
            acc = term if acc is None else acc + term
        halves.append(acc)
    ch = jnp.concatenate(halves, axis=1)
    ch_hi, ch_lo = _split_hi_lo(ch)
    ch_nat = (lax.dot_general(perm, ch_hi, tn_dims, preferred_element_type=F32)
              + lax.dot_general(perm, ch_lo, tn_dims, preferred_element_type=F32))
    z = _gelu_tanh(ch_nat + d_ref[...] * u)
    gate = jnp.dot(z.astype(BF16), wg_ref[...], preferred_element_type=F32) + bg_ref[...]
    o_ref[...] = z * jax.nn.sigmoid(gate)


def _s5_mixer(proj, lam_re, lam_im, b_re, b_im, c_re, c_im, d_skip, log_dt, w_glu, b_glu, bsz, seq):
    n = proj.shape[0]
    g, p, nn = SSM_GROUPS, SSM_STATE, SSM_GROUP
    s_tot = g * p
    rows = min(S5_ROWS, seq)
    seg_len = rows // SSM_SEQS

    dt = jnp.exp(log_dt.astype(F32))[:, None]
    lam = lax.complex(lam_re.astype(F32), lam_im.astype(F32))
    a_bar = jnp.exp(lam * dt)
    b_bar = ((a_bar - 1.0) / lam)[:, :, None] * lax.complex(b_re.astype(F32), b_im.astype(F32))
    a_pow = jnp.exp(lam * dt * seg_len)
    a_vec = jnp.concatenate([jnp.real(a_bar).reshape(1, s_tot), jnp.imag(a_bar).reshape(1, s_tot)], axis=1)
    apow_vec = jnp.concatenate([jnp.real(a_pow).reshape(1, s_tot), jnp.imag(a_pow).reshape(1, s_tot)], axis=1)
    eye = jnp.eye(g, dtype=F32)
    wb = jnp.concatenate(
        [jnp.einsum('gpn,gh->gnhp', jnp.real(b_bar), eye).reshape(g * nn, s_tot),
         jnp.einsum('gpn,gh->gnhp', jnp.imag(b_bar), eye).reshape(g * nn, s_tot)], axis=1).astype(BF16)
    wc = jnp.concatenate(
        [jnp.einsum('gnp,gh->gphn', c_re.astype(F32), eye).reshape(s_tot, g * nn),
         -jnp.einsum('gnp,gh->gphn', c_im.astype(F32), eye).reshape(s_tot, g * nn)], axis=0).astype(BF16)
    perm_np = np.zeros((rows, rows), np.float32)
    t_idx, j_idx = np.meshgrid(np.arange(seg_len), np.arange(SSM_SEQS), indexing='ij')
    perm_np[(t_idx * SSM_SEQS + j_idx).ravel(), (j_idx * seg_len + t_idx).ravel()] = 1.0
    perm = jnp.asarray(perm_np, BF16)

    return pl.pallas_call(
        functools.partial(_s5_kernel, seg_len=seg_len, steps_per_seq=seq // rows),
        out_shape=jax.ShapeDtypeStruct((n, SSM_WIDTH), F32),
        grid=(n // rows,),
        in_specs=[pl.BlockSpec((rows, SSM_WIDTH), lambda c: (c, COL_SSM // SSM_WIDTH)),
                  _const_spec(perm.shape), _const_spec(wb.shape), _const_spec(a_vec.shape),
                  _const_spec(apow_vec.shape), _const_spec(wc.shape), _const_spec((1, SSM_WIDTH)),
                  _const_spec((SSM_WIDTH, SSM_WIDTH)), _const_spec((1, SSM_WIDTH))],
        out_specs=pl.BlockSpec((rows, SSM_WIDTH), lambda c: (c, 0)),
        scratch_shapes=[pltpu.VMEM((rows, 2 * s_tot), F32), pltpu.VMEM((SSM_SEQS, 2 * s_tot), F32),
                        pltpu.VMEM((1, 2 * s_tot), F32)],
        compiler_params=_cparams(("arbitrary",)),
        name="s5_mixer",
    )(proj, perm, wb, a_vec, apow_vec, wc, d_skip.reshape(1, SSM_WIDTH).astype(F32),
      w_glu.astype(BF16), b_glu.reshape(1, SSM_WIDTH))


def _mem_attn_kernel(q_ref, kv_ref, o_ref):
    hw = MEM_HEADS * MEM_HEAD_DIM
    q = q_ref[...].astype(BF16)
    kv = kv_ref[...].astype(BF16)
    for h in range(MEM_HEADS):
        sl = slice(h * MEM_HEAD_DIM, (h + 1) * MEM_HEAD_DIM)
        k = kv[:, sl]
        v = kv[:, hw + h * MEM_HEAD_DIM:hw + (h + 1) * MEM_HEAD_DIM]
        lg = lax.dot_general(q[:, sl], k, (((1,), (1,)), ((), ())),
                             preferred_element_type=F32) * (MEM_HEAD_DIM ** -0.5)
        m = jnp.max(lg, axis=1, keepdims=True)
        p = jnp.exp(lg - m)
        p = p / jnp.sum(p, axis=1, keepdims=True)
        o_ref[:, sl] = jnp.dot(p.astype(BF16), v, preferred_element_type=F32)


def _memory_attention(proj, kvm, bsz, seq, mlen, tm=ROWS_MEM_ATT):
    n = proj.shape[0]
    tm = min(tm, seq)
    nb = seq // tm
    hw = MEM_HEADS * MEM_HEAD_DIM
    return pl.pallas_call(
        _mem_attn_kernel,
        out_shape=jax.ShapeDtypeStruct((n, hw), F32),
        grid=(bsz, nb),
        in_specs=[pl.BlockSpec((tm, hw), lambda b, i: (b * nb + i, COL_MEMQ // hw)),
                  pl.BlockSpec((mlen, 2 * hw), lambda b, i: (b, 0))],
        out_specs=pl.BlockSpec((tm, hw), lambda b, i: (b * nb + i, 0)),
        compiler_params=_cparams(("parallel", "parallel")),
        name="memory_attention",
    )(proj, kvm)


def _merge_kernel(att_ref, cnv_ref, ssm_ref, mem_ref, gl_ref, h_ref, wbr_ref, wo_ref, g_ref, b_ref,
                  wr_ref, br_ref, ltri_ref, h1_ref, te_ref, tg_ref, cnt_ref, base_ref, *, alpha):
    d = h_ref.shape[1]

    @pl.when(pl.program_id(0) == 0)
    def _():
        base_ref[...] = jnp.zeros(base_ref.shape, F32)

    merged = jnp.zeros(h_ref.shape, F32)
    for r, br in enumerate((att_ref, cnv_ref, ssm_ref, mem_ref)):
        y = jnp.dot(br[...].astype(BF16), wbr_ref[r], preferred_element_type=F32)
        merged = merged + y * jax.nn.sigmoid(gl_ref[:, r * d:(r + 1) * d])
    y = alpha * h_ref[...] + jnp.dot(merged.astype(BF16), wo_ref[...], preferred_element_type=F32)
    h1 = _layer_norm(y, g_ref[...], b_ref[...])
    h1_ref[...] = h1

    logits = jnp.dot(h1, wr_ref[...], preferred_element_type=F32, precision=lax.Precision.HIGHEST) + br_ref[...]
    tm = logits.shape[0]
    lane = lax.broadcasted_iota(I32, (tm, N_EXPERTS), 1)
    out_lane = lax.broadcasted_iota(I32, (tm, LANES), 1)
    work = logits
    top_e = jnp.zeros((tm, LANES), I32)
    top_v = jnp.zeros((tm, LANES), F32)
    vals = []
    onehots = []
    for k in range(TOP_K):
        mx = jnp.max(work, axis=1, keepdims=True)
        idx = jnp.min(jnp.where(work == mx, lane, N_EXPERTS), axis=1, keepdims=True)
        hit = lane == idx
        work = jnp.where(hit, -jnp.inf, work)
        top_e = jnp.where(out_lane == k, idx, top_e)
        vals.append(mx)
        onehots.append(jnp.where(hit, 1.0, 0.0))
    den = sum(jnp.exp(vk - vals[0]) for vk in vals)
    for k in range(TOP_K):
        top_v = jnp.where(out_lane == k, jnp.exp(vals[k] - vals[0]) / den, top_v)

    per_tok = onehots[0] + onehots[1] + onehots[2] + onehots[3]
    before = jnp.dot(ltri_ref[...], per_tok.astype(BF16), preferred_element_type=F32) + base_ref[...]
    for k in range(TOP_K):
        rank = jnp.sum(onehots[k] * before, axis=1, keepdims=True)
        top_e = jnp.where(out_lane == TOP_K + k, rank.astype(I32), top_e)
    base_ref[...] = base_ref[...] + jnp.sum(per_tok, axis=0, keepdims=True)
    cnt_ref[...] = base_ref[...]
    te_ref[...] = top_e
    tg_ref[...] = top_v


def _merge_router(att, cnv, ssm, mem_o, proj, h, w_branch, w_o, ln_g, ln_b, w_router, b_router, alpha, tm=ROWS_MERGE):
    n, d = h.shape
    bw = BRANCH_WIDTH
    row = lambda w: pl.BlockSpec((tm, w), lambda i: (i, 0))
    ltri = jnp.asarray(np.tril(np.ones((tm, tm), np.float32), -1), BF16)
    return pl.pallas_call(
        functools.partial(_merge_kernel, alpha=alpha),
        out_shape=(jax.ShapeDtypeStruct((n, d), F32),
                   jax.ShapeDtypeStruct((n, LANES), I32),
                   jax.ShapeDtypeStruct((n, LANES), F32),
                   jax.ShapeDtypeStruct((1, N_EXPERTS), F32)),
        grid=(n // tm,),
        in_specs=[row(bw), row(bw), row(bw), row(bw),
                  pl.BlockSpec((tm, N_BRANCH * d), lambda i: (i, COL_GATES // (N_BRANCH * d))),
                  row(d),
                  _const_spec((N_BRANCH, bw, d)), _const_spec((d, d)), _const_spec((1, d)), _const_spec((1, d)),
                  _const_spec((d, N_EXPERTS)), _const_spec((1, N_EXPERTS)), _const_spec((tm, tm))],
        out_specs=(row(d), row(LANES), row(LANES), _const_spec((1, N_EXPERTS))),
        scratch_shapes=[pltpu.VMEM((1, N_EXPERTS), F32)],
        compiler_params=_cparams(("arbitrary",)),
        name="merge_router",
    )(att, cnv, ssm, mem_o, proj, h, w_branch.astype(BF16), w_o.astype(BF16),
      ln_g.reshape(1, d), ln_b.reshape(1, d), w_router, b_router.reshape(1, N_EXPERTS), ltri)


DISPATCH_UNROLL = 8
COMBINE_UNROLL = 16


def _dispatch_kernel(dest_ref, h_ref, xs_in_hbm, xs_hbm, sem, *, tm):
    del xs_in_hbm

    def body(r, carry):
        for k in range(TOP_K):
            pltpu.make_async_copy(h_ref.at[pl.ds(r, 1), :],
                                  xs_hbm.at[pl.ds(dest_ref[0, r * TOP_K + k], 1), :], sem).start(priority=k % 2)
        return carry

    lax.fori_loop(0, tm, body, 0, unroll=DISPATCH_UNROLL)
    pltpu.make_async_copy(xs_hbm.at[pl.ds(0, tm * TOP_K), :], xs_hbm.at[pl.ds(0, tm * TOP_K), :], sem).wait()


def _dispatch(h1, dest, xs_prev, tm=ROWS_DISPATCH):
    n, d = h1.shape
    nb = n // tm
    return pl.pallas_call(
        functools.partial(_dispatch_kernel, tm=tm),
        out_shape=jax.ShapeDtypeStruct(xs_prev.shape, F32),
        grid=(nb,),
        in_specs=[pl.BlockSpec((None, 1, tm * TOP_K), lambda i: (i, 0, 0), memory_space=pltpu.SMEM),
                  pl.BlockSpec((tm, d), lambda i: (i, 0)),
                  pl.BlockSpec(memory_space=pl.ANY)],
        out_specs=pl.BlockSpec(memory_space=pl.ANY),
        scratch_shapes=[pltpu.SemaphoreType.DMA(())],
        input_output_aliases={2: 0},
        compiler_params=_cparams(("arbitrary",)),
        name="moe_dispatch",
    )(dest.reshape(nb, 1, tm * TOP_K), h1, xs_prev)


def _expert_kernel(meta_ref, x_ref, wu_ref, bu_ref, wd_ref, bd_ref, sel_ref, o_ref, wu_bf, wd_bf):
    j = pl.program_id(0)
    n_used = meta_ref[0]
    f2 = wu_ref.shape[1]

    @pl.when((j < n_used) & ((j == 0) | (meta_ref[1 + j] != meta_ref[jnp.maximum(j, 1)])))
    def _():
        sel = sel_ref[...]
        cw = sel.shape[0]
        for c in range(f2 // cw):
            grp = jnp.dot(wu_ref[:, c * cw:(c + 1) * cw].astype(BF16), sel, preferred_element_type=F32)
            wu_bf[:, c * cw:(c + 1) * cw] = grp.astype(BF16)
        wd_bf[...] = wd_ref[...].astype(BF16)

    @pl.when(j < n_used)
    def _():
        xb = x_ref[...].astype(BF16)
        hdn = jnp.dot(xb, wu_bf[...], preferred_element_type=F32) + bu_ref[...]
        cw = sel_ref.shape[0]
        half = cw // 2
        parts = []
        for c in range(f2 // cw):
            h_glu = jnp.minimum(hdn[:, c * cw:c * cw + half], SWIGLU_LIMIT)
            h_lin = jnp.clip(hdn[:, c * cw + half:(c + 1) * cw], -SWIGLU_LIMIT, SWIGLU_LIMIT)
            parts.append((h_glu * jax.nn.sigmoid(SWIGLU_ALPHA * h_glu) * (h_lin + 1.0)).astype(BF16))
        act_c = jnp.concatenate(parts, axis=1)
        o_ref[...] = jnp.dot(act_c, wd_bf[...], preferred_element_type=F32) + bd_ref[...]

    @pl.when(j >= n_used)
    def _():
        o_ref[...] = jnp.zeros(o_ref.shape, F32)


def _expert_ffn(xs, blk_expert, n_used, layer, w_up, b_up, w_down, b_down):
    n_rows, d = xs.shape
    n_blocks = n_rows // EXPERT_BLOCK
    f = D_EXPERT
    meta = jnp.concatenate([n_used.reshape(1).astype(I32), blk_expert.astype(I32)])
    sel_np = np.zeros((2 * LANES, 2 * LANES), np.float32)
    sel_np[2 * np.arange(LANES), np.arange(LANES)] = 1.0
    sel_np[2 * np.arange(LANES) + 1, LANES + np.arange(LANES)] = 1.0
    sel = jnp.asarray(sel_np, BF16)
    le = b_up.shape[:2]
    b_up = jnp.transpose(b_up.reshape(le + (f // LANES, LANES, 2)), (0, 1, 2, 4, 3)).reshape(le + (2 * f,))
    e_of = lambda j, m: m[1 + j]
    grid_spec = pltpu.PrefetchScalarGridSpec(
        num_scalar_prefetch=1,
        grid=(n_blocks,),
        in_specs=[
            pl.BlockSpec((EXPERT_BLOCK, d), lambda j, m: (j, 0)),
            pl.BlockSpec((None, None, d, 2 * f), lambda j, m: (layer, e_of(j, m), 0, 0)),
            pl.BlockSpec((None, None, 1, 2 * f), lambda j, m: (layer, e_of(j, m), 0, 0)),
            pl.BlockSpec((None, None, f, d), lambda j, m: (layer, e_of(j, m), 0, 0)),
            pl.BlockSpec((None, None, 1, d), lambda j, m: (layer, e_of(j, m), 0, 0)),
            pl.BlockSpec(sel.shape, lambda j, m: (0, 0)),
        ],
        out_specs=pl.BlockSpec((EXPERT_BLOCK, d), lambda j, m: (j, 0)),
        scratch_shapes=[pltpu.VMEM((d, 2 * f), BF16), pltpu.VMEM((f, d), BF16)],
    )
    return pl.pallas_call(
        _expert_kernel,
        out_shape=jax.ShapeDtypeStruct((n_rows, d), F32),
        grid_spec=grid_spec,
        compiler_params=_cparams(("arbitrary",)),
        name="moe_experts",
    )(meta, xs, w_up, b_up[:, :, None, :], w_down, b_down[:, :, None, :], sel)


def _combine_kernel(cur_ref, nxt_ref, ys_hbm, g4_ref, h_ref, lg_ref, lb_ref, o_ref, ybuf, sem, *, alpha, tm):
    j = pl.program_id(0)
    nblk = pl.num_programs(0)
    slot = j % 2
    n_rows = tm * TOP_K

    def gather(idx_ref, s):
        def body(g, carry):
            r0 = pl.multiple_of(g * COMBINE_UNROLL, COMBINE_UNROLL)
            for u in range(COMBINE_UNROLL):
                pltpu.make_async_copy(ys_hbm.at[pl.ds(idx_ref[0, r0 + u], 1), :],
                                      ybuf.at[s, pl.ds(r0 + u, 1), :], sem.at[s]).start(priority=u % 2)
            return carry
        lax.fori_loop(0, n_rows // COMBINE_UNROLL, body, 0)

    @pl.when(j == 0)
    def _():
        gather(cur_ref, 0)

    @pl.when(j + 1 < nblk)
    def _():
        gather(nxt_ref, 1 - slot)

    pltpu.make_async_copy(ys_hbm.at[pl.ds(0, n_rows), :], ybuf.at[slot], sem.at[slot]).wait()
    g4 = g4_ref[...]
    ffn = jnp.zeros((tm, h_ref.shape[1]), F32)
    for k in range(TOP_K):
        ffn = ffn + ybuf[slot, k * tm:(k + 1) * tm, :] * g4[:, k:k + 1]
    o_ref[...] = _layer_norm(alpha * h_ref[...] + ffn, lg_ref[...], lb_ref[...])


def _combine(ys, pos_km, gates_pad, h1, ln_g, ln_b, alpha, tm=ROWS_COMBINE):
    n, d = h1.shape
    nb = n // tm
    idx3 = pos_km.reshape(nb, 1, tm * TOP_K)
    return pl.pallas_call(
        functools.partial(_combine_kernel, alpha=alpha, tm=tm),
        out_shape=jax.ShapeDtypeStruct((n, d), F32),
        grid=(nb,),
        in_specs=[
            pl.BlockSpec((None, 1, tm * TOP_K), lambda j: (j, 0, 0), memory_space=pltpu.SMEM),
            pl.BlockSpec((None, 1, tm * TOP_K), lambda j: (jnp.minimum(j + 1, nb - 1), 0, 0),
                         memory_space=pltpu.SMEM),
            pl.BlockSpec(memory_space=pl.ANY),
            pl.BlockSpec((tm, LANES), lambda j: (j, 0)),
            pl.BlockSpec((tm, d), lambda j: (j, 0)),
            _const_spec((1, d)), _const_spec((1, d)),
        ],
        out_specs=pl.BlockSpec((tm, d), lambda j: (j, 0)),
        scratch_shapes=[pltpu.VMEM((2, tm * TOP_K, d), F32), pltpu.SemaphoreType.DMA((2,))],
        compiler_params=_cparams(("arbitrary",)),
        name="moe_combine",
    )(idx3, idx3, ys, gates_pad, h1, ln_g.reshape(1, d), ln_b.reshape(1, d))


def _moe_routing(top_e, rank, counts, n_tok, tm_combine):
    n_asg = n_tok * TOP_K
    counts = counts.reshape(N_EXPERTS).astype(I32)
    nblk_per = (counts + EXPERT_BLOCK - 1) // EXPERT_BLOCK
    blk_end = jnp.cumsum(nblk_per)
    pstarts = (blk_end - nblk_per) * EXPERT_BLOCK
    onehot = top_e[..., None] == jnp.arange(N_EXPERTS, dtype=I32)
    dest = jnp.sum(jnp.where(onehot, pstarts, 0), axis=-1) + rank
    n_blocks = -(-n_asg // EXPERT_BLOCK) + N_EXPERTS
    blk_expert = jnp.minimum(jnp.sum(blk_end[None, :] <= jnp.arange(n_blocks, dtype=I32)[:, None], axis=1),
                             N_EXPERTS - 1)
    n_used = blk_end[-1]
    nb = n_tok // tm_combine
    pos_km = jnp.transpose(dest.reshape(nb, tm_combine, TOP_K), (0, 2, 1)).reshape(nb, TOP_K * tm_combine)
    return dest.astype(I32), blk_expert.astype(I32), n_used.astype(I32), pos_km.astype(I32)


def _permute_w_in(w):
    sizes = (N_HEADS * HEAD_DIM, KV_LATENT, IDX_HEADS * IDX_DIM, IDX_DIM, IDX_HEADS,
             CONV_WIDTH, CONV_WIDTH, CONV_WIDTH, SSM_WIDTH, MEM_HEADS * MEM_HEAD_DIM)
    offs = np.cumsum((0,) + sizes)
    q, ckv, qidx, kidx, widx, cu, cgb, cgc, ssm, memq = [w[..., offs[k]:offs[k + 1]] for k in range(len(sizes))]
    gates = w[..., offs[-1]:]
    pad = jnp.zeros(w.shape[:-1] + (LANES - IDX_DIM - IDX_HEADS,), w.dtype)
    return jnp.concatenate([gates, q, cu, cgb, cgc, ssm, memq, qidx, ckv, kidx, widx, pad], axis=-1).astype(BF16)


def kernel(x, mem, ln_in_g, ln_in_b, w_in, kv_norm_g, w_uk, w_uv, conv_w, conv_b, lam_re, lam_im, b_re, b_im, c_re, c_im, d_skip, log_dt, w_glu, b_glu, w_mem_kv, w_branch, w_o, ln1_g, ln1_b, w_router, b_router, w_up, b_up, w_down, b_down, ln2_g, ln2_b):
    bsz, seq, d = x.shape
    depth = w_in.shape[0]
    mlen = mem.shape[1]
    n = bsz * seq
    alpha = float((2 * depth) ** 0.25)
    tm_combine = ROWS_COMBINE

    h = _ln_in(x.reshape(n, d), ln_in_g, ln_in_b)
    n_row_blocks = -(-n * TOP_K // EXPERT_BLOCK) + N_EXPERTS
    xs = jnp.zeros((n_row_blocks * EXPERT_BLOCK, d), F32)
    mem2 = mem.reshape(bsz * mlen, d)
    w_in_p = _permute_w_in(w_in)
    w_mem_kv_b = w_mem_kv.astype(BF16)
    for l in range(depth):
        proj = _matmul(h, w_in_p, l, ROWS_IN_PROJ, "in_proj")
        qcat, kcat, ckvn, ckvt = _prep(proj, kv_norm_g[l], bsz, seq)
        att = _dsa_attention(proj, qcat, kcat, ckvn, ckvt, w_uk[l], w_uv[l], kv_norm_g[l], bsz, seq)
        cnv = _short_conv(proj, conv_w[l], conv_b[l], bsz, seq)
        ssm = _s5_mixer(proj, lam_re[l], lam_im[l], b_re[l], b_im[l], c_re[l], c_im[l], d_skip[l], log_dt[l],
                        w_glu[l], b_glu[l], bsz, seq)
        kvm = _matmul(mem2, w_mem_kv_b, l, min(256, bsz * mlen), "mem_kv")
        mem_o = _memory_attention(proj, kvm, bsz, seq, mlen)
        h1, te_pad, tg_pad, counts = _merge_router(att, cnv, ssm, mem_o, proj, h, w_branch[l], w_o[l],
                                                   ln1_g[l], ln1_b[l], w_router[l], b_router[l], alpha)
        dest, blk_expert, n_used, pos_km = _moe_routing(te_pad[:, :TOP_K], te_pad[:, TOP_K:2 * TOP_K], counts,
                                                        n, tm_combine)
        xs = _dispatch(h1, dest, xs)
        ys = _expert_ffn(xs, blk_expert, n_used, l, w_up, b_up, w_down, b_down)
        h = _combine(ys, pos_km, tg_pad, h1, ln2_g[l], ln2_b[l], alpha, tm_combine)
    return h.reshape(bsz, seq, d)
```
